```python
import math
import jax, jax.numpy as jnp
from jax import lax
import numpy as np

D_MODEL = 1024
BATCH = 4
SEQ = 8192
DEPTH = 2
DEC_BATCH = 16
DEC_SEQ = 32
PAST_LEN = 1024

CHUNK = 64
Q_BLOCK = 128
PLE_DIM = 256
EPS = 1e-6

A_HEADS = 4
A_HEAD_DIM = 64
B_HEADS = 4
B_HEAD_DIM = 64
ROPE_BASE = 10000.0
C_HEADS = 8
C_HEAD_DIM = 64
C_INNER = C_HEADS * C_HEAD_DIM
C_GROUPS = 2
C_STATE = 128
C_CONV = 4
C_CONV_DIM = C_INNER + 2 * C_GROUPS * C_STATE
MIX_WIDTH = A_HEADS * A_HEAD_DIM + B_HEADS * B_HEAD_DIM + C_INNER

IN_SPLITS = (A_HEADS * A_HEAD_DIM, A_HEADS * A_HEAD_DIM, A_HEADS * A_HEAD_DIM, A_HEADS,
             B_HEADS * B_HEAD_DIM, B_HEADS * B_HEAD_DIM, B_HEADS * B_HEAD_DIM, B_HEADS * B_HEAD_DIM,
             C_INNER, C_INNER, C_GROUPS * C_STATE, C_GROUPS * C_STATE, C_HEADS)
IN_WIDTH = 3 * A_HEADS * A_HEAD_DIM + A_HEADS + 4 * B_HEADS * B_HEAD_DIM + 2 * C_INNER + 2 * C_GROUPS * C_STATE + C_HEADS

PEER_HEADS = 8
PEER_NKEYS = 128
PEER_EXPERTS = PEER_NKEYS * PEER_NKEYS
PEER_KEY_DIM = 256
PEER_HALF = PEER_KEY_DIM // 2
PEER_TOPK = 16
PEER_TOKEN_BLOCK = 128

kernel_name = 'hybrid_stream_encoder_step'

F32 = jnp.float32


def rmsnorm(x, g):
    xf = x.astype(F32)
    y = xf * lax.rsqrt(jnp.mean(xf * xf, axis=-1, keepdims=True) + EPS)
    return (y * g.astype(F32)).astype(x.dtype)


def split_last(u, sizes):
    points = []
    acc = 0
    for s in sizes[:-1]:
        acc += s
        points.append(acc)
    return jnp.split(u, points, axis=-1)


def rotary(x, pos):
    x = x.astype(F32)
    half = x.shape[-1] // 2
    inv_freq = ROPE_BASE ** (-jnp.arange(half, dtype=F32) / half)
    ang = pos.astype(F32)[:, None] * inv_freq[None, :]
    cos = jnp.cos(ang)[None, :, None, :]
    sin = jnp.sin(ang)[None, :, None, :]
    x1, x2 = x[..., :half], x[..., half:]
    return jnp.concatenate([x1 * cos - x2 * sin, x1 * sin + x2 * cos], axis=-1)


def head_layernorm(y, g):
    B, L, H, d = y.shape
    yc = y - jnp.mean(y, axis=-1, keepdims=True)
    yn = yc * lax.rsqrt(jnp.mean(yc * yc, axis=-1, keepdims=True) + EPS)
    return yn.reshape(B, L, H * d) * g.astype(F32)


def decay_chunk(state, q, k, v, log_a):
    L = q.shape[1]
    cum = jnp.cumsum(log_a, axis=1)
    cum_h = jnp.transpose(cum, (0, 2, 1))
    causal = jnp.tril(jnp.ones((L, L), dtype=bool))
    decay = jnp.exp(jnp.where(causal, cum_h[:, :, :, None] - cum_h[:, :, None, :], -jnp.inf))
    scores = jnp.einsum('blhd,bshd->bhls', q, k) * decay
    y = jnp.einsum('bhls,bshe->blhe', scores, v)
    y = y + jnp.einsum('blhd,bhde->blhe', q, state) * jnp.exp(cum)[..., None]
    total = cum[:, -1]
    w = jnp.exp(total[:, None, :] - cum)
    new_state = state * jnp.exp(total)[..., None, None] + jnp.einsum('blhd,blhe->bhde', k * w[..., None], v)
    return new_state, y


def decay_scan(state, q, k, v, log_a):
    state = state.astype(F32)
    q, k, v, log_a = q.astype(F32), k.astype(F32), v.astype(F32), log_a.astype(F32)
    B, L = q.shape[0], q.shape[1]
    if L <= CHUNK:
        return decay_chunk(state, q, k, v, log_a)
    n = L // CHUNK

    def to_chunks(t):
        return jnp.moveaxis(t.reshape((B, n, CHUNK) + t.shape[2:]), 1, 0)

    state, ys = lax.scan(lambda s, inp: decay_chunk(s, *inp), state,
                         (to_chunks(q), to_chunks(k), to_chunks(v), to_chunks(log_a)))
    return state, jnp.moveaxis(ys, 0, 1).reshape((B, L) + ys.shape[3:])


def fox_attend(q, k, v, c_q, c_k, q_pos):
    scale = q.shape[-1] ** -0.5
    logits = jnp.einsum('bqhd,bkhd->bhqk', q.astype(F32), k) * scale
    logits = logits + c_q[..., :, None] - c_k[..., None, :]
    k_pos = jnp.arange(k.shape[1])
    logits = jnp.where(k_pos[None, :] <= q_pos[:, None], logits, -jnp.inf)
    probs = jax.nn.softmax(logits, axis=-1)
    return jnp.einsum('bhqk,bkhd->bqhd', probs, v)


def fox_prompt(q, k, v, logf):
    B, S, H, d = q.shape
    nb = S // Q_BLOCK
    c = jnp.transpose(jnp.cumsum(logf, axis=1), (0, 2, 1))
    kf, vf = k.astype(F32), v.astype(F32)
    q_blocks = jnp.moveaxis(q.reshape(B, nb, Q_BLOCK, H, d), 1, 0)
    c_blocks = jnp.moveaxis(c.reshape(B, H, nb, Q_BLOCK), 2, 0)

    def one_block(args):
        i, q_i, c_i = args
        return fox_attend(q_i, kf, vf, c_i, c, i * Q_BLOCK + jnp.arange(Q_BLOCK))

    out = lax.map(one_block, (jnp.arange(nb), q_blocks, c_blocks))
    return jnp.moveaxis(out, 0, 1).reshape(B, S, H * d)


def fox_sample(q, k, v, logf, past_k, past_v, past_logf):
    B, L, H, d = q.shape
    P = past_k.shape[1]
    k_all = jnp.concatenate([past_k.astype(F32), k.astype(F32)], axis=1)
    v_all = jnp.concatenate([past_v.astype(F32), v.astype(F32)], axis=1)
    c = jnp.cumsum(jnp.concatenate([past_logf.astype(F32), logf], axis=1), axis=1)
    c = jnp.transpose(c, (0, 2, 1))
    out = fox_attend(q, k_all, v_all, c[:, :, P:], c, P + jnp.arange(L))
    return out.reshape(B, L, H * d)


def causal_conv(xbc, prev, w, b):
    L = xbc.shape[1]
    xp = jnp.concatenate([prev.astype(xbc.dtype), xbc], axis=1)
    out = xp[:, 0:L] * w[0]
    for j in range(1, C_CONV):
        out = out + xp[:, j:j + L] * w[j]
    return jax.nn.silu(out + b), xp[:, xp.shape[1] - (C_CONV - 1):]


def token_mixers(xn, pos, fox_past, ret_state, ssm_state, conv_state,
                 w_in, b_forget, g_ret, conv_w, conv_b, dt_bias, a_log, d_skip, g_ssm, w_out):
    B, L, _ = xn.shape
    (qa, ka, va, fa, qb, kb, vb, gb, zc, xc, bc, cc, dtc) = split_last(xn @ w_in, IN_SPLITS)

    def heads(t, h):
        return t.reshape(B, L, h, t.shape[-1] // h)

    qa, ka, va = heads(qa, A_HEADS), heads(ka, A_HEADS), heads(va, A_HEADS)
    logf = jax.nn.log_sigmoid((fa + b_forget).astype(F32))
    if fox_past is None:
        ya = fox_prompt(qa, ka, va, logf)
    else:
        ya = fox_sample(qa, ka, va, logf, fox_past[0], fox_past[1], fox_past[2])

    qr = rotary(heads(qb, B_HEADS), pos)
    kr = rotary(heads(kb, B_HEADS), pos) * (B_HEAD_DIM ** -0.5)
    log_gamma = jnp.log1p(-jnp.exp2(-5.0 - jnp.arange(B_HEADS, dtype=F32)))
    ret_new, yb = decay_scan(ret_state, qr, kr, heads(vb, B_HEADS),
                             jnp.broadcast_to(log_gamma, (B, L, B_HEADS)))
    yb = jax.nn.silu(gb.astype(F32)) * head_layernorm(yb, g_ret)

    xbc, conv_new = causal_conv(jnp.concatenate([xc, bc, cc], axis=-1), conv_state, conv_w, conv_b)
    xs, bm, cm = split_last(xbc, (C_INNER, C_GROUPS * C_STATE, C_GROUPS * C_STATE))
    xs = heads(xs, C_HEADS).astype(F32)
    rep = C_HEADS // C_GROUPS
    bm = jnp.repeat(heads(bm, C_GROUPS), rep, axis=2)
    cm = jnp.repeat(heads(cm, C_GROUPS), rep, axis=2)
    dt = jax.nn.softplus(dtc.astype(F32) + dt_bias.astype(F32))
    a = -jnp.exp(a_log.astype(F32))
    ssm_new, yc = decay_scan(ssm_state, cm, bm, xs * dt[..., None], dt * a)
    yc = (yc + xs * d_skip.astype(F32)[:, None]).reshape(B, L, C_INNER)
    yc = rmsnorm(yc * jax.nn.silu(zc.astype(F32)), g_ssm)

    mixed = jnp.concatenate([ya, yb, yc], axis=-1).astype(xn.dtype)
    return mixed @ w_out, (ka, va, logf, ret_new, ssm_new, conv_new)


def peer_ffn(xn, wq, k1, k2, u_tab, v_tab):
    B, L, D = xn.shape
    n = B * L
    nb = -(-n // PEER_TOKEN_BLOCK)
    xt = jnp.pad(xn.reshape(n, D), ((0, nb * PEER_TOKEN_BLOCK - n), (0, 0)))

    def one_block(xb):
        q = (xb @ wq).astype(F32).reshape(PEER_TOKEN_BLOCK, PEER_HEADS, 2, PEER_HALF)
        s1 = jnp.einsum('thd,hnd->thn', q[:, :, 0], k1.astype(F32))
        s2 = jnp.einsum('thd,hnd->thn', q[:, :, 1], k2.astype(F32))
        t1, i1 = lax.top_k(s1, PEER_TOPK)
        t2, i2 = lax.top_k(s2, PEER_TOPK)
        cand_s = (t1[..., :, None] + t2[..., None, :]).reshape(PEER_TOKEN_BLOCK, PEER_HEADS, PEER_TOPK * PEER_TOPK)
        cand_i = (i1[..., :, None] * PEER_NKEYS + i2[..., None, :]).reshape(PEER_TOKEN_BLOCK, PEER_HEADS, PEER_TOPK * PEER_TOPK)
        best_s, best_j = lax.top_k(cand_s, PEER_TOPK)
        expert = jnp.take_along_axis(cand_i, best_j, axis=-1)
        gate = jax.nn.softmax(best_s, axis=-1)
        act = jax.nn.gelu(jnp.einsum('thkd,td->thk', u_tab[expert], xb).astype(F32))
        return jnp.einsum('thk,thkd->td', (gate * act).astype(xb.dtype), v_tab[expert])

    out = lax.map(one_block, xt.reshape(nb, PEER_TOKEN_BLOCK, D))
    return out.reshape(nb * PEER_TOKEN_BLOCK, D)[:n].reshape(B, L, D)


def run_trunk(x, p, past, weights):
    (g_mix, w_in, b_forget, g_ret, conv_w, conv_b, dt_bias, a_log, d_skip, g_ssm, w_out,
     g_ffn, peer_wq, peer_k1, peer_k2, peer_u, peer_v, g_ple, w_ple_gate, w_ple, g_final) = weights
    B, L, _ = x.shape
    offset = 0 if past is None else past[0].shape[2]
    pos = offset + jnp.arange(L)
    h = x
    per_layer = []
    for i in range(DEPTH):
        if past is None:
            fox_past = None
            rs = jnp.zeros((B, B_HEADS, B_HEAD_DIM, B_HEAD_DIM), F32)
            ss = jnp.zeros((B, C_HEADS, C_STATE, C_HEAD_DIM), F32)
            cs = jnp.zeros((B, C_CONV - 1, C_CONV_DIM), x.dtype)
        else:
            fox_past = (past[0][i], past[1][i], past[2][i])
            rs, ss, cs = past[3][i], past[4][i], past[5][i]
        y, st = token_mixers(rmsnorm(h, g_mix[i]), pos, fox_past, rs, ss, cs,
                             w_in[i], b_forget[i], g_ret[i], conv_w[i], conv_b[i], dt_bias[i],
                             a_log[i], d_skip[i], g_ssm[i], w_out[i])
        h = h + y
        h = h + peer_ffn(rmsnorm(h, g_ffn[i]), peer_wq[i], peer_k1[i], peer_k2[i], peer_u[i], peer_v[i])
        gate = jax.nn.sigmoid(rmsnorm(h, g_ple[i]) @ w_ple_gate[i])
        h = h + (p[i] @ w_ple[i]) * gate
        per_layer.append(st)
    k_new = jnp.stack([s[0] for s in per_layer])
    v_new = jnp.stack([s[1] for s in per_layer])
    lf_new = jnp.stack([s[2] for s in per_layer])
    ret_new = jnp.stack([s[3] for s in per_layer])
    ssm_new = jnp.stack([s[4] for s in per_layer])
    conv_new = jnp.stack([s[5] for s in per_layer])
    return rmsnorm(h, g_final), (k_new, v_new, lf_new, ret_new, ssm_new, conv_new)


def setup_inputs(seed: int = 0) -> dict:
    key = jax.random.key(seed)
    ks = jax.random.split(key, 40)
    nrm = jax.random.normal
    dt0 = jnp.exp(jax.random.uniform(ks[20], (DEPTH, C_HEADS)) * (math.log(0.1) - math.log(0.001)) + math.log(0.001))
    return {
        'x_prompt': nrm(ks[0], (BATCH, SEQ, D_MODEL), F32),
        'x_sample': nrm(ks[1], (DEC_BATCH, DEC_SEQ, D_MODEL), F32),
        'cache_k_fox': nrm(ks[2], (DEPTH, DEC_BATCH, PAST_LEN, A_HEADS, A_HEAD_DIM), F32),
        'cache_v_fox': nrm(ks[3], (DEPTH, DEC_BATCH, PAST_LEN, A_HEADS, A_HEAD_DIM), F32),
        'cache_logf_fox': jax.nn.log_sigmoid(3.0 + nrm(ks[4], (DEPTH, DEC_BATCH, PAST_LEN, A_HEADS), F32)),
        'state_ret': nrm(ks[5], (DEPTH, DEC_BATCH, B_HEADS, B_HEAD_DIM, B_HEAD_DIM), F32),
        'state_ssm': 0.5 * nrm(ks[6], (DEPTH, DEC_BATCH, C_HEADS, C_STATE, C_HEAD_DIM), F32),
        'state_conv': nrm(ks[7], (DEPTH, DEC_BATCH, C_CONV - 1, C_CONV_DIM), F32),
        'p_prompt': nrm(ks[8], (DEPTH, BATCH, SEQ, PLE_DIM), F32),
        'p_sample': nrm(ks[9], (DEPTH, DEC_BATCH, DEC_SEQ, PLE_DIM), F32),
        'g_mix': 1.0 + 0.1 * nrm(ks[10], (DEPTH, D_MODEL), F32),
        'w_in': nrm(ks[11], (DEPTH, D_MODEL, IN_WIDTH), F32) * D_MODEL ** -0.5,
        'b_forget': 3.0 + 0.1 * nrm(ks[12], (DEPTH, A_HEADS), F32),
        'g_ret': 1.0 + 0.1 * nrm(ks[13], (DEPTH, B_HEADS * B_HEAD_DIM), F32),
        'conv_w': nrm(ks[14], (DEPTH, C_CONV, C_CONV_DIM), F32) * C_CONV ** -0.5,
        'conv_b': 0.01 * nrm(ks[15], (DEPTH, C_CONV_DIM), F32),
        'dt_bias': dt0 + jnp.log(-jnp.expm1(-dt0)),
        'a_log': jnp.log(jax.random.uniform(ks[16], (DEPTH, C_HEADS), F32, 1.0, 16.0)),
        'd_skip': 1.0 + 0.1 * nrm(ks[17], (DEPTH, C_HEADS), F32),
        'g_ssm': 1.0 + 0.1 * nrm(ks[18], (DEPTH, C_INNER), F32),
        'w_out': nrm(ks[19], (DEPTH, MIX_WIDTH, D_MODEL), F32) * MIX_WIDTH ** -0.5,
        'g_ffn': 1.0 + 0.1 * nrm(ks[21], (DEPTH, D_MODEL), F32),
        'peer_wq': nrm(ks[22], (DEPTH, D_MODEL, PEER_HEADS * PEER_KEY_DIM), F32) * D_MODEL ** -0.5,
        'peer_k1': nrm(ks[23], (DEPTH, PEER_HEADS, PEER_NKEYS, PEER_HALF), F32) * PEER_HALF ** -0.5,
        'peer_k2': nrm(ks[24], (DEPTH, PEER_HEADS, PEER_NKEYS, PEER_HALF), F32) * PEER_HALF ** -0.5,
        'peer_u': nrm(ks[25], (DEPTH, PEER_EXPERTS, D_MODEL), F32) * D_MODEL ** -0.5,
        'peer_v': nrm(ks[26], (DEPTH, PEER_EXPERTS, D_MODEL), F32) * PEER_HEADS ** -0.5,
        'g_ple': 1.0 + 0.1 * nrm(ks[27], (DEPTH, D_MODEL), F32),
        'w_ple_gate': nrm(ks[28], (DEPTH, D_MODEL, D_MODEL), F32) * D_MODEL ** -0.5,
        'w_ple': nrm(ks[29], (DEPTH, PLE_DIM, D_MODEL), F32) * PLE_DIM ** -0.5,
        'g_final': 1.0 + 0.1 * nrm(ks[30], (D_MODEL,), F32),
    }


def reference(x_prompt, x_sample, cache_k_fox, cache_v_fox, cache_logf_fox, state_ret, state_ssm, state_conv,
              p_prompt, p_sample, g_mix, w_in, b_forget, g_ret, conv_w, conv_b, dt_bias, a_log, d_skip, g_ssm,
              w_out, g_ffn, peer_wq, peer_k1, peer_k2, peer_u, peer_v, g_ple, w_ple_gate, w_ple, g_final):
    weights = (g_mix, w_in, b_forget, g_ret, conv_w, conv_b, dt_bias, a_log, d_skip, g_ssm, w_out,
               g_ffn, peer_wq, peer_k1, peer_k2, peer_u, peer_v, g_ple, w_ple_gate, w_ple, g_final)
    y_prompt, (k_p, v_p, lf_p, ret_p, ssm_p, conv_p) = run_trunk(x_prompt, p_prompt, None, weights)
    past = (cache_k_fox, cache_v_fox, cache_logf_fox, state_ret, state_ssm, state_conv)
    y_sample, (k_s, v_s, lf_s, ret_s, ssm_s, conv_s) = run_trunk(x_sample, p_sample, past, weights)
    return (y_prompt, y_sample, k_p, v_p, lf_p, ret_p, ssm_p, conv_p, k_s, v_s, lf_s, ret_s, ssm_s, conv_s)
```

```python
import functools
import math

import jax
import jax.numpy as jnp
from jax import lax
from jax.experimental import pallas as pl
from jax.experimental.pallas import tpu as pltpu

F32 = jnp.float32
BF16 = jnp.bfloat16

D_MODEL = 1024
PLE_DIM = 256
EPS = 1e-6
A_HEADS, A_HEAD_DIM = 4, 64
B_HEADS, B_HEAD_DIM = 4, 64
ROPE_BASE = 10000.0
C_HEADS, C_HEAD_DIM = 8, 64
C_INNER = C_HEADS * C_HEAD_DIM
C_GROUPS, C_STATE, C_CONV = 2, 128, 4
C_CONV_DIM = C_INNER + 2 * C_GROUPS * C_STATE
PEER_HEADS, PEER_NKEYS, PEER_KEY_DIM, PEER_TOPK = 8, 128, 256, 16
PEER_HALF = PEER_KEY_DIM // 2
PEER_EXPERTS = PEER_NKEYS * PEER_NKEYS

LANES = 128
A_W = A_HEADS * A_HEAD_DIM
B_W = B_HEADS * B_HEAD_DIM
G_W = C_GROUPS * C_STATE

COL_QA = 0
COL_KA = COL_QA + A_W
COL_VA = COL_KA + A_W
COL_QB = COL_VA + A_W
COL_KB = COL_QB + B_W
COL_VB = COL_KB + B_W
COL_GB = COL_VB + B_W
COL_FA = COL_GB + B_W
COL_ZC = COL_FA + 2 * LANES
COL_DT = COL_ZC + C_INNER
COL_XBC = COL_DT + C_INNER
PROJ_W = COL_XBC + C_CONV_DIM
assert all(c % A_W == 0 for c in (COL_QA, COL_KA, COL_VA, COL_QB, COL_KB, COL_VB, COL_GB))
assert COL_ZC % C_INNER == 0 and COL_DT % C_INNER == 0 and COL_XBC % C_CONV_DIM == 0 and COL_FA % LANES == 0

NEG_BIG = -1e30
VMEM_LIMIT_BYTES = 52 * 1024 * 1024


def _cparams(*sem):
    return pltpu.CompilerParams(dimension_semantics=sem, vmem_limit_bytes=VMEM_LIMIT_BYTES)


def _tile(n, pref):
    t = min(n, pref)
    while n % t:
        t -= 8
    assert t > 0
    return t


def _rms(x, g):
    return x * lax.rsqrt(jnp.mean(x * x, axis=-1, keepdims=True) + EPS) * g


def _split3(x):
    hi = x.astype(BF16)
    r1 = x - hi.astype(F32)
    mid = r1.astype(BF16)
    lo = (r1 - mid.astype(F32)).astype(BF16)
    return hi, mid, lo


def _dot_sel(sel_bf16, x):
    hi, mid, lo = _split3(x)
    d = lambda p: jnp.dot(sel_bf16, p, preferred_element_type=F32)
    return d(hi) + d(mid) + d(lo)


def _dot_sel_rhs(x, sel_bf16):
    hi, mid, lo = _split3(x)
    d = lambda p: jnp.dot(p, sel_bf16, preferred_element_type=F32)
    return d(hi) + d(mid) + d(lo)


def _inproj_kernel(x_ref, g_ref, w_ref, bf_ref, o_ref, lf_ref):
    xn = _rms(x_ref[...], g_ref[...])
    y = jnp.dot(xn.astype(BF16), w_ref[...], preferred_element_type=F32)
    o_ref[...] = y
    fa = y[:, COL_FA:COL_FA + LANES] + bf_ref[...]
    lane = lax.broadcasted_iota(jnp.int32, fa.shape, 1)
    lf_ref[...] = jnp.where(lane < A_HEADS, jax.nn.log_sigmoid(fa), 0.0)


def _inproj(x2d, g, w, bf_pad):
    T = x2d.shape[0]
    tm = _tile(T, 256)
    return pl.pallas_call(
        _inproj_kernel,
        grid=(T // tm,),
        in_specs=[pl.BlockSpec((tm, D_MODEL), lambda i: (i, 0)),
                  pl.BlockSpec((1, D_MODEL), lambda i: (0, 0)),
                  pl.BlockSpec((D_MODEL, PROJ_W), lambda i: (0, 0)),
                  pl.BlockSpec((1, LANES), lambda i: (0, 0))],
        out_specs=[pl.BlockSpec((tm, PROJ_W), lambda i: (i, 0)),
                   pl.BlockSpec((tm, LANES), lambda i: (i, 0))],
        out_shape=[jax.ShapeDtypeStruct((T, PROJ_W), F32),
                   jax.ShapeDtypeStruct((T, LANES), F32)],
        compiler_params=_cparams("parallel"),
        name="inproj",
    )(x2d, g.reshape(1, D_MODEL), w, bf_pad)


def _prep_w_in(w_in, b_forget, dt_bias):
    sizes = (A_W, A_W, A_W, A_HEADS, B_W, B_W, B_W, B_W, C_INNER, C_INNER, G_W, G_W, C_HEADS)
    pts = []
    acc = 0
    for s in sizes[:-1]:
        acc += s
        pts.append(acc)
    qa, ka, va, fa, qb, kb, vb, gb, zc, xc, bc, cc, dtc = jnp.split(w_in, pts, axis=-1)

    def rot_perm(w):
        w = w.reshape(D_MODEL, B_HEADS, 2, B_HEAD_DIM // 2)
        return jnp.transpose(w, (0, 2, 1, 3)).reshape(D_MODEL, B_W)

    z = lambda n: jnp.zeros((D_MODEL, n), w_in.dtype)
    w = jnp.concatenate([qa, ka, va, rot_perm(qb), rot_perm(kb), vb, gb, fa, z(2 * LANES - A_HEADS), zc,
                         jnp.repeat(dtc, C_HEAD_DIM, axis=1), xc, bc, cc], axis=1)
    assert w.shape[1] == PROJ_W
    bf_pad = jnp.concatenate([b_forget, jnp.zeros((LANES - A_HEADS,), F32)]).reshape(1, LANES)
    dtb_full = jnp.repeat(dt_bias, C_HEAD_DIM).reshape(1, C_INNER)
    return w.astype(BF16), bf_pad, dtb_full


def _cumsum_kernel(lf_ref, ccol_ref, crow_ref, carry_scr):
    @pl.when(pl.program_id(1) == 0)
    def _():
        carry_scr[...] = jnp.zeros_like(carry_scr)

    lf = lf_ref[0]
    tc = lf.shape[0]
    r = lax.broadcasted_iota(jnp.int32, (tc, tc), 0)
    c = lax.broadcasted_iota(jnp.int32, (tc, tc), 1)
    tri = jnp.where(r >= c, 1.0, 0.0).astype(BF16)
    cum = _dot_sel(tri, lf) + carry_scr[0:1, :]
    ccol_ref[0] = cum
    crow_ref[0] = cum.T[0:8, :]
    carry_scr[...] = jnp.broadcast_to(cum[tc - 1:tc, :], carry_scr.shape)


def _cumsum(lf):
    B, L, _ = lf.shape
    tc = _tile(L, 512)
    assert tc % LANES == 0
    return pl.pallas_call(
        _cumsum_kernel,
        grid=(B, L // tc),
        in_specs=[pl.BlockSpec((1, tc, LANES), lambda b, j: (b, j, 0))],
        out_specs=[pl.BlockSpec((1, tc, LANES), lambda b, j: (b, j, 0)),
                   pl.BlockSpec((1, 8, tc), lambda b, j: (b, 0, j))],
        out_shape=[jax.ShapeDtypeStruct((B, L, LANES), F32),
                   jax.ShapeDtypeStruct((B, 8, L), F32)],
        scratch_shapes=[pltpu.VMEM((8, LANES), F32)],
        compiler_params=_cparams("parallel", "arbitrary"),
        name="fox_cumsum",
    )(lf)


def _fox_kernel(q_ref, k_ref, v_ref, ccol_ref, crow_ref, o_ref, qm_scr, m_scr, l_scr, acc_scr, *,
                tq, tk, offset, nk):
    i = pl.program_id(1)
    j = pl.program_id(2)
    scale = A_HEAD_DIM ** -0.5
    lane_q = lax.broadcasted_iota(jnp.int32, (tq, A_W), 1)

    @pl.when(j == 0)
    def _():
        q = q_ref[0]
        for h in range(A_HEADS):
            in_head = (lane_q >= h * A_HEAD_DIM) & (lane_q < (h + 1) * A_HEAD_DIM)
            qm_scr[h] = jnp.where(in_head, q, 0.0).astype(BF16)
        m_scr[...] = jnp.full(m_scr.shape, NEG_BIG, F32)
        l_scr[...] = jnp.zeros(l_scr.shape, F32)
        acc_scr[...] = jnp.zeros(acc_scr.shape, F32)

    q_lo = offset + i * tq
    k_lo = j * tk

    def step(masked):
        k = k_ref[0].astype(BF16)
        v = v_ref[0].astype(BF16)
        ccol = ccol_ref[0]
        crow = crow_ref[0]
        if masked:
            qpos = q_lo + lax.broadcasted_iota(jnp.int32, (tq, tk), 0)
            kpos = k_lo + lax.broadcasted_iota(jnp.int32, (tq, tk), 1)
            vis = kpos <= qpos
        for h in range(A_HEADS):
            s = lax.dot_general(qm_scr[h], k, (((1,), (1,)), ((), ())), preferred_element_type=F32)
            s = s * scale + (ccol[:, h:h + 1] - crow[h:h + 1, :])
            if masked:
                s = jnp.where(vis, s, NEG_BIG)
            m_prev = m_scr[h]
            m_next = jnp.maximum(m_prev, jnp.max(s, axis=1, keepdims=True))
            alpha = jnp.exp(m_prev - m_next)
            p = jnp.exp(s - m_next[:, 0:1])
            l_scr[h] = alpha * l_scr[h] + jnp.sum(p, axis=1, keepdims=True)
            m_scr[h] = m_next
            pv = jnp.dot(p.astype(BF16), v, preferred_element_type=F32)
            acc_scr[h] = acc_scr[h] * alpha[:, 0:1] + pv

    visible = k_lo <= q_lo + tq - 1
    needs_mask = k_lo + tk - 1 > q_lo

    @pl.when(visible & needs_mask)
    def _():
        step(True)

    @pl.when(visible & jnp.logical_not(needs_mask))
    def _():
        step(False)

    @pl.when(j == nk - 1)
    def _():
        out = jnp.zeros((tq, A_W), F32)
        for h in range(A_HEADS):
            in_head = (lane_q >= h * A_HEAD_DIM) & (lane_q < (h + 1) * A_HEAD_DIM)
            out = out + jnp.where(in_head, acc_scr[h] / l_scr[h][:, 0:1], 0.0)
        o_ref[0] = out.astype(o_ref.dtype)


def _fox(q, k, v, ccol_q, crow_k, offset):
    B, Lq, _ = q.shape
    Lk = k.shape[1]
    tq = _tile(Lq, 256)
    tk = _tile(Lk, 512)
    assert tk % LANES == 0
    nq, nk = Lq // tq, Lk // tk

    def kv_map(b, i, j):
        last = (offset + i * tq + tq - 1) // tk
        return (b, jnp.minimum(j, last), 0)

    def crow_map(b, i, j):
        last = (offset + i * tq + tq - 1) // tk
        return (b, 0, jnp.minimum(j, last))

    kern = functools.partial(_fox_kernel, tq=tq, tk=tk, offset=offset, nk=nk)
    return pl.pallas_call(
        kern,
        grid=(B, nq, nk),
        in_specs=[pl.BlockSpec((1, tq, A_W), lambda b, i, j: (b, i, 0)),
                  pl.BlockSpec((1, tk, A_W), kv_map),
                  pl.BlockSpec((1, tk, A_W), kv_map),
                  pl.BlockSpec((1, tq, LANES), lambda b, i, j: (b, i, 0)),
                  pl.BlockSpec((1, 8, tk), crow_map)],
        out_specs=pl.BlockSpec((1, tq, A_W), lambda b, i, j: (b, i, 0)),
        out_shape=jax.ShapeDtypeStruct((B, Lq, A_W), BF16),
        scratch_shapes=[pltpu.VMEM((A_HEADS, tq, A_W), BF16),
                        pltpu.VMEM((A_HEADS, tq, LANES), F32),
                        pltpu.VMEM((A_HEADS, tq, LANES), F32),
                        pltpu.VMEM((A_HEADS, tq, A_W), F32)],
        compiler_params=_cparams("parallel", "parallel", "arbitrary"),
        name="fox_attention",
    )(q, k, v, ccol_q, crow_k)


def _ret_lane_head_v():
    return lax.broadcasted_iota(jnp.int32, (1, B_W), 1) // B_HEAD_DIM


def _ret_lane_head_qk():
    return (lax.broadcasted_iota(jnp.int32, (1, B_W), 1) % LANES) // (B_HEAD_DIM // 2)


def _log_gamma(h):
    return math.log1p(-2.0 ** (-5.0 - h))


def _ret_kernel(q_ref, k_ref, v_ref, g_ref, s0_ref, invf_ref, gret_ref, y_ref, sout_ref,
                state_scr, dec_scr, *, lc, offset, nchunks):
    c = pl.program_id(1)

    @pl.when(c == 0)
    def _():
        state_scr[...] = s0_ref[0]
        r = lax.broadcasted_iota(jnp.int32, (lc, lc), 0)
        s = lax.broadcasted_iota(jnp.int32, (lc, lc), 1)
        d = (r - s).astype(F32)
        for h in range(B_HEADS):
            dec_scr[h] = jnp.where(r >= s, jnp.exp(d * _log_gamma(h)), 0.0)

    hv = _ret_lane_head_v()
    hqk = _ret_lane_head_qk()
    lg_v = jnp.zeros((1, B_W), F32)
    lg_qk = jnp.zeros((1, B_W), F32)
    for h in range(B_HEADS):
        lg_v = jnp.where(hv == h, _log_gamma(h), lg_v)
        lg_qk = jnp.where(hqk == h, _log_gamma(h), lg_qk)

    pos = (offset + c * lc + lax.broadcasted_iota(jnp.int32, (lc, LANES), 0)).astype(F32)
    ang = pos * invf_ref[...]
    cos, sin = jnp.cos(ang), jnp.sin(ang)

    def rot(x):
        x1, x2 = x[:, :LANES], x[:, LANES:]
        return jnp.concatenate([x1 * cos - x2 * sin, x1 * sin + x2 * cos], axis=1)

    qr = rot(q_ref[0])
    kr = rot(k_ref[0]) * (B_HEAD_DIM ** -0.5)
    v = v_ref[0]
    vb = v.astype(BF16)
    krb = kr.astype(BF16)

    lpos = lax.broadcasted_iota(jnp.int32, (lc, B_W), 0).astype(F32)
    state = state_scr[...]
    y = jnp.dot(qr.astype(BF16), state.astype(BF16), preferred_element_type=F32) * jnp.exp((lpos + 1.0) * lg_v)
    for h in range(B_HEADS):
        qh = jnp.where(hqk == h, qr, 0.0).astype(BF16)
        sc = lax.dot_general(qh, krb, (((1,), (1,)), ((), ())), preferred_element_type=F32) * dec_scr[h]
        yh = jnp.dot(sc.astype(BF16), vb, preferred_element_type=F32)
        y = y + jnp.where(hv == h, yh, 0.0)

    kw = (kr * jnp.exp((lc - 1.0 - lpos) * lg_qk)).astype(BF16)
    upd = lax.dot_general(kw, vb, (((0,), (0,)), ((), ())), preferred_element_type=F32)
    row_head = (lax.broadcasted_iota(jnp.int32, (B_W, B_W), 0) % LANES) // (B_HEAD_DIM // 2)
    col_head = lax.broadcasted_iota(jnp.int32, (B_W, B_W), 1) // B_HEAD_DIM
    lg_rows = jnp.zeros((B_W, B_W), F32)
    for h in range(B_HEADS):
        lg_rows = jnp.where(row_head == h, _log_gamma(h), lg_rows)
    new_state = jnp.where(row_head == col_head, state * jnp.exp(lc * lg_rows) + upd, 0.0)
    state_scr[...] = new_state

    @pl.when(c == nchunks - 1)
    def _():
        sout_ref[0] = new_state

    ri = lax.broadcasted_iota(jnp.int32, (B_W, B_W), 0) // B_HEAD_DIM
    ci = lax.broadcasted_iota(jnp.int32, (B_W, B_W), 1) // B_HEAD_DIM
    avg = jnp.where(ri == ci, 1.0 / B_HEAD_DIM, 0.0).astype(BF16)
    yc = y - _dot_sel_rhs(y, avg)
    var = _dot_sel_rhs(yc * yc, avg)
    yn = yc * lax.rsqrt(var + EPS) * gret_ref[...]
    y_ref[0] = (jax.nn.silu(g_ref[0]) * yn).astype(y_ref.dtype)


def _retention(proj3, s0, invf, g_ret, offset):
    B, L, _ = proj3.shape
    lc = _tile(L, 256)
    nch = L // lc
    blk = lambda col: pl.BlockSpec((1, lc, B_W), lambda b, c: (b, c, col // B_W))
    kern = functools.partial(_ret_kernel, lc=lc, offset=offset, nchunks=nch)
    return pl.pallas_call(
        kern,
        grid=(B, nch),
        in_specs=[blk(COL_QB), blk(COL_KB), blk(COL_VB), blk(COL_GB),
                  pl.BlockSpec((1, B_W, B_W), lambda b, c: (b, 0, 0)),
                  pl.BlockSpec((1, LANES), lambda b, c: (0, 0)),
                  pl.BlockSpec((1, B_W), lambda b, c: (0, 0))],
        out_specs=[pl.BlockSpec((1, lc, B_W), lambda b, c: (b, c, 0)),
                   pl.BlockSpec((1, B_W, B_W), lambda b, c: (b, 0, 0))],
        out_shape=[jax.ShapeDtypeStruct((B, L, B_W), BF16),
                   jax.ShapeDtypeStruct((B, B_W, B_W), F32)],
        scratch_shapes=[pltpu.VMEM((B_W, B_W), F32),
                        pltpu.VMEM((B_HEADS, lc, lc), F32)],
        compiler_params=_cparams("parallel", "arbitrary"),
        name="retention",
    )(proj3, proj3, proj3, proj3, s0, invf, g_ret.reshape(1, B_W))


def _ret_state_to_kernel(s):
    B = s.shape[0]
    half = B_HEAD_DIM // 2
    s = s.reshape(B, B_HEADS, 2, half, B_HEAD_DIM)
    eye = jnp.eye(B_HEADS, dtype=s.dtype)
    full = jnp.einsum('bhkie,hg->bkhige', s, eye)
    return full.reshape(B, B_W, B_W)


def _ret_state_from_kernel(s):
    B = s.shape[0]
    half = B_HEAD_DIM // 2
    s = s.reshape(B, 2, B_HEADS, half, B_HEADS, B_HEAD_DIM)
    d = jnp.einsum('bkhihe->bhkie', s)
    return d.reshape(B, B_HEADS, B_HEAD_DIM, B_HEAD_DIM)


def _ssd_kernel(z_ref, xbc_ref, dt_ref, s0_ref, cs0_ref, cw_ref, cb_ref, dtb_ref, alog_ref, dsk_ref, gs_ref,
                y_ref, sout_ref, state_scr, xp_scr, *, lc, nchunks):
    c = pl.program_id(1)
    PADR = 8

    @pl.when(c == 0)
    def _():
        state_scr[...] = s0_ref[0]
        xp_scr[0:PADR, :] = cs0_ref[0]

    xbc = xbc_ref[0]
    xp_scr[PADR:PADR + lc, :] = xbc
    cw = cw_ref[...]
    conv = xp_scr[PADR - 3:PADR - 3 + lc, :] * cw[0:1, :]
    for jj in range(1, C_CONV):
        conv = conv + xp_scr[PADR - 3 + jj:PADR - 3 + jj + lc, :] * cw[jj:jj + 1, :]
    act = jax.nn.silu(conv + cb_ref[...])
    xp_scr[0:PADR, :] = xp_scr[lc:lc + PADR, :]

    xs = act[:, :C_INNER]
    bm = act[:, C_INNER:C_INNER + G_W]
    cm = act[:, C_INNER + G_W:]
    dt = jax.nn.softplus(dt_ref[0] + dtb_ref[...])
    loga = dt * (-jnp.exp(alog_ref[...]))

    r = lax.broadcasted_iota(jnp.int32, (lc, lc), 0)
    s = lax.broadcasted_iota(jnp.int32, (lc, lc), 1)
    causal = r >= s
    tri = jnp.where(causal, 1.0, 0.0).astype(BF16)
    cum = _dot_sel(tri, loga)
    total = cum[lc - 1:lc, :]
    xdt = xs * dt
    xdtb = xdt.astype(BF16)
    bmb = bm.astype(BF16)
    cmb = cm.astype(BF16)
    state = state_scr[...]

    gw = C_INNER // C_GROUPS
    cum_t = [cum[:, kk * LANES:(kk + 1) * LANES].T for kk in range(C_INNER // LANES)]
    ys = []
    for g in range(C_GROUPS):
        cg = cmb[:, g * C_STATE:(g + 1) * C_STATE]
        bg = bmb[:, g * C_STATE:(g + 1) * C_STATE]
        scores = lax.dot_general(cg, bg, (((1,), (1,)), ((), ())), preferred_element_type=F32)
        y_g = jnp.dot(cg, state[:, g * gw:(g + 1) * gw].astype(BF16), preferred_element_type=F32)
        y_g = y_g * jnp.exp(cum[:, g * gw:(g + 1) * gw])
        lane_head = lax.broadcasted_iota(jnp.int32, (1, gw), 1) // C_HEAD_DIM
        xg = xdtb[:, g * gw:(g + 1) * gw]
        for hh in range(C_HEADS // C_GROUPS):
            col = g * gw + hh * C_HEAD_DIM
            cum_col = cum[:, col:col + 1]
            cum_row = cum_t[col // LANES][col % LANES:col % LANES + 1, :]
            dec = jnp.exp(jnp.where(causal, cum_col - cum_row, NEG_BIG))
            yh = jnp.dot((scores * dec).astype(BF16), xg, preferred_element_type=F32)
            y_g = y_g + jnp.where(lane_head == hh, yh, 0.0)
        ys.append(y_g)
        xw = (xdt[:, g * gw:(g + 1) * gw] * jnp.exp(total[:, g * gw:(g + 1) * gw] - cum[:, g * gw:(g + 1) * gw]))
        upd = lax.dot_general(bg, xw.astype(BF16), (((0,), (0,)), ((), ())), preferred_element_type=F32)
        state_scr[:, g * gw:(g + 1) * gw] = state[:, g * gw:(g + 1) * gw] * jnp.exp(total[:, g * gw:(g + 1) * gw]) + upd
    y = jnp.concatenate(ys, axis=1)

    @pl.when(c == nchunks - 1)
    def _():
        sout_ref[0] = state_scr[...]

    yc = (y + xs * dsk_ref[...]) * jax.nn.silu(z_ref[0])
    y_ref[0] = _rms(yc, gs_ref[...]).astype(y_ref.dtype)


def _ssd(proj3, s0, cs0, conv_w, conv_b, dtb_full, a_log, d_skip, g_ssm):
    B, L, _ = proj3.shape
    lc = _tile(L, 256)
    nch = L // lc
    rep = lambda p: jnp.repeat(p, C_HEAD_DIM).reshape(1, C_INNER)
    kern = functools.partial(_ssd_kernel, lc=lc, nchunks=nch)
    cst = lambda shape: pl.BlockSpec(shape, lambda b, c: (0,) * len(shape))
    return pl.pallas_call(
        kern,
        grid=(B, nch),
        in_specs=[pl.BlockSpec((1, lc, C_INNER), lambda b, c: (b, c, COL_ZC // C_INNER)),
                  pl.BlockSpec((1, lc, C_CONV_DIM), lambda b, c: (b, c, COL_XBC // C_CONV_DIM)),
                  pl.BlockSpec((1, lc, C_INNER), lambda b, c: (b, c, COL_DT // C_INNER)),
                  pl.BlockSpec((1, C_STATE, C_INNER), lambda b, c: (b, 0, 0)),
                  pl.BlockSpec((1, 8, C_CONV_DIM), lambda b, c: (b, 0, 0)),
                  cst((C_CONV, C_CONV_DIM)), cst((1, C_CONV_DIM)), cst((1, C_INNER)), cst((1, C_INNER)),
                  cst((1, C_INNER)), cst((1, C_INNER))],
        out_specs=[pl.BlockSpec((1, lc, C_INNER), lambda b, c: (b, c, 0)),
                   pl.BlockSpec((1, C_STATE, C_INNER), lambda b, c: (b, 0, 0))],
        out_shape=[jax.ShapeDtypeStruct((B, L, C_INNER), BF16),
                   jax.ShapeDtypeStruct((B, C_STATE, C_INNER), F32)],
        scratch_shapes=[pltpu.VMEM((C_STATE, C_INNER), F32),
                        pltpu.VMEM((lc + 8, C_CONV_DIM), F32)],
        compiler_params=_cparams("parallel", "arbitrary"),
        name="ssd",
    )(proj3, proj3, proj3, s0, cs0, conv_w, conv_b.reshape(1, C_CONV_DIM), dtb_full, rep(a_log), rep(d_skip),
      g_ssm.reshape(1, C_INNER))


def _outproj_kernel(h_ref, ya_ref, yb_ref, yc_ref, w_ref, o_ref):
    acc = jnp.dot(ya_ref[...], w_ref[0:A_W, :], preferred_element_type=F32)
    acc = acc + jnp.dot(yb_ref[...], w_ref[A_W:A_W + B_W, :], preferred_element_type=F32)
    acc = acc + jnp.dot(yc_ref[...], w_ref[A_W + B_W:, :], preferred_element_type=F32)
    o_ref[...] = h_ref[...] + acc


def _outproj(h2d, ya, yb, yc, w_out_bf16):
    T = h2d.shape[0]
    tm = _tile(T, 512)
    row = lambda w: pl.BlockSpec((tm, w), lambda i: (i, 0))
    return pl.pallas_call(
        _outproj_kernel,
        grid=(T // tm,),
        in_specs=[row(D_MODEL), row(A_W), row(B_W), row(C_INNER),
                  pl.BlockSpec((D_MODEL, D_MODEL), lambda i: (0, 0))],
        out_specs=row(D_MODEL),
        out_shape=jax.ShapeDtypeStruct((T, D_MODEL), F32),
        compiler_params=_cparams("parallel"),
        name="outproj",
    )(h2d, ya, yb, yc, w_out_bf16)


PEER_CAND_COLS = 4


def _top16(x, want_rank):
    vals = []
    rank = jnp.full(x.shape, float(PEER_TOPK), F32) if want_rank else None
    for r in range(PEER_TOPK):
        m = jnp.max(x, axis=0, keepdims=True)
        vals.append(m)
        hit = x == m
        if want_rank:
            rank = jnp.where(hit, float(r), rank)
        x = jnp.where(hit, NEG_BIG, x)
    return jnp.concatenate(vals, axis=0), rank


def _peer_select(s1, s2):
    n = s1.shape[1]
    a16, _ = _top16(s1, False)
    b16, rank2 = _top16(s2, True)
    ridx = lax.broadcasted_iota(jnp.int32, (PEER_TOPK, n), 0)
    cands = []
    for r2 in range(PEER_CAND_COLS):
        cands.append(jnp.where((ridx + 1) * (r2 + 1) <= PEER_TOPK, a16 + b16[r2:r2 + 1, :], NEG_BIG))
    for r1 in range(PEER_TOPK // (PEER_CAND_COLS + 1)):
        ok = (ridx >= PEER_CAND_COLS) & ((ridx + 1) * (r1 + 1) <= PEER_TOPK)
        cands.append(jnp.where(ok, b16 + a16[r1:r1 + 1, :], NEG_BIG))
    cand = jnp.concatenate(cands, axis=0)
    x = cand
    tau = None
    for r in range(PEER_TOPK):
        tau = jnp.max(x, axis=0, keepdims=True)
        x = jnp.where(x == tau, NEG_BIG, x)
    top = a16[0:1, :] + b16[0:1, :]
    z = jnp.sum(jnp.where(cand >= tau, jnp.exp(cand - top), 0.0), axis=0, keepdims=True)
    cnt = jnp.zeros(s1.shape, F32)
    for r2 in range(PEER_TOPK):
        cnt = cnt + jnp.where(s1 + b16[r2:r2 + 1, :] >= tau, 1.0, 0.0)
    a = jnp.exp(s1 - a16[0:1, :]) / z
    b = jnp.exp(s2 - b16[0:1, :])
    return a, cnt, rank2, b


def _peer_kernel(h_ref, g_ref, wq_ref, k1_ref, k2_ref, u_ref, vt_ref, o_ref,
                 xt_scr, qt_scr, a_scr, cnt_scr, rank_scr, b_scr, acc_scr, act_scr, *, tm, te, ne):
    j = pl.program_id(1)
    n1 = te // PEER_NKEYS
    nchunk = tm // LANES

    @pl.when(j == 0)
    def _():
        xn = _rms(h_ref[...], g_ref[...])
        xt_scr[...] = xn.T.astype(BF16)
        qt_scr[...] = jnp.dot(wq_ref[...], xt_scr[...], preferred_element_type=F32)
        acc_scr[...] = jnp.zeros(acc_scr.shape, F32)

        def head_body(h, carry):
            base = pl.multiple_of(h * PEER_KEY_DIM, PEER_KEY_DIM)
            k1 = k1_ref[h].astype(BF16)
            k2 = k2_ref[h].astype(BF16)
            for cch in range(nchunk):
                sl = slice(cch * LANES, (cch + 1) * LANES)
                q1 = qt_scr[pl.ds(base, PEER_HALF), sl].astype(BF16)
                q2 = qt_scr[pl.ds(base + PEER_HALF, PEER_HALF), sl].astype(BF16)
                s1 = jnp.dot(k1, q1, preferred_element_type=F32)
                s2 = jnp.dot(k2, q2, preferred_element_type=F32)
                a, cnt, rank2, b = _peer_select(s1, s2)
                a_scr[h, :, sl] = a
                cnt_scr[h, :, sl] = cnt
                rank_scr[h, :, sl] = rank2
                b_scr[h, :, sl] = b
            return carry

        lax.fori_loop(0, PEER_HEADS, head_body, 0)

    ht = jnp.dot(u_ref[...], xt_scr[...], preferred_element_type=F32)
    for il in range(n1):
        i1 = j * n1 + il
        w = jnp.zeros((PEER_NKEYS, tm), F32)
        for h in range(PEER_HEADS):
            cnt_row = cnt_scr[h, pl.ds(i1, 1), :]
            a_row = a_scr[h, pl.ds(i1, 1), :]
            w = w + jnp.where(rank_scr[h] < cnt_row, b_scr[h], 0.0) * a_row
        hs = ht[il * PEER_NKEYS:(il + 1) * PEER_NKEYS, :]
        act_scr[il * PEER_NKEYS:(il + 1) * PEER_NKEYS, :] = (jax.nn.gelu(hs) * w).astype(BF16)
    acc_scr[...] += jnp.dot(vt_ref[...], act_scr[...], preferred_element_type=F32)

    @pl.when(j == ne - 1)
    def _():
        o_ref[...] = h_ref[...] + acc_scr[...].T


def _peer(h2d, g, wq_t, k1, k2, u_bf16, vt_bf16):
    T = h2d.shape[0]
    tm = _tile(T, 512)
    assert tm % LANES == 0
    te = 512
    ne = PEER_EXPERTS // te
    kern = functools.partial(_peer_kernel, tm=tm, te=te, ne=ne)
    hk = (PEER_HEADS, PEER_NKEYS, tm)
    return pl.pallas_call(
        kern,
        grid=(T // tm, ne),
        in_specs=[pl.BlockSpec((tm, D_MODEL), lambda i, j: (i, 0)),
                  pl.BlockSpec((1, D_MODEL), lambda i, j: (0, 0)),
                  pl.BlockSpec((PEER_HEADS * PEER_KEY_DIM, D_MODEL), lambda i, j: (0, 0)),
                  pl.BlockSpec((PEER_HEADS, PEER_NKEYS, PEER_HALF), lambda i, j: (0, 0, 0)),
                  pl.BlockSpec((PEER_HEADS, PEER_NKEYS, PEER_HALF), lambda i, j: (0, 0, 0)),
                  pl.BlockSpec((te, D_MODEL), lambda i, j: (j, 0)),
                  pl.BlockSpec((D_MODEL, te), lambda i, j: (0, j))],
        out_specs=pl.BlockSpec((tm, D_MODEL), lambda i, j: (i, 0)),
        out_shape=jax.ShapeDtypeStruct((T, D_MODEL), F32),
        scratch_shapes=[pltpu.VMEM((D_MODEL, tm), BF16),
                        pltpu.VMEM((PEER_HEADS * PEER_KEY_DIM, tm), F32),
                        pltpu.VMEM(hk, F32), pltpu.VMEM(hk, F32), pltpu.VMEM(hk, F32), pltpu.VMEM(hk, F32),
                        pltpu.VMEM((D_MODEL, tm), F32),
                        pltpu.VMEM((te, tm), BF16)],
        compiler_params=_cparams("parallel", "arbitrary"),
        name="peer",
    )(h2d, g.reshape(1, D_MODEL), wq_t, k1, k2, u_bf16, vt_bf16)


def _ple_kernel(h_ref, p_ref, g_ref, wg_ref, wp_ref, gf_ref, o_ref, *, final_norm):
    h = h_ref[...]
    xn = _rms(h, g_ref[...])
    gate = jax.nn.sigmoid(jnp.dot(xn.astype(BF16), wg_ref[...], preferred_element_type=F32))
    emb = jnp.dot(p_ref[...].astype(BF16), wp_ref[...], preferred_element_type=F32)
    out = h + emb * gate
    if final_norm:
        out = _rms(out, gf_ref[...])
    o_ref[...] = out


def _ple(h2d, p2d, g, wg_bf16, wp_bf16, g_final, final_norm):
    T = h2d.shape[0]
    tm = _tile(T, 512)
    kern = functools.partial(_ple_kernel, final_norm=final_norm)
    return pl.pallas_call(
        kern,
        grid=(T // tm,),
        in_specs=[pl.BlockSpec((tm, D_MODEL), lambda i: (i, 0)),
                  pl.BlockSpec((tm, PLE_DIM), lambda i: (i, 0)),
                  pl.BlockSpec((1, D_MODEL), lambda i: (0, 0)),
                  pl.BlockSpec((D_MODEL, D_MODEL), lambda i: (0, 0)),
                  pl.BlockSpec((PLE_DIM, D_MODEL), lambda i: (0, 0)),
                  pl.BlockSpec((1, D_MODEL), lambda i: (0, 0))],
        out_specs=pl.BlockSpec((tm, D_MODEL), lambda i: (i, 0)),
        out_shape=jax.ShapeDtypeStruct((T, D_MODEL), F32),
        compiler_params=_cparams("parallel"),
        name="ple",
    )(h2d, p2d, g.reshape(1, D_MODEL), wg_bf16, wp_bf16, g_final.reshape(1, D_MODEL))


def _pad_to(x, n, axis):
    pad = n - x.shape[axis]
    if pad == 0:
        return x
    widths = [(0, 0)] * x.ndim
    widths[axis] = (0, pad)
    return jnp.pad(x, widths)


def _run_trunk(x, p, past, weights):
    (g_mix, w_in, b_forget, g_ret, conv_w, conv_b, dt_bias, a_log, d_skip, g_ssm, w_out,
     g_ffn, peer_wq, peer_k1, peer_k2, peer_u, peer_v, g_ple, w_ple_gate, w_ple, g_final) = weights
    B, L, _ = x.shape
    depth = w_in.shape[0]
    T = B * L
    P = 0 if past is None else past[0].shape[2]
    half = B_HEAD_DIM // 2
    invf = jnp.tile(ROPE_BASE ** (-jnp.arange(half, dtype=F32) / half), LANES // half).reshape(1, LANES)

    h = x.reshape(T, D_MODEL)
    outs = []
    for i in range(depth):
        w_proj, bf_pad, dtb_full = _prep_w_in(w_in[i], b_forget[i], dt_bias[i])
        proj, lf = _inproj(h, g_mix[i], w_proj, bf_pad)
        proj3 = proj.reshape(B, L, PROJ_W)
        lf3 = lf.reshape(B, L, LANES)
        k_new = proj3[:, :, COL_KA:COL_KA + A_W]
        v_new = proj3[:, :, COL_VA:COL_VA + A_W]
        q_a = proj3[:, :, COL_QA:COL_QA + A_W]

        if past is None:
            ccol, crow = _cumsum(lf3)
            ya = _fox(q_a, k_new, v_new, ccol, crow, 0)
            rs0 = jnp.zeros((B, B_W, B_W), F32)
            ss0 = jnp.zeros((B, C_STATE, C_INNER), F32)
            cs0 = jnp.zeros((B, 8, C_CONV_DIM), F32)
        else:
            lk = P + L
            lk_pad = -(-lk // LANES) * LANES
            past_k = past[0][i].reshape(B, P, A_W)
            past_v = past[1][i].reshape(B, P, A_W)
            past_lf = _pad_to(past[2][i], LANES, 2)
            k_all = _pad_to(jnp.concatenate([past_k, k_new], axis=1), lk_pad, 1)
            v_all = _pad_to(jnp.concatenate([past_v, v_new], axis=1), lk_pad, 1)
            lf_all = _pad_to(jnp.concatenate([past_lf, lf3], axis=1), lk_pad, 1)
            ccol, crow = _cumsum(lf_all)
            ya = _fox(q_a, k_all, v_all, ccol[:, P:P + L], crow, P)
            rs0 = _ret_state_to_kernel(past[3][i])
            ss0 = jnp.transpose(past[4][i], (0, 2, 1, 3)).reshape(B, C_STATE, C_INNER)
            cs0 = jnp.concatenate([jnp.zeros((B, 8 - (C_CONV - 1), C_CONV_DIM), F32), past[5][i]], axis=1)

        yb, ret_k = _retention(proj3, rs0, invf, g_ret[i], P)
        ret_new = _ret_state_from_kernel(ret_k)

        yc, ssm_k = _ssd(proj3, ss0, cs0, conv_w[i], conv_b[i], dtb_full, a_log[i], d_skip[i], g_ssm[i])
        ssm_new = jnp.transpose(ssm_k.reshape(B, C_STATE, C_HEADS, C_HEAD_DIM), (0, 2, 1, 3))
        xbc = proj3[:, :, COL_XBC:COL_XBC + C_CONV_DIM]
        if L >= C_CONV - 1:
            conv_new = xbc[:, L - (C_CONV - 1):]
        else:
            prev = jnp.zeros((B, C_CONV - 1, C_CONV_DIM), F32) if past is None else past[5][i]
            conv_new = jnp.concatenate([prev, xbc], axis=1)[:, -(C_CONV - 1):]

        h = _outproj(h, ya.reshape(T, A_W), yb.reshape(T, B_W), yc.reshape(T, C_INNER), w_out[i].astype(BF16))
        h = _peer(h, g_ffn[i], peer_wq[i].T.astype(BF16), peer_k1[i], peer_k2[i],
                  peer_u[i].astype(BF16), peer_v[i].T.astype(BF16))
        h = _ple(h, p[i].reshape(T, PLE_DIM), g_ple[i], w_ple_gate[i].astype(BF16), w_ple[i].astype(BF16),
                 g_final, final_norm=(i == depth - 1))
        outs.append((k_new.reshape(B, L, A_HEADS, A_HEAD_DIM), v_new.reshape(B, L, A_HEADS, A_HEAD_DIM),
                     lf3[:, :, :A_HEADS], ret_new, ssm_new, conv_new))
    stacked = tuple(jnp.stack([o[n] for o in outs]) for n in range(6))
    return h.reshape(B, L, D_MODEL), stacked


def kernel(x_prompt, x_sample, cache_k_fox, cache_v_fox, cache_logf_fox, state_ret, state_ssm, state_conv, p_prompt, p_sample, g_mix, w_in, b_forget, g_ret, conv_w, conv_b, dt_bias, a_log, d_skip, g_ssm, w_out, g_ffn, peer_wq, peer_k1, peer_k2, peer_u, peer_v, g_ple, w_ple_gate, w_ple, g_final):
    weights = (g_mix, w_in, b_forget, g_ret, conv_w, conv_b, dt_bias, a_log, d_skip, g_ssm, w_out,
               g_ffn, peer_wq, peer_k1, peer_k2, peer_u, peer_v, g_ple, w_ple_gate, w_ple, g_final)
    y_prompt, (k_p, v_p, lf_p, ret_p, ssm_p, conv_p) = _run_trunk(x_prompt, p_prompt, None, weights)
    past = (cache_k_fox, cache_v_fox, cache_logf_fox, state_ret, state_ssm, state_conv)
    y_sample, (k_s, v_s, lf_s, ret_s, ssm_s, conv_s) = _run_trunk(x_sample, p_sample, past, weights)
    return (y_prompt, y_sample, k_p, v_p, lf_p, ret_p, ssm_p, conv_p, k_s, v_s, lf_s, ret_s, ssm_s, conv_s)
```

```python
import functools
import math

import jax
import jax.numpy as jnp
from jax import lax
from jax.experimental import pallas as pl
from jax.experimental.pallas import tpu as pltpu

F32 = jnp.float32
BF16 = jnp.bfloat16

D_MODEL = 1024
PLE_DIM = 256
EPS = 1e-6
A_HEADS, A_HEAD_DIM = 4, 64
B_HEADS, B_HEAD_DIM = 4, 64
ROPE_BASE = 10000.0
C_HEADS, C_HEAD_DIM = 8, 64
C_INNER = C_HEADS * C_HEAD_DIM
C_GROUPS, C_STATE, C_CONV = 2, 128, 4
C_CONV_DIM = C_INNER + 2 * C_GROUPS * C_STATE
PEER_HEADS, PEER_NKEYS, PEER_KEY_DIM, PEER_TOPK = 8, 128, 256, 16
PEER_HALF = PEER_KEY_DIM // 2
PEER_EXPERTS = PEER_NKEYS * PEER_NKEYS

LANES = 128
BF16_ROWS = 16
A_W = A_HEADS * A_HEAD_DIM
B_W = B_HEADS * B_HEAD_DIM
G_W = C_GROUPS * C_STATE

COL_QA = 0
COL_KA = COL_QA + A_W
COL_VA = COL_KA + A_W
COL_QB = COL_VA + A_W
COL_KB = COL_QB + B_W
COL_VB = COL_KB + B_W
COL_GB = COL_VB + B_W
COL_FA = COL_GB + B_W
COL_ZC = COL_FA + 2 * LANES
COL_DT = COL_ZC + C_INNER
COL_XBC = COL_DT + C_INNER
PROJ_W = COL_XBC + C_CONV_DIM
assert all(c % A_W == 0 for c in (COL_QA, COL_KA, COL_VA, COL_QB, COL_KB, COL_VB, COL_GB))
assert COL_ZC % C_INNER == 0 and COL_DT % C_INNER == 0 and COL_XBC % C_CONV_DIM == 0 and COL_FA % LANES == 0

NEG_BIG = -1e30
VMEM_LIMIT_BYTES = 52 * 1024 * 1024


def _cparams(*sem):
    return pltpu.CompilerParams(dimension_semantics=sem, vmem_limit_bytes=VMEM_LIMIT_BYTES)


def _tile(n, pref):
    t = min(n, pref)
    while n % t:
        t -= 8
    assert t > 0
    return t


def _rms(x, g):
    return x * lax.rsqrt(jnp.mean(x * x, axis=-1, keepdims=True) + EPS) * g


def _split3(x):
    hi = x.astype(BF16)
    r1 = x - hi.astype(F32)
    mid = r1.astype(BF16)
    lo = (r1 - mid.astype(F32)).astype(BF16)
    return hi, mid, lo


def _dot_sel(sel_bf16, x):
    hi, mid, lo = _split3(x)
    d = lambda p: jnp.dot(sel_bf16, p, preferred_element_type=F32)
    return d(hi) + d(mid) + d(lo)


def _dot_sel_rhs(x, sel_bf16):
    hi, mid, lo = _split3(x)
    d = lambda p: jnp.dot(p, sel_bf16, preferred_element_type=F32)
    return d(hi) + d(mid) + d(lo)


def _inproj_kernel(x_ref, g_ref, w_ref, bf_ref, o_ref, lf_ref):
    xn = _rms(x_ref[...], g_ref[...])
    y = jnp.dot(xn.astype(BF16), w_ref[...], preferred_element_type=F32)
    o_ref[...] = y
    fa = y[:, COL_FA:COL_FA + LANES] + bf_ref[...]
    lane = lax.broadcasted_iota(jnp.int32, fa.shape, 1)
    lf_ref[...] = jnp.where(lane < A_HEADS, jax.nn.log_sigmoid(fa), 0.0)


def _inproj(x2d, g, w, bf_pad):
    T = x2d.shape[0]
    tm = _tile(T, 256)
    return pl.pallas_call(
        _inproj_kernel,
        grid=(T // tm,),
        in_specs=[pl.BlockSpec((tm, D_MODEL), lambda i: (i, 0)),
                  pl.BlockSpec((1, D_MODEL), lambda i: (0, 0)),
                  pl.BlockSpec((D_MODEL, PROJ_W), lambda i: (0, 0)),
                  pl.BlockSpec((1, LANES), lambda i: (0, 0))],
        out_specs=[pl.BlockSpec((tm, PROJ_W), lambda i: (i, 0)),
                   pl.BlockSpec((tm, LANES), lambda i: (i, 0))],
        out_shape=[jax.ShapeDtypeStruct((T, PROJ_W), F32),
                   jax.ShapeDtypeStruct((T, LANES), F32)],
        compiler_params=_cparams("parallel"),
        name="inproj",
    )(x2d, g.reshape(1, D_MODEL), w, bf_pad)


def _prep_w_in(w_in, b_forget, dt_bias):
    sizes = (A_W, A_W, A_W, A_HEADS, B_W, B_W, B_W, B_W, C_INNER, C_INNER, G_W, G_W, C_HEADS)
    pts = []
    acc = 0
    for s in sizes[:-1]:
        acc += s
        pts.append(acc)
    qa, ka, va, fa, qb, kb, vb, gb, zc, xc, bc, cc, dtc = jnp.split(w_in, pts, axis=-1)

    def rot_perm(w):
        w = w.reshape(D_MODEL, B_HEADS, 2, B_HEAD_DIM // 2)
        return jnp.transpose(w, (0, 2, 1, 3)).reshape(D_MODEL, B_W)

    z = lambda n: jnp.zeros((D_MODEL, n), w_in.dtype)
    w = jnp.concatenate([qa, ka, va, rot_perm(qb), rot_perm(kb), vb, gb, fa, z(2 * LANES - A_HEADS), zc,
                         jnp.repeat(dtc, C_HEAD_DIM, axis=1), xc, bc, cc], axis=1)
    assert w.shape[1] == PROJ_W
    bf_pad = jnp.concatenate([b_forget, jnp.zeros((LANES - A_HEADS,), F32)]).reshape(1, LANES)
    dtb_full = jnp.repeat(dt_bias, C_HEAD_DIM).reshape(1, C_INNER)
    return w.astype(BF16), bf_pad, dtb_full


def _cumsum_kernel(lf_ref, ccol_ref, crow_ref, carry_scr):
    @pl.when(pl.program_id(1) == 0)
    def _():
        carry_scr[...] = jnp.zeros_like(carry_scr)

    lf = lf_ref[0]
    tc = lf.shape[0]
    r = lax.broadcasted_iota(jnp.int32, (tc, tc), 0)
    c = lax.broadcasted_iota(jnp.int32, (tc, tc), 1)
    tri = jnp.where(r >= c, 1.0, 0.0).astype(BF16)
    cum = _dot_sel(tri, lf) + carry_scr[0:1, :]
    ccol_ref[0] = cum
    crow_ref[0] = cum.T[0:8, :]
    carry_scr[...] = jnp.broadcast_to(cum[tc - 1:tc, :], carry_scr.shape)


def _cumsum(lf):
    B, L, _ = lf.shape
    tc = _tile(L, 512)
    assert tc % LANES == 0
    return pl.pallas_call(
        _cumsum_kernel,
        grid=(B, L // tc),
        in_specs=[pl.BlockSpec((1, tc, LANES), lambda b, j: (b, j, 0))],
        out_specs=[pl.BlockSpec((1, tc, LANES), lambda b, j: (b, j, 0)),
                   pl.BlockSpec((1, 8, tc), lambda b, j: (b, 0, j))],
        out_shape=[jax.ShapeDtypeStruct((B, L, LANES), F32),
                   jax.ShapeDtypeStruct((B, 8, L), F32)],
        scratch_shapes=[pltpu.VMEM((8, LANES), F32)],
        compiler_params=_cparams("parallel", "arbitrary"),
        name="fox_cumsum",
    )(lf)


def _fox_kernel(q_ref, k_ref, v_ref, ccol_ref, crow_ref, o_ref, qm_scr, m_scr, l_scr, acc_scr, *,
                tq, tk, offset, nk):
    i = pl.program_id(1)
    j = pl.program_id(2)
    scale = A_HEAD_DIM ** -0.5
    lane_q = lax.broadcasted_iota(jnp.int32, (tq, A_W), 1)

    @pl.when(j == 0)
    def _():
        q = q_ref[0]
        for h in range(A_HEADS):
            in_head = (lane_q >= h * A_HEAD_DIM) & (lane_q < (h + 1) * A_HEAD_DIM)
            qm_scr[h] = jnp.where(in_head, q, 0.0).astype(BF16)
        m_scr[...] = jnp.full(m_scr.shape, NEG_BIG, F32)
        l_scr[...] = jnp.zeros(l_scr.shape, F32)
        acc_scr[...] = jnp.zeros(acc_scr.shape, F32)

    q_lo = offset + i * tq
    k_lo = j * tk

    def step(masked):
        k = k_ref[0].astype(BF16)
        v = v_ref[0].astype(BF16)
        ccol = ccol_ref[0]
        crow = crow_ref[0]
        if masked:
            qpos = q_lo + lax.broadcasted_iota(jnp.int32, (tq, tk), 0)
            kpos = k_lo + lax.broadcasted_iota(jnp.int32, (tq, tk), 1)
            vis = kpos <= qpos
        for h in range(A_HEADS):
            s = lax.dot_general(qm_scr[h], k, (((1,), (1,)), ((), ())), preferred_element_type=F32)
            s = s * scale + (ccol[:, h:h + 1] - crow[h:h + 1, :])
            if masked:
                s = jnp.where(vis, s, NEG_BIG)
            m_prev = m_scr[h]
            m_next = jnp.maximum(m_prev, jnp.max(s, axis=1, keepdims=True))
            alpha = jnp.exp(m_prev - m_next)
            p = jnp.exp(s - m_next[:, 0:1])
            l_scr[h] = alpha * l_scr[h] + jnp.sum(p, axis=1, keepdims=True)
            m_scr[h] = m_next
            pv = jnp.dot(p.astype(BF16), v, preferred_element_type=F32)
            acc_scr[h] = acc_scr[h] * alpha[:, 0:1] + pv

    visible = k_lo <= q_lo + tq - 1
    needs_mask = k_lo + tk - 1 > q_lo

    @pl.when(visible & needs_mask)
    def _():
        step(True)

    @pl.when(visible & jnp.logical_not(needs_mask))
    def _():
        step(False)

    @pl.when(j == nk - 1)
    def _():
        out = jnp.zeros((tq, A_W), F32)
        for h in range(A_HEADS):
            in_head = (lane_q >= h * A_HEAD_DIM) & (lane_q < (h + 1) * A_HEAD_DIM)
            out = out + jnp.where(in_head, acc_scr[h] / l_scr[h][:, 0:1], 0.0)
        o_ref[0] = out.astype(o_ref.dtype)


def _fox(q, k, v, ccol_q, crow_k, offset):
    B, Lq, _ = q.shape
    Lk = k.shape[1]
    tq = _tile(Lq, 256)
    tk = _tile(Lk, 512)
    assert tk % LANES == 0
    nq, nk = Lq // tq, Lk // tk

    def kv_map(b, i, j):
        last = (offset + i * tq + tq - 1) // tk
        return (b, jnp.minimum(j, last), 0)

    def crow_map(b, i, j):
        last = (offset + i * tq + tq - 1) // tk
        return (b, 0, jnp.minimum(j, last))

    kern = functools.partial(_fox_kernel, tq=tq, tk=tk, offset=offset, nk=nk)
    return pl.pallas_call(
        kern,
        grid=(B, nq, nk),
        in_specs=[pl.BlockSpec((1, tq, A_W), lambda b, i, j: (b, i, 0)),
                  pl.BlockSpec((1, tk, A_W), kv_map),
                  pl.BlockSpec((1, tk, A_W), kv_map),
                  pl.BlockSpec((1, tq, LANES), lambda b, i, j: (b, i, 0)),
                  pl.BlockSpec((1, 8, tk), crow_map)],
        out_specs=pl.BlockSpec((1, tq, A_W), lambda b, i, j: (b, i, 0)),
        out_shape=jax.ShapeDtypeStruct((B, Lq, A_W), BF16),
        scratch_shapes=[pltpu.VMEM((A_HEADS, tq, A_W), BF16),
                        pltpu.VMEM((A_HEADS, tq, LANES), F32),
                        pltpu.VMEM((A_HEADS, tq, LANES), F32),
                        pltpu.VMEM((A_HEADS, tq, A_W), F32)],
        compiler_params=_cparams("parallel", "parallel", "arbitrary"),
        name="fox_attention",
    )(q, k, v, ccol_q, crow_k)


def _ret_lane_head_v():
    return lax.broadcasted_iota(jnp.int32, (1, B_W), 1) // B_HEAD_DIM


def _ret_lane_head_qk():
    return (lax.broadcasted_iota(jnp.int32, (1, B_W), 1) % LANES) // (B_HEAD_DIM // 2)


def _log_gamma(h):
    return math.log1p(-2.0 ** (-5.0 - h))


def _ret_kernel(q_ref, k_ref, v_ref, g_ref, s0_ref, invf_ref, gret_ref, y_ref, sout_ref,
                state_scr, dec_scr, *, lc, offset, nchunks):
    c = pl.program_id(1)

    @pl.when(c == 0)
    def _():
        state_scr[...] = s0_ref[0]
        r = lax.broadcasted_iota(jnp.int32, (lc, lc), 0)
        s = lax.broadcasted_iota(jnp.int32, (lc, lc), 1)
        d = (r - s).astype(F32)
        for h in range(B_HEADS):
            dec_scr[h] = jnp.where(r >= s, jnp.exp(d * _log_gamma(h)), 0.0)

    hv = _ret_lane_head_v()
    hqk = _ret_lane_head_qk()
    lg_v = jnp.zeros((1, B_W), F32)
    lg_qk = jnp.zeros((1, B_W), F32)
    for h in range(B_HEADS):
        lg_v = jnp.where(hv == h, _log_gamma(h), lg_v)
        lg_qk = jnp.where(hqk == h, _log_gamma(h), lg_qk)

    pos = (offset + c * lc + lax.broadcasted_iota(jnp.int32, (lc, LANES), 0)).astype(F32)
    ang = pos * invf_ref[...]
    cos, sin = jnp.cos(ang), jnp.sin(ang)

    def rot(x):
        x1, x2 = x[:, :LANES], x[:, LANES:]
        return jnp.concatenate([x1 * cos - x2 * sin, x1 * sin + x2 * cos], axis=1)

    qr = rot(q_ref[0])
    kr = rot(k_ref[0]) * (B_HEAD_DIM ** -0.5)
    v = v_ref[0]
    vb = v.astype(BF16)
    krb = kr.astype(BF16)

    lpos = lax.broadcasted_iota(jnp.int32, (lc, B_W), 0).astype(F32)
    state = state_scr[...]
    y = jnp.dot(qr.astype(BF16), state.astype(BF16), preferred_element_type=F32) * jnp.exp((lpos + 1.0) * lg_v)
    for h in range(B_HEADS):
        qh = jnp.where(hqk == h, qr, 0.0).astype(BF16)
        sc = lax.dot_general(qh, krb, (((1,), (1,)), ((), ())), preferred_element_type=F32) * dec_scr[h]
        yh = jnp.dot(sc.astype(BF16), vb, preferred_element_type=F32)
        y = y + jnp.where(hv == h, yh, 0.0)

    kw = (kr * jnp.exp((lc - 1.0 - lpos) * lg_qk)).astype(BF16)
    upd = lax.dot_general(kw, vb, (((0,), (0,)), ((), ())), preferred_element_type=F32)
    row_head = (lax.broadcasted_iota(jnp.int32, (B_W, B_W), 0) % LANES) // (B_HEAD_DIM // 2)
    col_head = lax.broadcasted_iota(jnp.int32, (B_W, B_W), 1) // B_HEAD_DIM
    lg_rows = jnp.zeros((B_W, B_W), F32)
    for h in range(B_HEADS):
        lg_rows = jnp.where(row_head == h, _log_gamma(h), lg_rows)
    new_state = jnp.where(row_head == col_head, state * jnp.exp(lc * lg_rows) + upd, 0.0)
    state_scr[...] = new_state

    @pl.when(c == nchunks - 1)
    def _():
        sout_ref[0] = new_state

    ri = lax.broadcasted_iota(jnp.int32, (B_W, B_W), 0) // B_HEAD_DIM
    ci = lax.broadcasted_iota(jnp.int32, (B_W, B_W), 1) // B_HEAD_DIM
    avg = jnp.where(ri == ci, 1.0 / B_HEAD_DIM, 0.0).astype(BF16)
    yc = y - _dot_sel_rhs(y, avg)
    var = _dot_sel_rhs(yc * yc, avg)
    yn = yc * lax.rsqrt(var + EPS) * gret_ref[...]
    y_ref[0] = (jax.nn.silu(g_ref[0]) * yn).astype(y_ref.dtype)


def _retention(proj3, s0, invf, g_ret, offset):
    B, L, _ = proj3.shape
    lc = _tile(L, 256)
    nch = L // lc
    blk = lambda col: pl.BlockSpec((1, lc, B_W), lambda b, c: (b, c, col // B_W))
    kern = functools.partial(_ret_kernel, lc=lc, offset=offset, nchunks=nch)
    return pl.pallas_call(
        kern,
        grid=(B, nch),
        in_specs=[blk(COL_QB), blk(COL_KB), blk(COL_VB), blk(COL_GB),
                  pl.BlockSpec((1, B_W, B_W), lambda b, c: (b, 0, 0)),
                  pl.BlockSpec((1, LANES), lambda b, c: (0, 0)),
                  pl.BlockSpec((1, B_W), lambda b, c: (0, 0))],
        out_specs=[pl.BlockSpec((1, lc, B_W), lambda b, c: (b, c, 0)),
                   pl.BlockSpec((1, B_W, B_W), lambda b, c: (b, 0, 0))],
        out_shape=[jax.ShapeDtypeStruct((B, L, B_W), BF16),
                   jax.ShapeDtypeStruct((B, B_W, B_W), F32)],
        scratch_shapes=[pltpu.VMEM((B_W, B_W), F32),
                        pltpu.VMEM((B_HEADS, lc, lc), F32)],
        compiler_params=_cparams("parallel", "arbitrary"),
        name="retention",
    )(proj3, proj3, proj3, proj3, s0, invf, g_ret.reshape(1, B_W))


def _ret_state_to_kernel(s):
    B = s.shape[0]
    half = B_HEAD_DIM // 2
    s = s.reshape(B, B_HEADS, 2, half, B_HEAD_DIM)
    eye = jnp.eye(B_HEADS, dtype=s.dtype)
    full = jnp.einsum('bhkie,hg->bkhige', s, eye)
    return full.reshape(B, B_W, B_W)


def _ret_state_from_kernel(s):
    B = s.shape[0]
    half = B_HEAD_DIM // 2
    s = s.reshape(B, 2, B_HEADS, half, B_HEADS, B_HEAD_DIM)
    d = jnp.einsum('bkhihe->bhkie', s)
    return d.reshape(B, B_HEADS, B_HEAD_DIM, B_HEAD_DIM)


def _ssd_kernel(z_ref, xbc_ref, dt_ref, s0_ref, cs0_ref, cw_ref, cb_ref, dtb_ref, alog_ref, dsk_ref, gs_ref,
                y_ref, sout_ref, state_scr, xp_scr, *, lc, nchunks):
    c = pl.program_id(1)
    PADR = 8

    @pl.when(c == 0)
    def _():
        state_scr[...] = s0_ref[0]
        xp_scr[0:PADR, :] = cs0_ref[0]

    xbc = xbc_ref[0]
    xp_scr[PADR:PADR + lc, :] = xbc
    cw = cw_ref[...]
    conv = xp_scr[PADR - 3:PADR - 3 + lc, :] * cw[0:1, :]
    for jj in range(1, C_CONV):
        conv = conv + xp_scr[PADR - 3 + jj:PADR - 3 + jj + lc, :] * cw[jj:jj + 1, :]
    act = jax.nn.silu(conv + cb_ref[...])
    xp_scr[0:PADR, :] = xp_scr[lc:lc + PADR, :]

    xs = act[:, :C_INNER]
    bm = act[:, C_INNER:C_INNER + G_W]
    cm = act[:, C_INNER + G_W:]
    dt = jax.nn.softplus(dt_ref[0] + dtb_ref[...])
    loga = dt * (-jnp.exp(alog_ref[...]))

    r = lax.broadcasted_iota(jnp.int32, (lc, lc), 0)
    s = lax.broadcasted_iota(jnp.int32, (lc, lc), 1)
    causal = r >= s
    tri = jnp.where(causal, 1.0, 0.0).astype(BF16)
    cum = _dot_sel(tri, loga)
    total = cum[lc - 1:lc, :]
    xdt = xs * dt
    xdtb = xdt.astype(BF16)
    bmb = bm.astype(BF16)
    cmb = cm.astype(BF16)
    state = state_scr[...]

    gw = C_INNER // C_GROUPS
    cum_t = [cum[:, kk * LANES:(kk + 1) * LANES].T for kk in range(C_INNER // LANES)]
    ys = []
    for g in range(C_GROUPS):
        cg = cmb[:, g * C_STATE:(g + 1) * C_STATE]
        bg = bmb[:, g * C_STATE:(g + 1) * C_STATE]
        scores = lax.dot_general(cg, bg, (((1,), (1,)), ((), ())), preferred_element_type=F32)
        y_g = jnp.dot(cg, state[:, g * gw:(g + 1) * gw].astype(BF16), preferred_element_type=F32)
        y_g = y_g * jnp.exp(cum[:, g * gw:(g + 1) * gw])
        lane_head = lax.broadcasted_iota(jnp.int32, (1, gw), 1) // C_HEAD_DIM
        xg = xdtb[:, g * gw:(g + 1) * gw]
        for hh in range(C_HEADS // C_GROUPS):
            col = g * gw + hh * C_HEAD_DIM
            cum_col = cum[:, col:col + 1]
            cum_row = cum_t[col // LANES][col % LANES:col % LANES + 1, :]
            dec = jnp.exp(jnp.where(causal, cum_col - cum_row, NEG_BIG))
            yh = jnp.dot((scores * dec).astype(BF16), xg, preferred_element_type=F32)
            y_g = y_g + jnp.where(lane_head == hh, yh, 0.0)
        ys.append(y_g)
        xw = (xdt[:, g * gw:(g + 1) * gw] * jnp.exp(total[:, g * gw:(g + 1) * gw] - cum[:, g * gw:(g + 1) * gw]))
        upd = lax.dot_general(bg, xw.astype(BF16), (((0,), (0,)), ((), ())), preferred_element_type=F32)
        state_scr[:, g * gw:(g + 1) * gw] = state[:, g * gw:(g + 1) * gw] * jnp.exp(total[:, g * gw:(g + 1) * gw]) + upd
    y = jnp.concatenate(ys, axis=1)

    @pl.when(c == nchunks - 1)
    def _():
        sout_ref[0] = state_scr[...]

    yc = (y + xs * dsk_ref[...]) * jax.nn.silu(z_ref[0])
    y_ref[0] = _rms(yc, gs_ref[...]).astype(y_ref.dtype)


def _ssd(proj3, s0, cs0, conv_w, conv_b, dtb_full, a_log, d_skip, g_ssm):
    B, L, _ = proj3.shape
    lc = _tile(L, 256)
    nch = L // lc
    rep = lambda p: jnp.repeat(p, C_HEAD_DIM).reshape(1, C_INNER)
    kern = functools.partial(_ssd_kernel, lc=lc, nchunks=nch)
    cst = lambda shape: pl.BlockSpec(shape, lambda b, c: (0,) * len(shape))
    return pl.pallas_call(
        kern,
        grid=(B, nch),
        in_specs=[pl.BlockSpec((1, lc, C_INNER), lambda b, c: (b, c, COL_ZC // C_INNER)),
                  pl.BlockSpec((1, lc, C_CONV_DIM), lambda b, c: (b, c, COL_XBC // C_CONV_DIM)),
                  pl.BlockSpec((1, lc, C_INNER), lambda b, c: (b, c, COL_DT // C_INNER)),
                  pl.BlockSpec((1, C_STATE, C_INNER), lambda b, c: (b, 0, 0)),
                  pl.BlockSpec((1, 8, C_CONV_DIM), lambda b, c: (b, 0, 0)),
                  cst((C_CONV, C_CONV_DIM)), cst((1, C_CONV_DIM)), cst((1, C_INNER)), cst((1, C_INNER)),
                  cst((1, C_INNER)), cst((1, C_INNER))],
        out_specs=[pl.BlockSpec((1, lc, C_INNER), lambda b, c: (b, c, 0)),
                   pl.BlockSpec((1, C_STATE, C_INNER), lambda b, c: (b, 0, 0))],
        out_shape=[jax.ShapeDtypeStruct((B, L, C_INNER), BF16),
                   jax.ShapeDtypeStruct((B, C_STATE, C_INNER), F32)],
        scratch_shapes=[pltpu.VMEM((C_STATE, C_INNER), F32),
                        pltpu.VMEM((lc + 8, C_CONV_DIM), F32)],
        compiler_params=_cparams("parallel", "arbitrary"),
        name="ssd",
    )(proj3, proj3, proj3, s0, cs0, conv_w, conv_b.reshape(1, C_CONV_DIM), dtb_full, rep(a_log), rep(d_skip),
      g_ssm.reshape(1, C_INNER))


def _outproj_kernel(h_ref, ya_ref, yb_ref, yc_ref, w_ref, o_ref):
    acc = jnp.dot(ya_ref[...], w_ref[0:A_W, :], preferred_element_type=F32)
    acc = acc + jnp.dot(yb_ref[...], w_ref[A_W:A_W + B_W, :], preferred_element_type=F32)
    acc = acc + jnp.dot(yc_ref[...], w_ref[A_W + B_W:, :], preferred_element_type=F32)
    o_ref[...] = h_ref[...] + acc


def _outproj(h2d, ya, yb, yc, w_out_bf16):
    T = h2d.shape[0]
    tm = _tile(T, 512)
    row = lambda w: pl.BlockSpec((tm, w), lambda i: (i, 0))
    return pl.pallas_call(
        _outproj_kernel,
        grid=(T // tm,),
        in_specs=[row(D_MODEL), row(A_W), row(B_W), row(C_INNER),
                  pl.BlockSpec((D_MODEL, D_MODEL), lambda i: (0, 0))],
        out_specs=row(D_MODEL),
        out_shape=jax.ShapeDtypeStruct((T, D_MODEL), F32),
        compiler_params=_cparams("parallel"),
        name="outproj",
    )(h2d, ya, yb, yc, w_out_bf16)


PEER_SUB = 512
PEER_STEP = 2048
PEER_CAND_COLS = 4


def _gelu_tanh(x):
    k0 = math.sqrt(2.0 / math.pi)
    inner = x * (k0 + (k0 * 0.044715) * (x * x))
    return (0.5 * x) * (1.0 + jnp.tanh(inner))


def _top16(x, want_rank):
    vals = []
    rank = jnp.full(x.shape, float(PEER_TOPK), F32) if want_rank else None
    for r in range(PEER_TOPK):
        m = jnp.max(x, axis=0, keepdims=True)
        vals.append(m)
        hit = x == m
        if want_rank:
            rank = jnp.where(hit, float(r), rank)
        x = jnp.where(hit, NEG_BIG, x)
    return jnp.concatenate(vals, axis=0), rank


def _peer_select(s1, s2):
    n = s1.shape[1]
    a16, _ = _top16(s1, False)
    b16, rank2 = _top16(s2, True)
    ridx = lax.broadcasted_iota(jnp.int32, (PEER_TOPK, n), 0)
    cands = []
    for r2 in range(PEER_CAND_COLS):
        cands.append(jnp.where((ridx + 1) * (r2 + 1) <= PEER_TOPK, a16 + b16[r2:r2 + 1, :], NEG_BIG))
    for r1 in range(PEER_TOPK // (PEER_CAND_COLS + 1)):
        ok = (ridx >= PEER_CAND_COLS) & ((ridx + 1) * (r1 + 1) <= PEER_TOPK)
        cands.append(jnp.where(ok, b16 + a16[r1:r1 + 1, :], NEG_BIG))
    cand = jnp.concatenate(cands, axis=0)
    x = cand
    tau = None
    for r in range(PEER_TOPK):
        tau = jnp.max(x, axis=0, keepdims=True)
        x = jnp.where(x == tau, NEG_BIG, x)
    top = a16[0:1, :] + b16[0:1, :]
    z = jnp.sum(jnp.where(cand >= tau, jnp.exp(cand - top), 0.0), axis=0, keepdims=True)
    cnt = jnp.zeros(s1.shape, F32)
    for r2 in range(PEER_TOPK):
        cnt = cnt + jnp.where(s1 + b16[r2:r2 + 1, :] >= tau, 1.0, 0.0)
    a = jnp.exp(s1 - a16[0:1, :]) / z
    b = jnp.exp(s2 - b16[0:1, :])
    return a, cnt, rank2, b


def _peer_kernel(h_ref, g_ref, wq_ref, k1_ref, k2_ref, u_ref, vt_ref, o_ref,
                 xt_scr, a_scr, cnt_scr, rank_scr, b_scr, acc_scr, *, tm, te, ne):
    j = pl.program_id(1)
    n1 = te // PEER_NKEYS
    nchunk = tm // LANES
    nsub = te // PEER_SUB
    slabs = PEER_SUB // PEER_NKEYS

    def gate_weights(sb):
        ws = []
        for il in range(slabs):
            i1 = j * n1 + sb * slabs + il
            w = jnp.zeros((PEER_NKEYS, tm), BF16)
            for h in range(PEER_HEADS):
                row = lambda ref: jnp.concatenate(
                    [jnp.broadcast_to(ref[h, cch, pl.ds(i1, 1), :], (BF16_ROWS, LANES)) for cch in range(nchunk)],
                    axis=1).astype(BF16)
                cnt_b = jnp.tile(row(cnt_scr), (PEER_NKEYS // BF16_ROWS, 1))
                a_b = jnp.tile(row(a_scr), (PEER_NKEYS // BF16_ROWS, 1))
                w = w + jnp.where(rank_scr[h] < cnt_b, b_scr[h], jnp.zeros((), BF16)) * a_b
            ws.append(w)
        return jnp.concatenate(ws, axis=0)

    @pl.when(j == 0)
    def _():
        xn = _rms(h_ref[...], g_ref[...])
        xt_scr[...] = xn.T.astype(BF16)
        acc_scr[...] = jnp.zeros(acc_scr.shape, F32)

        def head_body(h, carry):
            base = pl.multiple_of(h * PEER_KEY_DIM, PEER_KEY_DIM)
            k1 = k1_ref[h].astype(BF16)
            k2 = k2_ref[h].astype(BF16)
            qh = jnp.dot(wq_ref[pl.ds(base, PEER_KEY_DIM), :], xt_scr[...], preferred_element_type=F32)
            for cch in range(nchunk):
                sl = slice(cch * LANES, (cch + 1) * LANES)
                q1 = qh[0:PEER_HALF, sl].astype(BF16)
                q2 = qh[PEER_HALF:, sl].astype(BF16)
                s1 = jnp.dot(k1, q1, preferred_element_type=F32)
                s2 = jnp.dot(k2, q2, preferred_element_type=F32)
                a, cnt, rank2, b = _peer_select(s1, s2)
                a_scr[h, cch] = a
                cnt_scr[h, cch] = cnt
                rank_scr[h, :, sl] = rank2.astype(BF16)
                b_scr[h, :, sl] = b.astype(BF16)
            return carry

        lax.fori_loop(0, PEER_HEADS, head_body, 0)

    def mm1(sb):
        return jnp.dot(u_ref[sb * PEER_SUB:(sb + 1) * PEER_SUB, :], xt_scr[...], preferred_element_type=F32)

    def mm2(sb, act):
        return jnp.dot(vt_ref[:, sb * PEER_SUB:(sb + 1) * PEER_SUB], act, preferred_element_type=F32)

    acc = acc_scr[...]
    ht = mm1(0)
    for sb in range(nsub):
        ht_next = mm1(sb + 1) if sb + 1 < nsub else None
        act = _gelu_tanh(ht.astype(BF16)) * gate_weights(sb)
        acc = acc + mm2(sb, act)
        ht = ht_next
    acc_scr[...] = acc

    @pl.when(j == ne - 1)
    def _():
        o_ref[...] = h_ref[...] + acc_scr[...].T


def _peer(h2d, g, wq_t, k1, k2, u_bf16, vt_bf16):
    T = h2d.shape[0]
    tm = _tile(T, 512)
    assert tm % LANES == 0
    te = PEER_STEP
    ne = PEER_EXPERTS // te
    kern = functools.partial(_peer_kernel, tm=tm, te=te, ne=ne)
    hk = (PEER_HEADS, PEER_NKEYS, tm)
    hkc = (PEER_HEADS, tm // LANES, PEER_NKEYS, LANES)
    return pl.pallas_call(
        kern,
        grid=(T // tm, ne),
        in_specs=[pl.BlockSpec((tm, D_MODEL), lambda i, j: (i, 0)),
                  pl.BlockSpec((1, D_MODEL), lambda i, j: (0, 0)),
                  pl.BlockSpec((PEER_HEADS * PEER_KEY_DIM, D_MODEL), lambda i, j: (0, 0)),
                  pl.BlockSpec((PEER_HEADS, PEER_NKEYS, PEER_HALF), lambda i, j: (0, 0, 0)),
                  pl.BlockSpec((PEER_HEADS, PEER_NKEYS, PEER_HALF), lambda i, j: (0, 0, 0)),
                  pl.BlockSpec((te, D_MODEL), lambda i, j: (j, 0)),
                  pl.BlockSpec((D_MODEL, te), lambda i, j: (0, j))],
        out_specs=pl.BlockSpec((tm, D_MODEL), lambda i, j: (i, 0)),
        out_shape=jax.ShapeDtypeStruct((T, D_MODEL), F32),
        scratch_shapes=[pltpu.VMEM((D_MODEL, tm), BF16),
                        pltpu.VMEM(hkc, F32), pltpu.VMEM(hkc, F32), pltpu.VMEM(hk, BF16), pltpu.VMEM(hk, BF16),
                        pltpu.VMEM((D_MODEL, tm), F32)],
        compiler_params=_cparams("parallel", "arbitrary"),
        name="peer",
    )(h2d, g.reshape(1, D_MODEL), wq_t, k1, k2, u_bf16, vt_bf16)


def _ple_kernel(h_ref, p_ref, g_ref, wg_ref, wp_ref, gf_ref, o_ref, *, final_norm):
    h = h_ref[...]
    xn = _rms(h, g_ref[...])
    gate = jax.nn.sigmoid(jnp.dot(xn.astype(BF16), wg_ref[...], preferred_element_type=F32))
    emb = jnp.dot(p_ref[...].astype(BF16), wp_ref[...], preferred_element_type=F32)
    out = h + emb * gate
    if final_norm:
        out = _rms(out, gf_ref[...])
    o_ref[...] = out


def _ple(h2d, p2d, g, wg_bf16, wp_bf16, g_final, final_norm):
    T = h2d.shape[0]
    tm = _tile(T, 512)
    kern = functools.partial(_ple_kernel, final_norm=final_norm)
    return pl.pallas_call(
        kern,
        grid=(T // tm,),
        in_specs=[pl.BlockSpec((tm, D_MODEL), lambda i: (i, 0)),
                  pl.BlockSpec((tm, PLE_DIM), lambda i: (i, 0)),
                  pl.BlockSpec((1, D_MODEL), lambda i: (0, 0)),
                  pl.BlockSpec((D_MODEL, D_MODEL), lambda i: (0, 0)),
                  pl.BlockSpec((PLE_DIM, D_MODEL), lambda i: (0, 0)),
                  pl.BlockSpec((1, D_MODEL), lambda i: (0, 0))],
        out_specs=pl.BlockSpec((tm, D_MODEL), lambda i: (i, 0)),
        out_shape=jax.ShapeDtypeStruct((T, D_MODEL), F32),
        compiler_params=_cparams("parallel"),
        name="ple",
    )(h2d, p2d, g.reshape(1, D_MODEL), wg_bf16, wp_bf16, g_final.reshape(1, D_MODEL))


def _pad_to(x, n, axis):
    pad = n - x.shape[axis]
    if pad == 0:
        return x
    widths = [(0, 0)] * x.ndim
    widths[axis] = (0, pad)
    return jnp.pad(x, widths)


def _run_trunk(x, p, past, weights):
    (g_mix, w_in, b_forget, g_ret, conv_w, conv_b, dt_bias, a_log, d_skip, g_ssm, w_out,
     g_ffn, peer_wq, peer_k1, peer_k2, peer_u, peer_v, g_ple, w_ple_gate, w_ple, g_final) = weights
    B, L, _ = x.shape
    depth = w_in.shape[0]
    T = B * L
    P = 0 if past is None else past[0].shape[2]
    half = B_HEAD_DIM // 2
    invf = jnp.tile(ROPE_BASE ** (-jnp.arange(half, dtype=F32) / half), LANES // half).reshape(1, LANES)

    h = x.reshape(T, D_MODEL)
    outs = []
    for i in range(depth):
        w_proj, bf_pad, dtb_full = _prep_w_in(w_in[i], b_forget[i], dt_bias[i])
        proj, lf = _inproj(h, g_mix[i], w_proj, bf_pad)
        proj3 = proj.reshape(B, L, PROJ_W)
        lf3 = lf.reshape(B, L, LANES)
        k_new = proj3[:, :, COL_KA:COL_KA + A_W]
        v_new = proj3[:, :, COL_VA:COL_VA + A_W]
        q_a = proj3[:, :, COL_QA:COL_QA + A_W]

        if past is None:
            ccol, crow = _cumsum(lf3)
            ya = _fox(q_a, k_new, v_new, ccol, crow, 0)
            rs0 = jnp.zeros((B, B_W, B_W), F32)
            ss0 = jnp.zeros((B, C_STATE, C_INNER), F32)
            cs0 = jnp.zeros((B, 8, C_CONV_DIM), F32)
        else:
            lk = P + L
            lk_pad = -(-lk // LANES) * LANES
            past_k = past[0][i].reshape(B, P, A_W)
            past_v = past[1][i].reshape(B, P, A_W)
            past_lf = _pad_to(past[2][i], LANES, 2)
            k_all = _pad_to(jnp.concatenate([past_k, k_new], axis=1), lk_pad, 1)
            v_all = _pad_to(jnp.concatenate([past_v, v_new], axis=1), lk_pad, 1)
            lf_all = _pad_to(jnp.concatenate([past_lf, lf3], axis=1), lk_pad, 1)
            ccol, crow = _cumsum(lf_all)
            ya = _fox(q_a, k_all, v_all, ccol[:, P:P + L], crow, P)
            rs0 = _ret_state_to_kernel(past[3][i])
            ss0 = jnp.transpose(past[4][i], (0, 2, 1, 3)).reshape(B, C_STATE, C_INNER)
            cs0 = jnp.concatenate([jnp.zeros((B, 8 - (C_CONV - 1), C_CONV_DIM), F32), past[5][i]], axis=1)

        yb, ret_k = _retention(proj3, rs0, invf, g_ret[i], P)
        ret_new = _ret_state_from_kernel(ret_k)

        yc, ssm_k = _ssd(proj3, ss0, cs0, conv_w[i], conv_b[i], dtb_full, a_log[i], d_skip[i], g_ssm[i])
        ssm_new = jnp.transpose(ssm_k.reshape(B, C_STATE, C_HEADS, C_HEAD_DIM), (0, 2, 1, 3))
        xbc = proj3[:, :, COL_XBC:COL_XBC + C_CONV_DIM]
        if L >= C_CONV - 1:
            conv_new = xbc[:, L - (C_CONV - 1):]
        else:
            prev = jnp.zeros((B, C_CONV - 1, C_CONV_DIM), F32) if past is None else past[5][i]
            conv_new = jnp.concatenate([prev, xbc], axis=1)[:, -(C_CONV - 1):]

        h = _outproj(h, ya.reshape(T, A_W), yb.reshape(T, B_W), yc.reshape(T, C_INNER), w_out[i].astype(BF16))
        h = _peer(h, g_ffn[i], peer_wq[i].T.astype(BF16), peer_k1[i], peer_k2[i],
                  peer_u[i].astype(BF16), peer_v[i].T.astype(BF16))
        h = _ple(h, p[i].reshape(T, PLE_DIM), g_ple[i], w_ple_gate[i].astype(BF16), w_ple[i].astype(BF16),
                 g_final, final_norm=(i == depth - 1))
        outs.append((k_new.reshape(B, L, A_HEADS, A_HEAD_DIM), v_new.reshape(B, L, A_HEADS, A_HEAD_DIM),
                     lf3[:, :, :A_HEADS], ret_new, ssm_new, conv_new))
    stacked = tuple(jnp.stack([o[n] for o in outs]) for n in range(6))
    return h.reshape(B, L, D_MODEL), stacked


def kernel(x_prompt, x_sample, cache_k_fox, cache_v_fox, cache_logf_fox, state_ret, state_ssm, state_conv, p_prompt, p_sample, g_mix, w_in, b_forget, g_ret, conv_w, conv_b, dt_bias, a_log, d_skip, g_ssm, w_out, g_ffn, peer_wq, peer_k1, peer_k2, peer_u, peer_v, g_ple, w_ple_gate, w_ple, g_final):
    weights = (g_mix, w_in, b_forget, g_ret, conv_w, conv_b, dt_bias, a_log, d_skip, g_ssm, w_out,
               g_ffn, peer_wq, peer_k1, peer_k2, peer_u, peer_v, g_ple, w_ple_gate, w_ple, g_final)
    y_prompt, (k_p, v_p, lf_p, ret_p, ssm_p, conv_p) = _run_trunk(x_prompt, p_prompt, None, weights)
    past = (cache_k_fox, cache_v_fox, cache_logf_fox, state_ret, state_ssm, state_conv)
    y_sample, (k_s, v_s, lf_s, ret_s, ssm_s, conv_s) = _run_trunk(x_sample, p_sample, past, weights)
    return (y_prompt, y_sample, k_p, v_p, lf_p, ret_p, ssm_p, conv_p, k_s, v_s, lf_s, ret_s, ssm_s, conv_s)
```

```python
import functools
import math

import jax
import jax.numpy as jnp
from jax import lax
from jax.experimental import pallas as pl
from jax.experimental.pallas import tpu as pltpu

F32 = jnp.float32
BF16 = jnp.bfloat16

D_MODEL = 1024
PLE_DIM = 256
EPS = 1e-6
A_HEADS, A_HEAD_DIM = 4, 64
B_HEADS, B_HEAD_DIM = 4, 64
ROPE_BASE = 10000.0
C_HEADS, C_HEAD_DIM = 8, 64
C_INNER = C_HEADS * C_HEAD_DIM
C_GROUPS, C_STATE, C_CONV = 2, 128, 4
C_CONV_DIM = C_INNER + 2 * C_GROUPS * C_STATE
PEER_HEADS, PEER_NKEYS, PEER_KEY_DIM, PEER_TOPK = 8, 128, 256, 16
PEER_HALF = PEER_KEY_DIM // 2
PEER_EXPERTS = PEER_NKEYS * PEER_NKEYS

LANES = 128
BF16_ROWS = 16
A_W = A_HEADS * A_HEAD_DIM
B_W = B_HEADS * B_HEAD_DIM
G_W = C_GROUPS * C_STATE

COL_QA = 0
COL_KA = COL_QA + A_W
COL_VA = COL_KA + A_W
COL_QB = COL_VA + A_W
COL_KB = COL_QB + B_W
COL_VB = COL_KB + B_W
COL_GB = COL_VB + B_W
COL_FA = COL_GB + B_W
COL_ZC = COL_FA + 2 * LANES
COL_DT = COL_ZC + C_INNER
COL_XBC = COL_DT + C_INNER
PROJ_W = COL_XBC + C_CONV_DIM
assert all(c % A_W == 0 for c in (COL_QA, COL_KA, COL_VA, COL_QB, COL_KB, COL_VB, COL_GB))
assert COL_ZC % C_INNER == 0 and COL_DT % C_INNER == 0 and COL_XBC % C_CONV_DIM == 0 and COL_FA % LANES == 0

NEG_BIG = -1e30
VMEM_LIMIT_BYTES = 52 * 1024 * 1024


def _cparams(*sem):
    return pltpu.CompilerParams(dimension_semantics=sem, vmem_limit_bytes=VMEM_LIMIT_BYTES)


def _tile(n, pref):
    t = min(n, pref)
    while n % t:
        t -= 8
    assert t > 0
    return t


def _rms(x, g):
    return x * lax.rsqrt(jnp.mean(x * x, axis=-1, keepdims=True) + EPS) * g


def _split3(x):
    hi = x.astype(BF16)
    r1 = x - hi.astype(F32)
    mid = r1.astype(BF16)
    lo = (r1 - mid.astype(F32)).astype(BF16)
    return hi, mid, lo


def _dot_sel(sel_bf16, x):
    hi, mid, lo = _split3(x)
    d = lambda p: jnp.dot(sel_bf16, p, preferred_element_type=F32)
    return d(hi) + d(mid) + d(lo)


def _dot_sel_rhs(x, sel_bf16):
    hi, mid, lo = _split3(x)
    d = lambda p: jnp.dot(p, sel_bf16, preferred_element_type=F32)
    return d(hi) + d(mid) + d(lo)


def _inproj_kernel(x_ref, g_ref, w_ref, bf_ref, o_ref, lf_ref, kb_ref, vb_ref):
    xn = _rms(x_ref[...], g_ref[...])
    y = jnp.dot(xn.astype(BF16), w_ref[...], preferred_element_type=F32)
    o_ref[...] = y
    kb_ref[...] = y[:, COL_KA:COL_KA + A_W].astype(BF16)
    vb_ref[...] = y[:, COL_VA:COL_VA + A_W].astype(BF16)
    fa = y[:, COL_FA:COL_FA + LANES] + bf_ref[...]
    lane = lax.broadcasted_iota(jnp.int32, fa.shape, 1)
    lf_ref[...] = jnp.where(lane < A_HEADS, jax.nn.log_sigmoid(fa), 0.0)


def _inproj(x2d, g, w, bf_pad):
    T = x2d.shape[0]
    tm = _tile(T, 256)
    return pl.pallas_call(
        _inproj_kernel,
        grid=(T // tm,),
        in_specs=[pl.BlockSpec((tm, D_MODEL), lambda i: (i, 0)),
                  pl.BlockSpec((1, D_MODEL), lambda i: (0, 0)),
                  pl.BlockSpec((D_MODEL, PROJ_W), lambda i: (0, 0)),
                  pl.BlockSpec((1, LANES), lambda i: (0, 0))],
        out_specs=[pl.BlockSpec((tm, PROJ_W), lambda i: (i, 0)),
                   pl.BlockSpec((tm, LANES), lambda i: (i, 0)),
                   pl.BlockSpec((tm, A_W), lambda i: (i, 0)),
                   pl.BlockSpec((tm, A_W), lambda i: (i, 0))],
        out_shape=[jax.ShapeDtypeStruct((T, PROJ_W), F32),
                   jax.ShapeDtypeStruct((T, LANES), F32),
                   jax.ShapeDtypeStruct((T, A_W), BF16),
                   jax.ShapeDtypeStruct((T, A_W), BF16)],
        compiler_params=_cparams("parallel"),
        name="inproj",
    )(x2d, g.reshape(1, D_MODEL), w, bf_pad)


def _prep_w_in(w_in, b_forget, dt_bias):
    sizes = (A_W, A_W, A_W, A_HEADS, B_W, B_W, B_W, B_W, C_INNER, C_INNER, G_W, G_W, C_HEADS)
    pts = []
    acc = 0
    for s in sizes[:-1]:
        acc += s
        pts.append(acc)
    qa, ka, va, fa, qb, kb, vb, gb, zc, xc, bc, cc, dtc = jnp.split(w_in, pts, axis=-1)

    def rot_perm(w):
        w = w.reshape(D_MODEL, B_HEADS, 2, B_HEAD_DIM // 2)
        return jnp.transpose(w, (0, 2, 1, 3)).reshape(D_MODEL, B_W)

    z = lambda n: jnp.zeros((D_MODEL, n), w_in.dtype)
    w = jnp.concatenate([qa, ka, va, rot_perm(qb), rot_perm(kb), vb, gb, fa, z(2 * LANES - A_HEADS), zc,
                         jnp.repeat(dtc, C_HEAD_DIM, axis=1), xc, bc, cc], axis=1)
    assert w.shape[1] == PROJ_W
    bf_pad = jnp.concatenate([b_forget, jnp.zeros((LANES - A_HEADS,), F32)]).reshape(1, LANES)
    dtb_full = jnp.repeat(dt_bias, C_HEAD_DIM).reshape(1, C_INNER)
    return w.astype(BF16), bf_pad, dtb_full


FOX_TQ = 512
FOX_TK = 512
FOX_XTRA = 6


def _fox_xtra_base(h):
    return A_HEAD_DIM * ((h + 1) % A_HEADS)


def _cumsum_kernel(lf_ref, eq_ref, ek_ref, carry_scr):
    @pl.when(pl.program_id(1) == 0)
    def _():
        carry_scr[...] = jnp.zeros_like(carry_scr)

    lf = lf_ref[0]
    tc = lf.shape[0]
    r = lax.broadcasted_iota(jnp.int32, (tc, tc), 0)
    c = lax.broadcasted_iota(jnp.int32, (tc, tc), 1)
    tri = jnp.where(r >= c, 1.0, 0.0).astype(BF16)
    cum = _dot_sel(tri, lf) + carry_scr[0:1, :]
    carry_scr[...] = jnp.broadcast_to(cum[tc - 1:tc, :], carry_scr.shape)

    pieces = _split3(cum)
    src = lax.broadcasted_iota(jnp.int32, (LANES, A_W), 0)
    dst = lax.broadcasted_iota(jnp.int32, (LANES, A_W), 1)
    lane = lax.broadcasted_iota(jnp.int32, (1, A_W), 1)
    eq = jnp.zeros((tc, A_W), F32)
    ek = jnp.zeros((tc, A_W), F32)
    one_q = jnp.zeros((1, A_W), F32)
    one_k = jnp.zeros((1, A_W), F32)
    for h in range(A_HEADS):
        base = _fox_xtra_base(h)
        for k, piece in enumerate(pieces):
            place_q = jnp.where((src == h) & (dst == base + k), 1.0, 0.0).astype(BF16)
            place_k = jnp.where((src == h) & (dst == base + 3 + k), -1.0, 0.0).astype(BF16)
            eq = eq + jnp.dot(piece, place_q, preferred_element_type=F32)
            ek = ek + jnp.dot(piece, place_k, preferred_element_type=F32)
        one_q = jnp.where((lane >= base + 3) & (lane < base + FOX_XTRA), 1.0, one_q)
        one_k = jnp.where((lane >= base) & (lane < base + 3), 1.0, one_k)
    eq_ref[0] = (eq + one_q).astype(BF16)
    ek_ref[0] = (ek + one_k).astype(BF16)


def _cumsum(lf):
    B, L, _ = lf.shape
    tc = _tile(L, 512)
    return pl.pallas_call(
        _cumsum_kernel,
        grid=(B, L // tc),
        in_specs=[pl.BlockSpec((1, tc, LANES), lambda b, j: (b, j, 0))],
        out_specs=[pl.BlockSpec((1, tc, A_W), lambda b, j: (b, j, 0)),
                   pl.BlockSpec((1, tc, A_W), lambda b, j: (b, j, 0))],
        out_shape=[jax.ShapeDtypeStruct((B, L, A_W), BF16),
                   jax.ShapeDtypeStruct((B, L, A_W), BF16)],
        scratch_shapes=[pltpu.VMEM((8, LANES), F32)],
        compiler_params=_cparams("parallel", "arbitrary"),
        name="fox_cumsum",
    )(lf)


def _fox_kernel(q_ref, eq_ref, kb_ref, ek_ref, vt_ref, o_ref, qp_scr, m_scr, l_scr, acc_scr, *,
                tq, tk, offset, nk):
    i = pl.program_id(1)
    scale = A_HEAD_DIM ** -0.5
    q_lo = offset + i * tq

    def in_head(lane, h):
        return (lane >= h * A_HEAD_DIM) & (lane < (h + 1) * A_HEAD_DIM)

    def in_xtra(lane, h):
        return (lane >= _fox_xtra_base(h)) & (lane < _fox_xtra_base(h) + FOX_XTRA)

    lane_q = lax.broadcasted_iota(jnp.int32, (tq, A_W), 1)
    q = (q_ref[0] * scale).astype(BF16)
    eq = eq_ref[0]
    for h in range(A_HEADS):
        qp_scr[h] = jnp.where(in_xtra(lane_q, h), eq, jnp.where(in_head(lane_q, h), q, jnp.zeros((), BF16)))
    m_scr[...] = jnp.full(m_scr.shape, NEG_BIG, F32)
    l_scr[...] = jnp.zeros(l_scr.shape, F32)
    acc_scr[...] = jnp.zeros(acc_scr.shape, F32)

    lane_k = lax.broadcasted_iota(jnp.int32, (tk, A_W), 1)

    def block(jj, masked):
        ks = pl.multiple_of(jj * tk, tk)
        k = kb_ref[0, pl.ds(ks, tk), :]
        ek = ek_ref[0, pl.ds(ks, tk), :]
        vt = vt_ref[0, jj]
        if masked:
            kpos = ks + lax.broadcasted_iota(jnp.int32, (tk, tq), 0)
            qpos = q_lo + lax.broadcasted_iota(jnp.int32, (tk, tq), 1)
            vis = kpos <= qpos
        for h in range(A_HEADS):
            kp = jnp.where(in_xtra(lane_k, h), ek, k)
            s = lax.dot_general(kp, qp_scr[h], (((1,), (1,)), ((), ())), preferred_element_type=F32)
            if masked:
                s = jnp.where(vis, s, NEG_BIG)
            m_prev = m_scr[h, 0:1, :]
            m_next = jnp.maximum(m_prev, jnp.max(s, axis=0, keepdims=True))
            alpha = jnp.exp(m_prev - m_next)
            p = jnp.exp(s - m_next)
            l_scr[h] = jnp.broadcast_to(alpha * l_scr[h, 0:1, :] + jnp.sum(p, axis=0, keepdims=True), (8, tq))
            m_scr[h] = jnp.broadcast_to(m_next, (8, tq))
            acc_scr[h] = acc_scr[h] * alpha + jnp.dot(vt, p.astype(BF16), preferred_element_type=F32)

    n_full = (q_lo + 1) // tk
    n_vis = jnp.minimum((q_lo + tq - 1) // tk + 1, nk)

    def full_body(jj, carry):
        block(jj, False)
        return carry

    def edge_body(jj, carry):
        block(jj, True)
        return carry

    lax.fori_loop(0, n_full, full_body, 0)
    lax.fori_loop(n_full, n_vis, edge_body, 0)

    row = lax.broadcasted_iota(jnp.int32, (A_W, tq), 0)
    out_t = jnp.zeros((A_W, tq), F32)
    for h in range(A_HEADS):
        out_t = out_t + jnp.where(in_head(row, h), acc_scr[h] / l_scr[h, 0:1, :], 0.0)
    o_ref[0] = out_t.T.astype(o_ref.dtype)


def _fox(q_src, q_col, eq, kb, ek, vb, offset):
    B, Lq, _ = q_src.shape
    Lk = kb.shape[1]
    tq = _tile(Lq, FOX_TQ)
    tk = _tile(Lk, FOX_TK)
    assert tq % LANES == 0 and tk % LANES == 0
    nq, nk = Lq // tq, Lk // tk
    vt = jnp.transpose(vb.reshape(B, nk, tk, A_W), (0, 1, 3, 2))
    whole = lambda b, i: (b, 0, 0)
    kern = functools.partial(_fox_kernel, tq=tq, tk=tk, offset=offset, nk=nk)
    return pl.pallas_call(
        kern,
        grid=(B, nq),
        in_specs=[pl.BlockSpec((1, tq, A_W), lambda b, i: (b, i, q_col // A_W)),
                  pl.BlockSpec((1, tq, A_W), lambda b, i: (b, i, 0)),
                  pl.BlockSpec((1, Lk, A_W), whole),
                  pl.BlockSpec((1, Lk, A_W), whole),
                  pl.BlockSpec((1, nk, A_W, tk), lambda b, i: (b, 0, 0, 0))],
        out_specs=pl.BlockSpec((1, tq, A_W), lambda b, i: (b, i, 0)),
        out_shape=jax.ShapeDtypeStruct((B, Lq, A_W), BF16),
        scratch_shapes=[pltpu.VMEM((A_HEADS, tq, A_W), BF16),
                        pltpu.VMEM((A_HEADS, 8, tq), F32),
                        pltpu.VMEM((A_HEADS, 8, tq), F32),
                        pltpu.VMEM((A_HEADS, A_W, tq), F32)],
        compiler_params=_cparams("parallel", "arbitrary"),
        name="fox_attention",
    )(q_src, eq, kb, ek, vt)


def _ret_lane_head_v():
    return lax.broadcasted_iota(jnp.int32, (1, B_W), 1) // B_HEAD_DIM


def _ret_lane_head_qk():
    return (lax.broadcasted_iota(jnp.int32, (1, B_W), 1) % LANES) // (B_HEAD_DIM // 2)


def _log_gamma(h):
    return math.log1p(-2.0 ** (-5.0 - h))


def _ret_kernel(q_ref, k_ref, v_ref, g_ref, s0_ref, invf_ref, gret_ref, y_ref, sout_ref,
                state_scr, dec_scr, *, lc, offset, nchunks):
    c = pl.program_id(1)

    @pl.when(c == 0)
    def _():
        state_scr[...] = s0_ref[0]
        r = lax.broadcasted_iota(jnp.int32, (lc, lc), 0)
        s = lax.broadcasted_iota(jnp.int32, (lc, lc), 1)
        d = (r - s).astype(F32)
        for h in range(B_HEADS):
            dec_scr[h] = jnp.where(r >= s, jnp.exp(d * _log_gamma(h)), 0.0)

    hv = _ret_lane_head_v()
    hqk = _ret_lane_head_qk()
    lg_v = jnp.zeros((1, B_W), F32)
    lg_qk = jnp.zeros((1, B_W), F32)
    for h in range(B_HEADS):
        lg_v = jnp.where(hv == h, _log_gamma(h), lg_v)
        lg_qk = jnp.where(hqk == h, _log_gamma(h), lg_qk)

    pos = (offset + c * lc + lax.broadcasted_iota(jnp.int32, (lc, LANES), 0)).astype(F32)
    ang = pos * invf_ref[...]
    cos, sin = jnp.cos(ang), jnp.sin(ang)

    def rot(x):
        x1, x2 = x[:, :LANES], x[:, LANES:]
        return jnp.concatenate([x1 * cos - x2 * sin, x1 * sin + x2 * cos], axis=1)

    qr = rot(q_ref[0])
    kr = rot(k_ref[0]) * (B_HEAD_DIM ** -0.5)
    v = v_ref[0]
    vb = v.astype(BF16)
    krb = kr.astype(BF16)

    lpos = lax.broadcasted_iota(jnp.int32, (lc, B_W), 0).astype(F32)
    state = state_scr[...]
    y = jnp.dot(qr.astype(BF16), state.astype(BF16), preferred_element_type=F32) * jnp.exp((lpos + 1.0) * lg_v)
    for h in range(B_HEADS):
        qh = jnp.where(hqk == h, qr, 0.0).astype(BF16)
        sc = lax.dot_general(qh, krb, (((1,), (1,)), ((), ())), preferred_element_type=F32) * dec_scr[h]
        yh = jnp.dot(sc.astype(BF16), vb, preferred_element_type=F32)
        y = y + jnp.where(hv == h, yh, 0.0)

    kw = (kr * jnp.exp((lc - 1.0 - lpos) * lg_qk)).astype(BF16)
    upd = lax.dot_general(kw, vb, (((0,), (0,)), ((), ())), preferred_element_type=F32)
    row_head = (lax.broadcasted_iota(jnp.int32, (B_W, B_W), 0) % LANES) // (B_HEAD_DIM // 2)
    col_head = lax.broadcasted_iota(jnp.int32, (B_W, B_W), 1) // B_HEAD_DIM
    lg_rows = jnp.zeros((B_W, B_W), F32)
    for h in range(B_HEADS):
        lg_rows = jnp.where(row_head == h, _log_gamma(h), lg_rows)
    new_state = jnp.where(row_head == col_head, state * jnp.exp(lc * lg_rows) + upd, 0.0)
    state_scr[...] = new_state

    @pl.when(c == nchunks - 1)
    def _():
        sout_ref[0] = new_state

    ri = lax.broadcasted_iota(jnp.int32, (B_W, B_W), 0) // B_HEAD_DIM
    ci = lax.broadcasted_iota(jnp.int32, (B_W, B_W), 1) // B_HEAD_DIM
    avg = jnp.where(ri == ci, 1.0 / B_HEAD_DIM, 0.0).astype(BF16)
    yc = y - _dot_sel_rhs(y, avg)
    var = _dot_sel_rhs(yc * yc, avg)
    yn = yc * lax.rsqrt(var + EPS) * gret_ref[...]
    y_ref[0] = (jax.nn.silu(g_ref[0]) * yn).astype(y_ref.dtype)


def _retention(proj3, s0, invf, g_ret, offset):
    B, L, _ = proj3.shape
    lc = _tile(L, 256)
    nch = L // lc
    blk = lambda col: pl.BlockSpec((1, lc, B_W), lambda b, c: (b, c, col // B_W))
    kern = functools.partial(_ret_kernel, lc=lc, offset=offset, nchunks=nch)
    return pl.pallas_call(
        kern,
        grid=(B, nch),
        in_specs=[blk(COL_QB), blk(COL_KB), blk(COL_VB), blk(COL_GB),
                  pl.BlockSpec((1, B_W, B_W), lambda b, c: (b, 0, 0)),
                  pl.BlockSpec((1, LANES), lambda b, c: (0, 0)),
                  pl.BlockSpec((1, B_W), lambda b, c: (0, 0))],
        out_specs=[pl.BlockSpec((1, lc, B_W), lambda b, c: (b, c, 0)),
                   pl.BlockSpec((1, B_W, B_W), lambda b, c: (b, 0, 0))],
        out_shape=[jax.ShapeDtypeStruct((B, L, B_W), BF16),
                   jax.ShapeDtypeStruct((B, B_W, B_W), F32)],
        scratch_shapes=[pltpu.VMEM((B_W, B_W), F32),
                        pltpu.VMEM((B_HEADS, lc, lc), F32)],
        compiler_params=_cparams("parallel", "arbitrary"),
        name="retention",
    )(proj3, proj3, proj3, proj3, s0, invf, g_ret.reshape(1, B_W))


def _ret_state_to_kernel(s):
    B = s.shape[0]
    half = B_HEAD_DIM // 2
    s = s.reshape(B, B_HEADS, 2, half, B_HEAD_DIM)
    eye = jnp.eye(B_HEADS, dtype=s.dtype)
    full = jnp.einsum('bhkie,hg->bkhige', s, eye)
    return full.reshape(B, B_W, B_W)


def _ret_state_from_kernel(s):
    B = s.shape[0]
    half = B_HEAD_DIM // 2
    s = s.reshape(B, 2, B_HEADS, half, B_HEADS, B_HEAD_DIM)
    d = jnp.einsum('bkhihe->bhkie', s)
    return d.reshape(B, B_HEADS, B_HEAD_DIM, B_HEAD_DIM)


def _ssd_kernel(z_ref, xbc_ref, dt_ref, s0_ref, cs0_ref, cw_ref, cb_ref, dtb_ref, alog_ref, dsk_ref, gs_ref,
                y_ref, sout_ref, state_scr, xp_scr, *, lc, nchunks):
    c = pl.program_id(1)
    PADR = 8

    @pl.when(c == 0)
    def _():
        state_scr[...] = s0_ref[0]
        xp_scr[0:PADR, :] = cs0_ref[0]

    xbc = xbc_ref[0]
    xp_scr[PADR:PADR + lc, :] = xbc
    cw = cw_ref[...]
    conv = xp_scr[PADR - 3:PADR - 3 + lc, :] * cw[0:1, :]
    for jj in range(1, C_CONV):
        conv = conv + xp_scr[PADR - 3 + jj:PADR - 3 + jj + lc, :] * cw[jj:jj + 1, :]
    act = jax.nn.silu(conv + cb_ref[...])
    xp_scr[0:PADR, :] = xp_scr[lc:lc + PADR, :]

    xs = act[:, :C_INNER]
    bm = act[:, C_INNER:C_INNER + G_W]
    cm = act[:, C_INNER + G_W:]
    dt = jax.nn.softplus(dt_ref[0] + dtb_ref[...])
    loga = dt * (-jnp.exp(alog_ref[...]))

    r = lax.broadcasted_iota(jnp.int32, (lc, lc), 0)
    s = lax.broadcasted_iota(jnp.int32, (lc, lc), 1)
    causal = r >= s
    tri = jnp.where(causal, 1.0, 0.0).astype(BF16)
    cum = _dot_sel(tri, loga)
    total = cum[lc - 1:lc, :]
    xdt = xs * dt
    xdtb = xdt.astype(BF16)
    bmb = bm.astype(BF16)
    cmb = cm.astype(BF16)
    state = state_scr[...]

    gw = C_INNER // C_GROUPS
    cum_t = [cum[:, kk * LANES:(kk + 1) * LANES].T for kk in range(C_INNER // LANES)]
    ys = []
    for g in range(C_GROUPS):
        cg = cmb[:, g * C_STATE:(g + 1) * C_STATE]
        bg = bmb[:, g * C_STATE:(g + 1) * C_STATE]
        scores = lax.dot_general(cg, bg, (((1,), (1,)), ((), ())), preferred_element_type=F32)
        y_g = jnp.dot(cg, state[:, g * gw:(g + 1) * gw].astype(BF16), preferred_element_type=F32)
        y_g = y_g * jnp.exp(cum[:, g * gw:(g + 1) * gw])
        lane_head = lax.broadcasted_iota(jnp.int32, (1, gw), 1) // C_HEAD_DIM
        xg = xdtb[:, g * gw:(g + 1) * gw]
        for hh in range(C_HEADS // C_GROUPS):
            col = g * gw + hh * C_HEAD_DIM
            cum_col = cum[:, col:col + 1]
            cum_row = cum_t[col // LANES][col % LANES:col % LANES + 1, :]
            dec = jnp.exp(jnp.where(causal, cum_col - cum_row, NEG_BIG))
            yh = jnp.dot((scores * dec).astype(BF16), xg, preferred_element_type=F32)
            y_g = y_g + jnp.where(lane_head == hh, yh, 0.0)
        ys.append(y_g)
        xw = (xdt[:, g * gw:(g + 1) * gw] * jnp.exp(total[:, g * gw:(g + 1) * gw] - cum[:, g * gw:(g + 1) * gw]))
        upd = lax.dot_general(bg, xw.astype(BF16), (((0,), (0,)), ((), ())), preferred_element_type=F32)
        state_scr[:, g * gw:(g + 1) * gw] = state[:, g * gw:(g + 1) * gw] * jnp.exp(total[:, g * gw:(g + 1) * gw]) + upd
    y = jnp.concatenate(ys, axis=1)

    @pl.when(c == nchunks - 1)
    def _():
        sout_ref[0] = state_scr[...]

    yc = (y + xs * dsk_ref[...]) * jax.nn.silu(z_ref[0])
    y_ref[0] = _rms(yc, gs_ref[...]).astype(y_ref.dtype)


def _ssd(proj3, s0, cs0, conv_w, conv_b, dtb_full, a_log, d_skip, g_ssm):
    B, L, _ = proj3.shape
    lc = _tile(L, 256)
    nch = L // lc
    rep = lambda p: jnp.repeat(p, C_HEAD_DIM).reshape(1, C_INNER)
    kern = functools.partial(_ssd_kernel, lc=lc, nchunks=nch)
    cst = lambda shape: pl.BlockSpec(shape, lambda b, c: (0,) * len(shape))
    return pl.pallas_call(
        kern,
        grid=(B, nch),
        in_specs=[pl.BlockSpec((1, lc, C_INNER), lambda b, c: (b, c, COL_ZC // C_INNER)),
                  pl.BlockSpec((1, lc, C_CONV_DIM), lambda b, c: (b, c, COL_XBC // C_CONV_DIM)),
                  pl.BlockSpec((1, lc, C_INNER), lambda b, c: (b, c, COL_DT // C_INNER)),
                  pl.BlockSpec((1, C_STATE, C_INNER), lambda b, c: (b, 0, 0)),
                  pl.BlockSpec((1, 8, C_CONV_DIM), lambda b, c: (b, 0, 0)),
                  cst((C_CONV, C_CONV_DIM)), cst((1, C_CONV_DIM)), cst((1, C_INNER)), cst((1, C_INNER)),
                  cst((1, C_INNER)), cst((1, C_INNER))],
        out_specs=[pl.BlockSpec((1, lc, C_INNER), lambda b, c: (b, c, 0)),
                   pl.BlockSpec((1, C_STATE, C_INNER), lambda b, c: (b, 0, 0))],
        out_shape=[jax.ShapeDtypeStruct((B, L, C_INNER), BF16),
                   jax.ShapeDtypeStruct((B, C_STATE, C_INNER), F32)],
        scratch_shapes=[pltpu.VMEM((C_STATE, C_INNER), F32),
                        pltpu.VMEM((lc + 8, C_CONV_DIM), F32)],
        compiler_params=_cparams("parallel", "arbitrary"),
        name="ssd",
    )(proj3, proj3, proj3, s0, cs0, conv_w, conv_b.reshape(1, C_CONV_DIM), dtb_full, rep(a_log), rep(d_skip),
      g_ssm.reshape(1, C_INNER))


def _outproj_kernel(h_ref, ya_ref, yb_ref, yc_ref, w_ref, o_ref):
    acc = jnp.dot(ya_ref[...], w_ref[0:A_W, :], preferred_element_type=F32)
    acc = acc + jnp.dot(yb_ref[...], w_ref[A_W:A_W + B_W, :], preferred_element_type=F32)
    acc = acc + jnp.dot(yc_ref[...], w_ref[A_W + B_W:, :], preferred_element_type=F32)
    o_ref[...] = h_ref[...] + acc


def _outproj(h2d, ya, yb, yc, w_out_bf16):
    T = h2d.shape[0]
    tm = _tile(T, 512)
    row = lambda w: pl.BlockSpec((tm, w), lambda i: (i, 0))
    return pl.pallas_call(
        _outproj_kernel,
        grid=(T // tm,),
        in_specs=[row(D_MODEL), row(A_W), row(B_W), row(C_INNER),
                  pl.BlockSpec((D_MODEL, D_MODEL), lambda i: (0, 0))],
        out_specs=row(D_MODEL),
        out_shape=jax.ShapeDtypeStruct((T, D_MODEL), F32),
        compiler_params=_cparams("parallel"),
        name="outproj",
    )(h2d, ya, yb, yc, w_out_bf16)


PEER_SUB = 512
PEER_STEP = 2048
PEER_CAND_COLS = 4


def _gelu_tanh(x):
    k0 = math.sqrt(2.0 / math.pi)
    inner = x * (k0 + (k0 * 0.044715) * (x * x))
    return (0.5 * x) * (1.0 + jnp.tanh(inner))


def _top16(x, want_rank):
    vals = []
    rank = jnp.full(x.shape, float(PEER_TOPK), F32) if want_rank else None
    for r in range(PEER_TOPK):
        m = jnp.max(x, axis=0, keepdims=True)
        vals.append(m)
        hit = x == m
        if want_rank:
            rank = jnp.where(hit, float(r), rank)
        x = jnp.where(hit, NEG_BIG, x)
    return jnp.concatenate(vals, axis=0), rank


def _sort16_pairs():
    def merge(lo, hi, r):
        step = r * 2
        if step < hi - lo:
            yield from merge(lo, hi, step)
            yield from merge(lo + r, hi, step)
            yield from [(i, i + r) for i in range(lo + r, hi - r, step)]
        else:
            yield (lo, lo + r)

    def sort(lo, hi):
        if hi - lo >= 1:
            mid = lo + (hi - lo) // 2
            yield from sort(lo, mid)
            yield from sort(mid + 1, hi)
            yield from merge(lo, hi, 1)

    return tuple(sort(0, PEER_TOPK - 1))


def _top16_values(x):
    v = [x[8 * g:8 * g + 8, :] for g in range(PEER_NKEYS // 8)]
    for a, b in _sort16_pairs():
        v[a], v[b] = jnp.maximum(v[a], v[b]), jnp.minimum(v[a], v[b])
    vals = []
    for r in range(PEER_TOPK):
        m = jnp.max(v[0], axis=0, keepdims=True)
        vals.append(m)
        hit = v[0] == m
        for k in range(PEER_TOPK - 1 - r):
            v[k] = jnp.where(hit, v[k + 1], v[k])
    return jnp.concatenate(vals, axis=0)


def _peer_select(s1, s2):
    n = s1.shape[1]
    a16 = _top16_values(s1)
    b16, rank2 = _top16(s2, True)
    ridx = lax.broadcasted_iota(jnp.int32, (PEER_TOPK, n), 0)
    cands = []
    for r2 in range(PEER_CAND_COLS):
        cands.append(jnp.where((ridx + 1) * (r2 + 1) <= PEER_TOPK, a16 + b16[r2:r2 + 1, :], NEG_BIG))
    for r1 in range(PEER_TOPK // (PEER_CAND_COLS + 1)):
        ok = (ridx >= PEER_CAND_COLS) & ((ridx + 1) * (r1 + 1) <= PEER_TOPK)
        cands.append(jnp.where(ok, b16 + a16[r1:r1 + 1, :], NEG_BIG))
    cand = jnp.concatenate(cands, axis=0)
    x = cand
    tau = None
    for r in range(PEER_TOPK):
        tau = jnp.max(x, axis=0, keepdims=True)
        x = jnp.where(x == tau, NEG_BIG, x)
    top = a16[0:1, :] + b16[0:1, :]
    z = jnp.sum(jnp.where(cand >= tau, jnp.exp(cand - top), 0.0), axis=0, keepdims=True)
    cnt = jnp.zeros(s1.shape, F32)
    for r2 in range(PEER_CAND_COLS):
        cnt = cnt + jnp.where(s1 + b16[r2:r2 + 1, :] >= tau, 1.0, 0.0)
    for r1 in range(PEER_TOPK // (PEER_CAND_COLS + 1)):
        a_r = a16[r1:r1 + 1, :]
        tail = jnp.where((ridx >= PEER_CAND_COLS) & (b16 + a_r >= tau), 1.0, 0.0)
        cnt = cnt + jnp.where(s1 == a_r, jnp.sum(tail, axis=0, keepdims=True), 0.0)
    a = jnp.exp(s1 - a16[0:1, :]) / z
    b = jnp.exp(s2 - b16[0:1, :])
    return a, cnt, rank2, b


def _peer_kernel(h_ref, g_ref, wq_ref, k1_ref, k2_ref, u_ref, vt_ref, o_ref,
                 xt_scr, a_scr, cnt_scr, rank_scr, b_scr, acc_scr, *, tm, te, ne):
    j = pl.program_id(1)
    n1 = te // PEER_NKEYS
    nchunk = tm // LANES
    nsub = te // PEER_SUB
    slabs = PEER_SUB // PEER_NKEYS

    def gate_weights(sb):
        ws = []
        for il in range(slabs):
            i1 = j * n1 + sb * slabs + il
            w = jnp.zeros((PEER_NKEYS, tm), BF16)
            for h in range(PEER_HEADS):
                row = lambda ref: jnp.concatenate(
                    [jnp.broadcast_to(ref[h, cch, pl.ds(i1, 1), :], (BF16_ROWS, LANES)) for cch in range(nchunk)],
                    axis=1).astype(BF16)
                cnt_b = jnp.tile(row(cnt_scr), (PEER_NKEYS // BF16_ROWS, 1))
                a_b = jnp.tile(row(a_scr), (PEER_NKEYS // BF16_ROWS, 1))
                w = w + jnp.where(rank_scr[h] < cnt_b, b_scr[h], jnp.zeros((), BF16)) * a_b
            ws.append(w)
        return jnp.concatenate(ws, axis=0)

    @pl.when(j == 0)
    def _():
        xn = _rms(h_ref[...], g_ref[...])
        xt_scr[...] = xn.T.astype(BF16)
        acc_scr[...] = jnp.zeros(acc_scr.shape, F32)

        def head_body(h, carry):
            base = pl.multiple_of(h * PEER_KEY_DIM, PEER_KEY_DIM)
            k1 = k1_ref[h].astype(BF16)
            k2 = k2_ref[h].astype(BF16)
            qh = jnp.dot(wq_ref[pl.ds(base, PEER_KEY_DIM), :], xt_scr[...], preferred_element_type=F32)
            for cch in range(nchunk):
                sl = slice(cch * LANES, (cch + 1) * LANES)
                q1 = qh[0:PEER_HALF, sl].astype(BF16)
                q2 = qh[PEER_HALF:, sl].astype(BF16)
                s1 = jnp.dot(k1, q1, preferred_element_type=F32)
                s2 = jnp.dot(k2, q2, preferred_element_type=F32)
                a, cnt, rank2, b = _peer_select(s1, s2)
                a_scr[h, cch] = a
                cnt_scr[h, cch] = cnt
                rank_scr[h, :, sl] = rank2.astype(BF16)
                b_scr[h, :, sl] = b.astype(BF16)
            return carry

        lax.fori_loop(0, PEER_HEADS, head_body, 0)

    def mm1(sb):
        return jnp.dot(u_ref[sb * PEER_SUB:(sb + 1) * PEER_SUB, :], xt_scr[...], preferred_element_type=F32)

    def mm2(sb, act):
        return jnp.dot(vt_ref[:, sb * PEER_SUB:(sb + 1) * PEER_SUB], act, preferred_element_type=F32)

    acc = acc_scr[...]
    ht = mm1(0)
    for sb in range(nsub):
        ht_next = mm1(sb + 1) if sb + 1 < nsub else None
        act = _gelu_tanh(ht.astype(BF16)) * gate_weights(sb)
        acc = acc + mm2(sb, act)
        ht = ht_next
    acc_scr[...] = acc

    @pl.when(j == ne - 1)
    def _():
        o_ref[...] = h_ref[...] + acc_scr[...].T


def _peer(h2d, g, wq_t, k1, k2, u_bf16, vt_bf16):
    T = h2d.shape[0]
    tm = _tile(T, 512)
    assert tm % LANES == 0
    te = PEER_STEP
    ne = PEER_EXPERTS // te
    kern = functools.partial(_peer_kernel, tm=tm, te=te, ne=ne)
    hk = (PEER_HEADS, PEER_NKEYS, tm)
    hkc = (PEER_HEADS, tm // LANES, PEER_NKEYS, LANES)
    return pl.pallas_call(
        kern,
        grid=(T // tm, ne),
        in_specs=[pl.BlockSpec((tm, D_MODEL), lambda i, j: (i, 0)),
                  pl.BlockSpec((1, D_MODEL), lambda i, j: (0, 0)),
                  pl.BlockSpec((PEER_HEADS * PEER_KEY_DIM, D_MODEL), lambda i, j: (0, 0)),
                  pl.BlockSpec((PEER_HEADS, PEER_NKEYS, PEER_HALF), lambda i, j: (0, 0, 0)),
                  pl.BlockSpec((PEER_HEADS, PEER_NKEYS, PEER_HALF), lambda i, j: (0, 0, 0)),
                  pl.BlockSpec((te, D_MODEL), lambda i, j: (j, 0)),
                  pl.BlockSpec((D_MODEL, te), lambda i, j: (0, j))],
        out_specs=pl.BlockSpec((tm, D_MODEL), lambda i, j: (i, 0)),
        out_shape=jax.ShapeDtypeStruct((T, D_MODEL), F32),
        scratch_shapes=[pltpu.VMEM((D_MODEL, tm), BF16),
                        pltpu.VMEM(hkc, F32), pltpu.VMEM(hkc, F32), pltpu.VMEM(hk, BF16), pltpu.VMEM(hk, BF16),
                        pltpu.VMEM((D_MODEL, tm), F32)],
        compiler_params=_cparams("parallel", "arbitrary"),
        name="peer",
    )(h2d, g.reshape(1, D_MODEL), wq_t, k1, k2, u_bf16, vt_bf16)


def _ple_kernel(h_ref, p_ref, g_ref, wg_ref, wp_ref, gf_ref, o_ref, *, final_norm):
    h = h_ref[...]
    xn = _rms(h, g_ref[...])
    gate = jax.nn.sigmoid(jnp.dot(xn.astype(BF16), wg_ref[...], preferred_element_type=F32))
    emb = jnp.dot(p_ref[...].astype(BF16), wp_ref[...], preferred_element_type=F32)
    out = h + emb * gate
    if final_norm:
        out = _rms(out, gf_ref[...])
    o_ref[...] = out


def _ple(h2d, p2d, g, wg_bf16, wp_bf16, g_final, final_norm):
    T = h2d.shape[0]
    tm = _tile(T, 512)
    kern = functools.partial(_ple_kernel, final_norm=final_norm)
    return pl.pallas_call(
        kern,
        grid=(T // tm,),
        in_specs=[pl.BlockSpec((tm, D_MODEL), lambda i: (i, 0)),
                  pl.BlockSpec((tm, PLE_DIM), lambda i: (i, 0)),
                  pl.BlockSpec((1, D_MODEL), lambda i: (0, 0)),
                  pl.BlockSpec((D_MODEL, D_MODEL), lambda i: (0, 0)),
                  pl.BlockSpec((PLE_DIM, D_MODEL), lambda i: (0, 0)),
                  pl.BlockSpec((1, D_MODEL), lambda i: (0, 0))],
        out_specs=pl.BlockSpec((tm, D_MODEL), lambda i: (i, 0)),
        out_shape=jax.ShapeDtypeStruct((T, D_MODEL), F32),
        compiler_params=_cparams("parallel"),
        name="ple",
    )(h2d, p2d, g.reshape(1, D_MODEL), wg_bf16, wp_bf16, g_final.reshape(1, D_MODEL))


def _pad_to(x, n, axis):
    pad = n - x.shape[axis]
    if pad == 0:
        return x
    widths = [(0, 0)] * x.ndim
    widths[axis] = (0, pad)
    return jnp.pad(x, widths)


def _run_trunk(x, p, past, weights):
    (g_mix, w_in, b_forget, g_ret, conv_w, conv_b, dt_bias, a_log, d_skip, g_ssm, w_out,
     g_ffn, peer_wq, peer_k1, peer_k2, peer_u, peer_v, g_ple, w_ple_gate, w_ple, g_final) = weights
    B, L, _ = x.shape
    depth = w_in.shape[0]
    T = B * L
    P = 0 if past is None else past[0].shape[2]
    half = B_HEAD_DIM // 2
    invf = jnp.tile(ROPE_BASE ** (-jnp.arange(half, dtype=F32) / half), LANES // half).reshape(1, LANES)

    h = x.reshape(T, D_MODEL)
    outs = []
    for i in range(depth):
        w_proj, bf_pad, dtb_full = _prep_w_in(w_in[i], b_forget[i], dt_bias[i])
        proj, lf, kb, vb = _inproj(h, g_mix[i], w_proj, bf_pad)
        proj3 = proj.reshape(B, L, PROJ_W)
        lf3 = lf.reshape(B, L, LANES)
        kb = kb.reshape(B, L, A_W)
        vb = vb.reshape(B, L, A_W)
        k_new = proj3[:, :, COL_KA:COL_KA + A_W]
        v_new = proj3[:, :, COL_VA:COL_VA + A_W]

        if past is None:
            eq, ek = _cumsum(lf3)
            ya = _fox(proj3, COL_QA, eq, kb, ek, vb, 0)
            rs0 = jnp.zeros((B, B_W, B_W), F32)
            ss0 = jnp.zeros((B, C_STATE, C_INNER), F32)
            cs0 = jnp.zeros((B, 8, C_CONV_DIM), F32)
        else:
            lk_pad = -(-(P + L) // FOX_TK) * FOX_TK
            lq_pad = -(-L // LANES) * LANES
            past_k = past[0][i].reshape(B, P, A_W).astype(BF16)
            past_v = past[1][i].reshape(B, P, A_W).astype(BF16)
            past_lf = _pad_to(past[2][i], LANES, 2)
            k_all = _pad_to(jnp.concatenate([past_k, kb], axis=1), lk_pad, 1)
            v_all = _pad_to(jnp.concatenate([past_v, vb], axis=1), lk_pad, 1)
            lf_all = _pad_to(jnp.concatenate([past_lf, lf3], axis=1), lk_pad, 1)
            eq, ek = _cumsum(lf_all)
            q_pad = _pad_to(proj3[:, :, COL_QA:COL_QA + A_W], lq_pad, 1)
            ya = _fox(q_pad, 0, _pad_to(eq[:, P:P + L], lq_pad, 1), k_all, ek, v_all, P)[:, :L]
            rs0 = _ret_state_to_kernel(past[3][i])
            ss0 = jnp.transpose(past[4][i], (0, 2, 1, 3)).reshape(B, C_STATE, C_INNER)
            cs0 = jnp.concatenate([jnp.zeros((B, 8 - (C_CONV - 1), C_CONV_DIM), F32), past[5][i]], axis=1)

        yb, ret_k = _retention(proj3, rs0, invf, g_ret[i], P)
        ret_new = _ret_state_from_kernel(ret_k)

        yc, ssm_k = _ssd(proj3, ss0, cs0, conv_w[i], conv_b[i], dtb_full, a_log[i], d_skip[i], g_ssm[i])
        ssm_new = jnp.transpose(ssm_k.reshape(B, C_STATE, C_HEADS, C_HEAD_DIM), (0, 2, 1, 3))
        xbc = proj3[:, :, COL_XBC:COL_XBC + C_CONV_DIM]
        if L >= C_CONV - 1:
            conv_new = xbc[:, L - (C_CONV - 1):]
        else:
            prev = jnp.zeros((B, C_CONV - 1, C_CONV_DIM), F32) if past is None else past[5][i]
            conv_new = jnp.concatenate([prev, xbc], axis=1)[:, -(C_CONV - 1):]

        h = _outproj(h, ya.reshape(T, A_W), yb.reshape(T, B_W), yc.reshape(T, C_INNER), w_out[i].astype(BF16))
        h = _peer(h, g_ffn[i], peer_wq[i].T.astype(BF16), peer_k1[i], peer_k2[i],
                  peer_u[i].astype(BF16), peer_v[i].T.astype(BF16))
        h = _ple(h, p[i].reshape(T, PLE_DIM), g_ple[i], w_ple_gate[i].astype(BF16), w_ple[i].astype(BF16),
                 g_final, final_norm=(i == depth - 1))
        outs.append((k_new.reshape(B, L, A_HEADS, A_HEAD_DIM), v_new.reshape(B, L, A_HEADS, A_HEAD_DIM),
                     lf3[:, :, :A_HEADS], ret_new, ssm_new, conv_new))
    stacked = tuple(jnp.stack([o[n] for o in outs]) for n in range(6))
    return h.reshape(B, L, D_MODEL), stacked


def kernel(x_prompt, x_sample, cache_k_fox, cache_v_fox, cache_logf_fox, state_ret, state_ssm, state_conv, p_prompt, p_sample, g_mix, w_in, b_forget, g_ret, conv_w, conv_b, dt_bias, a_log, d_skip, g_ssm, w_out, g_ffn, peer_wq, peer_k1, peer_k2, peer_u, peer_v, g_ple, w_ple_gate, w_ple, g_final):
    weights = (g_mix, w_in, b_forget, g_ret, conv_w, conv_b, dt_bias, a_log, d_skip, g_ssm, w_out,
               g_ffn, peer_wq, peer_k1, peer_k2, peer_u, peer_v, g_ple, w_ple_gate, w_ple, g_final)
    y_prompt, (k_p, v_p, lf_p, ret_p, ssm_p, conv_p) = _run_trunk(x_prompt, p_prompt, None, weights)
    past = (cache_k_fox, cache_v_fox, cache_logf_fox, state_ret, state_ssm, state_conv)
    y_sample, (k_s, v_s, lf_s, ret_s, ssm_s, conv_s) = _run_trunk(x_sample, p_sample, past, weights)
    return (y_prompt, y_sample, k_p, v_p, lf_p, ret_p, ssm_p, conv_p, k_s, v_s, lf_s, ret_s, ssm_s, conv_s)
```

```python
import functools
import math

import jax
import jax.numpy as jnp
from jax import lax
from jax.experimental import pallas as pl
from jax.experimental.pallas import tpu as pltpu

F32 = jnp.float32
BF16 = jnp.bfloat16

D_MODEL = 1024
PLE_DIM = 256
EPS = 1e-6
A_HEADS, A_HEAD_DIM = 4, 64
B_HEADS, B_HEAD_DIM = 4, 64
ROPE_BASE = 10000.0
C_HEADS, C_HEAD_DIM = 8, 64
C_INNER = C_HEADS * C_HEAD_DIM
C_GROUPS, C_STATE, C_CONV = 2, 128, 4
C_CONV_DIM = C_INNER + 2 * C_GROUPS * C_STATE
PEER_HEADS, PEER_NKEYS, PEER_KEY_DIM, PEER_TOPK = 8, 128, 256, 16
PEER_HALF = PEER_KEY_DIM // 2
PEER_EXPERTS = PEER_NKEYS * PEER_NKEYS

LANES = 128
BF16_ROWS = 16
A_W = A_HEADS * A_HEAD_DIM
B_W = B_HEADS * B_HEAD_DIM
G_W = C_GROUPS * C_STATE

COL_QA = 0
COL_KA = COL_QA + A_W
COL_VA = COL_KA + A_W
COL_QB = COL_VA + A_W
COL_KB = COL_QB + B_W
COL_VB = COL_KB + B_W
COL_GB = COL_VB + B_W
COL_FA = COL_GB + B_W
COL_ZC = COL_FA + 2 * LANES
COL_DT = COL_ZC + C_INNER
COL_XBC = COL_DT + C_INNER
PROJ_W = COL_XBC + C_CONV_DIM
assert all(c % A_W == 0 for c in (COL_QA, COL_KA, COL_VA, COL_QB, COL_KB, COL_VB, COL_GB))
assert COL_ZC % C_INNER == 0 and COL_DT % C_INNER == 0 and COL_XBC % C_CONV_DIM == 0 and COL_FA % LANES == 0

NEG_BIG = -1e30
VMEM_LIMIT_BYTES = 52 * 1024 * 1024


def _cparams(*sem):
    return pltpu.CompilerParams(dimension_semantics=sem, vmem_limit_bytes=VMEM_LIMIT_BYTES)


def _tile(n, pref):
    t = min(n, pref)
    while n % t:
        t -= 8
    assert t > 0
    return t


def _rms(x, g):
    return x * lax.rsqrt(jnp.mean(x * x, axis=-1, keepdims=True) + EPS) * g


def _split3(x):
    hi = x.astype(BF16)
    r1 = x - hi.astype(F32)
    mid = r1.astype(BF16)
    lo = (r1 - mid.astype(F32)).astype(BF16)
    return hi, mid, lo


def _dot_sel(sel_bf16, x):
    hi, mid, lo = _split3(x)
    d = lambda p: jnp.dot(sel_bf16, p, preferred_element_type=F32)
    return d(hi) + d(mid) + d(lo)


def _dot_sel_rhs(x, sel_bf16):
    hi, mid, lo = _split3(x)
    d = lambda p: jnp.dot(p, sel_bf16, preferred_element_type=F32)
    return d(hi) + d(mid) + d(lo)


def _inproj_kernel(x_ref, g_ref, w_ref, bf_ref, o_ref, lf_ref, kb_ref, vb_ref):
    xn = _rms(x_ref[...], g_ref[...])
    y = jnp.dot(xn.astype(BF16), w_ref[...], preferred_element_type=F32)
    o_ref[...] = y
    kb_ref[...] = y[:, COL_KA:COL_KA + A_W].astype(BF16)
    vb_ref[...] = y[:, COL_VA:COL_VA + A_W].astype(BF16)
    fa = y[:, COL_FA:COL_FA + LANES] + bf_ref[...]
    lane = lax.broadcasted_iota(jnp.int32, fa.shape, 1)
    lf_ref[...] = jnp.where(lane < A_HEADS, jax.nn.log_sigmoid(fa), 0.0)


def _inproj(x2d, g, w, bf_pad):
    T = x2d.shape[0]
    tm = _tile(T, 256)
    return pl.pallas_call(
        _inproj_kernel,
        grid=(T // tm,),
        in_specs=[pl.BlockSpec((tm, D_MODEL), lambda i: (i, 0)),
                  pl.BlockSpec((1, D_MODEL), lambda i: (0, 0)),
                  pl.BlockSpec((D_MODEL, PROJ_W), lambda i: (0, 0)),
                  pl.BlockSpec((1, LANES), lambda i: (0, 0))],
        out_specs=[pl.BlockSpec((tm, PROJ_W), lambda i: (i, 0)),
                   pl.BlockSpec((tm, LANES), lambda i: (i, 0)),
                   pl.BlockSpec((tm, A_W), lambda i: (i, 0)),
                   pl.BlockSpec((tm, A_W), lambda i: (i, 0))],
        out_shape=[jax.ShapeDtypeStruct((T, PROJ_W), F32),
                   jax.ShapeDtypeStruct((T, LANES), F32),
                   jax.ShapeDtypeStruct((T, A_W), BF16),
                   jax.ShapeDtypeStruct((T, A_W), BF16)],
        compiler_params=_cparams("parallel"),
        name="inproj",
    )(x2d, g.reshape(1, D_MODEL), w, bf_pad)


def _prep_w_in(w_in, b_forget, dt_bias):
    sizes = (A_W, A_W, A_W, A_HEADS, B_W, B_W, B_W, B_W, C_INNER, C_INNER, G_W, G_W, C_HEADS)
    pts = []
    acc = 0
    for s in sizes[:-1]:
        acc += s
        pts.append(acc)
    qa, ka, va, fa, qb, kb, vb, gb, zc, xc, bc, cc, dtc = jnp.split(w_in, pts, axis=-1)

    def rot_perm(w):
        w = w.reshape(D_MODEL, B_HEADS, 2, B_HEAD_DIM // 2)
        return jnp.transpose(w, (0, 2, 1, 3)).reshape(D_MODEL, B_W)

    z = lambda n: jnp.zeros((D_MODEL, n), w_in.dtype)
    w = jnp.concatenate([qa, ka, va, rot_perm(qb), rot_perm(kb), vb, gb, fa, z(2 * LANES - A_HEADS), zc,
                         jnp.repeat(dtc, C_HEAD_DIM, axis=1), xc, bc, cc], axis=1)
    assert w.shape[1] == PROJ_W
    bf_pad = jnp.concatenate([b_forget, jnp.zeros((LANES - A_HEADS,), F32)]).reshape(1, LANES)
    dtb_full = jnp.repeat(dt_bias, C_HEAD_DIM).reshape(1, C_INNER)
    return w.astype(BF16), bf_pad, dtb_full


FOX_TQ = 512
FOX_TK = 512
FOX_XTRA = 6


def _fox_xtra_base(h):
    return A_HEAD_DIM * ((h + 1) % A_HEADS)


def _cumsum_kernel(lf_ref, eq_ref, ek_ref, carry_scr):
    @pl.when(pl.program_id(1) == 0)
    def _():
        carry_scr[...] = jnp.zeros_like(carry_scr)

    lf = lf_ref[0]
    tc = lf.shape[0]
    r = lax.broadcasted_iota(jnp.int32, (tc, tc), 0)
    c = lax.broadcasted_iota(jnp.int32, (tc, tc), 1)
    tri = jnp.where(r >= c, 1.0, 0.0).astype(BF16)
    cum = _dot_sel(tri, lf) + carry_scr[0:1, :]
    carry_scr[...] = jnp.broadcast_to(cum[tc - 1:tc, :], carry_scr.shape)

    pieces = _split3(cum)
    src = lax.broadcasted_iota(jnp.int32, (LANES, A_W), 0)
    dst = lax.broadcasted_iota(jnp.int32, (LANES, A_W), 1)
    lane = lax.broadcasted_iota(jnp.int32, (1, A_W), 1)
    eq = jnp.zeros((tc, A_W), F32)
    ek = jnp.zeros((tc, A_W), F32)
    one_q = jnp.zeros((1, A_W), F32)
    one_k = jnp.zeros((1, A_W), F32)
    for h in range(A_HEADS):
        base = _fox_xtra_base(h)
        for k, piece in enumerate(pieces):
            place_q = jnp.where((src == h) & (dst == base + k), 1.0, 0.0).astype(BF16)
            place_k = jnp.where((src == h) & (dst == base + 3 + k), -1.0, 0.0).astype(BF16)
            eq = eq + jnp.dot(piece, place_q, preferred_element_type=F32)
            ek = ek + jnp.dot(piece, place_k, preferred_element_type=F32)
        one_q = jnp.where((lane >= base + 3) & (lane < base + FOX_XTRA), 1.0, one_q)
        one_k = jnp.where((lane >= base) & (lane < base + 3), 1.0, one_k)
    eq_ref[0] = (eq + one_q).astype(BF16)
    ek_ref[0] = (ek + one_k).astype(BF16)


def _cumsum(lf):
    B, L, _ = lf.shape
    tc = _tile(L, 512)
    return pl.pallas_call(
        _cumsum_kernel,
        grid=(B, L // tc),
        in_specs=[pl.BlockSpec((1, tc, LANES), lambda b, j: (b, j, 0))],
        out_specs=[pl.BlockSpec((1, tc, A_W), lambda b, j: (b, j, 0)),
                   pl.BlockSpec((1, tc, A_W), lambda b, j: (b, j, 0))],
        out_shape=[jax.ShapeDtypeStruct((B, L, A_W), BF16),
                   jax.ShapeDtypeStruct((B, L, A_W), BF16)],
        scratch_shapes=[pltpu.VMEM((8, LANES), F32)],
        compiler_params=_cparams("parallel", "arbitrary"),
        name="fox_cumsum",
    )(lf)


def _fox_kernel(q_ref, eq_ref, kb_ref, ek_ref, vt_ref, o_ref, qp_scr, m_scr, l_scr, acc_scr, *,
                tq, tk, offset, nk):
    i = pl.program_id(1)
    scale = A_HEAD_DIM ** -0.5
    q_lo = offset + i * tq

    def in_head(lane, h):
        return (lane >= h * A_HEAD_DIM) & (lane < (h + 1) * A_HEAD_DIM)

    def in_xtra(lane, h):
        return (lane >= _fox_xtra_base(h)) & (lane < _fox_xtra_base(h) + FOX_XTRA)

    lane_q = lax.broadcasted_iota(jnp.int32, (tq, A_W), 1)
    q = (q_ref[0] * scale).astype(BF16)
    eq = eq_ref[0]
    for h in range(A_HEADS):
        qp_scr[h] = jnp.where(in_xtra(lane_q, h), eq, jnp.where(in_head(lane_q, h), q, jnp.zeros((), BF16)))
    m_scr[...] = jnp.full(m_scr.shape, NEG_BIG, F32)
    l_scr[...] = jnp.zeros(l_scr.shape, F32)
    acc_scr[...] = jnp.zeros(acc_scr.shape, F32)

    lane_k = lax.broadcasted_iota(jnp.int32, (tk, A_W), 1)

    def block(jj, masked):
        ks = pl.multiple_of(jj * tk, tk)
        k = kb_ref[0, pl.ds(ks, tk), :]
        ek = ek_ref[0, pl.ds(ks, tk), :]
        vt = vt_ref[0, jj]
        if masked:
            kpos = ks + lax.broadcasted_iota(jnp.int32, (tk, tq), 0)
            qpos = q_lo + lax.broadcasted_iota(jnp.int32, (tk, tq), 1)
            vis = kpos <= qpos
        ps, alphas = [], []
        for h in range(A_HEADS):
            kp = jnp.where(in_xtra(lane_k, h), ek, k)
            s = lax.dot_general(kp, qp_scr[h], (((1,), (1,)), ((), ())), preferred_element_type=F32)
            if masked:
                s = jnp.where(vis, s, NEG_BIG)
            m_prev = m_scr[h, 0:1, :]
            m_next = jnp.maximum(m_prev, jnp.max(s, axis=0, keepdims=True))
            alpha = jnp.exp(m_prev - m_next)
            p = jnp.exp(s - m_next)
            l_scr[h] = jnp.broadcast_to(alpha * l_scr[h, 0:1, :] + jnp.sum(p, axis=0, keepdims=True), (8, tq))
            m_scr[h] = jnp.broadcast_to(m_next, (8, tq))
            ps.append(p.astype(BF16))
            alphas.append(alpha)
        pv = jnp.dot(vt, jnp.concatenate(ps, axis=1), preferred_element_type=F32)
        for h in range(A_HEADS):
            acc_scr[h] = acc_scr[h] * alphas[h] + pv[:, h * tq:(h + 1) * tq]

    n_full = (q_lo + 1) // tk
    n_vis = jnp.minimum((q_lo + tq - 1) // tk + 1, nk)

    def full_body(jj, carry):
        block(jj, False)
        return carry

    def edge_body(jj, carry):
        block(jj, True)
        return carry

    lax.fori_loop(0, n_full, full_body, 0)
    lax.fori_loop(n_full, n_vis, edge_body, 0)

    row = lax.broadcasted_iota(jnp.int32, (A_W, tq), 0)
    out_t = jnp.zeros((A_W, tq), F32)
    for h in range(A_HEADS):
        out_t = out_t + jnp.where(in_head(row, h), acc_scr[h] / l_scr[h, 0:1, :], 0.0)
    o_ref[0] = out_t.T.astype(o_ref.dtype)


def _fox(q_src, q_col, eq, kb, ek, vb, offset):
    B, Lq, _ = q_src.shape
    Lk = kb.shape[1]
    tq = _tile(Lq, FOX_TQ)
    tk = _tile(Lk, FOX_TK)
    assert tq % LANES == 0 and tk % LANES == 0
    nq, nk = Lq // tq, Lk // tk
    vt = jnp.transpose(vb.reshape(B, nk, tk, A_W), (0, 1, 3, 2))
    whole = lambda b, i: (b, 0, 0)
    kern = functools.partial(_fox_kernel, tq=tq, tk=tk, offset=offset, nk=nk)
    return pl.pallas_call(
        kern,
        grid=(B, nq),
        in_specs=[pl.BlockSpec((1, tq, A_W), lambda b, i: (b, i, q_col // A_W)),
                  pl.BlockSpec((1, tq, A_W), lambda b, i: (b, i, 0)),
                  pl.BlockSpec((1, Lk, A_W), whole),
                  pl.BlockSpec((1, Lk, A_W), whole),
                  pl.BlockSpec((1, nk, A_W, tk), lambda b, i: (b, 0, 0, 0))],
        out_specs=pl.BlockSpec((1, tq, A_W), lambda b, i: (b, i, 0)),
        out_shape=jax.ShapeDtypeStruct((B, Lq, A_W), BF16),
        scratch_shapes=[pltpu.VMEM((A_HEADS, tq, A_W), BF16),
                        pltpu.VMEM((A_HEADS, 8, tq), F32),
                        pltpu.VMEM((A_HEADS, 8, tq), F32),
                        pltpu.VMEM((A_HEADS, A_W, tq), F32)],
        compiler_params=_cparams("parallel", "arbitrary"),
        name="fox_attention",
    )(q_src, eq, kb, ek, vt)


def _ret_lane_head_v():
    return lax.broadcasted_iota(jnp.int32, (1, B_W), 1) // B_HEAD_DIM


def _ret_lane_head_qk():
    return (lax.broadcasted_iota(jnp.int32, (1, B_W), 1) % LANES) // (B_HEAD_DIM // 2)


def _log_gamma(h):
    return math.log1p(-2.0 ** (-5.0 - h))


def _ret_kernel(q_ref, k_ref, v_ref, g_ref, s0_ref, invf_ref, gret_ref, y_ref, sout_ref,
                state_scr, dec_scr, *, lc, offset, nchunks):
    c = pl.program_id(1)

    @pl.when(c == 0)
    def _():
        state_scr[...] = s0_ref[0]
        r = lax.broadcasted_iota(jnp.int32, (lc, lc), 0)
        s = lax.broadcasted_iota(jnp.int32, (lc, lc), 1)
        d = (r - s).astype(F32)
        for h in range(B_HEADS):
            dec_scr[h] = jnp.where(r >= s, jnp.exp(d * _log_gamma(h)), 0.0)

    hv = _ret_lane_head_v()
    hqk = _ret_lane_head_qk()
    lg_v = jnp.zeros((1, B_W), F32)
    lg_qk = jnp.zeros((1, B_W), F32)
    for h in range(B_HEADS):
        lg_v = jnp.where(hv == h, _log_gamma(h), lg_v)
        lg_qk = jnp.where(hqk == h, _log_gamma(h), lg_qk)

    pos = (offset + c * lc + lax.broadcasted_iota(jnp.int32, (lc, LANES), 0)).astype(F32)
    ang = pos * invf_ref[...]
    cos, sin = jnp.cos(ang), jnp.sin(ang)

    def rot(x):
        x1, x2 = x[:, :LANES], x[:, LANES:]
        return jnp.concatenate([x1 * cos - x2 * sin, x1 * sin + x2 * cos], axis=1)

    qr = rot(q_ref[0])
    kr = rot(k_ref[0]) * (B_HEAD_DIM ** -0.5)
    v = v_ref[0]
    vb = v.astype(BF16)
    krb = kr.astype(BF16)

    lpos = lax.broadcasted_iota(jnp.int32, (lc, B_W), 0).astype(F32)
    state = state_scr[...]
    y = jnp.dot(qr.astype(BF16), state.astype(BF16), preferred_element_type=F32) * jnp.exp((lpos + 1.0) * lg_v)
    for h in range(B_HEADS):
        qh = jnp.where(hqk == h, qr, 0.0).astype(BF16)
        sc = lax.dot_general(qh, krb, (((1,), (1,)), ((), ())), preferred_element_type=F32) * dec_scr[h]
        yh = jnp.dot(sc.astype(BF16), vb, preferred_element_type=F32)
        y = y + jnp.where(hv == h, yh, 0.0)

    kw = (kr * jnp.exp((lc - 1.0 - lpos) * lg_qk)).astype(BF16)
    upd = lax.dot_general(kw, vb, (((0,), (0,)), ((), ())), preferred_element_type=F32)
    row_head = (lax.broadcasted_iota(jnp.int32, (B_W, B_W), 0) % LANES) // (B_HEAD_DIM // 2)
    col_head = lax.broadcasted_iota(jnp.int32, (B_W, B_W), 1) // B_HEAD_DIM
    lg_rows = jnp.zeros((B_W, B_W), F32)
    for h in range(B_HEADS):
        lg_rows = jnp.where(row_head == h, _log_gamma(h), lg_rows)
    new_state = jnp.where(row_head == col_head, state * jnp.exp(lc * lg_rows) + upd, 0.0)
    state_scr[...] = new_state

    @pl.when(c == nchunks - 1)
    def _():
        sout_ref[0] = new_state

    ri = lax.broadcasted_iota(jnp.int32, (B_W, B_W), 0) // B_HEAD_DIM
    ci = lax.broadcasted_iota(jnp.int32, (B_W, B_W), 1) // B_HEAD_DIM
    avg = jnp.where(ri == ci, 1.0 / B_HEAD_DIM, 0.0).astype(BF16)
    yc = y - _dot_sel_rhs(y, avg)
    var = _dot_sel_rhs(yc * yc, avg)
    yn = yc * lax.rsqrt(var + EPS) * gret_ref[...]
    y_ref[0] = (jax.nn.silu(g_ref[0]) * yn).astype(y_ref.dtype)


def _retention(proj3, s0, invf, g_ret, offset):
    B, L, _ = proj3.shape
    lc = _tile(L, 256)
    nch = L // lc
    blk = lambda col: pl.BlockSpec((1, lc, B_W), lambda b, c: (b, c, col // B_W))
    kern = functools.partial(_ret_kernel, lc=lc, offset=offset, nchunks=nch)
    return pl.pallas_call(
        kern,
        grid=(B, nch),
        in_specs=[blk(COL_QB), blk(COL_KB), blk(COL_VB), blk(COL_GB),
                  pl.BlockSpec((1, B_W, B_W), lambda b, c: (b, 0, 0)),
                  pl.BlockSpec((1, LANES), lambda b, c: (0, 0)),
                  pl.BlockSpec((1, B_W), lambda b, c: (0, 0))],
        out_specs=[pl.BlockSpec((1, lc, B_W), lambda b, c: (b, c, 0)),
                   pl.BlockSpec((1, B_W, B_W), lambda b, c: (b, 0, 0))],
        out_shape=[jax.ShapeDtypeStruct((B, L, B_W), BF16),
                   jax.ShapeDtypeStruct((B, B_W, B_W), F32)],
        scratch_shapes=[pltpu.VMEM((B_W, B_W), F32),
                        pltpu.VMEM((B_HEADS, lc, lc), F32)],
        compiler_params=_cparams("parallel", "arbitrary"),
        name="retention",
    )(proj3, proj3, proj3, proj3, s0, invf, g_ret.reshape(1, B_W))


def _ret_state_to_kernel(s):
    B = s.shape[0]
    half = B_HEAD_DIM // 2
    s = s.reshape(B, B_HEADS, 2, half, B_HEAD_DIM)
    eye = jnp.eye(B_HEADS, dtype=s.dtype)
    full = jnp.einsum('bhkie,hg->bkhige', s, eye)
    return full.reshape(B, B_W, B_W)


def _ret_state_from_kernel(s):
    B = s.shape[0]
    half = B_HEAD_DIM // 2
    s = s.reshape(B, 2, B_HEADS, half, B_HEADS, B_HEAD_DIM)
    d = jnp.einsum('bkhihe->bhkie', s)
    return d.reshape(B, B_HEADS, B_HEAD_DIM, B_HEAD_DIM)


def _ssd_kernel(z_ref, xbc_ref, dt_ref, s0_ref, cs0_ref, cw_ref, cb_ref, dtb_ref, alog_ref, dsk_ref, gs_ref,
                y_ref, sout_ref, state_scr, xp_scr, *, lc, nchunks):
    c = pl.program_id(1)
    PADR = 8

    @pl.when(c == 0)
    def _():
        state_scr[...] = s0_ref[0]
        xp_scr[0:PADR, :] = cs0_ref[0]

    xbc = xbc_ref[0]
    xp_scr[PADR:PADR + lc, :] = xbc
    cw = cw_ref[...]
    conv = xp_scr[PADR - 3:PADR - 3 + lc, :] * cw[0:1, :]
    for jj in range(1, C_CONV):
        conv = conv + xp_scr[PADR - 3 + jj:PADR - 3 + jj + lc, :] * cw[jj:jj + 1, :]
    act = jax.nn.silu(conv + cb_ref[...])
    xp_scr[0:PADR, :] = xp_scr[lc:lc + PADR, :]

    xs = act[:, :C_INNER]
    bm = act[:, C_INNER:C_INNER + G_W]
    cm = act[:, C_INNER + G_W:]
    dt = jax.nn.softplus(dt_ref[0] + dtb_ref[...])
    loga = dt * (-jnp.exp(alog_ref[...]))

    r = lax.broadcasted_iota(jnp.int32, (lc, lc), 0)
    s = lax.broadcasted_iota(jnp.int32, (lc, lc), 1)
    causal = r >= s
    tri = jnp.where(causal, 1.0, 0.0).astype(BF16)
    cum = _dot_sel(tri, loga)
    total = cum[lc - 1:lc, :]
    xdt = xs * dt
    xdtb = xdt.astype(BF16)
    bmb = bm.astype(BF16)
    cmb = cm.astype(BF16)
    state = state_scr[...]

    gw = C_INNER // C_GROUPS
    cum_t = [cum[:, kk * LANES:(kk + 1) * LANES].T for kk in range(C_INNER // LANES)]
    ys = []
    for g in range(C_GROUPS):
        cg = cmb[:, g * C_STATE:(g + 1) * C_STATE]
        bg = bmb[:, g * C_STATE:(g + 1) * C_STATE]
        scores = lax.dot_general(cg, bg, (((1,), (1,)), ((), ())), preferred_element_type=F32)
        y_g = jnp.dot(cg, state[:, g * gw:(g + 1) * gw].astype(BF16), preferred_element_type=F32)
        y_g = y_g * jnp.exp(cum[:, g * gw:(g + 1) * gw])
        lane_head = lax.broadcasted_iota(jnp.int32, (1, gw), 1) // C_HEAD_DIM
        xg = xdtb[:, g * gw:(g + 1) * gw]
        for hh in range(C_HEADS // C_GROUPS):
            col = g * gw + hh * C_HEAD_DIM
            cum_col = cum[:, col:col + 1]
            cum_row = cum_t[col // LANES][col % LANES:col % LANES + 1, :]
            dec = jnp.exp(jnp.where(causal, cum_col - cum_row, NEG_BIG))
            yh = jnp.dot((scores * dec).astype(BF16), xg, preferred_element_type=F32)
            y_g = y_g + jnp.where(lane_head == hh, yh, 0.0)
        ys.append(y_g)
        xw = (xdt[:, g * gw:(g + 1) * gw] * jnp.exp(total[:, g * gw:(g + 1) * gw] - cum[:, g * gw:(g + 1) * gw]))
        upd = lax.dot_general(bg, xw.astype(BF16), (((0,), (0,)), ((), ())), preferred_element_type=F32)
        state_scr[:, g * gw:(g + 1) * gw] = state[:, g * gw:(g + 1) * gw] * jnp.exp(total[:, g * gw:(g + 1) * gw]) + upd
    y = jnp.concatenate(ys, axis=1)

    @pl.when(c == nchunks - 1)
    def _():
        sout_ref[0] = state_scr[...]

    yc = (y + xs * dsk_ref[...]) * jax.nn.silu(z_ref[0])
    y_ref[0] = _rms(yc, gs_ref[...]).astype(y_ref.dtype)


def _ssd(proj3, s0, cs0, conv_w, conv_b, dtb_full, a_log, d_skip, g_ssm):
    B, L, _ = proj3.shape
    lc = _tile(L, 256)
    nch = L // lc
    rep = lambda p: jnp.repeat(p, C_HEAD_DIM).reshape(1, C_INNER)
    kern = functools.partial(_ssd_kernel, lc=lc, nchunks=nch)
    cst = lambda shape: pl.BlockSpec(shape, lambda b, c: (0,) * len(shape))
    return pl.pallas_call(
        kern,
        grid=(B, nch),
        in_specs=[pl.BlockSpec((1, lc, C_INNER), lambda b, c: (b, c, COL_ZC // C_INNER)),
                  pl.BlockSpec((1, lc, C_CONV_DIM), lambda b, c: (b, c, COL_XBC // C_CONV_DIM)),
                  pl.BlockSpec((1, lc, C_INNER), lambda b, c: (b, c, COL_DT // C_INNER)),
                  pl.BlockSpec((1, C_STATE, C_INNER), lambda b, c: (b, 0, 0)),
                  pl.BlockSpec((1, 8, C_CONV_DIM), lambda b, c: (b, 0, 0)),
                  cst((C_CONV, C_CONV_DIM)), cst((1, C_CONV_DIM)), cst((1, C_INNER)), cst((1, C_INNER)),
                  cst((1, C_INNER)), cst((1, C_INNER))],
        out_specs=[pl.BlockSpec((1, lc, C_INNER), lambda b, c: (b, c, 0)),
                   pl.BlockSpec((1, C_STATE, C_INNER), lambda b, c: (b, 0, 0))],
        out_shape=[jax.ShapeDtypeStruct((B, L, C_INNER), BF16),
                   jax.ShapeDtypeStruct((B, C_STATE, C_INNER), F32)],
        scratch_shapes=[pltpu.VMEM((C_STATE, C_INNER), F32),
                        pltpu.VMEM((lc + 8, C_CONV_DIM), F32)],
        compiler_params=_cparams("parallel", "arbitrary"),
        name="ssd",
    )(proj3, proj3, proj3, s0, cs0, conv_w, conv_b.reshape(1, C_CONV_DIM), dtb_full, rep(a_log), rep(d_skip),
      g_ssm.reshape(1, C_INNER))


def _outproj_kernel(h_ref, ya_ref, yb_ref, yc_ref, w_ref, o_ref):
    acc = jnp.dot(ya_ref[...], w_ref[0:A_W, :], preferred_element_type=F32)
    acc = acc + jnp.dot(yb_ref[...], w_ref[A_W:A_W + B_W, :], preferred_element_type=F32)
    acc = acc + jnp.dot(yc_ref[...], w_ref[A_W + B_W:, :], preferred_element_type=F32)
    o_ref[...] = h_ref[...] + acc


def _outproj(h2d, ya, yb, yc, w_out_bf16):
    T = h2d.shape[0]
    tm = _tile(T, 512)
    row = lambda w: pl.BlockSpec((tm, w), lambda i: (i, 0))
    return pl.pallas_call(
        _outproj_kernel,
        grid=(T // tm,),
        in_specs=[row(D_MODEL), row(A_W), row(B_W), row(C_INNER),
                  pl.BlockSpec((D_MODEL, D_MODEL), lambda i: (0, 0))],
        out_specs=row(D_MODEL),
        out_shape=jax.ShapeDtypeStruct((T, D_MODEL), F32),
        compiler_params=_cparams("parallel"),
        name="outproj",
    )(h2d, ya, yb, yc, w_out_bf16)


PEER_SUB = 256
PEER_STEP = 2048
PEER_CAND_COLS = 4


def _gelu_tanh(x):
    k0 = math.sqrt(2.0 / math.pi)
    inner = x * (k0 + (k0 * 0.044715) * (x * x))
    return (0.5 * x) * (1.0 + jnp.tanh(inner))


def _top16(x, want_rank):
    vals = []
    rank = jnp.full(x.shape, float(PEER_TOPK), F32) if want_rank else None
    for r in range(PEER_TOPK):
        m = jnp.max(x, axis=0, keepdims=True)
        vals.append(m)
        hit = x == m
        if want_rank:
            rank = jnp.where(hit, float(r), rank)
        x = jnp.where(hit, NEG_BIG, x)
    return jnp.concatenate(vals, axis=0), rank


def _sort16_pairs():
    def merge(lo, hi, r):
        step = r * 2
        if step < hi - lo:
            yield from merge(lo, hi, step)
            yield from merge(lo + r, hi, step)
            yield from [(i, i + r) for i in range(lo + r, hi - r, step)]
        else:
            yield (lo, lo + r)

    def sort(lo, hi):
        if hi - lo >= 1:
            mid = lo + (hi - lo) // 2
            yield from sort(lo, mid)
            yield from sort(mid + 1, hi)
            yield from merge(lo, hi, 1)

    return tuple(sort(0, PEER_TOPK - 1))


def _top16_values(x):
    v = [x[8 * g:8 * g + 8, :] for g in range(PEER_NKEYS // 8)]
    for a, b in _sort16_pairs():
        v[a], v[b] = jnp.maximum(v[a], v[b]), jnp.minimum(v[a], v[b])
    vals = []
    for r in range(PEER_TOPK):
        m = jnp.max(v[0], axis=0, keepdims=True)
        vals.append(m)
        hit = v[0] == m
        for k in range(PEER_TOPK - 1 - r):
            v[k] = jnp.where(hit, v[k + 1], v[k])
    return jnp.concatenate(vals, axis=0)


def _peer_select(s1, s2):
    n = s1.shape[1]
    a16 = _top16_values(s1)
    b16, rank2 = _top16(s2, True)
    ridx = lax.broadcasted_iota(jnp.int32, (PEER_TOPK, n), 0)
    cands = []
    for r2 in range(PEER_CAND_COLS):
        cands.append(jnp.where((ridx + 1) * (r2 + 1) <= PEER_TOPK, a16 + b16[r2:r2 + 1, :], NEG_BIG))
    for r1 in range(PEER_TOPK // (PEER_CAND_COLS + 1)):
        ok = (ridx >= PEER_CAND_COLS) & ((ridx + 1) * (r1 + 1) <= PEER_TOPK)
        cands.append(jnp.where(ok, b16 + a16[r1:r1 + 1, :], NEG_BIG))
    cand = jnp.concatenate(cands, axis=0)
    x = cand
    tau = None
    for r in range(PEER_TOPK):
        tau = jnp.max(x, axis=0, keepdims=True)
        x = jnp.where(x == tau, NEG_BIG, x)
    top = a16[0:1, :] + b16[0:1, :]
    z = jnp.sum(jnp.where(cand >= tau, jnp.exp(cand - top), 0.0), axis=0, keepdims=True)
    cnt = jnp.zeros(s1.shape, F32)
    for r2 in range(PEER_CAND_COLS):
        cnt = cnt + jnp.where(s1 + b16[r2:r2 + 1, :] >= tau, 1.0, 0.0)
    for r1 in range(PEER_TOPK // (PEER_CAND_COLS + 1)):
        a_r = a16[r1:r1 + 1, :]
        tail = jnp.where((ridx >= PEER_CAND_COLS) & (b16 + a_r >= tau), 1.0, 0.0)
        cnt = cnt + jnp.where(s1 == a_r, jnp.sum(tail, axis=0, keepdims=True), 0.0)
    a = jnp.exp(s1 - a16[0:1, :]) / z
    b = jnp.exp(s2 - b16[0:1, :])
    return a, cnt, rank2, b


def _peer_kernel(h_ref, g_ref, wq_ref, k1_ref, k2_ref, u_ref, vt_ref, o_ref,
                 xt_scr, a_scr, cnt_scr, rank_scr, b_scr, acc_scr, *, tm, te, ne):
    j = pl.program_id(1)
    n1 = te // PEER_NKEYS
    nchunk = tm // LANES
    nsub = te // PEER_SUB
    slabs = PEER_SUB // PEER_NKEYS

    def gate_weights(sb):
        ws = []
        for il in range(slabs):
            i1 = j * n1 + sb * slabs + il
            w = jnp.zeros((PEER_NKEYS, tm), BF16)
            for h in range(PEER_HEADS):
                row = lambda ref: jnp.concatenate(
                    [jnp.broadcast_to(ref[h, cch, pl.ds(i1, 1), :], (BF16_ROWS, LANES)) for cch in range(nchunk)],
                    axis=1).astype(BF16)
                cnt_b = jnp.tile(row(cnt_scr), (PEER_NKEYS // BF16_ROWS, 1))
                a_b = jnp.tile(row(a_scr), (PEER_NKEYS // BF16_ROWS, 1))
                w = w + jnp.where(rank_scr[h] < cnt_b, b_scr[h], jnp.zeros((), BF16)) * a_b
            ws.append(w)
        return jnp.concatenate(ws, axis=0)

    @pl.when(j == 0)
    def _():
        xn = _rms(h_ref[...], g_ref[...])
        xt_scr[...] = xn.T.astype(BF16)
        acc_scr[...] = jnp.zeros(acc_scr.shape, F32)

        def head_body(h, carry):
            base = pl.multiple_of(h * PEER_KEY_DIM, PEER_KEY_DIM)
            k1 = k1_ref[h].astype(BF16)
            k2 = k2_ref[h].astype(BF16)
            qh = jnp.dot(wq_ref[pl.ds(base, PEER_KEY_DIM), :], xt_scr[...], preferred_element_type=F32)
            for cch in range(nchunk):
                sl = slice(cch * LANES, (cch + 1) * LANES)
                q1 = qh[0:PEER_HALF, sl].astype(BF16)
                q2 = qh[PEER_HALF:, sl].astype(BF16)
                s1 = jnp.dot(k1, q1, preferred_element_type=F32)
                s2 = jnp.dot(k2, q2, preferred_element_type=F32)
                a, cnt, rank2, b = _peer_select(s1, s2)
                a_scr[h, cch] = a
                cnt_scr[h, cch] = cnt
                rank_scr[h, :, sl] = rank2.astype(BF16)
                b_scr[h, :, sl] = b.astype(BF16)
            return carry

        lax.fori_loop(0, PEER_HEADS, head_body, 0)

    acts = []
    for sb in range(nsub):
        ht = jnp.dot(u_ref[sb * PEER_SUB:(sb + 1) * PEER_SUB, :], xt_scr[...], preferred_element_type=F32)
        acts.append(_gelu_tanh(ht.astype(BF16)) * gate_weights(sb))
    acc_scr[...] += jnp.dot(vt_ref[...], jnp.concatenate(acts, axis=0), preferred_element_type=F32)

    @pl.when(j == ne - 1)
    def _():
        o_ref[...] = h_ref[...] + acc_scr[...].T


def _peer(h2d, g, wq_t, k1, k2, u_bf16, vt_bf16):
    T = h2d.shape[0]
    tm = _tile(T, 512)
    assert tm % LANES == 0
    te = PEER_STEP
    ne = PEER_EXPERTS // te
    kern = functools.partial(_peer_kernel, tm=tm, te=te, ne=ne)
    hk = (PEER_HEADS, PEER_NKEYS, tm)
    hkc = (PEER_HEADS, tm // LANES, PEER_NKEYS, LANES)
    return pl.pallas_call(
        kern,
        grid=(T // tm, ne),
        in_specs=[pl.BlockSpec((tm, D_MODEL), lambda i, j: (i, 0)),
                  pl.BlockSpec((1, D_MODEL), lambda i, j: (0, 0)),
                  pl.BlockSpec((PEER_HEADS * PEER_KEY_DIM, D_MODEL), lambda i, j: (0, 0)),
                  pl.BlockSpec((PEER_HEADS, PEER_NKEYS, PEER_HALF), lambda i, j: (0, 0, 0)),
                  pl.BlockSpec((PEER_HEADS, PEER_NKEYS, PEER_HALF), lambda i, j: (0, 0, 0)),
                  pl.BlockSpec((te, D_MODEL), lambda i, j: (j, 0)),
                  pl.BlockSpec((D_MODEL, te), lambda i, j: (0, j))],
        out_specs=pl.BlockSpec((tm, D_MODEL), lambda i, j: (i, 0)),
        out_shape=jax.ShapeDtypeStruct((T, D_MODEL), F32),
        scratch_shapes=[pltpu.VMEM((D_MODEL, tm), BF16),
                        pltpu.VMEM(hkc, F32), pltpu.VMEM(hkc, F32), pltpu.VMEM(hk, BF16), pltpu.VMEM(hk, BF16),
                        pltpu.VMEM((D_MODEL, tm), F32)],
        compiler_params=_cparams("parallel", "arbitrary"),
        name="peer",
    )(h2d, g.reshape(1, D_MODEL), wq_t, k1, k2, u_bf16, vt_bf16)


def _ple_kernel(h_ref, p_ref, g_ref, wg_ref, wp_ref, gf_ref, o_ref, *, final_norm):
    h = h_ref[...]
    xn = _rms(h, g_ref[...])
    gate = jax.nn.sigmoid(jnp.dot(xn.astype(BF16), wg_ref[...], preferred_element_type=F32))
    emb = jnp.dot(p_ref[...].astype(BF16), wp_ref[...], preferred_element_type=F32)
    out = h + emb * gate
    if final_norm:
        out = _rms(out, gf_ref[...])
    o_ref[...] = out


def _ple(h2d, p2d, g, wg_bf16, wp_bf16, g_final, final_norm):
    T = h2d.shape[0]
    tm = _tile(T, 512)
    kern = functools.partial(_ple_kernel, final_norm=final_norm)
    return pl.pallas_call(
        kern,
        grid=(T // tm,),
        in_specs=[pl.BlockSpec((tm, D_MODEL), lambda i: (i, 0)),
                  pl.BlockSpec((tm, PLE_DIM), lambda i: (i, 0)),
                  pl.BlockSpec((1, D_MODEL), lambda i: (0, 0)),
                  pl.BlockSpec((D_MODEL, D_MODEL), lambda i: (0, 0)),
                  pl.BlockSpec((PLE_DIM, D_MODEL), lambda i: (0, 0)),
                  pl.BlockSpec((1, D_MODEL), lambda i: (0, 0))],
        out_specs=pl.BlockSpec((tm, D_MODEL), lambda i: (i, 0)),
        out_shape=jax.ShapeDtypeStruct((T, D_MODEL), F32),
        compiler_params=_cparams("parallel"),
        name="ple",
    )(h2d, p2d, g.reshape(1, D_MODEL), wg_bf16, wp_bf16, g_final.reshape(1, D_MODEL))


def _pad_to(x, n, axis):
    pad = n - x.shape[axis]
    if pad == 0:
        return x
    widths = [(0, 0)] * x.ndim
    widths[axis] = (0, pad)
    return jnp.pad(x, widths)


def _run_trunk(x, p, past, weights):
    (g_mix, w_in, b_forget, g_ret, conv_w, conv_b, dt_bias, a_log, d_skip, g_ssm, w_out,
     g_ffn, peer_wq, peer_k1, peer_k2, peer_u, peer_v, g_ple, w_ple_gate, w_ple, g_final) = weights
    B, L, _ = x.shape
    depth = w_in.shape[0]
    T = B * L
    P = 0 if past is None else past[0].shape[2]
    half = B_HEAD_DIM // 2
    invf = jnp.tile(ROPE_BASE ** (-jnp.arange(half, dtype=F32) / half), LANES // half).reshape(1, LANES)

    h = x.reshape(T, D_MODEL)
    outs = []
    for i in range(depth):
        w_proj, bf_pad, dtb_full = _prep_w_in(w_in[i], b_forget[i], dt_bias[i])
        proj, lf, kb, vb = _inproj(h, g_mix[i], w_proj, bf_pad)
        proj3 = proj.reshape(B, L, PROJ_W)
        lf3 = lf.reshape(B, L, LANES)
        kb = kb.reshape(B, L, A_W)
        vb = vb.reshape(B, L, A_W)
        k_new = proj3[:, :, COL_KA:COL_KA + A_W]
        v_new = proj3[:, :, COL_VA:COL_VA + A_W]

        if past is None:
            eq, ek = _cumsum(lf3)
            ya = _fox(proj3, COL_QA, eq, kb, ek, vb, 0)
            rs0 = jnp.zeros((B, B_W, B_W), F32)
            ss0 = jnp.zeros((B, C_STATE, C_INNER), F32)
            cs0 = jnp.zeros((B, 8, C_CONV_DIM), F32)
        else:
            lk_pad = -(-(P + L) // FOX_TK) * FOX_TK
            lq_pad = -(-L // LANES) * LANES
            past_k = past[0][i].reshape(B, P, A_W).astype(BF16)
            past_v = past[1][i].reshape(B, P, A_W).astype(BF16)
            past_lf = _pad_to(past[2][i], LANES, 2)
            k_all = _pad_to(jnp.concatenate([past_k, kb], axis=1), lk_pad, 1)
            v_all = _pad_to(jnp.concatenate([past_v, vb], axis=1), lk_pad, 1)
            lf_all = _pad_to(jnp.concatenate([past_lf, lf3], axis=1), lk_pad, 1)
            eq, ek = _cumsum(lf_all)
            q_pad = _pad_to(proj3[:, :, COL_QA:COL_QA + A_W], lq_pad, 1)
            ya = _fox(q_pad, 0, _pad_to(eq[:, P:P + L], lq_pad, 1), k_all, ek, v_all, P)[:, :L]
            rs0 = _ret_state_to_kernel(past[3][i])
            ss0 = jnp.transpose(past[4][i], (0, 2, 1, 3)).reshape(B, C_STATE, C_INNER)
            cs0 = jnp.concatenate([jnp.zeros((B, 8 - (C_CONV - 1), C_CONV_DIM), F32), past[5][i]], axis=1)

        yb, ret_k = _retention(proj3, rs0, invf, g_ret[i], P)
        ret_new = _ret_state_from_kernel(ret_k)

        yc, ssm_k = _ssd(proj3, ss0, cs0, conv_w[i], conv_b[i], dtb_full, a_log[i], d_skip[i], g_ssm[i])
        ssm_new = jnp.transpose(ssm_k.reshape(B, C_STATE, C_HEADS, C_HEAD_DIM), (0, 2, 1, 3))
        xbc = proj3[:, :, COL_XBC:COL_XBC + C_CONV_DIM]
        if L >= C_CONV - 1:
            conv_new = xbc[:, L - (C_CONV - 1):]
        else:
            prev = jnp.zeros((B, C_CONV - 1, C_CONV_DIM), F32) if past is None else past[5][i]
            conv_new = jnp.concatenate([prev, xbc], axis=1)[:, -(C_CONV - 1):]

        h = _outproj(h, ya.reshape(T, A_W), yb.reshape(T, B_W), yc.reshape(T, C_INNER), w_out[i].astype(BF16))
        h = _peer(h, g_ffn[i], peer_wq[i].T.astype(BF16), peer_k1[i], peer_k2[i],
                  peer_u[i].astype(BF16), peer_v[i].T.astype(BF16))
        h = _ple(h, p[i].reshape(T, PLE_DIM), g_ple[i], w_ple_gate[i].astype(BF16), w_ple[i].astype(BF16),
                 g_final, final_norm=(i == depth - 1))
        outs.append((k_new.reshape(B, L, A_HEADS, A_HEAD_DIM), v_new.reshape(B, L, A_HEADS, A_HEAD_DIM),
                     lf3[:, :, :A_HEADS], ret_new, ssm_new, conv_new))
    stacked = tuple(jnp.stack([o[n] for o in outs]) for n in range(6))
    return h.reshape(B, L, D_MODEL), stacked


def kernel(x_prompt, x_sample, cache_k_fox, cache_v_fox, cache_logf_fox, state_ret, state_ssm, state_conv, p_prompt, p_sample, g_mix, w_in, b_forget, g_ret, conv_w, conv_b, dt_bias, a_log, d_skip, g_ssm, w_out, g_ffn, peer_wq, peer_k1, peer_k2, peer_u, peer_v, g_ple, w_ple_gate, w_ple, g_final):
    weights = (g_mix, w_in, b_forget, g_ret, conv_w, conv_b, dt_bias, a_log, d_skip, g_ssm, w_out,
               g_ffn, peer_wq, peer_k1, peer_k2, peer_u, peer_v, g_ple, w_ple_gate, w_ple, g_final)
    y_prompt, (k_p, v_p, lf_p, ret_p, ssm_p, conv_p) = _run_trunk(x_prompt, p_prompt, None, weights)
    past = (cache_k_fox, cache_v_fox, cache_logf_fox, state_ret, state_ssm, state_conv)
    y_sample, (k_s, v_s, lf_s, ret_s, ssm_s, conv_s) = _run_trunk(x_sample, p_sample, past, weights)
    return (y_prompt, y_sample, k_p, v_p, lf_p, ret_p, ssm_p, conv_p, k_s, v_s, lf_s, ret_s, ssm_s, conv_s)
```

```python
import functools
import math

import jax
import jax.numpy as jnp
from jax import lax
from jax.experimental import pallas as pl
from jax.experimental.pallas import tpu as pltpu

F32 = jnp.float32
BF16 = jnp.bfloat16

D_MODEL = 1024
PLE_DIM = 256
EPS = 1e-6
A_HEADS, A_HEAD_DIM = 4, 64
B_HEADS, B_HEAD_DIM = 4, 64
ROPE_BASE = 10000.0
C_HEADS, C_HEAD_DIM = 8, 64
C_INNER = C_HEADS * C_HEAD_DIM
C_GROUPS, C_STATE, C_CONV = 2, 128, 4
C_CONV_DIM = C_INNER + 2 * C_GROUPS * C_STATE
PEER_HEADS, PEER_NKEYS, PEER_KEY_DIM, PEER_TOPK = 8, 128, 256, 16
PEER_HALF = PEER_KEY_DIM // 2
PEER_EXPERTS = PEER_NKEYS * PEER_NKEYS

LANES = 128
BF16_ROWS = 16
A_W = A_HEADS * A_HEAD_DIM
B_W = B_HEADS * B_HEAD_DIM
G_W = C_GROUPS * C_STATE

COL_QA = 0
COL_KA = COL_QA + A_W
COL_VA = COL_KA + A_W
COL_QB = COL_VA + A_W
COL_KB = COL_QB + B_W
COL_VB = COL_KB + B_W
COL_GB = COL_VB + B_W
COL_FA = COL_GB + B_W
COL_ZC = COL_FA + 2 * LANES
COL_DT = COL_ZC + C_INNER
COL_XBC = COL_DT + C_INNER
PROJ_W = COL_XBC + C_CONV_DIM
assert all(c % A_W == 0 for c in (COL_QA, COL_KA, COL_VA, COL_QB, COL_KB, COL_VB, COL_GB))
assert COL_ZC % C_INNER == 0 and COL_DT % C_INNER == 0 and COL_XBC % C_CONV_DIM == 0 and COL_FA % LANES == 0

NEG_BIG = -1e30
VMEM_LIMIT_BYTES = 52 * 1024 * 1024


def _cparams(*sem):
    return pltpu.CompilerParams(dimension_semantics=sem, vmem_limit_bytes=VMEM_LIMIT_BYTES)


def _tile(n, pref):
    t = min(n, pref)
    while n % t:
        t -= 8
    assert t > 0
    return t


def _rms(x, g):
    return x * lax.rsqrt(jnp.mean(x * x, axis=-1, keepdims=True) + EPS) * g


def _split3(x):
    hi = x.astype(BF16)
    r1 = x - hi.astype(F32)
    mid = r1.astype(BF16)
    lo = (r1 - mid.astype(F32)).astype(BF16)
    return hi, mid, lo


def _dot_sel(sel_bf16, x):
    hi, mid, lo = _split3(x)
    d = lambda p: jnp.dot(sel_bf16, p, preferred_element_type=F32)
    return d(hi) + d(mid) + d(lo)


def _dot_sel_rhs(x, sel_bf16):
    hi, mid, lo = _split3(x)
    d = lambda p: jnp.dot(p, sel_bf16, preferred_element_type=F32)
    return d(hi) + d(mid) + d(lo)


def _inproj_kernel(x_ref, g_ref, w_ref, bf_ref, o_ref, lf_ref, kb_ref, vb_ref, kf_ref, vf_ref):
    xn = _rms(x_ref[...], g_ref[...])
    y = jnp.dot(xn.astype(BF16), w_ref[...], preferred_element_type=F32)
    o_ref[...] = y
    kf_ref[...] = y[:, COL_KA:COL_KA + A_W]
    vf_ref[...] = y[:, COL_VA:COL_VA + A_W]
    kb_ref[...] = y[:, COL_KA:COL_KA + A_W].astype(BF16)
    vb_ref[...] = y[:, COL_VA:COL_VA + A_W].astype(BF16)
    fa = y[:, COL_FA:COL_FA + LANES] + bf_ref[...]
    lane = lax.broadcasted_iota(jnp.int32, fa.shape, 1)
    lf_ref[...] = jnp.where(lane < A_HEADS, jax.nn.log_sigmoid(fa), 0.0)


def _inproj(x2d, g, w, bf_pad):
    T = x2d.shape[0]
    tm = _tile(T, 256)
    return pl.pallas_call(
        _inproj_kernel,
        grid=(T // tm,),
        in_specs=[pl.BlockSpec((tm, D_MODEL), lambda i: (i, 0)),
                  pl.BlockSpec((1, D_MODEL), lambda i: (0, 0)),
                  pl.BlockSpec((D_MODEL, PROJ_W), lambda i: (0, 0)),
                  pl.BlockSpec((1, LANES), lambda i: (0, 0))],
        out_specs=[pl.BlockSpec((tm, PROJ_W), lambda i: (i, 0)),
                   pl.BlockSpec((tm, LANES), lambda i: (i, 0)),
                   pl.BlockSpec((tm, A_W), lambda i: (i, 0)),
                   pl.BlockSpec((tm, A_W), lambda i: (i, 0)),
                   pl.BlockSpec((tm, A_W), lambda i: (i, 0)),
                   pl.BlockSpec((tm, A_W), lambda i: (i, 0))],
        out_shape=[jax.ShapeDtypeStruct((T, PROJ_W), F32),
                   jax.ShapeDtypeStruct((T, LANES), F32),
                   jax.ShapeDtypeStruct((T, A_W), BF16),
                   jax.ShapeDtypeStruct((T, A_W), BF16),
                   jax.ShapeDtypeStruct((T, A_W), F32),
                   jax.ShapeDtypeStruct((T, A_W), F32)],
        compiler_params=_cparams("parallel"),
        name="inproj",
    )(x2d, g.reshape(1, D_MODEL), w, bf_pad)


def _prep_w_in(w_in, b_forget, dt_bias):
    sizes = (A_W, A_W, A_W, A_HEADS, B_W, B_W, B_W, B_W, C_INNER, C_INNER, G_W, G_W, C_HEADS)
    pts = []
    acc = 0
    for s in sizes[:-1]:
        acc += s
        pts.append(acc)
    qa, ka, va, fa, qb, kb, vb, gb, zc, xc, bc, cc, dtc = jnp.split(w_in, pts, axis=-1)

    def rot_perm(w):
        w = w.reshape(D_MODEL, B_HEADS, 2, B_HEAD_DIM // 2)
        return jnp.transpose(w, (0, 2, 1, 3)).reshape(D_MODEL, B_W)

    z = lambda n: jnp.zeros((D_MODEL, n), w_in.dtype)
    w = jnp.concatenate([qa, ka, va, rot_perm(qb), rot_perm(kb), vb, gb, fa, z(2 * LANES - A_HEADS), zc,
                         jnp.repeat(dtc, C_HEAD_DIM, axis=1), xc, bc, cc], axis=1)
    assert w.shape[1] == PROJ_W
    bf_pad = jnp.concatenate([b_forget, jnp.zeros((LANES - A_HEADS,), F32)]).reshape(1, LANES)
    dtb_full = jnp.repeat(dt_bias, C_HEAD_DIM).reshape(1, C_INNER)
    return w.astype(BF16), bf_pad, dtb_full


FOX_TQ = 512
FOX_TK = 512
FOX_XTRA = 6


def _fox_xtra_base(h):
    return A_HEAD_DIM * ((h + 1) % A_HEADS)


def _cumsum_kernel(lf_ref, eq_ref, ek_ref, carry_scr):
    @pl.when(pl.program_id(1) == 0)
    def _():
        carry_scr[...] = jnp.zeros_like(carry_scr)

    lf = lf_ref[0]
    tc = lf.shape[0]
    r = lax.broadcasted_iota(jnp.int32, (tc, tc), 0)
    c = lax.broadcasted_iota(jnp.int32, (tc, tc), 1)
    tri = jnp.where(r >= c, 1.0, 0.0).astype(BF16)
    cum = _dot_sel(tri, lf) + carry_scr[0:1, :]
    carry_scr[...] = jnp.broadcast_to(cum[tc - 1:tc, :], carry_scr.shape)

    pieces = _split3(cum)
    src = lax.broadcasted_iota(jnp.int32, (LANES, A_W), 0)
    dst = lax.broadcasted_iota(jnp.int32, (LANES, A_W), 1)
    lane = lax.broadcasted_iota(jnp.int32, (1, A_W), 1)
    eq = jnp.zeros((tc, A_W), F32)
    ek = jnp.zeros((tc, A_W), F32)
    one_q = jnp.zeros((1, A_W), F32)
    one_k = jnp.zeros((1, A_W), F32)
    for h in range(A_HEADS):
        base = _fox_xtra_base(h)
        for k, piece in enumerate(pieces):
            place_q = jnp.where((src == h) & (dst == base + k), 1.0, 0.0).astype(BF16)
            place_k = jnp.where((src == h) & (dst == base + 3 + k), -1.0, 0.0).astype(BF16)
            eq = eq + jnp.dot(piece, place_q, preferred_element_type=F32)
            ek = ek + jnp.dot(piece, place_k, preferred_element_type=F32)
        one_q = jnp.where((lane >= base + 3) & (lane < base + FOX_XTRA), 1.0, one_q)
        one_k = jnp.where((lane >= base) & (lane < base + 3), 1.0, one_k)
    eq_ref[0] = (eq + one_q).astype(BF16)
    ek_ref[0] = (ek + one_k).astype(BF16)


def _cumsum(lf):
    B, L, _ = lf.shape
    tc = _tile(L, 512)
    return pl.pallas_call(
        _cumsum_kernel,
        grid=(B, L // tc),
        in_specs=[pl.BlockSpec((1, tc, LANES), lambda b, j: (b, j, 0))],
        out_specs=[pl.BlockSpec((1, tc, A_W), lambda b, j: (b, j, 0)),
                   pl.BlockSpec((1, tc, A_W), lambda b, j: (b, j, 0))],
        out_shape=[jax.ShapeDtypeStruct((B, L, A_W), BF16),
                   jax.ShapeDtypeStruct((B, L, A_W), BF16)],
        scratch_shapes=[pltpu.VMEM((8, LANES), F32)],
        compiler_params=_cparams("parallel", "arbitrary"),
        name="fox_cumsum",
    )(lf)


def _fox_kernel(q_ref, eq_ref, kb_ref, ek_ref, vt_ref, o_ref, qp_scr, m_scr, l_scr, acc_scr, *,
                tq, tk, offset, nk):
    i = pl.program_id(1)
    scale = A_HEAD_DIM ** -0.5
    q_lo = offset + i * tq

    def in_head(lane, h):
        return (lane >= h * A_HEAD_DIM) & (lane < (h + 1) * A_HEAD_DIM)

    def in_xtra(lane, h):
        return (lane >= _fox_xtra_base(h)) & (lane < _fox_xtra_base(h) + FOX_XTRA)

    lane_q = lax.broadcasted_iota(jnp.int32, (tq, A_W), 1)
    q = (q_ref[0] * scale).astype(BF16)
    eq = eq_ref[0]
    for h in range(A_HEADS):
        qp_scr[h] = jnp.where(in_xtra(lane_q, h), eq, jnp.where(in_head(lane_q, h), q, jnp.zeros((), BF16)))
    m_scr[...] = jnp.full(m_scr.shape, NEG_BIG, F32)
    l_scr[...] = jnp.zeros(l_scr.shape, F32)
    acc_scr[...] = jnp.zeros(acc_scr.shape, F32)

    lane_k = lax.broadcasted_iota(jnp.int32, (tk, A_W), 1)

    def block(jj, masked):
        ks = pl.multiple_of(jj * tk, tk)
        k = kb_ref[0, pl.ds(ks, tk), :]
        ek = ek_ref[0, pl.ds(ks, tk), :]
        vt = vt_ref[0, jj]
        if masked:
            kpos = ks + lax.broadcasted_iota(jnp.int32, (tk, tq), 0)
            qpos = q_lo + lax.broadcasted_iota(jnp.int32, (tk, tq), 1)
            vis = kpos <= qpos
        ps, alphas = [], []
        for h in range(A_HEADS):
            kp = jnp.where(in_xtra(lane_k, h), ek, k)
            s = lax.dot_general(kp, qp_scr[h], (((1,), (1,)), ((), ())), preferred_element_type=F32)
            if masked:
                s = jnp.where(vis, s, NEG_BIG)
            m_prev = m_scr[h, 0:1, :]
            m_next = jnp.maximum(m_prev, jnp.max(s, axis=0, keepdims=True))
            alpha = jnp.exp(m_prev - m_next)
            p = jnp.exp(s - m_next)
            l_scr[h] = jnp.broadcast_to(alpha * l_scr[h, 0:1, :] + jnp.sum(p, axis=0, keepdims=True), (8, tq))
            m_scr[h] = jnp.broadcast_to(m_next, (8, tq))
            ps.append(p.astype(BF16))
            alphas.append(alpha)
        pv = jnp.dot(vt, jnp.concatenate(ps, axis=1), preferred_element_type=F32)
        for h in range(A_HEADS):
            acc_scr[h] = acc_scr[h] * alphas[h] + pv[:, h * tq:(h + 1) * tq]

    n_full = (q_lo + 1) // tk
    n_vis = jnp.minimum((q_lo + tq - 1) // tk + 1, nk)

    def full_body(jj, carry):
        block(jj, False)
        return carry

    def edge_body(jj, carry):
        block(jj, True)
        return carry

    lax.fori_loop(0, n_full, full_body, 0)
    lax.fori_loop(n_full, n_vis, edge_body, 0)

    row = lax.broadcasted_iota(jnp.int32, (A_W, tq), 0)
    out_t = jnp.zeros((A_W, tq), F32)
    for h in range(A_HEADS):
        out_t = out_t + jnp.where(in_head(row, h), acc_scr[h] / l_scr[h, 0:1, :], 0.0)
    o_ref[0] = out_t.T.astype(o_ref.dtype)


def _fox(q_src, q_col, eq, kb, ek, vb, offset):
    B, Lq, _ = q_src.shape
    Lk = kb.shape[1]
    tq = _tile(Lq, FOX_TQ)
    tk = _tile(Lk, FOX_TK)
    assert tq % LANES == 0 and tk % LANES == 0
    nq, nk = Lq // tq, Lk // tk
    vt = jnp.transpose(vb.reshape(B, nk, tk, A_W), (0, 1, 3, 2))
    whole = lambda b, i: (b, 0, 0)
    kern = functools.partial(_fox_kernel, tq=tq, tk=tk, offset=offset, nk=nk)
    return pl.pallas_call(
        kern,
        grid=(B, nq),
        in_specs=[pl.BlockSpec((1, tq, A_W), lambda b, i: (b, i, q_col // A_W)),
                  pl.BlockSpec((1, tq, A_W), lambda b, i: (b, i, 0)),
                  pl.BlockSpec((1, Lk, A_W), whole),
                  pl.BlockSpec((1, Lk, A_W), whole),
                  pl.BlockSpec((1, nk, A_W, tk), lambda b, i: (b, 0, 0, 0))],
        out_specs=pl.BlockSpec((1, tq, A_W), lambda b, i: (b, i, 0)),
        out_shape=jax.ShapeDtypeStruct((B, Lq, A_W), BF16),
        scratch_shapes=[pltpu.VMEM((A_HEADS, tq, A_W), BF16),
                        pltpu.VMEM((A_HEADS, 8, tq), F32),
                        pltpu.VMEM((A_HEADS, 8, tq), F32),
                        pltpu.VMEM((A_HEADS, A_W, tq), F32)],
        compiler_params=_cparams("parallel", "arbitrary"),
        name="fox_attention",
    )(q_src, eq, kb, ek, vt)


def _ret_lane_head_v():
    return lax.broadcasted_iota(jnp.int32, (1, B_W), 1) // B_HEAD_DIM


def _ret_lane_head_qk():
    return (lax.broadcasted_iota(jnp.int32, (1, B_W), 1) % LANES) // (B_HEAD_DIM // 2)


def _log_gamma(h):
    return math.log1p(-2.0 ** (-5.0 - h))


def _ret_kernel(q_ref, k_ref, v_ref, g_ref, s0_ref, invf_ref, gret_ref, y_ref, sout_ref,
                state_scr, dec_scr, *, lc, offset, nchunks):
    c = pl.program_id(1)

    @pl.when(c == 0)
    def _():
        state_scr[...] = s0_ref[0]
        r = lax.broadcasted_iota(jnp.int32, (lc, lc), 0)
        s = lax.broadcasted_iota(jnp.int32, (lc, lc), 1)
        d = (r - s).astype(F32)
        for h in range(B_HEADS):
            dec_scr[h] = jnp.where(r >= s, jnp.exp(d * _log_gamma(h)), 0.0)

    hv = _ret_lane_head_v()
    hqk = _ret_lane_head_qk()
    lg_v = jnp.zeros((1, B_W), F32)
    lg_qk = jnp.zeros((1, B_W), F32)
    for h in range(B_HEADS):
        lg_v = jnp.where(hv == h, _log_gamma(h), lg_v)
        lg_qk = jnp.where(hqk == h, _log_gamma(h), lg_qk)

    pos = (offset + c * lc + lax.broadcasted_iota(jnp.int32, (lc, LANES), 0)).astype(F32)
    ang = pos * invf_ref[...]
    cos, sin = jnp.cos(ang), jnp.sin(ang)

    def rot(x):
        x1, x2 = x[:, :LANES], x[:, LANES:]
        return jnp.concatenate([x1 * cos - x2 * sin, x1 * sin + x2 * cos], axis=1)

    qr = rot(q_ref[0])
    kr = rot(k_ref[0]) * (B_HEAD_DIM ** -0.5)
    v = v_ref[0]
    vb = v.astype(BF16)
    krb = kr.astype(BF16)

    lpos = lax.broadcasted_iota(jnp.int32, (lc, B_W), 0).astype(F32)
    state = state_scr[...]
    y = jnp.dot(qr.astype(BF16), state.astype(BF16), preferred_element_type=F32) * jnp.exp((lpos + 1.0) * lg_v)
    for h in range(B_HEADS):
        qh = jnp.where(hqk == h, qr, 0.0).astype(BF16)
        sc = lax.dot_general(qh, krb, (((1,), (1,)), ((), ())), preferred_element_type=F32) * dec_scr[h]
        yh = jnp.dot(sc.astype(BF16), vb, preferred_element_type=F32)
        y = y + jnp.where(hv == h, yh, 0.0)

    kw = (kr * jnp.exp((lc - 1.0 - lpos) * lg_qk)).astype(BF16)
    upd = lax.dot_general(kw, vb, (((0,), (0,)), ((), ())), preferred_element_type=F32)
    row_head = (lax.broadcasted_iota(jnp.int32, (B_W, B_W), 0) % LANES) // (B_HEAD_DIM // 2)
    col_head = lax.broadcasted_iota(jnp.int32, (B_W, B_W), 1) // B_HEAD_DIM
    lg_rows = jnp.zeros((B_W, B_W), F32)
    for h in range(B_HEADS):
        lg_rows = jnp.where(row_head == h, _log_gamma(h), lg_rows)
    new_state = jnp.where(row_head == col_head, state * jnp.exp(lc * lg_rows) + upd, 0.0)
    state_scr[...] = new_state

    @pl.when(c == nchunks - 1)
    def _():
        sout_ref[0] = new_state

    ri = lax.broadcasted_iota(jnp.int32, (B_W, B_W), 0) // B_HEAD_DIM
    ci = lax.broadcasted_iota(jnp.int32, (B_W, B_W), 1) // B_HEAD_DIM
    avg = jnp.where(ri == ci, 1.0 / B_HEAD_DIM, 0.0).astype(BF16)
    yc = y - _dot_sel_rhs(y, avg)
    var = _dot_sel_rhs(yc * yc, avg)
    yn = yc * lax.rsqrt(var + EPS) * gret_ref[...]
    y_ref[0] = (jax.nn.silu(g_ref[0]) * yn).astype(y_ref.dtype)


def _retention(proj3, s0, invf, g_ret, offset):
    B, L, _ = proj3.shape
    lc = _tile(L, 256)
    nch = L // lc
    blk = lambda col: pl.BlockSpec((1, lc, B_W), lambda b, c: (b, c, col // B_W))
    kern = functools.partial(_ret_kernel, lc=lc, offset=offset, nchunks=nch)
    return pl.pallas_call(
        kern,
        grid=(B, nch),
        in_specs=[blk(COL_QB), blk(COL_KB), blk(COL_VB), blk(COL_GB),
                  pl.BlockSpec((1, B_W, B_W), lambda b, c: (b, 0, 0)),
                  pl.BlockSpec((1, LANES), lambda b, c: (0, 0)),
                  pl.BlockSpec((1, B_W), lambda b, c: (0, 0))],
        out_specs=[pl.BlockSpec((1, lc, B_W), lambda b, c: (b, c, 0)),
                   pl.BlockSpec((1, B_W, B_W), lambda b, c: (b, 0, 0))],
        out_shape=[jax.ShapeDtypeStruct((B, L, B_W), BF16),
                   jax.ShapeDtypeStruct((B, B_W, B_W), F32)],
        scratch_shapes=[pltpu.VMEM((B_W, B_W), F32),
                        pltpu.VMEM((B_HEADS, lc, lc), F32)],
        compiler_params=_cparams("parallel", "arbitrary"),
        name="retention",
    )(proj3, proj3, proj3, proj3, s0, invf, g_ret.reshape(1, B_W))


def _ret_state_to_kernel(s):
    B = s.shape[0]
    half = B_HEAD_DIM // 2
    s = s.reshape(B, B_HEADS, 2, half, B_HEAD_DIM)
    eye = jnp.eye(B_HEADS, dtype=s.dtype)
    full = jnp.einsum('bhkie,hg->bkhige', s, eye)
    return full.reshape(B, B_W, B_W)


def _ret_state_from_kernel(s):
    B = s.shape[0]
    half = B_HEAD_DIM // 2
    s = s.reshape(B, 2, B_HEADS, half, B_HEADS, B_HEAD_DIM)
    d = jnp.einsum('bkhihe->bhkie', s)
    return d.reshape(B, B_HEADS, B_HEAD_DIM, B_HEAD_DIM)


def _ssd_kernel(z_ref, xbc_ref, dt_ref, s0_ref, cs0_ref, cw_ref, cb_ref, dtb_ref, alog_ref, dsk_ref, gs_ref,
                y_ref, sout_ref, state_scr, xp_scr, *, lc, nchunks):
    c = pl.program_id(1)
    PADR = 8

    @pl.when(c == 0)
    def _():
        state_scr[...] = s0_ref[0]
        xp_scr[0:PADR, :] = cs0_ref[0]

    xbc = xbc_ref[0]
    xp_scr[PADR:PADR + lc, :] = xbc
    cw = cw_ref[...]
    conv = xp_scr[PADR - 3:PADR - 3 + lc, :] * cw[0:1, :]
    for jj in range(1, C_CONV):
        conv = conv + xp_scr[PADR - 3 + jj:PADR - 3 + jj + lc, :] * cw[jj:jj + 1, :]
    act = jax.nn.silu(conv + cb_ref[...])
    xp_scr[0:PADR, :] = xp_scr[lc:lc + PADR, :]

    xs = act[:, :C_INNER]
    bm = act[:, C_INNER:C_INNER + G_W]
    cm = act[:, C_INNER + G_W:]
    dt = jax.nn.softplus(dt_ref[0] + dtb_ref[...])
    loga = dt * (-jnp.exp(alog_ref[...]))

    r = lax.broadcasted_iota(jnp.int32, (lc, lc), 0)
    s = lax.broadcasted_iota(jnp.int32, (lc, lc), 1)
    causal = r >= s
    tri = jnp.where(causal, 1.0, 0.0).astype(BF16)
    cum = _dot_sel(tri, loga)
    total = cum[lc - 1:lc, :]
    xdt = xs * dt
    xdtb = xdt.astype(BF16)
    bmb = bm.astype(BF16)
    cmb = cm.astype(BF16)
    state = state_scr[...]

    gw = C_INNER // C_GROUPS
    cum_t = [cum[:, kk * LANES:(kk + 1) * LANES].T for kk in range(C_INNER // LANES)]
    ys = []
    for g in range(C_GROUPS):
        cg = cmb[:, g * C_STATE:(g + 1) * C_STATE]
        bg = bmb[:, g * C_STATE:(g + 1) * C_STATE]
        scores = lax.dot_general(cg, bg, (((1,), (1,)), ((), ())), preferred_element_type=F32)
        y_g = jnp.dot(cg, state[:, g * gw:(g + 1) * gw].astype(BF16), preferred_element_type=F32)
        y_g = y_g * jnp.exp(cum[:, g * gw:(g + 1) * gw])
        lane_head = lax.broadcasted_iota(jnp.int32, (1, gw), 1) // C_HEAD_DIM
        xg = xdtb[:, g * gw:(g + 1) * gw]
        for hh in range(C_HEADS // C_GROUPS):
            col = g * gw + hh * C_HEAD_DIM
            cum_col = cum[:, col:col + 1]
            cum_row = cum_t[col // LANES][col % LANES:col % LANES + 1, :]
            dec = jnp.exp(jnp.where(causal, cum_col - cum_row, NEG_BIG))
            yh = jnp.dot((scores * dec).astype(BF16), xg, preferred_element_type=F32)
            y_g = y_g + jnp.where(lane_head == hh, yh, 0.0)
        ys.append(y_g)
        xw = (xdt[:, g * gw:(g + 1) * gw] * jnp.exp(total[:, g * gw:(g + 1) * gw] - cum[:, g * gw:(g + 1) * gw]))
        upd = lax.dot_general(bg, xw.astype(BF16), (((0,), (0,)), ((), ())), preferred_element_type=F32)
        state_scr[:, g * gw:(g + 1) * gw] = state[:, g * gw:(g + 1) * gw] * jnp.exp(total[:, g * gw:(g + 1) * gw]) + upd
    y = jnp.concatenate(ys, axis=1)

    @pl.when(c == nchunks - 1)
    def _():
        sout_ref[0] = state_scr[...]

    yc = (y + xs * dsk_ref[...]) * jax.nn.silu(z_ref[0])
    y_ref[0] = _rms(yc, gs_ref[...]).astype(y_ref.dtype)


def _ssd(proj3, s0, cs0, conv_w, conv_b, dtb_full, a_log, d_skip, g_ssm):
    B, L, _ = proj3.shape
    lc = _tile(L, 256)
    nch = L // lc
    rep = lambda p: jnp.repeat(p, C_HEAD_DIM).reshape(1, C_INNER)
    kern = functools.partial(_ssd_kernel, lc=lc, nchunks=nch)
    cst = lambda shape: pl.BlockSpec(shape, lambda b, c: (0,) * len(shape))
    return pl.pallas_call(
        kern,
        grid=(B, nch),
        in_specs=[pl.BlockSpec((1, lc, C_INNER), lambda b, c: (b, c, COL_ZC // C_INNER)),
                  pl.BlockSpec((1, lc, C_CONV_DIM), lambda b, c: (b, c, COL_XBC // C_CONV_DIM)),
                  pl.BlockSpec((1, lc, C_INNER), lambda b, c: (b, c, COL_DT // C_INNER)),
                  pl.BlockSpec((1, C_STATE, C_INNER), lambda b, c: (b, 0, 0)),
                  pl.BlockSpec((1, 8, C_CONV_DIM), lambda b, c: (b, 0, 0)),
                  cst((C_CONV, C_CONV_DIM)), cst((1, C_CONV_DIM)), cst((1, C_INNER)), cst((1, C_INNER)),
                  cst((1, C_INNER)), cst((1, C_INNER))],
        out_specs=[pl.BlockSpec((1, lc, C_INNER), lambda b, c: (b, c, 0)),
                   pl.BlockSpec((1, C_STATE, C_INNER), lambda b, c: (b, 0, 0))],
        out_shape=[jax.ShapeDtypeStruct((B, L, C_INNER), BF16),
                   jax.ShapeDtypeStruct((B, C_STATE, C_INNER), F32)],
        scratch_shapes=[pltpu.VMEM((C_STATE, C_INNER), F32),
                        pltpu.VMEM((lc + 8, C_CONV_DIM), F32)],
        compiler_params=_cparams("parallel", "arbitrary"),
        name="ssd",
    )(proj3, proj3, proj3, s0, cs0, conv_w, conv_b.reshape(1, C_CONV_DIM), dtb_full, rep(a_log), rep(d_skip),
      g_ssm.reshape(1, C_INNER))


def _outproj_kernel(h_ref, ya_ref, yb_ref, yc_ref, w_ref, o_ref):
    acc = jnp.dot(ya_ref[...], w_ref[0:A_W, :], preferred_element_type=F32)
    acc = acc + jnp.dot(yb_ref[...], w_ref[A_W:A_W + B_W, :], preferred_element_type=F32)
    acc = acc + jnp.dot(yc_ref[...], w_ref[A_W + B_W:, :], preferred_element_type=F32)
    o_ref[...] = h_ref[...] + acc


def _outproj(h2d, ya, yb, yc, w_out_bf16):
    T = h2d.shape[0]
    tm = _tile(T, 512)
    row = lambda w: pl.BlockSpec((tm, w), lambda i: (i, 0))
    return pl.pallas_call(
        _outproj_kernel,
        grid=(T // tm,),
        in_specs=[row(D_MODEL), row(A_W), row(B_W), row(C_INNER),
                  pl.BlockSpec((D_MODEL, D_MODEL), lambda i: (0, 0))],
        out_specs=row(D_MODEL),
        out_shape=jax.ShapeDtypeStruct((T, D_MODEL), F32),
        compiler_params=_cparams("parallel"),
        name="outproj",
    )(h2d, ya, yb, yc, w_out_bf16)


PEER_SUB = 256
PEER_STEP = 2048
PEER_CAND_COLS = 4


def _gelu_tanh(x):
    k0 = math.sqrt(2.0 / math.pi)
    inner = x * (k0 + (k0 * 0.044715) * (x * x))
    return (0.5 * x) * (1.0 + jnp.tanh(inner))


def _top16(x, want_rank):
    vals = []
    rank = jnp.full(x.shape, float(PEER_TOPK), F32) if want_rank else None
    for r in range(PEER_TOPK):
        m = jnp.max(x, axis=0, keepdims=True)
        vals.append(m)
        hit = x == m
        if want_rank:
            rank = jnp.where(hit, float(r), rank)
        x = jnp.where(hit, NEG_BIG, x)
    return jnp.concatenate(vals, axis=0), rank


def _sort_pairs(n):
    def merge(lo, hi, r):
        step = r * 2
        if step < hi - lo:
            yield from merge(lo, hi, step)
            yield from merge(lo + r, hi, step)
            yield from [(i, i + r) for i in range(lo + r, hi - r, step)]
        else:
            yield (lo, lo + r)

    def sort(lo, hi):
        if hi - lo >= 1:
            mid = lo + (hi - lo) // 2
            yield from sort(lo, mid)
            yield from sort(mid + 1, hi)
            yield from merge(lo, hi, 1)

    return tuple(sort(0, n - 1))


def _top16_values(x):
    v = [x[8 * g:8 * g + 8, :] for g in range(PEER_NKEYS // 8)]
    for a, b in _sort_pairs(PEER_NKEYS // 8):
        v[a], v[b] = jnp.maximum(v[a], v[b]), jnp.minimum(v[a], v[b])
    vals = []
    for r in range(PEER_TOPK):
        m = jnp.max(v[0], axis=0, keepdims=True)
        vals.append(m)
        hit = v[0] == m
        for k in range(PEER_TOPK - 1 - r):
            v[k] = jnp.where(hit, v[k + 1], v[k])
    return jnp.concatenate(vals, axis=0)


def _peer_select(s1, s2):
    n = s1.shape[1]
    a16 = _top16_values(s1)
    b16, rank2 = _top16(s2, True)
    ridx = lax.broadcasted_iota(jnp.int32, (PEER_TOPK, n), 0)
    sub = lax.broadcasted_iota(jnp.int32, (8, n), 0)
    lo, hi = slice(0, 8), slice(8, PEER_TOPK)
    col = lambda r2, rows: a16[rows, :] + b16[r2:r2 + 1, :]
    row = lambda r1, cols: b16[cols, :] + a16[r1:r1 + 1, :]
    assert PEER_CAND_COLS == 4 and PEER_TOPK == 16
    groups = [col(0, lo), col(0, hi), col(1, lo),
              jnp.where(sub <= 4, col(2, lo), NEG_BIG),
              jnp.where(sub <= 3, col(3, lo), row(1, lo)),
              jnp.where(sub >= 4, row(0, lo), NEG_BIG),
              row(0, hi),
              jnp.where(sub == 4, row(2, lo), NEG_BIG)]
    cand = jnp.concatenate(groups, axis=0)
    v = list(groups)
    for a_i, b_i in _sort_pairs(8):
        v[a_i], v[b_i] = jnp.maximum(v[a_i], v[b_i]), jnp.minimum(v[a_i], v[b_i])
    tau = None
    for r in range(PEER_TOPK):
        tau = jnp.max(v[0], axis=0, keepdims=True)
        hit = v[0] == tau
        depth = min(8, PEER_TOPK - r)
        for k in range(depth - 1):
            v[k] = jnp.where(hit, v[k + 1], v[k])
        if depth == 8:
            v[7] = jnp.where(hit, NEG_BIG, v[7])
    top = a16[0:1, :] + b16[0:1, :]
    z = jnp.sum(jnp.where(cand >= tau, jnp.exp(cand - top), 0.0), axis=0, keepdims=True)
    cnt = jnp.zeros(s1.shape, F32)
    for r2 in range(PEER_CAND_COLS):
        cnt = cnt + jnp.where(s1 + b16[r2:r2 + 1, :] >= tau, 1.0, 0.0)
    for r1 in range(PEER_TOPK // (PEER_CAND_COLS + 1)):
        a_r = a16[r1:r1 + 1, :]
        tail = jnp.where((ridx >= PEER_CAND_COLS) & (b16 + a_r >= tau), 1.0, 0.0)
        cnt = cnt + jnp.where(s1 == a_r, jnp.sum(tail, axis=0, keepdims=True), 0.0)
    a = jnp.exp(s1 - a16[0:1, :]) / z
    b = jnp.exp(s2 - b16[0:1, :])
    return a, cnt, rank2, b


def _peer_kernel(h_ref, g_ref, wq_ref, k1_ref, k2_ref, u_ref, vt_ref, o_ref,
                 xt_scr, a_scr, cnt_scr, rank_scr, b_scr, acc_scr, *, tm, te, ne):
    j = pl.program_id(1)
    n1 = te // PEER_NKEYS
    nchunk = tm // LANES
    nsub = te // PEER_SUB
    slabs = PEER_SUB // PEER_NKEYS

    def gate_weights(sb):
        ws = []
        for il in range(slabs):
            i1 = j * n1 + sb * slabs + il
            w = jnp.zeros((PEER_NKEYS, tm), BF16)
            for h in range(PEER_HEADS):
                row = lambda ref: jnp.concatenate(
                    [jnp.broadcast_to(ref[h, cch, pl.ds(i1, 1), :], (BF16_ROWS, LANES)) for cch in range(nchunk)],
                    axis=1).astype(BF16)
                cnt_b = jnp.tile(row(cnt_scr), (PEER_NKEYS // BF16_ROWS, 1))
                a_b = jnp.tile(row(a_scr), (PEER_NKEYS // BF16_ROWS, 1))
                w = w + jnp.where(rank_scr[h] < cnt_b, b_scr[h], jnp.zeros((), BF16)) * a_b
            ws.append(w)
        return jnp.concatenate(ws, axis=0)

    @pl.when(j == 0)
    def _():
        xn = _rms(h_ref[...], g_ref[...])
        xt_scr[...] = xn.T.astype(BF16)
        acc_scr[...] = jnp.zeros(acc_scr.shape, F32)

        def head_body(h, carry):
            base = pl.multiple_of(h * PEER_KEY_DIM, PEER_KEY_DIM)
            k1 = k1_ref[h].astype(BF16)
            k2 = k2_ref[h].astype(BF16)
            qh = jnp.dot(wq_ref[pl.ds(base, PEER_KEY_DIM), :], xt_scr[...], preferred_element_type=F32)
            for cch in range(nchunk):
                sl = slice(cch * LANES, (cch + 1) * LANES)
                q1 = qh[0:PEER_HALF, sl].astype(BF16)
                q2 = qh[PEER_HALF:, sl].astype(BF16)
                s1 = jnp.dot(k1, q1, preferred_element_type=F32)
                s2 = jnp.dot(k2, q2, preferred_element_type=F32)
                a, cnt, rank2, b = _peer_select(s1, s2)
                a_scr[h, cch] = a
                cnt_scr[h, cch] = cnt
                rank_scr[h, :, sl] = rank2.astype(BF16)
                b_scr[h, :, sl] = b.astype(BF16)
            return carry

        lax.fori_loop(0, PEER_HEADS, head_body, 0)

    acts = []
    for sb in range(nsub):
        ht = jnp.dot(u_ref[sb * PEER_SUB:(sb + 1) * PEER_SUB, :], xt_scr[...], preferred_element_type=F32)
        acts.append(_gelu_tanh(ht.astype(BF16)) * gate_weights(sb))
    acc_scr[...] += jnp.dot(vt_ref[...], jnp.concatenate(acts, axis=0), preferred_element_type=F32)

    @pl.when(j == ne - 1)
    def _():
        o_ref[...] = h_ref[...] + acc_scr[...].T


def _peer(h2d, g, wq_t, k1, k2, u_bf16, vt_bf16):
    T = h2d.shape[0]
    tm = _tile(T, 512)
    assert tm % LANES == 0
    te = PEER_STEP
    ne = PEER_EXPERTS // te
    kern = functools.partial(_peer_kernel, tm=tm, te=te, ne=ne)
    hk = (PEER_HEADS, PEER_NKEYS, tm)
    hkc = (PEER_HEADS, tm // LANES, PEER_NKEYS, LANES)
    return pl.pallas_call(
        kern,
        grid=(T // tm, ne),
        in_specs=[pl.BlockSpec((tm, D_MODEL), lambda i, j: (i, 0)),
                  pl.BlockSpec((1, D_MODEL), lambda i, j: (0, 0)),
                  pl.BlockSpec((PEER_HEADS * PEER_KEY_DIM, D_MODEL), lambda i, j: (0, 0)),
                  pl.BlockSpec((PEER_HEADS, PEER_NKEYS, PEER_HALF), lambda i, j: (0, 0, 0)),
                  pl.BlockSpec((PEER_HEADS, PEER_NKEYS, PEER_HALF), lambda i, j: (0, 0, 0)),
                  pl.BlockSpec((te, D_MODEL), lambda i, j: (j, 0)),
                  pl.BlockSpec((D_MODEL, te), lambda i, j: (0, j))],
        out_specs=pl.BlockSpec((tm, D_MODEL), lambda i, j: (i, 0)),
        out_shape=jax.ShapeDtypeStruct((T, D_MODEL), F32),
        scratch_shapes=[pltpu.VMEM((D_MODEL, tm), BF16),
                        pltpu.VMEM(hkc, F32), pltpu.VMEM(hkc, F32), pltpu.VMEM(hk, BF16), pltpu.VMEM(hk, BF16),
                        pltpu.VMEM((D_MODEL, tm), F32)],
        compiler_params=_cparams("parallel", "arbitrary"),
        name="peer",
    )(h2d, g.reshape(1, D_MODEL), wq_t, k1, k2, u_bf16, vt_bf16)


def _ple_kernel(h_ref, p_ref, g_ref, wg_ref, wp_ref, gf_ref, o_ref, *, final_norm):
    h = h_ref[...]
    xn = _rms(h, g_ref[...])
    gate = jax.nn.sigmoid(jnp.dot(xn.astype(BF16), wg_ref[...], preferred_element_type=F32))
    emb = jnp.dot(p_ref[...].astype(BF16), wp_ref[...], preferred_element_type=F32)
    out = h + emb * gate
    if final_norm:
        out = _rms(out, gf_ref[...])
    o_ref[...] = out


def _ple(h2d, p2d, g, wg_bf16, wp_bf16, g_final, final_norm):
    T = h2d.shape[0]
    tm = _tile(T, 512)
    kern = functools.partial(_ple_kernel, final_norm=final_norm)
    return pl.pallas_call(
        kern,
        grid=(T // tm,),
        in_specs=[pl.BlockSpec((tm, D_MODEL), lambda i: (i, 0)),
                  pl.BlockSpec((tm, PLE_DIM), lambda i: (i, 0)),
                  pl.BlockSpec((1, D_MODEL), lambda i: (0, 0)),
                  pl.BlockSpec((D_MODEL, D_MODEL), lambda i: (0, 0)),
                  pl.BlockSpec((PLE_DIM, D_MODEL), lambda i: (0, 0)),
                  pl.BlockSpec((1, D_MODEL), lambda i: (0, 0))],
        out_specs=pl.BlockSpec((tm, D_MODEL), lambda i: (i, 0)),
        out_shape=jax.ShapeDtypeStruct((T, D_MODEL), F32),
        compiler_params=_cparams("parallel"),
        name="ple",
    )(h2d, p2d, g.reshape(1, D_MODEL), wg_bf16, wp_bf16, g_final.reshape(1, D_MODEL))


def _pad_to(x, n, axis):
    pad = n - x.shape[axis]
    if pad == 0:
        return x
    widths = [(0, 0)] * x.ndim
    widths[axis] = (0, pad)
    return jnp.pad(x, widths)


def _run_trunk(x, p, past, weights):
    (g_mix, w_in, b_forget, g_ret, conv_w, conv_b, dt_bias, a_log, d_skip, g_ssm, w_out,
     g_ffn, peer_wq, peer_k1, peer_k2, peer_u, peer_v, g_ple, w_ple_gate, w_ple, g_final) = weights
    B, L, _ = x.shape
    depth = w_in.shape[0]
    T = B * L
    P = 0 if past is None else past[0].shape[2]
    half = B_HEAD_DIM // 2
    invf = jnp.tile(ROPE_BASE ** (-jnp.arange(half, dtype=F32) / half), LANES // half).reshape(1, LANES)

    h = x.reshape(T, D_MODEL)
    outs = []
    for i in range(depth):
        w_proj, bf_pad, dtb_full = _prep_w_in(w_in[i], b_forget[i], dt_bias[i])
        proj, lf, kb, vb, k_new, v_new = _inproj(h, g_mix[i], w_proj, bf_pad)
        proj3 = proj.reshape(B, L, PROJ_W)
        lf3 = lf.reshape(B, L, LANES)
        kb = kb.reshape(B, L, A_W)
        vb = vb.reshape(B, L, A_W)

        if past is None:
            eq, ek = _cumsum(lf3)
            ya = _fox(proj3, COL_QA, eq, kb, ek, vb, 0)
            rs0 = jnp.zeros((B, B_W, B_W), F32)
            ss0 = jnp.zeros((B, C_STATE, C_INNER), F32)
            cs0 = jnp.zeros((B, 8, C_CONV_DIM), F32)
        else:
            lk_pad = -(-(P + L) // FOX_TK) * FOX_TK
            lq_pad = -(-L // LANES) * LANES
            past_k = past[0][i].reshape(B, P, A_W).astype(BF16)
            past_v = past[1][i].reshape(B, P, A_W).astype(BF16)
            past_lf = _pad_to(past[2][i], LANES, 2)
            k_all = _pad_to(jnp.concatenate([past_k, kb], axis=1), lk_pad, 1)
            v_all = _pad_to(jnp.concatenate([past_v, vb], axis=1), lk_pad, 1)
            lf_all = _pad_to(jnp.concatenate([past_lf, lf3], axis=1), lk_pad, 1)
            eq, ek = _cumsum(lf_all)
            q_pad = _pad_to(proj3[:, :, COL_QA:COL_QA + A_W], lq_pad, 1)
            ya = _fox(q_pad, 0, _pad_to(eq[:, P:P + L], lq_pad, 1), k_all, ek, v_all, P)[:, :L]
            rs0 = _ret_state_to_kernel(past[3][i])
            ss0 = jnp.transpose(past[4][i], (0, 2, 1, 3)).reshape(B, C_STATE, C_INNER)
            cs0 = jnp.concatenate([jnp.zeros((B, 8 - (C_CONV - 1), C_CONV_DIM), F32), past[5][i]], axis=1)

        yb, ret_k = _retention(proj3, rs0, invf, g_ret[i], P)
        ret_new = _ret_state_from_kernel(ret_k)

        yc, ssm_k = _ssd(proj3, ss0, cs0, conv_w[i], conv_b[i], dtb_full, a_log[i], d_skip[i], g_ssm[i])
        ssm_new = jnp.transpose(ssm_k.reshape(B, C_STATE, C_HEADS, C_HEAD_DIM), (0, 2, 1, 3))
        xbc = proj3[:, :, COL_XBC:COL_XBC + C_CONV_DIM]
        if L >= C_CONV - 1:
            conv_new = xbc[:, L - (C_CONV - 1):]
        else:
            prev = jnp.zeros((B, C_CONV - 1, C_CONV_DIM), F32) if past is None else past[5][i]
            conv_new = jnp.concatenate([prev, xbc], axis=1)[:, -(C_CONV - 1):]

        h = _outproj(h, ya.reshape(T, A_W), yb.reshape(T, B_W), yc.reshape(T, C_INNER), w_out[i].astype(BF16))
        h = _peer(h, g_ffn[i], peer_wq[i].T.astype(BF16), peer_k1[i], peer_k2[i],
                  peer_u[i].astype(BF16), peer_v[i].T.astype(BF16))
        h = _ple(h, p[i].reshape(T, PLE_DIM), g_ple[i], w_ple_gate[i].astype(BF16), w_ple[i].astype(BF16),
                 g_final, final_norm=(i == depth - 1))
        outs.append((k_new.reshape(B, L, A_HEADS, A_HEAD_DIM), v_new.reshape(B, L, A_HEADS, A_HEAD_DIM),
                     lf3[:, :, :A_HEADS], ret_new, ssm_new, conv_new))
    stacked = tuple(jnp.stack([o[n] for o in outs]) for n in range(6))
    return h.reshape(B, L, D_MODEL), stacked


def kernel(x_prompt, x_sample, cache_k_fox, cache_v_fox, cache_logf_fox, state_ret, state_ssm, state_conv, p_prompt, p_sample, g_mix, w_in, b_forget, g_ret, conv_w, conv_b, dt_bias, a_log, d_skip, g_ssm, w_out, g_ffn, peer_wq, peer_k1, peer_k2, peer_u, peer_v, g_ple, w_ple_gate, w_ple, g_final):
    weights = (g_mix, w_in, b_forget, g_ret, conv_w, conv_b, dt_bias, a_log, d_skip, g_ssm, w_out,
               g_ffn, peer_wq, peer_k1, peer_k2, peer_u, peer_v, g_ple, w_ple_gate, w_ple, g_final)
    y_prompt, (k_p, v_p, lf_p, ret_p, ssm_p, conv_p) = _run_trunk(x_prompt, p_prompt, None, weights)
    past = (cache_k_fox, cache_v_fox, cache_logf_fox, state_ret, state_ssm, state_conv)
    y_sample, (k_s, v_s, lf_s, ret_s, ssm_s, conv_s) = _run_trunk(x_sample, p_sample, past, weights)
    return (y_prompt, y_sample, k_p, v_p, lf_p, ret_p, ssm_p, conv_p, k_s, v_s, lf_s, ret_s, ssm_s, conv_s)
```

```python
import functools
import math

import jax
import jax.numpy as jnp
from jax import lax
from jax.experimental import pallas as pl
from jax.experimental.pallas import tpu as pltpu

F32 = jnp.float32
BF16 = jnp.bfloat16

D_MODEL = 1024
PLE_DIM = 256
EPS = 1e-6
A_HEADS, A_HEAD_DIM = 4, 64
B_HEADS, B_HEAD_DIM = 4, 64
ROPE_BASE = 10000.0
C_HEADS, C_HEAD_DIM = 8, 64
C_INNER = C_HEADS * C_HEAD_DIM
C_GROUPS, C_STATE, C_CONV = 2, 128, 4
C_CONV_DIM = C_INNER + 2 * C_GROUPS * C_STATE
PEER_HEADS, PEER_NKEYS, PEER_KEY_DIM, PEER_TOPK = 8, 128, 256, 16
PEER_HALF = PEER_KEY_DIM // 2
PEER_EXPERTS = PEER_NKEYS * PEER_NKEYS

LANES = 128
BF16_ROWS = 16
A_W = A_HEADS * A_HEAD_DIM
B_W = B_HEADS * B_HEAD_DIM
G_W = C_GROUPS * C_STATE

COL_QA = 0
COL_KA = COL_QA + A_W
COL_VA = COL_KA + A_W
COL_QB = COL_VA + A_W
COL_KB = COL_QB + B_W
COL_VB = COL_KB + B_W
COL_GB = COL_VB + B_W
COL_FA = COL_GB + B_W
COL_ZC = COL_FA + 2 * LANES
COL_DT = COL_ZC + C_INNER
COL_XBC = COL_DT + C_INNER
PROJ_W = COL_XBC + C_CONV_DIM
assert all(c % A_W == 0 for c in (COL_QA, COL_KA, COL_VA, COL_QB, COL_KB, COL_VB, COL_GB))
assert COL_ZC % C_INNER == 0 and COL_DT % C_INNER == 0 and COL_XBC % C_CONV_DIM == 0 and COL_FA % LANES == 0

NEG_BIG = -1e30
VMEM_LIMIT_BYTES = 52 * 1024 * 1024


def _cparams(*sem):
    return pltpu.CompilerParams(dimension_semantics=sem, vmem_limit_bytes=VMEM_LIMIT_BYTES)


def _tile(n, pref):
    t = min(n, pref)
    while n % t:
        t -= 8
    assert t > 0
    return t


def _rms(x, g):
    return x * lax.rsqrt(jnp.mean(x * x, axis=-1, keepdims=True) + EPS) * g


def _split3(x):
    hi = x.astype(BF16)
    r1 = x - hi.astype(F32)
    mid = r1.astype(BF16)
    lo = (r1 - mid.astype(F32)).astype(BF16)
    return hi, mid, lo


def _dot_sel(sel_bf16, x):
    hi, mid, lo = _split3(x)
    d = lambda p: jnp.dot(sel_bf16, p, preferred_element_type=F32)
    return d(hi) + d(mid) + d(lo)


def _dot_sel_rhs(x, sel_bf16):
    hi, mid, lo = _split3(x)
    d = lambda p: jnp.dot(p, sel_bf16, preferred_element_type=F32)
    return d(hi) + d(mid) + d(lo)


def _inproj_kernel(x_ref, g_ref, w_ref, bf_ref, o_ref, lf_ref, kb_ref, vb_ref, kf_ref, vf_ref):
    xn = _rms(x_ref[...], g_ref[...])
    y = jnp.dot(xn.astype(BF16), w_ref[...], preferred_element_type=F32)
    o_ref[...] = y
    kf_ref[...] = y[:, COL_KA:COL_KA + A_W]
    vf_ref[...] = y[:, COL_VA:COL_VA + A_W]
    kb_ref[...] = y[:, COL_KA:COL_KA + A_W].astype(BF16)
    vb_ref[...] = y[:, COL_VA:COL_VA + A_W].astype(BF16)
    fa = y[:, COL_FA:COL_FA + LANES] + bf_ref[...]
    lane = lax.broadcasted_iota(jnp.int32, fa.shape, 1)
    lf_ref[...] = jnp.where(lane < A_HEADS, jax.nn.log_sigmoid(fa), 0.0)


def _inproj(x2d, g, w, bf_pad):
    T = x2d.shape[0]
    tm = _tile(T, 256)
    return pl.pallas_call(
        _inproj_kernel,
        grid=(T // tm,),
        in_specs=[pl.BlockSpec((tm, D_MODEL), lambda i: (i, 0)),
                  pl.BlockSpec((1, D_MODEL), lambda i: (0, 0)),
                  pl.BlockSpec((D_MODEL, PROJ_W), lambda i: (0, 0)),
                  pl.BlockSpec((1, LANES), lambda i: (0, 0))],
        out_specs=[pl.BlockSpec((tm, PROJ_W), lambda i: (i, 0)),
                   pl.BlockSpec((tm, LANES), lambda i: (i, 0)),
                   pl.BlockSpec((tm, A_W), lambda i: (i, 0)),
                   pl.BlockSpec((tm, A_W), lambda i: (i, 0)),
                   pl.BlockSpec((tm, A_W), lambda i: (i, 0)),
                   pl.BlockSpec((tm, A_W), lambda i: (i, 0))],
        out_shape=[jax.ShapeDtypeStruct((T, PROJ_W), F32),
                   jax.ShapeDtypeStruct((T, LANES), F32),
                   jax.ShapeDtypeStruct((T, A_W), BF16),
                   jax.ShapeDtypeStruct((T, A_W), BF16),
                   jax.ShapeDtypeStruct((T, A_W), F32),
                   jax.ShapeDtypeStruct((T, A_W), F32)],
        compiler_params=_cparams("parallel"),
        name="inproj",
    )(x2d, g.reshape(1, D_MODEL), w, bf_pad)


def _prep_w_in(w_in, b_forget, dt_bias):
    sizes = (A_W, A_W, A_W, A_HEADS, B_W, B_W, B_W, B_W, C_INNER, C_INNER, G_W, G_W, C_HEADS)
    pts = []
    acc = 0
    for s in sizes[:-1]:
        acc += s
        pts.append(acc)
    qa, ka, va, fa, qb, kb, vb, gb, zc, xc, bc, cc, dtc = jnp.split(w_in, pts, axis=-1)

    def rot_perm(w):
        w = w.reshape(D_MODEL, B_HEADS, 2, B_HEAD_DIM // 2)
        return jnp.transpose(w, (0, 2, 1, 3)).reshape(D_MODEL, B_W)

    z = lambda n: jnp.zeros((D_MODEL, n), w_in.dtype)
    w = jnp.concatenate([qa, ka, va, rot_perm(qb), rot_perm(kb), vb, gb, fa, z(2 * LANES - A_HEADS), zc,
                         jnp.repeat(dtc, C_HEAD_DIM, axis=1), xc, bc, cc], axis=1)
    assert w.shape[1] == PROJ_W
    bf_pad = jnp.concatenate([b_forget, jnp.zeros((LANES - A_HEADS,), F32)]).reshape(1, LANES)
    dtb_full = jnp.repeat(dt_bias, C_HEAD_DIM).reshape(1, C_INNER)
    return w.astype(BF16), bf_pad, dtb_full


FOX_TQ = 512
FOX_TK = 512
FOX_TQ_SUB = 512
FOX_XTRA = 6


def _fox_xtra_base(h):
    return A_HEAD_DIM * ((h + 1) % A_HEADS)


def _cumsum_kernel(lf_ref, eq_ref, ek_ref, carry_scr):
    @pl.when(pl.program_id(1) == 0)
    def _():
        carry_scr[...] = jnp.zeros_like(carry_scr)

    lf = lf_ref[0]
    tc = lf.shape[0]
    r = lax.broadcasted_iota(jnp.int32, (tc, tc), 0)
    c = lax.broadcasted_iota(jnp.int32, (tc, tc), 1)
    tri = jnp.where(r >= c, 1.0, 0.0).astype(BF16)
    cum = _dot_sel(tri, lf) + carry_scr[0:1, :]
    carry_scr[...] = jnp.broadcast_to(cum[tc - 1:tc, :], carry_scr.shape)

    pieces = _split3(cum)
    src = lax.broadcasted_iota(jnp.int32, (LANES, A_W), 0)
    dst = lax.broadcasted_iota(jnp.int32, (LANES, A_W), 1)
    lane = lax.broadcasted_iota(jnp.int32, (1, A_W), 1)
    eq = jnp.zeros((tc, A_W), F32)
    ek = jnp.zeros((tc, A_W), F32)
    one_q = jnp.zeros((1, A_W), F32)
    one_k = jnp.zeros((1, A_W), F32)
    for h in range(A_HEADS):
        base = _fox_xtra_base(h)
        for k, piece in enumerate(pieces):
            place_q = jnp.where((src == h) & (dst == base + k), 1.0, 0.0).astype(BF16)
            place_k = jnp.where((src == h) & (dst == base + 3 + k), -1.0, 0.0).astype(BF16)
            eq = eq + jnp.dot(piece, place_q, preferred_element_type=F32)
            ek = ek + jnp.dot(piece, place_k, preferred_element_type=F32)
        one_q = jnp.where((lane >= base + 3) & (lane < base + FOX_XTRA), 1.0, one_q)
        one_k = jnp.where((lane >= base) & (lane < base + 3), 1.0, one_k)
    eq_ref[0] = (eq + one_q).astype(BF16)
    ek_ref[0] = (ek + one_k).astype(BF16)


def _cumsum(lf):
    B, L, _ = lf.shape
    tc = _tile(L, 512)
    return pl.pallas_call(
        _cumsum_kernel,
        grid=(B, L // tc),
        in_specs=[pl.BlockSpec((1, tc, LANES), lambda b, j: (b, j, 0))],
        out_specs=[pl.BlockSpec((1, tc, A_W), lambda b, j: (b, j, 0)),
                   pl.BlockSpec((1, tc, A_W), lambda b, j: (b, j, 0))],
        out_shape=[jax.ShapeDtypeStruct((B, L, A_W), BF16),
                   jax.ShapeDtypeStruct((B, L, A_W), BF16)],
        scratch_shapes=[pltpu.VMEM((8, LANES), F32)],
        compiler_params=_cparams("parallel", "arbitrary"),
        name="fox_cumsum",
    )(lf)


def _fox_kernel(q_ref, eq_ref, kb_ref, ek_ref, vb_ref, o_ref, qp_scr, m_scr, l_scr, acc_scr, *,
                tq, tk, offset, nk):
    i = pl.program_id(1)
    scale = A_HEAD_DIM ** -0.5
    q_lo = offset + i * tq

    def in_head(lane, h):
        return (lane >= h * A_HEAD_DIM) & (lane < (h + 1) * A_HEAD_DIM)

    def in_xtra(lane, h):
        return (lane >= _fox_xtra_base(h)) & (lane < _fox_xtra_base(h) + FOX_XTRA)

    lane_q = lax.broadcasted_iota(jnp.int32, (tq, A_W), 1)
    q = (q_ref[0] * scale).astype(BF16)
    eq = eq_ref[0]
    for h in range(A_HEADS):
        qp_scr[h] = jnp.where(in_xtra(lane_q, h), eq, jnp.where(in_head(lane_q, h), q, jnp.zeros((), BF16)))
    m_scr[...] = jnp.full(m_scr.shape, NEG_BIG, F32)
    l_scr[...] = jnp.zeros(l_scr.shape, F32)
    acc_scr[...] = jnp.zeros(acc_scr.shape, F32)

    lane_k = lax.broadcasted_iota(jnp.int32, (tk, A_W), 1)

    def block(jj, masked):
        ks = pl.multiple_of(jj * tk, tk)
        k = kb_ref[0, pl.ds(ks, tk), :]
        ek = ek_ref[0, pl.ds(ks, tk), :]
        v = vb_ref[0, pl.ds(ks, tk), :]
        if masked:
            kpos = ks + lax.broadcasted_iota(jnp.int32, (tk, tq), 0)
            qpos = q_lo + lax.broadcasted_iota(jnp.int32, (tk, tq), 1)
            vis = kpos <= qpos
        tqs = min(tq, FOX_TQ_SUB)
        ps, alphas = [], []
        for h in range(A_HEADS):
            kp = jnp.where(in_xtra(lane_k, h), ek, k)
            for qs in range(tq // tqs):
                cols = slice(qs * tqs, (qs + 1) * tqs)
                s = lax.dot_general(kp, qp_scr[h, cols, :], (((1,), (1,)), ((), ())),
                                    preferred_element_type=F32)
                if masked:
                    s = jnp.where(vis[:, cols], s, NEG_BIG)
                m_prev = m_scr[h, 0:1, cols]
                m_next = jnp.maximum(m_prev, jnp.max(s, axis=0, keepdims=True))
                alpha = jnp.exp(m_prev - m_next)
                p = jnp.exp(s - m_next)
                l_scr[h, :, cols] = jnp.broadcast_to(
                    alpha * l_scr[h, 0:1, cols] + jnp.sum(p, axis=0, keepdims=True), (8, tqs))
                m_scr[h, :, cols] = jnp.broadcast_to(m_next, (8, tqs))
                ps.append(p.astype(BF16))
                alphas.append(alpha)
        alphas = [jnp.concatenate(alphas[h * (tq // tqs):(h + 1) * (tq // tqs)], axis=1) for h in range(A_HEADS)]
        pv = lax.dot_general(v, jnp.concatenate(ps, axis=1), (((0,), (0,)), ((), ())),
                             preferred_element_type=F32)
        for h in range(A_HEADS):
            acc_scr[h] = acc_scr[h] * alphas[h] + pv[:, h * tq:(h + 1) * tq]

    n_full = (q_lo + 1) // tk
    n_vis = jnp.minimum((q_lo + tq - 1) // tk + 1, nk)

    def full_body(jj, carry):
        block(jj, False)
        return carry

    def edge_body(jj, carry):
        block(jj, True)
        return carry

    lax.fori_loop(0, n_full, full_body, 0)
    lax.fori_loop(n_full, n_vis, edge_body, 0)

    row = lax.broadcasted_iota(jnp.int32, (A_W, tq), 0)
    out_t = jnp.zeros((A_W, tq), F32)
    for h in range(A_HEADS):
        out_t = out_t + jnp.where(in_head(row, h), acc_scr[h] / l_scr[h, 0:1, :], 0.0)
    o_ref[0] = out_t.T.astype(o_ref.dtype)


def _fox(q_src, q_col, eq, kb, ek, vb, offset):
    B, Lq, _ = q_src.shape
    Lk = kb.shape[1]
    tq = _tile(Lq, FOX_TQ)
    tk = _tile(Lk, FOX_TK)
    assert tq % LANES == 0 and tk % LANES == 0
    nq, nk = Lq // tq, Lk // tk
    whole = lambda b, i: (b, 0, 0)
    kern = functools.partial(_fox_kernel, tq=tq, tk=tk, offset=offset, nk=nk)
    return pl.pallas_call(
        kern,
        grid=(B, nq),
        in_specs=[pl.BlockSpec((1, tq, A_W), lambda b, i: (b, i, q_col // A_W)),
                  pl.BlockSpec((1, tq, A_W), lambda b, i: (b, i, 0)),
                  pl.BlockSpec((1, Lk, A_W), whole),
                  pl.BlockSpec((1, Lk, A_W), whole),
                  pl.BlockSpec((1, Lk, A_W), whole)],
        out_specs=pl.BlockSpec((1, tq, A_W), lambda b, i: (b, i, 0)),
        out_shape=jax.ShapeDtypeStruct((B, Lq, A_W), BF16),
        scratch_shapes=[pltpu.VMEM((A_HEADS, tq, A_W), BF16),
                        pltpu.VMEM((A_HEADS, 8, tq), F32),
                        pltpu.VMEM((A_HEADS, 8, tq), F32),
                        pltpu.VMEM((A_HEADS, A_W, tq), F32)],
        compiler_params=_cparams("parallel", "arbitrary"),
        name="fox_attention",
    )(q_src, eq, kb, ek, vb)


def _ret_lane_head_v():
    return lax.broadcasted_iota(jnp.int32, (1, B_W), 1) // B_HEAD_DIM


def _ret_lane_head_qk():
    return (lax.broadcasted_iota(jnp.int32, (1, B_W), 1) % LANES) // (B_HEAD_DIM // 2)


def _log_gamma(h):
    return math.log1p(-2.0 ** (-5.0 - h))


def _ret_kernel(q_ref, k_ref, v_ref, g_ref, s0_ref, invf_ref, gret_ref, y_ref, sout_ref,
                state_scr, dec_scr, *, lc, offset, nchunks):
    c = pl.program_id(1)

    @pl.when(c == 0)
    def _():
        state_scr[...] = s0_ref[0]
        r = lax.broadcasted_iota(jnp.int32, (lc, lc), 0)
        s = lax.broadcasted_iota(jnp.int32, (lc, lc), 1)
        d = (r - s).astype(F32)
        for h in range(B_HEADS):
            dec_scr[h] = jnp.where(r >= s, jnp.exp(d * _log_gamma(h)), 0.0)

    hv = _ret_lane_head_v()
    hqk = _ret_lane_head_qk()
    lg_v = jnp.zeros((1, B_W), F32)
    lg_qk = jnp.zeros((1, B_W), F32)
    for h in range(B_HEADS):
        lg_v = jnp.where(hv == h, _log_gamma(h), lg_v)
        lg_qk = jnp.where(hqk == h, _log_gamma(h), lg_qk)

    pos = (offset + c * lc + lax.broadcasted_iota(jnp.int32, (lc, LANES), 0)).astype(F32)
    ang = pos * invf_ref[...]
    cos, sin = jnp.cos(ang), jnp.sin(ang)

    def rot(x):
        x1, x2 = x[:, :LANES], x[:, LANES:]
        return jnp.concatenate([x1 * cos - x2 * sin, x1 * sin + x2 * cos], axis=1)

    qr = rot(q_ref[0])
    kr = rot(k_ref[0]) * (B_HEAD_DIM ** -0.5)
    v = v_ref[0]
    vb = v.astype(BF16)
    krb = kr.astype(BF16)

    lpos = lax.broadcasted_iota(jnp.int32, (lc, B_W), 0).astype(F32)
    state = state_scr[...]
    y = jnp.dot(qr.astype(BF16), state.astype(BF16), preferred_element_type=F32) * jnp.exp((lpos + 1.0) * lg_v)
    for h in range(B_HEADS):
        qh = jnp.where(hqk == h, qr, 0.0).astype(BF16)
        sc = lax.dot_general(qh, krb, (((1,), (1,)), ((), ())), preferred_element_type=F32) * dec_scr[h]
        yh = jnp.dot(sc.astype(BF16), vb, preferred_element_type=F32)
        y = y + jnp.where(hv == h, yh, 0.0)

    kw = (kr * jnp.exp((lc - 1.0 - lpos) * lg_qk)).astype(BF16)
    upd = lax.dot_general(kw, vb, (((0,), (0,)), ((), ())), preferred_element_type=F32)
    row_head = (lax.broadcasted_iota(jnp.int32, (B_W, B_W), 0) % LANES) // (B_HEAD_DIM // 2)
    col_head = lax.broadcasted_iota(jnp.int32, (B_W, B_W), 1) // B_HEAD_DIM
    lg_rows = jnp.zeros((B_W, B_W), F32)
    for h in range(B_HEADS):
        lg_rows = jnp.where(row_head == h, _log_gamma(h), lg_rows)
    new_state = jnp.where(row_head == col_head, state * jnp.exp(lc * lg_rows) + upd, 0.0)
    state_scr[...] = new_state

    @pl.when(c == nchunks - 1)
    def _():
        sout_ref[0] = new_state

    ri = lax.broadcasted_iota(jnp.int32, (B_W, B_W), 0) // B_HEAD_DIM
    ci = lax.broadcasted_iota(jnp.int32, (B_W, B_W), 1) // B_HEAD_DIM
    avg = jnp.where(ri == ci, 1.0 / B_HEAD_DIM, 0.0).astype(BF16)
    yc = y - _dot_sel_rhs(y, avg)
    var = _dot_sel_rhs(yc * yc, avg)
    yn = yc * lax.rsqrt(var + EPS) * gret_ref[...]
    y_ref[0] = (jax.nn.silu(g_ref[0]) * yn).astype(y_ref.dtype)


def _retention(proj3, s0, invf, g_ret, offset):
    B, L, _ = proj3.shape
    lc = _tile(L, 256)
    nch = L // lc
    blk = lambda col: pl.BlockSpec((1, lc, B_W), lambda b, c: (b, c, col // B_W))
    kern = functools.partial(_ret_kernel, lc=lc, offset=offset, nchunks=nch)
    return pl.pallas_call(
        kern,
        grid=(B, nch),
        in_specs=[blk(COL_QB), blk(COL_KB), blk(COL_VB), blk(COL_GB),
                  pl.BlockSpec((1, B_W, B_W), lambda b, c: (b, 0, 0)),
                  pl.BlockSpec((1, LANES), lambda b, c: (0, 0)),
                  pl.BlockSpec((1, B_W), lambda b, c: (0, 0))],
        out_specs=[pl.BlockSpec((1, lc, B_W), lambda b, c: (b, c, 0)),
                   pl.BlockSpec((1, B_W, B_W), lambda b, c: (b, 0, 0))],
        out_shape=[jax.ShapeDtypeStruct((B, L, B_W), BF16),
                   jax.ShapeDtypeStruct((B, B_W, B_W), F32)],
        scratch_shapes=[pltpu.VMEM((B_W, B_W), F32),
                        pltpu.VMEM((B_HEADS, lc, lc), F32)],
        compiler_params=_cparams("parallel", "arbitrary"),
        name="retention",
    )(proj3, proj3, proj3, proj3, s0, invf, g_ret.reshape(1, B_W))


def _ret_state_to_kernel(s):
    B = s.shape[0]
    half = B_HEAD_DIM // 2
    s = s.reshape(B, B_HEADS, 2, half, B_HEAD_DIM)
    eye = jnp.eye(B_HEADS, dtype=s.dtype)
    full = jnp.einsum('bhkie,hg->bkhige', s, eye)
    return full.reshape(B, B_W, B_W)


def _ret_state_from_kernel(s):
    B = s.shape[0]
    half = B_HEAD_DIM // 2
    s = s.reshape(B, 2, B_HEADS, half, B_HEADS, B_HEAD_DIM)
    d = jnp.einsum('bkhihe->bhkie', s)
    return d.reshape(B, B_HEADS, B_HEAD_DIM, B_HEAD_DIM)


def _ssd_kernel(z_ref, xbc_ref, dt_ref, s0_ref, cs0_ref, cw_ref, cb_ref, dtb_ref, alog_ref, dsk_ref, gs_ref,
                y_ref, sout_ref, state_scr, xp_scr, *, lc, nchunks):
    c = pl.program_id(1)
    PADR = 8

    @pl.when(c == 0)
    def _():
        state_scr[...] = s0_ref[0]
        xp_scr[0:PADR, :] = cs0_ref[0]

    xbc = xbc_ref[0]
    xp_scr[PADR:PADR + lc, :] = xbc
    cw = cw_ref[...]
    conv = xp_scr[PADR - 3:PADR - 3 + lc, :] * cw[0:1, :]
    for jj in range(1, C_CONV):
        conv = conv + xp_scr[PADR - 3 + jj:PADR - 3 + jj + lc, :] * cw[jj:jj + 1, :]
    act = jax.nn.silu(conv + cb_ref[...])
    xp_scr[0:PADR, :] = xp_scr[lc:lc + PADR, :]

    xs = act[:, :C_INNER]
    bm = act[:, C_INNER:C_INNER + G_W]
    cm = act[:, C_INNER + G_W:]
    dt = jax.nn.softplus(dt_ref[0] + dtb_ref[...])
    loga = dt * (-jnp.exp(alog_ref[...]))

    r = lax.broadcasted_iota(jnp.int32, (lc, lc), 0)
    s = lax.broadcasted_iota(jnp.int32, (lc, lc), 1)
    causal = r >= s
    tri = jnp.where(causal, 1.0, 0.0).astype(BF16)
    cum = _dot_sel(tri, loga)
    total = cum[lc - 1:lc, :]
    xdt = xs * dt
    xdtb = xdt.astype(BF16)
    bmb = bm.astype(BF16)
    cmb = cm.astype(BF16)
    state = state_scr[...]

    gw = C_INNER // C_GROUPS
    cum_t = [cum[:, kk * LANES:(kk + 1) * LANES].T for kk in range(C_INNER // LANES)]
    ys = []
    for g in range(C_GROUPS):
        cg = cmb[:, g * C_STATE:(g + 1) * C_STATE]
        bg = bmb[:, g * C_STATE:(g + 1) * C_STATE]
        scores = lax.dot_general(cg, bg, (((1,), (1,)), ((), ())), preferred_element_type=F32)
        y_g = jnp.dot(cg, state[:, g * gw:(g + 1) * gw].astype(BF16), preferred_element_type=F32)
        y_g = y_g * jnp.exp(cum[:, g * gw:(g + 1) * gw])
        lane_head = lax.broadcasted_iota(jnp.int32, (1, gw), 1) // C_HEAD_DIM
        xg = xdtb[:, g * gw:(g + 1) * gw]
        for hh in range(C_HEADS // C_GROUPS):
            col = g * gw + hh * C_HEAD_DIM
            cum_col = cum[:, col:col + 1]
            cum_row = cum_t[col // LANES][col % LANES:col % LANES + 1, :]
            dec = jnp.exp(jnp.where(causal, cum_col - cum_row, NEG_BIG))
            yh = jnp.dot((scores * dec).astype(BF16), xg, preferred_element_type=F32)
            y_g = y_g + jnp.where(lane_head == hh, yh, 0.0)
        ys.append(y_g)
        xw = (xdt[:, g * gw:(g + 1) * gw] * jnp.exp(total[:, g * gw:(g + 1) * gw] - cum[:, g * gw:(g + 1) * gw]))
        upd = lax.dot_general(bg, xw.astype(BF16), (((0,), (0,)), ((), ())), preferred_element_type=F32)
        state_scr[:, g * gw:(g + 1) * gw] = state[:, g * gw:(g + 1) * gw] * jnp.exp(total[:, g * gw:(g + 1) * gw]) + upd
    y = jnp.concatenate(ys, axis=1)

    @pl.when(c == nchunks - 1)
    def _():
        sout_ref[0] = state_scr[...]

    yc = (y + xs * dsk_ref[...]) * jax.nn.silu(z_ref[0])
    y_ref[0] = _rms(yc, gs_ref[...]).astype(y_ref.dtype)


def _ssd(proj3, s0, cs0, conv_w, conv_b, dtb_full, a_log, d_skip, g_ssm):
    B, L, _ = proj3.shape
    lc = _tile(L, 256)
    nch = L // lc
    rep = lambda p: jnp.repeat(p, C_HEAD_DIM).reshape(1, C_INNER)
    kern = functools.partial(_ssd_kernel, lc=lc, nchunks=nch)
    cst = lambda shape: pl.BlockSpec(shape, lambda b, c: (0,) * len(shape))
    return pl.pallas_call(
        kern,
        grid=(B, nch),
        in_specs=[pl.BlockSpec((1, lc, C_INNER), lambda b, c: (b, c, COL_ZC // C_INNER)),
                  pl.BlockSpec((1, lc, C_CONV_DIM), lambda b, c: (b, c, COL_XBC // C_CONV_DIM)),
                  pl.BlockSpec((1, lc, C_INNER), lambda b, c: (b, c, COL_DT // C_INNER)),
                  pl.BlockSpec((1, C_STATE, C_INNER), lambda b, c: (b, 0, 0)),
                  pl.BlockSpec((1, 8, C_CONV_DIM), lambda b, c: (b, 0, 0)),
                  cst((C_CONV, C_CONV_DIM)), cst((1, C_CONV_DIM)), cst((1, C_INNER)), cst((1, C_INNER)),
                  cst((1, C_INNER)), cst((1, C_INNER))],
        out_specs=[pl.BlockSpec((1, lc, C_INNER), lambda b, c: (b, c, 0)),
                   pl.BlockSpec((1, C_STATE, C_INNER), lambda b, c: (b, 0, 0))],
        out_shape=[jax.ShapeDtypeStruct((B, L, C_INNER), BF16),
                   jax.ShapeDtypeStruct((B, C_STATE, C_INNER), F32)],
        scratch_shapes=[pltpu.VMEM((C_STATE, C_INNER), F32),
                        pltpu.VMEM((lc + 8, C_CONV_DIM), F32)],
        compiler_params=_cparams("parallel", "arbitrary"),
        name="ssd",
    )(proj3, proj3, proj3, s0, cs0, conv_w, conv_b.reshape(1, C_CONV_DIM), dtb_full, rep(a_log), rep(d_skip),
      g_ssm.reshape(1, C_INNER))


def _outproj_kernel(h_ref, ya_ref, yb_ref, yc_ref, w_ref, o_ref):
    acc = jnp.dot(ya_ref[...], w_ref[0:A_W, :], preferred_element_type=F32)
    acc = acc + jnp.dot(yb_ref[...], w_ref[A_W:A_W + B_W, :], preferred_element_type=F32)
    acc = acc + jnp.dot(yc_ref[...], w_ref[A_W + B_W:, :], preferred_element_type=F32)
    o_ref[...] = h_ref[...] + acc


def _outproj(h2d, ya, yb, yc, w_out_bf16):
    T = h2d.shape[0]
    tm = _tile(T, 512)
    row = lambda w: pl.BlockSpec((tm, w), lambda i: (i, 0))
    return pl.pallas_call(
        _outproj_kernel,
        grid=(T // tm,),
        in_specs=[row(D_MODEL), row(A_W), row(B_W), row(C_INNER),
                  pl.BlockSpec((D_MODEL, D_MODEL), lambda i: (0, 0))],
        out_specs=row(D_MODEL),
        out_shape=jax.ShapeDtypeStruct((T, D_MODEL), F32),
        compiler_params=_cparams("parallel"),
        name="outproj",
    )(h2d, ya, yb, yc, w_out_bf16)


PEER_SUB = 128
PEER_STEP = 2048
PEER_CAND_COLS = 4


def _gelu_tanh(x):
    k0 = math.sqrt(2.0 / math.pi)
    inner = x * (k0 + (k0 * 0.044715) * (x * x))
    return (0.5 * x) * (1.0 + jnp.tanh(inner))


def _top16(x, want_rank):
    vals = []
    rank = jnp.full(x.shape, float(PEER_TOPK), F32) if want_rank else None
    for r in range(PEER_TOPK):
        m = jnp.max(x, axis=0, keepdims=True)
        vals.append(m)
        hit = x == m
        if want_rank:
            rank = jnp.where(hit, float(r), rank)
        x = jnp.where(hit, NEG_BIG, x)
    return jnp.concatenate(vals, axis=0), rank


def _sort_pairs(n):
    def merge(lo, hi, r):
        step = r * 2
        if step < hi - lo:
            yield from merge(lo, hi, step)
            yield from merge(lo + r, hi, step)
            yield from [(i, i + r) for i in range(lo + r, hi - r, step)]
        else:
            yield (lo, lo + r)

    def sort(lo, hi):
        if hi - lo >= 1:
            mid = lo + (hi - lo) // 2
            yield from sort(lo, mid)
            yield from sort(mid + 1, hi)
            yield from merge(lo, hi, 1)

    return tuple(sort(0, n - 1))


def _top16_values(x):
    v = [x[8 * g:8 * g + 8, :] for g in range(PEER_NKEYS // 8)]
    for a, b in _sort_pairs(PEER_NKEYS // 8):
        v[a], v[b] = jnp.maximum(v[a], v[b]), jnp.minimum(v[a], v[b])
    vals = []
    for r in range(PEER_TOPK):
        m = jnp.max(v[0], axis=0, keepdims=True)
        vals.append(m)
        hit = v[0] == m
        for k in range(PEER_TOPK - 1 - r):
            v[k] = jnp.where(hit, v[k + 1], v[k])
    return jnp.concatenate(vals, axis=0)


def _peer_select(s1, s2):
    n = s1.shape[1]
    a16 = _top16_values(s1)
    b16, rank2 = _top16(s2, True)
    ridx = lax.broadcasted_iota(jnp.int32, (PEER_TOPK, n), 0)
    sub = lax.broadcasted_iota(jnp.int32, (8, n), 0)
    lo, hi = slice(0, 8), slice(8, PEER_TOPK)
    col = lambda r2, rows: a16[rows, :] + b16[r2:r2 + 1, :]
    row = lambda r1, cols: b16[cols, :] + a16[r1:r1 + 1, :]
    assert PEER_CAND_COLS == 4 and PEER_TOPK == 16
    groups = [col(0, lo), col(0, hi), col(1, lo),
              jnp.where(sub <= 4, col(2, lo), NEG_BIG),
              jnp.where(sub <= 3, col(3, lo), row(1, lo)),
              jnp.where(sub >= 4, row(0, lo), NEG_BIG),
              row(0, hi),
              jnp.where(sub == 4, row(2, lo), NEG_BIG)]
    cand = jnp.concatenate(groups, axis=0)
    v = list(groups)
    for a_i, b_i in _sort_pairs(8):
        v[a_i], v[b_i] = jnp.maximum(v[a_i], v[b_i]), jnp.minimum(v[a_i], v[b_i])
    tau = None
    for r in range(PEER_TOPK):
        tau = jnp.max(v[0], axis=0, keepdims=True)
        hit = v[0] == tau
        depth = min(8, PEER_TOPK - r)
        for k in range(depth - 1):
            v[k] = jnp.where(hit, v[k + 1], v[k])
        if depth == 8:
            v[7] = jnp.where(hit, NEG_BIG, v[7])
    top = a16[0:1, :] + b16[0:1, :]
    z = jnp.sum(jnp.where(cand >= tau, jnp.exp(cand - top), 0.0), axis=0, keepdims=True)
    cnt = jnp.zeros(s1.shape, F32)
    for r2 in range(PEER_CAND_COLS):
        cnt = cnt + jnp.where(s1 + b16[r2:r2 + 1, :] >= tau, 1.0, 0.0)
    for r1 in range(PEER_TOPK // (PEER_CAND_COLS + 1)):
        a_r = a16[r1:r1 + 1, :]
        tail = jnp.where((ridx >= PEER_CAND_COLS) & (b16 + a_r >= tau), 1.0, 0.0)
        cnt = cnt + jnp.where(s1 == a_r, jnp.sum(tail, axis=0, keepdims=True), 0.0)
    a = jnp.exp(s1 - a16[0:1, :]) / z
    b = jnp.exp(s2 - b16[0:1, :])
    return a, cnt, rank2, b


def _peer_kernel(h_ref, g_ref, wq_ref, k1_ref, k2_ref, u_ref, vt_ref, o_ref,
                 xt_scr, a_scr, cnt_scr, rank_scr, b_scr, acc_scr, *, tm, te, ne):
    j = pl.program_id(1)
    n1 = te // PEER_NKEYS
    nchunk = tm // LANES
    nsub = te // PEER_SUB
    slabs = PEER_SUB // PEER_NKEYS

    def gate_weights(sb):
        ws = []
        for il in range(slabs):
            i1 = j * n1 + sb * slabs + il
            w = jnp.zeros((PEER_NKEYS, tm), BF16)
            for h in range(PEER_HEADS):
                row = lambda ref: jnp.concatenate(
                    [jnp.broadcast_to(ref[h, cch, pl.ds(i1, 1), :], (BF16_ROWS, LANES)) for cch in range(nchunk)],
                    axis=1).astype(BF16)
                cnt_b = jnp.tile(row(cnt_scr), (PEER_NKEYS // BF16_ROWS, 1))
                a_b = jnp.tile(row(a_scr), (PEER_NKEYS // BF16_ROWS, 1))
                w = w + jnp.where(rank_scr[h] < cnt_b, b_scr[h], jnp.zeros((), BF16)) * a_b
            ws.append(w)
        return jnp.concatenate(ws, axis=0)

    @pl.when(j == 0)
    def _():
        xn = _rms(h_ref[...], g_ref[...])
        xt_scr[...] = xn.T.astype(BF16)
        acc_scr[...] = jnp.zeros(acc_scr.shape, F32)

        def head_body(h, carry):
            base = pl.multiple_of(h * PEER_KEY_DIM, PEER_KEY_DIM)
            k1 = k1_ref[h].astype(BF16)
            k2 = k2_ref[h].astype(BF16)
            qh = jnp.dot(wq_ref[pl.ds(base, PEER_KEY_DIM), :], xt_scr[...], preferred_element_type=F32)
            for cch in range(nchunk):
                sl = slice(cch * LANES, (cch + 1) * LANES)
                q1 = qh[0:PEER_HALF, sl].astype(BF16)
                q2 = qh[PEER_HALF:, sl].astype(BF16)
                s1 = jnp.dot(k1, q1, preferred_element_type=F32)
                s2 = jnp.dot(k2, q2, preferred_element_type=F32)
                a, cnt, rank2, b = _peer_select(s1, s2)
                a_scr[h, cch] = a
                cnt_scr[h, cch] = cnt
                rank_scr[h, :, sl] = rank2.astype(BF16)
                b_scr[h, :, sl] = b.astype(BF16)
            return carry

        lax.fori_loop(0, PEER_HEADS, head_body, 0)

    acts = []
    for sb in range(nsub):
        ht = jnp.dot(u_ref[sb * PEER_SUB:(sb + 1) * PEER_SUB, :], xt_scr[...], preferred_element_type=F32)
        acts.append(_gelu_tanh(ht.astype(BF16)) * gate_weights(sb))
    acc_scr[...] += jnp.dot(vt_ref[...], jnp.concatenate(acts, axis=0), preferred_element_type=F32)

    @pl.when(j == ne - 1)
    def _():
        o_ref[...] = h_ref[...] + acc_scr[...].T


def _transpose_cast_kernel(x_ref, o_ref):
    o_ref[...] = x_ref[...].T.astype(o_ref.dtype)


def _transpose_cast(x, dtype):
    R, C = x.shape
    tr = _tile(R, 512)
    return pl.pallas_call(
        _transpose_cast_kernel,
        grid=(R // tr,),
        in_specs=[pl.BlockSpec((tr, C), lambda i: (i, 0))],
        out_specs=pl.BlockSpec((C, tr), lambda i: (0, i)),
        out_shape=jax.ShapeDtypeStruct((C, R), dtype),
        compiler_params=_cparams("parallel"),
        name="transpose_cast",
    )(x)


def _peer(h2d, g, wq_t, k1, k2, u_bf16, vt_bf16):
    T = h2d.shape[0]
    tm = _tile(T, 512)
    assert tm % LANES == 0
    te = PEER_STEP
    ne = PEER_EXPERTS // te
    kern = functools.partial(_peer_kernel, tm=tm, te=te, ne=ne)
    hk = (PEER_HEADS, PEER_NKEYS, tm)
    hkc = (PEER_HEADS, tm // LANES, PEER_NKEYS, LANES)
    return pl.pallas_call(
        kern,
        grid=(T // tm, ne),
        in_specs=[pl.BlockSpec((tm, D_MODEL), lambda i, j: (i, 0)),
                  pl.BlockSpec((1, D_MODEL), lambda i, j: (0, 0)),
                  pl.BlockSpec((PEER_HEADS * PEER_KEY_DIM, D_MODEL), lambda i, j: (0, 0)),
                  pl.BlockSpec((PEER_HEADS, PEER_NKEYS, PEER_HALF), lambda i, j: (0, 0, 0)),
                  pl.BlockSpec((PEER_HEADS, PEER_NKEYS, PEER_HALF), lambda i, j: (0, 0, 0)),
                  pl.BlockSpec((te, D_MODEL), lambda i, j: (j, 0)),
                  pl.BlockSpec((D_MODEL, te), lambda i, j: (0, j))],
        out_specs=pl.BlockSpec((tm, D_MODEL), lambda i, j: (i, 0)),
        out_shape=jax.ShapeDtypeStruct((T, D_MODEL), F32),
        scratch_shapes=[pltpu.VMEM((D_MODEL, tm), BF16),
                        pltpu.VMEM(hkc, F32), pltpu.VMEM(hkc, F32), pltpu.VMEM(hk, BF16), pltpu.VMEM(hk, BF16),
                        pltpu.VMEM((D_MODEL, tm), F32)],
        compiler_params=_cparams("parallel", "arbitrary"),
        name="peer",
    )(h2d, g.reshape(1, D_MODEL), wq_t, k1, k2, u_bf16, vt_bf16)


def _ple_kernel(h_ref, p_ref, g_ref, wg_ref, wp_ref, gf_ref, o_ref, *, final_norm):
    h = h_ref[...]
    xn = _rms(h, g_ref[...])
    gate = jax.nn.sigmoid(jnp.dot(xn.astype(BF16), wg_ref[...], preferred_element_type=F32))
    emb = jnp.dot(p_ref[...].astype(BF16), wp_ref[...], preferred_element_type=F32)
    out = h + emb * gate
    if final_norm:
        out = _rms(out, gf_ref[...])
    o_ref[...] = out


def _ple(h2d, p2d, g, wg_bf16, wp_bf16, g_final, final_norm):
    T = h2d.shape[0]
    tm = _tile(T, 512)
    kern = functools.partial(_ple_kernel, final_norm=final_norm)
    return pl.pallas_call(
        kern,
        grid=(T // tm,),
        in_specs=[pl.BlockSpec((tm, D_MODEL), lambda i: (i, 0)),
                  pl.BlockSpec((tm, PLE_DIM), lambda i: (i, 0)),
                  pl.BlockSpec((1, D_MODEL), lambda i: (0, 0)),
                  pl.BlockSpec((D_MODEL, D_MODEL), lambda i: (0, 0)),
                  pl.BlockSpec((PLE_DIM, D_MODEL), lambda i: (0, 0)),
                  pl.BlockSpec((1, D_MODEL), lambda i: (0, 0))],
        out_specs=pl.BlockSpec((tm, D_MODEL), lambda i: (i, 0)),
        out_shape=jax.ShapeDtypeStruct((T, D_MODEL), F32),
        compiler_params=_cparams("parallel"),
        name="ple",
    )(h2d, p2d, g.reshape(1, D_MODEL), wg_bf16, wp_bf16, g_final.reshape(1, D_MODEL))


def _pad_to(x, n, axis):
    pad = n - x.shape[axis]
    if pad == 0:
        return x
    widths = [(0, 0)] * x.ndim
    widths[axis] = (0, pad)
    return jnp.pad(x, widths)


def _prep_layer_weights(weights, i):
    (g_mix, w_in, b_forget, g_ret, conv_w, conv_b, dt_bias, a_log, d_skip, g_ssm, w_out,
     g_ffn, peer_wq, peer_k1, peer_k2, peer_u, peer_v, g_ple, w_ple_gate, w_ple, g_final) = weights
    w_proj, bf_pad, dtb_full = _prep_w_in(w_in[i], b_forget[i], dt_bias[i])
    return dict(w_proj=w_proj, bf_pad=bf_pad, dtb_full=dtb_full, w_out=w_out[i].astype(BF16),
                wq_t=peer_wq[i].T.astype(BF16), u=peer_u[i].astype(BF16), vt=_transpose_cast(peer_v[i], BF16),
                wg=w_ple_gate[i].astype(BF16), wp=w_ple[i].astype(BF16))


def _run_trunk(x, p, past, weights, prepped):
    (g_mix, w_in, b_forget, g_ret, conv_w, conv_b, dt_bias, a_log, d_skip, g_ssm, w_out,
     g_ffn, peer_wq, peer_k1, peer_k2, peer_u, peer_v, g_ple, w_ple_gate, w_ple, g_final) = weights
    B, L, _ = x.shape
    depth = w_in.shape[0]
    T = B * L
    P = 0 if past is None else past[0].shape[2]
    half = B_HEAD_DIM // 2
    invf = jnp.tile(ROPE_BASE ** (-jnp.arange(half, dtype=F32) / half), LANES // half).reshape(1, LANES)

    h = x.reshape(T, D_MODEL)
    outs = []
    for i in range(depth):
        pw = prepped[i]
        dtb_full = pw["dtb_full"]
        proj, lf, kb, vb, k_new, v_new = _inproj(h, g_mix[i], pw["w_proj"], pw["bf_pad"])
        proj3 = proj.reshape(B, L, PROJ_W)
        lf3 = lf.reshape(B, L, LANES)
        kb = kb.reshape(B, L, A_W)
        vb = vb.reshape(B, L, A_W)

        if past is None:
            eq, ek = _cumsum(lf3)
            ya = _fox(proj3, COL_QA, eq, kb, ek, vb, 0)
            rs0 = jnp.zeros((B, B_W, B_W), F32)
            ss0 = jnp.zeros((B, C_STATE, C_INNER), F32)
            cs0 = jnp.zeros((B, 8, C_CONV_DIM), F32)
        else:
            lk_pad = -(-(P + L) // FOX_TK) * FOX_TK
            lq_pad = -(-L // LANES) * LANES
            past_k = past[0][i].reshape(B, P, A_W).astype(BF16)
            past_v = past[1][i].reshape(B, P, A_W).astype(BF16)
            past_lf = _pad_to(past[2][i], LANES, 2)
            k_all = _pad_to(jnp.concatenate([past_k, kb], axis=1), lk_pad, 1)
            v_all = _pad_to(jnp.concatenate([past_v, vb], axis=1), lk_pad, 1)
            lf_all = _pad_to(jnp.concatenate([past_lf, lf3], axis=1), lk_pad, 1)
            eq, ek = _cumsum(lf_all)
            q_pad = _pad_to(proj3[:, :, COL_QA:COL_QA + A_W], lq_pad, 1)
            ya = _fox(q_pad, 0, _pad_to(eq[:, P:P + L], lq_pad, 1), k_all, ek, v_all, P)[:, :L]
            rs0 = _ret_state_to_kernel(past[3][i])
            ss0 = jnp.transpose(past[4][i], (0, 2, 1, 3)).reshape(B, C_STATE, C_INNER)
            cs0 = jnp.concatenate([jnp.zeros((B, 8 - (C_CONV - 1), C_CONV_DIM), F32), past[5][i]], axis=1)

        yb, ret_k = _retention(proj3, rs0, invf, g_ret[i], P)
        ret_new = _ret_state_from_kernel(ret_k)

        yc, ssm_k = _ssd(proj3, ss0, cs0, conv_w[i], conv_b[i], dtb_full, a_log[i], d_skip[i], g_ssm[i])
        ssm_new = jnp.transpose(ssm_k.reshape(B, C_STATE, C_HEADS, C_HEAD_DIM), (0, 2, 1, 3))
        xbc = proj3[:, :, COL_XBC:COL_XBC + C_CONV_DIM]
        if L >= C_CONV - 1:
            conv_new = xbc[:, L - (C_CONV - 1):]
        else:
            prev = jnp.zeros((B, C_CONV - 1, C_CONV_DIM), F32) if past is None else past[5][i]
            conv_new = jnp.concatenate([prev, xbc], axis=1)[:, -(C_CONV - 1):]

        h = _outproj(h, ya.reshape(T, A_W), yb.reshape(T, B_W), yc.reshape(T, C_INNER), pw["w_out"])
        h = _peer(h, g_ffn[i], pw["wq_t"], peer_k1[i], peer_k2[i], pw["u"], pw["vt"])
        h = _ple(h, p[i].reshape(T, PLE_DIM), g_ple[i], pw["wg"], pw["wp"], g_final, final_norm=(i == depth - 1))
        outs.append((k_new.reshape(B, L, A_HEADS, A_HEAD_DIM), v_new.reshape(B, L, A_HEADS, A_HEAD_DIM),
                     lf3[:, :, :A_HEADS], ret_new, ssm_new, conv_new))
    stacked = tuple(jnp.stack([o[n] for o in outs]) for n in range(6))
    return h.reshape(B, L, D_MODEL), stacked


def kernel(x_prompt, x_sample, cache_k_fox, cache_v_fox, cache_logf_fox, state_ret, state_ssm, state_conv, p_prompt, p_sample, g_mix, w_in, b_forget, g_ret, conv_w, conv_b, dt_bias, a_log, d_skip, g_ssm, w_out, g_ffn, peer_wq, peer_k1, peer_k2, peer_u, peer_v, g_ple, w_ple_gate, w_ple, g_final):
    weights = (g_mix, w_in, b_forget, g_ret, conv_w, conv_b, dt_bias, a_log, d_skip, g_ssm, w_out,
               g_ffn, peer_wq, peer_k1, peer_k2, peer_u, peer_v, g_ple, w_ple_gate, w_ple, g_final)
    prepped = [_prep_layer_weights(weights, i) for i in range(w_in.shape[0])]
    y_prompt, (k_p, v_p, lf_p, ret_p, ssm_p, conv_p) = _run_trunk(x_prompt, p_prompt, None, weights, prepped)
    past = (cache_k_fox, cache_v_fox, cache_logf_fox, state_ret, state_ssm, state_conv)
    y_sample, (k_s, v_s, lf_s, ret_s, ssm_s, conv_s) = _run_trunk(x_sample, p_sample, past, weights, prepped)
    return (y_prompt, y_sample, k_p, v_p, lf_p, ret_p, ssm_p, conv_p, k_s, v_s, lf_s, ret_s, ssm_s, conv_s)
```

```python
import functools
import math

import jax
import jax.numpy as jnp
from jax import lax
from jax.experimental import pallas as pl
from jax.experimental.pallas import tpu as pltpu

F32 = jnp.float32
BF16 = jnp.bfloat16

D_MODEL = 1024
PLE_DIM = 256
EPS = 1e-6
A_HEADS, A_HEAD_DIM = 4, 64
B_HEADS, B_HEAD_DIM = 4, 64
ROPE_BASE = 10000.0
C_HEADS, C_HEAD_DIM = 8, 64
C_INNER = C_HEADS * C_HEAD_DIM
C_GROUPS, C_STATE, C_CONV = 2, 128, 4
C_CONV_DIM = C_INNER + 2 * C_GROUPS * C_STATE
PEER_HEADS, PEER_NKEYS, PEER_KEY_DIM, PEER_TOPK = 8, 128, 256, 16
PEER_HALF = PEER_KEY_DIM // 2
PEER_EXPERTS = PEER_NKEYS * PEER_NKEYS

LANES = 128
BF16_ROWS = 16
A_W = A_HEADS * A_HEAD_DIM
B_W = B_HEADS * B_HEAD_DIM
G_W = C_GROUPS * C_STATE

COL_QA = 0
COL_KA = COL_QA + A_W
COL_VA = COL_KA + A_W
COL_QB = COL_VA + A_W
COL_KB = COL_QB + B_W
COL_VB = COL_KB + B_W
COL_GB = COL_VB + B_W
COL_FA = COL_GB + B_W
COL_ZC = COL_FA + 2 * LANES
COL_DT = COL_ZC + C_INNER
COL_XBC = COL_DT + C_INNER
PROJ_W = COL_XBC + C_CONV_DIM
assert all(c % A_W == 0 for c in (COL_QA, COL_KA, COL_VA, COL_QB, COL_KB, COL_VB, COL_GB))
assert COL_ZC % C_INNER == 0 and COL_DT % C_INNER == 0 and COL_XBC % C_CONV_DIM == 0 and COL_FA % LANES == 0

NEG_BIG = -1e30
VMEM_LIMIT_BYTES = 52 * 1024 * 1024


def _cparams(*sem):
    return pltpu.CompilerParams(dimension_semantics=sem, vmem_limit_bytes=VMEM_LIMIT_BYTES)


def _tile(n, pref):
    t = min(n, pref)
    while n % t:
        t -= 8
    assert t > 0
    return t


def _rms(x, g):
    return x * lax.rsqrt(jnp.mean(x * x, axis=-1, keepdims=True) + EPS) * g


def _split3(x):
    hi = x.astype(BF16)
    r1 = x - hi.astype(F32)
    mid = r1.astype(BF16)
    lo = (r1 - mid.astype(F32)).astype(BF16)
    return hi, mid, lo


def _dot_sel(sel_bf16, x):
    hi, mid, lo = _split3(x)
    d = lambda p: jnp.dot(sel_bf16, p, preferred_element_type=F32)
    return d(hi) + d(mid) + d(lo)


def _dot_sel_rhs(x, sel_bf16):
    hi, mid, lo = _split3(x)
    d = lambda p: jnp.dot(p, sel_bf16, preferred_element_type=F32)
    return d(hi) + d(mid) + d(lo)


def _inproj_kernel(x_ref, g_ref, w_ref, bf_ref, o_ref, lf_ref, kb_ref, vb_ref, kf_ref, vf_ref):
    xn = _rms(x_ref[...], g_ref[...])
    y = jnp.dot(xn.astype(BF16), w_ref[...], preferred_element_type=F32)
    o_ref[...] = y
    kf_ref[...] = y[:, COL_KA:COL_KA + A_W]
    vf_ref[...] = y[:, COL_VA:COL_VA + A_W]
    kb_ref[...] = y[:, COL_KA:COL_KA + A_W].astype(BF16)
    vb_ref[...] = y[:, COL_VA:COL_VA + A_W].astype(BF16)
    fa = y[:, COL_FA:COL_FA + LANES] + bf_ref[...]
    lane = lax.broadcasted_iota(jnp.int32, fa.shape, 1)
    lf_ref[...] = jnp.where(lane < A_HEADS, jax.nn.log_sigmoid(fa), 0.0)


def _inproj(x2d, g, w, bf_pad):
    T = x2d.shape[0]
    tm = _tile(T, 256)
    return pl.pallas_call(
        _inproj_kernel,
        grid=(T // tm,),
        in_specs=[pl.BlockSpec((tm, D_MODEL), lambda i: (i, 0)),
                  pl.BlockSpec((1, D_MODEL), lambda i: (0, 0)),
                  pl.BlockSpec((D_MODEL, PROJ_W), lambda i: (0, 0)),
                  pl.BlockSpec((1, LANES), lambda i: (0, 0))],
        out_specs=[pl.BlockSpec((tm, PROJ_W), lambda i: (i, 0)),
                   pl.BlockSpec((tm, LANES), lambda i: (i, 0)),
                   pl.BlockSpec((tm, A_W), lambda i: (i, 0)),
                   pl.BlockSpec((tm, A_W), lambda i: (i, 0)),
                   pl.BlockSpec((tm, A_W), lambda i: (i, 0)),
                   pl.BlockSpec((tm, A_W), lambda i: (i, 0))],
        out_shape=[jax.ShapeDtypeStruct((T, PROJ_W), F32),
                   jax.ShapeDtypeStruct((T, LANES), F32),
                   jax.ShapeDtypeStruct((T, A_W), BF16),
                   jax.ShapeDtypeStruct((T, A_W), BF16),
                   jax.ShapeDtypeStruct((T, A_W), F32),
                   jax.ShapeDtypeStruct((T, A_W), F32)],
        compiler_params=_cparams("parallel"),
        name="inproj",
    )(x2d, g.reshape(1, D_MODEL), w, bf_pad)


def _prep_w_in(w_in, b_forget, dt_bias):
    sizes = (A_W, A_W, A_W, A_HEADS, B_W, B_W, B_W, B_W, C_INNER, C_INNER, G_W, G_W, C_HEADS)
    pts = []
    acc = 0
    for s in sizes[:-1]:
        acc += s
        pts.append(acc)
    qa, ka, va, fa, qb, kb, vb, gb, zc, xc, bc, cc, dtc = jnp.split(w_in, pts, axis=-1)

    def rot_perm(w):
        w = w.reshape(D_MODEL, B_HEADS, 2, B_HEAD_DIM // 2)
        return jnp.transpose(w, (0, 2, 1, 3)).reshape(D_MODEL, B_W)

    z = lambda n: jnp.zeros((D_MODEL, n), w_in.dtype)
    w = jnp.concatenate([qa, ka, va, rot_perm(qb), rot_perm(kb), vb, gb, fa, z(2 * LANES - A_HEADS), zc,
                         jnp.repeat(dtc, C_HEAD_DIM, axis=1), xc, bc, cc], axis=1)
    assert w.shape[1] == PROJ_W
    bf_pad = jnp.concatenate([b_forget, jnp.zeros((LANES - A_HEADS,), F32)]).reshape(1, LANES)
    dtb_full = jnp.repeat(dt_bias, C_HEAD_DIM).reshape(1, C_INNER)
    return w.astype(BF16), bf_pad, dtb_full


FOX_TQ = 512
FOX_TK = 512
FOX_TQ_SUB = 512
FOX_XTRA = 6


def _fox_xtra_base(h):
    return A_HEAD_DIM * ((h + 1) % A_HEADS)


def _cumsum_kernel(lf_ref, eq_ref, ek_ref, carry_scr):
    @pl.when(pl.program_id(1) == 0)
    def _():
        carry_scr[...] = jnp.zeros_like(carry_scr)

    lf = lf_ref[0]
    tc = lf.shape[0]
    r = lax.broadcasted_iota(jnp.int32, (tc, tc), 0)
    c = lax.broadcasted_iota(jnp.int32, (tc, tc), 1)
    tri = jnp.where(r >= c, 1.0, 0.0).astype(BF16)
    cum = _dot_sel(tri, lf) + carry_scr[0:1, :]
    carry_scr[...] = jnp.broadcast_to(cum[tc - 1:tc, :], carry_scr.shape)

    pieces = _split3(cum)
    src = lax.broadcasted_iota(jnp.int32, (LANES, A_W), 0)
    dst = lax.broadcasted_iota(jnp.int32, (LANES, A_W), 1)
    lane = lax.broadcasted_iota(jnp.int32, (1, A_W), 1)
    one_q = jnp.zeros((1, A_W), F32)
    one_k = jnp.zeros((1, A_W), F32)
    for h in range(A_HEADS):
        base = _fox_xtra_base(h)
        one_q = jnp.where((lane >= base + 3) & (lane < base + FOX_XTRA), 1.0, one_q)
        one_k = jnp.where((lane >= base) & (lane < base + 3), 1.0, one_k)
    eq, ek = one_q, one_k
    for k, piece in enumerate(pieces):
        hit_q = functools.reduce(jnp.logical_or, [(src == h) & (dst == _fox_xtra_base(h) + k) for h in range(A_HEADS)])
        hit_k = functools.reduce(jnp.logical_or,
                                 [(src == h) & (dst == _fox_xtra_base(h) + 3 + k) for h in range(A_HEADS)])
        eq = eq + jnp.dot(piece, jnp.where(hit_q, 1.0, 0.0).astype(BF16), preferred_element_type=F32)
        ek = ek + jnp.dot(piece, jnp.where(hit_k, -1.0, 0.0).astype(BF16), preferred_element_type=F32)
    eq_ref[0] = eq.astype(BF16)
    ek_ref[0] = ek.astype(BF16)


def _cumsum(lf):
    B, L, _ = lf.shape
    tc = _tile(L, 512)
    return pl.pallas_call(
        _cumsum_kernel,
        grid=(B, L // tc),
        in_specs=[pl.BlockSpec((1, tc, LANES), lambda b, j: (b, j, 0))],
        out_specs=[pl.BlockSpec((1, tc, A_W), lambda b, j: (b, j, 0)),
                   pl.BlockSpec((1, tc, A_W), lambda b, j: (b, j, 0))],
        out_shape=[jax.ShapeDtypeStruct((B, L, A_W), BF16),
                   jax.ShapeDtypeStruct((B, L, A_W), BF16)],
        scratch_shapes=[pltpu.VMEM((8, LANES), F32)],
        compiler_params=_cparams("parallel", "arbitrary"),
        name="fox_cumsum",
    )(lf)


def _fox_kernel(q_ref, eq_ref, kb_ref, ek_ref, vb_ref, o_ref, qp_scr, m_scr, l_scr, acc_scr, *,
                tq, tk, offset, nk):
    i = pl.program_id(1)
    scale = A_HEAD_DIM ** -0.5
    q_lo = offset + i * tq

    def in_head(lane, h):
        return (lane >= h * A_HEAD_DIM) & (lane < (h + 1) * A_HEAD_DIM)

    def in_xtra(lane, h):
        return (lane >= _fox_xtra_base(h)) & (lane < _fox_xtra_base(h) + FOX_XTRA)

    lane_q = lax.broadcasted_iota(jnp.int32, (tq, A_W), 1)
    q = (q_ref[0] * scale).astype(BF16)
    eq = eq_ref[0]
    for h in range(A_HEADS):
        qp_scr[h] = jnp.where(in_xtra(lane_q, h), eq, jnp.where(in_head(lane_q, h), q, jnp.zeros((), BF16)))
    m_scr[...] = jnp.full(m_scr.shape, NEG_BIG, F32)
    l_scr[...] = jnp.zeros(l_scr.shape, F32)
    acc_scr[...] = jnp.zeros(acc_scr.shape, F32)

    lane_k = lax.broadcasted_iota(jnp.int32, (tk, A_W), 1)

    def block(jj, masked):
        ks = pl.multiple_of(jj * tk, tk)
        k = kb_ref[0, pl.ds(ks, tk), :]
        ek = ek_ref[0, pl.ds(ks, tk), :]
        v = vb_ref[0, pl.ds(ks, tk), :]
        if masked:
            kpos = ks + lax.broadcasted_iota(jnp.int32, (tk, tq), 0)
            qpos = q_lo + lax.broadcasted_iota(jnp.int32, (tk, tq), 1)
            vis = kpos <= qpos
        tqs = min(tq, FOX_TQ_SUB)
        ps, alphas = [], []
        for h in range(A_HEADS):
            kp = jnp.where(in_xtra(lane_k, h), ek, k)
            for qs in range(tq // tqs):
                cols = slice(qs * tqs, (qs + 1) * tqs)
                s = lax.dot_general(kp, qp_scr[h, cols, :], (((1,), (1,)), ((), ())),
                                    preferred_element_type=F32)
                if masked:
                    s = jnp.where(vis[:, cols], s, NEG_BIG)
                m_prev = m_scr[h, 0:1, cols]
                m_next = jnp.maximum(m_prev, jnp.max(s, axis=0, keepdims=True))
                alpha = jnp.exp(m_prev - m_next)
                p = jnp.exp(s - m_next)
                l_scr[h, :, cols] = jnp.broadcast_to(
                    alpha * l_scr[h, 0:1, cols] + jnp.sum(p, axis=0, keepdims=True), (8, tqs))
                m_scr[h, :, cols] = jnp.broadcast_to(m_next, (8, tqs))
                ps.append(p.astype(BF16))
                alphas.append(alpha)
        alphas = [jnp.concatenate(alphas[h * (tq // tqs):(h + 1) * (tq // tqs)], axis=1) for h in range(A_HEADS)]
        pv = lax.dot_general(v, jnp.concatenate(ps, axis=1), (((0,), (0,)), ((), ())),
                             preferred_element_type=F32)
        for h in range(A_HEADS):
            acc_scr[h] = acc_scr[h] * alphas[h] + pv[:, h * tq:(h + 1) * tq]

    n_full = (q_lo + 1) // tk
    n_vis = jnp.minimum((q_lo + tq - 1) // tk + 1, nk)

    def full_body(jj, carry):
        block(jj, False)
        return carry

    def edge_body(jj, carry):
        block(jj, True)
        return carry

    lax.fori_loop(0, n_full, full_body, 0)
    lax.fori_loop(n_full, n_vis, edge_body, 0)

    row = lax.broadcasted_iota(jnp.int32, (A_W, tq), 0)
    out_t = jnp.zeros((A_W, tq), F32)
    for h in range(A_HEADS):
        out_t = out_t + jnp.where(in_head(row, h), acc_scr[h] / l_scr[h, 0:1, :], 0.0)
    o_ref[0] = out_t.T.astype(o_ref.dtype)


def _fox(q_src, q_col, eq, kb, ek, vb, offset):
    B, Lq, _ = q_src.shape
    Lk = kb.shape[1]
    tq = _tile(Lq, FOX_TQ)
    tk = _tile(Lk, FOX_TK)
    assert tq % LANES == 0 and tk % LANES == 0
    nq, nk = Lq // tq, Lk // tk
    whole = lambda b, i: (b, 0, 0)
    kern = functools.partial(_fox_kernel, tq=tq, tk=tk, offset=offset, nk=nk)
    return pl.pallas_call(
        kern,
        grid=(B, nq),
        in_specs=[pl.BlockSpec((1, tq, A_W), lambda b, i: (b, i, q_col // A_W)),
                  pl.BlockSpec((1, tq, A_W), lambda b, i: (b, i, 0)),
                  pl.BlockSpec((1, Lk, A_W), whole),
                  pl.BlockSpec((1, Lk, A_W), whole),
                  pl.BlockSpec((1, Lk, A_W), whole)],
        out_specs=pl.BlockSpec((1, tq, A_W), lambda b, i: (b, i, 0)),
        out_shape=jax.ShapeDtypeStruct((B, Lq, A_W), BF16),
        scratch_shapes=[pltpu.VMEM((A_HEADS, tq, A_W), BF16),
                        pltpu.VMEM((A_HEADS, 8, tq), F32),
                        pltpu.VMEM((A_HEADS, 8, tq), F32),
                        pltpu.VMEM((A_HEADS, A_W, tq), F32)],
        compiler_params=_cparams("parallel", "arbitrary"),
        name="fox_attention",
    )(q_src, eq, kb, ek, vb)


def _ret_lane_head_v():
    return lax.broadcasted_iota(jnp.int32, (1, B_W), 1) // B_HEAD_DIM


def _ret_lane_head_qk():
    return (lax.broadcasted_iota(jnp.int32, (1, B_W), 1) % LANES) // (B_HEAD_DIM // 2)


def _log_gamma(h):
    return math.log1p(-2.0 ** (-5.0 - h))


def _ret_kernel(q_ref, k_ref, v_ref, g_ref, s0_ref, invf_ref, gret_ref, y_ref, sout_ref,
                state_scr, dec_scr, *, lc, offset, nchunks):
    c = pl.program_id(1)

    @pl.when(c == 0)
    def _():
        state_scr[...] = s0_ref[0]
        r = lax.broadcasted_iota(jnp.int32, (lc, lc), 0)
        s = lax.broadcasted_iota(jnp.int32, (lc, lc), 1)
        d = (r - s).astype(F32)
        for h in range(B_HEADS):
            dec_scr[h] = jnp.where(r >= s, jnp.exp(d * _log_gamma(h)), 0.0)

    hv = _ret_lane_head_v()
    hqk = _ret_lane_head_qk()
    lg_v = jnp.zeros((1, B_W), F32)
    lg_qk = jnp.zeros((1, B_W), F32)
    for h in range(B_HEADS):
        lg_v = jnp.where(hv == h, _log_gamma(h), lg_v)
        lg_qk = jnp.where(hqk == h, _log_gamma(h), lg_qk)

    pos = (offset + c * lc + lax.broadcasted_iota(jnp.int32, (lc, LANES), 0)).astype(F32)
    ang = pos * invf_ref[...]
    cos, sin = jnp.cos(ang), jnp.sin(ang)

    def rot(x):
        x1, x2 = x[:, :LANES], x[:, LANES:]
        return jnp.concatenate([x1 * cos - x2 * sin, x1 * sin + x2 * cos], axis=1)

    qr = rot(q_ref[0])
    kr = rot(k_ref[0]) * (B_HEAD_DIM ** -0.5)
    v = v_ref[0]
    vb = v.astype(BF16)
    krb = kr.astype(BF16)

    lpos = lax.broadcasted_iota(jnp.int32, (lc, B_W), 0).astype(F32)
    state = state_scr[...]
    y = jnp.dot(qr.astype(BF16), state.astype(BF16), preferred_element_type=F32) * jnp.exp((lpos + 1.0) * lg_v)
    for h in range(B_HEADS):
        qh = jnp.where(hqk == h, qr, 0.0).astype(BF16)
        sc = lax.dot_general(qh, krb, (((1,), (1,)), ((), ())), preferred_element_type=F32) * dec_scr[h]
        yh = jnp.dot(sc.astype(BF16), vb, preferred_element_type=F32)
        y = y + jnp.where(hv == h, yh, 0.0)

    kw = (kr * jnp.exp((lc - 1.0 - lpos) * lg_qk)).astype(BF16)
    upd = lax.dot_general(kw, vb, (((0,), (0,)), ((), ())), preferred_element_type=F32)
    row_head = (lax.broadcasted_iota(jnp.int32, (B_W, B_W), 0) % LANES) // (B_HEAD_DIM // 2)
    col_head = lax.broadcasted_iota(jnp.int32, (B_W, B_W), 1) // B_HEAD_DIM
    lg_rows = jnp.zeros((B_W, B_W), F32)
    for h in range(B_HEADS):
        lg_rows = jnp.where(row_head == h, _log_gamma(h), lg_rows)
    new_state = jnp.where(row_head == col_head, state * jnp.exp(lc * lg_rows) + upd, 0.0)
    state_scr[...] = new_state

    @pl.when(c == nchunks - 1)
    def _():
        sout_ref[0] = new_state

    ri = lax.broadcasted_iota(jnp.int32, (B_W, B_W), 0) // B_HEAD_DIM
    ci = lax.broadcasted_iota(jnp.int32, (B_W, B_W), 1) // B_HEAD_DIM
    avg = jnp.where(ri == ci, 1.0 / B_HEAD_DIM, 0.0).astype(BF16)
    yc = y - _dot_sel_rhs(y, avg)
    var = _dot_sel_rhs(yc * yc, avg)
    yn = yc * lax.rsqrt(var + EPS) * gret_ref[...]
    y_ref[0] = (jax.nn.silu(g_ref[0]) * yn).astype(y_ref.dtype)


def _retention(proj3, s0, invf, g_ret, offset):
    B, L, _ = proj3.shape
    lc = _tile(L, 256)
    nch = L // lc
    blk = lambda col: pl.BlockSpec((1, lc, B_W), lambda b, c: (b, c, col // B_W))
    kern = functools.partial(_ret_kernel, lc=lc, offset=offset, nchunks=nch)
    return pl.pallas_call(
        kern,
        grid=(B, nch),
        in_specs=[blk(COL_QB), blk(COL_KB), blk(COL_VB), blk(COL_GB),
                  pl.BlockSpec((1, B_W, B_W), lambda b, c: (b, 0, 0)),
                  pl.BlockSpec((1, LANES), lambda b, c: (0, 0)),
                  pl.BlockSpec((1, B_W), lambda b, c: (0, 0))],
        out_specs=[pl.BlockSpec((1, lc, B_W), lambda b, c: (b, c, 0)),
                   pl.BlockSpec((1, B_W, B_W), lambda b, c: (b, 0, 0))],
        out_shape=[jax.ShapeDtypeStruct((B, L, B_W), BF16),
                   jax.ShapeDtypeStruct((B, B_W, B_W), F32)],
        scratch_shapes=[pltpu.VMEM((B_W, B_W), F32),
                        pltpu.VMEM((B_HEADS, lc, lc), F32)],
        compiler_params=_cparams("parallel", "arbitrary"),
        name="retention",
    )(proj3, proj3, proj3, proj3, s0, invf, g_ret.reshape(1, B_W))


def _ret_state_to_kernel(s):
    B = s.shape[0]
    half = B_HEAD_DIM // 2
    s = s.reshape(B, B_HEADS, 2, half, B_HEAD_DIM)
    eye = jnp.eye(B_HEADS, dtype=s.dtype)
    full = jnp.einsum('bhkie,hg->bkhige', s, eye)
    return full.reshape(B, B_W, B_W)


def _ret_state_from_kernel(s):
    B = s.shape[0]
    half = B_HEAD_DIM // 2
    s = s.reshape(B, 2, B_HEADS, half, B_HEADS, B_HEAD_DIM)
    d = jnp.einsum('bkhihe->bhkie', s)
    return d.reshape(B, B_HEADS, B_HEAD_DIM, B_HEAD_DIM)


def _ssd_kernel(z_ref, xbc_ref, dt_ref, s0_ref, cs0_ref, cw_ref, cb_ref, dtb_ref, alog_ref, dsk_ref, gs_ref,
                y_ref, sout_ref, state_scr, xp_scr, *, lc, nchunks):
    c = pl.program_id(1)
    PADR = 8

    @pl.when(c == 0)
    def _():
        state_scr[...] = s0_ref[0]
        xp_scr[0:PADR, :] = cs0_ref[0]

    xbc = xbc_ref[0]
    xp_scr[PADR:PADR + lc, :] = xbc
    cw = cw_ref[...]
    conv = xp_scr[PADR - 3:PADR - 3 + lc, :] * cw[0:1, :]
    for jj in range(1, C_CONV):
        conv = conv + xp_scr[PADR - 3 + jj:PADR - 3 + jj + lc, :] * cw[jj:jj + 1, :]
    act = jax.nn.silu(conv + cb_ref[...])
    xp_scr[0:PADR, :] = xp_scr[lc:lc + PADR, :]

    xs = act[:, :C_INNER]
    bm = act[:, C_INNER:C_INNER + G_W]
    cm = act[:, C_INNER + G_W:]
    dt = jax.nn.softplus(dt_ref[0] + dtb_ref[...])
    loga = dt * (-jnp.exp(alog_ref[...]))

    r = lax.broadcasted_iota(jnp.int32, (lc, lc), 0)
    s = lax.broadcasted_iota(jnp.int32, (lc, lc), 1)
    causal = r >= s
    tri = jnp.where(causal, 1.0, 0.0).astype(BF16)
    cum = _dot_sel(tri, loga)
    total = cum[lc - 1:lc, :]
    xdt = xs * dt
    xdtb = xdt.astype(BF16)
    bmb = bm.astype(BF16)
    cmb = cm.astype(BF16)
    state = state_scr[...]

    gw = C_INNER // C_GROUPS
    cum_t = [cum[:, kk * LANES:(kk + 1) * LANES].T for kk in range(C_INNER // LANES)]
    ys = []
    for g in range(C_GROUPS):
        cg = cmb[:, g * C_STATE:(g + 1) * C_STATE]
        bg = bmb[:, g * C_STATE:(g + 1) * C_STATE]
        scores = lax.dot_general(cg, bg, (((1,), (1,)), ((), ())), preferred_element_type=F32)
        y_g = jnp.dot(cg, state[:, g * gw:(g + 1) * gw].astype(BF16), preferred_element_type=F32)
        y_g = y_g * jnp.exp(cum[:, g * gw:(g + 1) * gw])
        lane_head = lax.broadcasted_iota(jnp.int32, (1, gw), 1) // C_HEAD_DIM
        xg = xdtb[:, g * gw:(g + 1) * gw]
        for hh in range(C_HEADS // C_GROUPS):
            col = g * gw + hh * C_HEAD_DIM
            cum_col = cum[:, col:col + 1]
            cum_row = cum_t[col // LANES][col % LANES:col % LANES + 1, :]
            dec = jnp.exp(jnp.where(causal, cum_col - cum_row, NEG_BIG))
            yh = jnp.dot((scores * dec).astype(BF16), xg, preferred_element_type=F32)
            y_g = y_g + jnp.where(lane_head == hh, yh, 0.0)
        ys.append(y_g)
        xw = (xdt[:, g * gw:(g + 1) * gw] * jnp.exp(total[:, g * gw:(g + 1) * gw] - cum[:, g * gw:(g + 1) * gw]))
        upd = lax.dot_general(bg, xw.astype(BF16), (((0,), (0,)), ((), ())), preferred_element_type=F32)
        state_scr[:, g * gw:(g + 1) * gw] = state[:, g * gw:(g + 1) * gw] * jnp.exp(total[:, g * gw:(g + 1) * gw]) + upd
    y = jnp.concatenate(ys, axis=1)

    @pl.when(c == nchunks - 1)
    def _():
        sout_ref[0] = state_scr[...]

    yc = (y + xs * dsk_ref[...]) * jax.nn.silu(z_ref[0])
    y_ref[0] = _rms(yc, gs_ref[...]).astype(y_ref.dtype)


def _ssd(proj3, s0, cs0, conv_w, conv_b, dtb_full, a_log, d_skip, g_ssm):
    B, L, _ = proj3.shape
    lc = _tile(L, 256)
    nch = L // lc
    rep = lambda p: jnp.repeat(p, C_HEAD_DIM).reshape(1, C_INNER)
    kern = functools.partial(_ssd_kernel, lc=lc, nchunks=nch)
    cst = lambda shape: pl.BlockSpec(shape, lambda b, c: (0,) * len(shape))
    return pl.pallas_call(
        kern,
        grid=(B, nch),
        in_specs=[pl.BlockSpec((1, lc, C_INNER), lambda b, c: (b, c, COL_ZC // C_INNER)),
                  pl.BlockSpec((1, lc, C_CONV_DIM), lambda b, c: (b, c, COL_XBC // C_CONV_DIM)),
                  pl.BlockSpec((1, lc, C_INNER), lambda b, c: (b, c, COL_DT // C_INNER)),
                  pl.BlockSpec((1, C_STATE, C_INNER), lambda b, c: (b, 0, 0)),
                  pl.BlockSpec((1, 8, C_CONV_DIM), lambda b, c: (b, 0, 0)),
                  cst((C_CONV, C_CONV_DIM)), cst((1, C_CONV_DIM)), cst((1, C_INNER)), cst((1, C_INNER)),
                  cst((1, C_INNER)), cst((1, C_INNER))],
        out_specs=[pl.BlockSpec((1, lc, C_INNER), lambda b, c: (b, c, 0)),
                   pl.BlockSpec((1, C_STATE, C_INNER), lambda b, c: (b, 0, 0))],
        out_shape=[jax.ShapeDtypeStruct((B, L, C_INNER), BF16),
                   jax.ShapeDtypeStruct((B, C_STATE, C_INNER), F32)],
        scratch_shapes=[pltpu.VMEM((C_STATE, C_INNER), F32),
                        pltpu.VMEM((lc + 8, C_CONV_DIM), F32)],
        compiler_params=_cparams("parallel", "arbitrary"),
        name="ssd",
    )(proj3, proj3, proj3, s0, cs0, conv_w, conv_b.reshape(1, C_CONV_DIM), dtb_full, rep(a_log), rep(d_skip),
      g_ssm.reshape(1, C_INNER))


def _outproj_kernel(h_ref, ya_ref, yb_ref, yc_ref, w_ref, o_ref):
    acc = jnp.dot(ya_ref[...], w_ref[0:A_W, :], preferred_element_type=F32)
    acc = acc + jnp.dot(yb_ref[...], w_ref[A_W:A_W + B_W, :], preferred_element_type=F32)
    acc = acc + jnp.dot(yc_ref[...], w_ref[A_W + B_W:, :], preferred_element_type=F32)
    o_ref[...] = h_ref[...] + acc


def _outproj(h2d, ya, yb, yc, w_out_bf16):
    T = h2d.shape[0]
    tm = _tile(T, 512)
    row = lambda w: pl.BlockSpec((tm, w), lambda i: (i, 0))
    return pl.pallas_call(
        _outproj_kernel,
        grid=(T // tm,),
        in_specs=[row(D_MODEL), row(A_W), row(B_W), row(C_INNER),
                  pl.BlockSpec((D_MODEL, D_MODEL), lambda i: (0, 0))],
        out_specs=row(D_MODEL),
        out_shape=jax.ShapeDtypeStruct((T, D_MODEL), F32),
        compiler_params=_cparams("parallel"),
        name="outproj",
    )(h2d, ya, yb, yc, w_out_bf16)


PEER_SUB = 128
PEER_STEP = 2048
PEER_CAND_COLS = 4


def _gelu_tanh(x):
    k0 = math.sqrt(2.0 / math.pi)
    inner = x * (k0 + (k0 * 0.044715) * (x * x))
    return (0.5 * x) * (1.0 + jnp.tanh(inner))


def _top16(x, want_rank):
    vals = []
    rank = jnp.full(x.shape, float(PEER_TOPK), F32) if want_rank else None
    for r in range(PEER_TOPK):
        m = jnp.max(x, axis=0, keepdims=True)
        vals.append(m)
        hit = x == m
        if want_rank:
            rank = jnp.where(hit, float(r), rank)
        x = jnp.where(hit, NEG_BIG, x)
    return jnp.concatenate(vals, axis=0), rank


def _sort_pairs(n):
    def merge(lo, hi, r):
        step = r * 2
        if step < hi - lo:
            yield from merge(lo, hi, step)
            yield from merge(lo + r, hi, step)
            yield from [(i, i + r) for i in range(lo + r, hi - r, step)]
        else:
            yield (lo, lo + r)

    def sort(lo, hi):
        if hi - lo >= 1:
            mid = lo + (hi - lo) // 2
            yield from sort(lo, mid)
            yield from sort(mid + 1, hi)
            yield from merge(lo, hi, 1)

    return tuple(sort(0, n - 1))


def _top16_values(x):
    v = [x[8 * g:8 * g + 8, :] for g in range(PEER_NKEYS // 8)]
    for a, b in _sort_pairs(PEER_NKEYS // 8):
        v[a], v[b] = jnp.maximum(v[a], v[b]), jnp.minimum(v[a], v[b])
    vals = []
    for r in range(PEER_TOPK):
        m = jnp.max(v[0], axis=0, keepdims=True)
        vals.append(m)
        hit = v[0] == m
        for k in range(PEER_TOPK - 1 - r):
            v[k] = jnp.where(hit, v[k + 1], v[k])
    return jnp.concatenate(vals, axis=0)


def _peer_select(s1, s2):
    n = s1.shape[1]
    a16 = _top16_values(s1)
    b16, rank2 = _top16(s2, True)
    ridx = lax.broadcasted_iota(jnp.int32, (PEER_TOPK, n), 0)
    sub = lax.broadcasted_iota(jnp.int32, (8, n), 0)
    lo, hi = slice(0, 8), slice(8, PEER_TOPK)
    col = lambda r2, rows: a16[rows, :] + b16[r2:r2 + 1, :]
    row = lambda r1, cols: b16[cols, :] + a16[r1:r1 + 1, :]
    assert PEER_CAND_COLS == 4 and PEER_TOPK == 16
    groups = [col(0, lo), col(0, hi), col(1, lo),
              jnp.where(sub <= 4, col(2, lo), NEG_BIG),
              jnp.where(sub <= 3, col(3, lo), row(1, lo)),
              jnp.where(sub >= 4, row(0, lo), NEG_BIG),
              row(0, hi),
              jnp.where(sub == 4, row(2, lo), NEG_BIG)]
    cand = jnp.concatenate(groups, axis=0)
    v = list(groups)
    for a_i, b_i in _sort_pairs(8):
        v[a_i], v[b_i] = jnp.maximum(v[a_i], v[b_i]), jnp.minimum(v[a_i], v[b_i])
    tau = None
    for r in range(PEER_TOPK):
        tau = jnp.max(v[0], axis=0, keepdims=True)
        hit = v[0] == tau
        depth = min(8, PEER_TOPK - r)
        for k in range(depth - 1):
            v[k] = jnp.where(hit, v[k + 1], v[k])
        if depth == 8:
            v[7] = jnp.where(hit, NEG_BIG, v[7])
    top = a16[0:1, :] + b16[0:1, :]
    z = jnp.sum(jnp.where(cand >= tau, jnp.exp(cand - top), 0.0), axis=0, keepdims=True)
    cnt = jnp.zeros(s1.shape, F32)
    for r2 in range(PEER_CAND_COLS):
        cnt = cnt + jnp.where(s1 + b16[r2:r2 + 1, :] >= tau, 1.0, 0.0)
    for r1 in range(PEER_TOPK // (PEER_CAND_COLS + 1)):
        a_r = a16[r1:r1 + 1, :]
        tail = jnp.where((ridx >= PEER_CAND_COLS) & (b16 + a_r >= tau), 1.0, 0.0)
        cnt = cnt + jnp.where(s1 == a_r, jnp.sum(tail, axis=0, keepdims=True), 0.0)
    a = jnp.exp(s1 - a16[0:1, :]) / z
    b = jnp.exp(s2 - b16[0:1, :])
    return a, cnt, rank2, b


def _peer_kernel(h_ref, g_ref, wq_ref, wql_ref, k1_ref, k2_ref, u_ref, vt_ref, o_ref,
                 xt_scr, xlo_scr, a_scr, cnt_scr, rank_scr, b_scr, acc_scr, *, tm, te, ne):
    j = pl.program_id(1)
    n1 = te // PEER_NKEYS
    nchunk = tm // LANES
    nsub = te // PEER_SUB
    slabs = PEER_SUB // PEER_NKEYS

    def gate_weights(sb):
        ws = []
        for il in range(slabs):
            i1 = j * n1 + sb * slabs + il
            w = jnp.zeros((PEER_NKEYS, tm), BF16)
            for h in range(PEER_HEADS):
                row = lambda ref: jnp.concatenate(
                    [jnp.broadcast_to(ref[h, cch, pl.ds(i1, 1), :], (BF16_ROWS, LANES)) for cch in range(nchunk)],
                    axis=1).astype(BF16)
                cnt_b = jnp.tile(row(cnt_scr), (PEER_NKEYS // BF16_ROWS, 1))
                a_b = jnp.tile(row(a_scr), (PEER_NKEYS // BF16_ROWS, 1))
                w = w + jnp.where(rank_scr[h] < cnt_b, b_scr[h], jnp.zeros((), BF16)) * a_b
            ws.append(w)
        return jnp.concatenate(ws, axis=0)

    @pl.when(j == 0)
    def _():
        xt = _rms(h_ref[...], g_ref[...]).T
        xt_hi = xt.astype(BF16)
        xt_scr[...] = xt_hi
        xlo_scr[...] = (xt - xt_hi.astype(F32)).astype(BF16)
        acc_scr[...] = jnp.zeros(acc_scr.shape, F32)

        def dot2(x_hi, x_lo, y_hi, y_lo):
            d = lambda p, q: jnp.dot(p, q, preferred_element_type=F32)
            return d(x_hi, y_hi) + d(x_hi, y_lo) + d(x_lo, y_hi)

        def head_tail(x):
            hi = x.astype(BF16)
            return hi, (x - hi.astype(F32)).astype(BF16)

        def head_body(h, carry):
            base = pl.multiple_of(h * PEER_KEY_DIM, PEER_KEY_DIM)
            k1h, k1l = head_tail(k1_ref[h])
            k2h, k2l = head_tail(k2_ref[h])
            rows = pl.ds(base, PEER_KEY_DIM)
            qh = dot2(wq_ref[rows, :], wql_ref[rows, :], xt_scr[...], xlo_scr[...])
            for cch in range(nchunk):
                sl = slice(cch * LANES, (cch + 1) * LANES)
                q1h, q1l = head_tail(qh[0:PEER_HALF, sl])
                q2h, q2l = head_tail(qh[PEER_HALF:, sl])
                s1 = dot2(k1h, k1l, q1h, q1l)
                s2 = dot2(k2h, k2l, q2h, q2l)
                a, cnt, rank2, b = _peer_select(s1, s2)
                a_scr[h, cch] = a
                cnt_scr[h, cch] = cnt
                rank_scr[h, :, sl] = rank2.astype(BF16)
                b_scr[h, :, sl] = b.astype(BF16)
            return carry

        lax.fori_loop(0, PEER_HEADS, head_body, 0)

    acts = []
    for sb in range(nsub):
        ht = jnp.dot(u_ref[sb * PEER_SUB:(sb + 1) * PEER_SUB, :], xt_scr[...], preferred_element_type=F32)
        acts.append(_gelu_tanh(ht.astype(BF16)) * gate_weights(sb))
    acc_scr[...] += jnp.dot(vt_ref[...], jnp.concatenate(acts, axis=0), preferred_element_type=F32)

    @pl.when(j == ne - 1)
    def _():
        o_ref[...] = h_ref[...] + acc_scr[...].T


def _transpose_cast_kernel(x_ref, o_ref):
    o_ref[...] = x_ref[...].T.astype(o_ref.dtype)


def _transpose_cast(x, dtype):
    R, C = x.shape
    tr = _tile(R, 512)
    return pl.pallas_call(
        _transpose_cast_kernel,
        grid=(R // tr,),
        in_specs=[pl.BlockSpec((tr, C), lambda i: (i, 0))],
        out_specs=pl.BlockSpec((C, tr), lambda i: (0, i)),
        out_shape=jax.ShapeDtypeStruct((C, R), dtype),
        compiler_params=_cparams("parallel"),
        name="transpose_cast",
    )(x)


def _peer(h2d, g, wq_t, wq_t_lo, k1, k2, u_bf16, vt_bf16):
    T = h2d.shape[0]
    const = dict(pipeline_mode=pl.Buffered(1))
    tm = _tile(T, 512)
    assert tm % LANES == 0
    te = PEER_STEP
    ne = PEER_EXPERTS // te
    kern = functools.partial(_peer_kernel, tm=tm, te=te, ne=ne)
    hk = (PEER_HEADS, PEER_NKEYS, tm)
    hkc = (PEER_HEADS, tm // LANES, PEER_NKEYS, LANES)
    return pl.pallas_call(
        kern,
        grid=(T // tm, ne),
        in_specs=[pl.BlockSpec((tm, D_MODEL), lambda i, j: (i, 0)),
                  pl.BlockSpec((1, D_MODEL), lambda i, j: (0, 0)),
                  pl.BlockSpec((PEER_HEADS * PEER_KEY_DIM, D_MODEL), lambda i, j: (0, 0), **const),
                  pl.BlockSpec((PEER_HEADS * PEER_KEY_DIM, D_MODEL), lambda i, j: (0, 0), **const),
                  pl.BlockSpec((PEER_HEADS, PEER_NKEYS, PEER_HALF), lambda i, j: (0, 0, 0), **const),
                  pl.BlockSpec((PEER_HEADS, PEER_NKEYS, PEER_HALF), lambda i, j: (0, 0, 0), **const),
                  pl.BlockSpec((te, D_MODEL), lambda i, j: (j, 0)),
                  pl.BlockSpec((D_MODEL, te), lambda i, j: (0, j))],
        out_specs=pl.BlockSpec((tm, D_MODEL), lambda i, j: (i, 0)),
        out_shape=jax.ShapeDtypeStruct((T, D_MODEL), F32),
        scratch_shapes=[pltpu.VMEM((D_MODEL, tm), BF16), pltpu.VMEM((D_MODEL, tm), BF16),
                        pltpu.VMEM(hkc, F32), pltpu.VMEM(hkc, F32), pltpu.VMEM(hk, BF16), pltpu.VMEM(hk, BF16),
                        pltpu.VMEM((D_MODEL, tm), F32)],
        compiler_params=_cparams("parallel", "arbitrary"),
        name="peer",
    )(h2d, g.reshape(1, D_MODEL), wq_t, wq_t_lo, k1, k2, u_bf16, vt_bf16)


def _ple_kernel(h_ref, p_ref, g_ref, wg_ref, wp_ref, gf_ref, o_ref, *, final_norm):
    h = h_ref[...]
    xn = _rms(h, g_ref[...])
    gate = jax.nn.sigmoid(jnp.dot(xn.astype(BF16), wg_ref[...], preferred_element_type=F32))
    emb = jnp.dot(p_ref[...].astype(BF16), wp_ref[...], preferred_element_type=F32)
    out = h + emb * gate
    if final_norm:
        out = _rms(out, gf_ref[...])
    o_ref[...] = out


def _ple(h2d, p2d, g, wg_bf16, wp_bf16, g_final, final_norm):
    T = h2d.shape[0]
    tm = _tile(T, 512)
    kern = functools.partial(_ple_kernel, final_norm=final_norm)
    return pl.pallas_call(
        kern,
        grid=(T // tm,),
        in_specs=[pl.BlockSpec((tm, D_MODEL), lambda i: (i, 0)),
                  pl.BlockSpec((tm, PLE_DIM), lambda i: (i, 0)),
                  pl.BlockSpec((1, D_MODEL), lambda i: (0, 0)),
                  pl.BlockSpec((D_MODEL, D_MODEL), lambda i: (0, 0)),
                  pl.BlockSpec((PLE_DIM, D_MODEL), lambda i: (0, 0)),
                  pl.BlockSpec((1, D_MODEL), lambda i: (0, 0))],
        out_specs=pl.BlockSpec((tm, D_MODEL), lambda i: (i, 0)),
        out_shape=jax.ShapeDtypeStruct((T, D_MODEL), F32),
        compiler_params=_cparams("parallel"),
        name="ple",
    )(h2d, p2d, g.reshape(1, D_MODEL), wg_bf16, wp_bf16, g_final.reshape(1, D_MODEL))


def _pad_to(x, n, axis):
    pad = n - x.shape[axis]
    if pad == 0:
        return x
    widths = [(0, 0)] * x.ndim
    widths[axis] = (0, pad)
    return jnp.pad(x, widths)


def _prep_layer_weights(weights, i):
    (g_mix, w_in, b_forget, g_ret, conv_w, conv_b, dt_bias, a_log, d_skip, g_ssm, w_out,
     g_ffn, peer_wq, peer_k1, peer_k2, peer_u, peer_v, g_ple, w_ple_gate, w_ple, g_final) = weights
    w_proj, bf_pad, dtb_full = _prep_w_in(w_in[i], b_forget[i], dt_bias[i])
    wq_hi = peer_wq[i].astype(BF16)
    wq_lo = peer_wq[i] - wq_hi.astype(F32)
    return dict(w_proj=w_proj, bf_pad=bf_pad, dtb_full=dtb_full, w_out=w_out[i].astype(BF16),
                wq_t=_transpose_cast(peer_wq[i], BF16), wq_t_lo=_transpose_cast(wq_lo, BF16),
                u=peer_u[i].astype(BF16),
                vt=_transpose_cast(peer_v[i], BF16),
                wg=w_ple_gate[i].astype(BF16), wp=w_ple[i].astype(BF16))


def _run_trunk(x, p, past, weights, prepped):
    (g_mix, w_in, b_forget, g_ret, conv_w, conv_b, dt_bias, a_log, d_skip, g_ssm, w_out,
     g_ffn, peer_wq, peer_k1, peer_k2, peer_u, peer_v, g_ple, w_ple_gate, w_ple, g_final) = weights
    B, L, _ = x.shape
    depth = w_in.shape[0]
    T = B * L
    P = 0 if past is None else past[0].shape[2]
    half = B_HEAD_DIM // 2
    invf = jnp.tile(ROPE_BASE ** (-jnp.arange(half, dtype=F32) / half), LANES // half).reshape(1, LANES)

    h = x.reshape(T, D_MODEL)
    outs = []
    for i in range(depth):
        pw = prepped[i]
        dtb_full = pw["dtb_full"]
        proj, lf, kb, vb, k_new, v_new = _inproj(h, g_mix[i], pw["w_proj"], pw["bf_pad"])
        proj3 = proj.reshape(B, L, PROJ_W)
        lf3 = lf.reshape(B, L, LANES)
        kb = kb.reshape(B, L, A_W)
        vb = vb.reshape(B, L, A_W)

        if past is None:
            eq, ek = _cumsum(lf3)
            ya = _fox(proj3, COL_QA, eq, kb, ek, vb, 0)
            rs0 = jnp.zeros((B, B_W, B_W), F32)
            ss0 = jnp.zeros((B, C_STATE, C_INNER), F32)
            cs0 = jnp.zeros((B, 8, C_CONV_DIM), F32)
        else:
            lk_pad = -(-(P + L) // FOX_TK) * FOX_TK
            lq_pad = -(-L // LANES) * LANES
            past_k = past[0][i].reshape(B, P, A_W).astype(BF16)
            past_v = past[1][i].reshape(B, P, A_W).astype(BF16)
            past_lf = _pad_to(past[2][i], LANES, 2)
            k_all = _pad_to(jnp.concatenate([past_k, kb], axis=1), lk_pad, 1)
            v_all = _pad_to(jnp.concatenate([past_v, vb], axis=1), lk_pad, 1)
            lf_all = _pad_to(jnp.concatenate([past_lf, lf3], axis=1), lk_pad, 1)
            eq, ek = _cumsum(lf_all)
            q_pad = _pad_to(proj3[:, :, COL_QA:COL_QA + A_W], lq_pad, 1)
            ya = _fox(q_pad, 0, _pad_to(eq[:, P:P + L], lq_pad, 1), k_all, ek, v_all, P)[:, :L]
            rs0 = _ret_state_to_kernel(past[3][i])
            ss0 = jnp.transpose(past[4][i], (0, 2, 1, 3)).reshape(B, C_STATE, C_INNER)
            cs0 = jnp.concatenate([jnp.zeros((B, 8 - (C_CONV - 1), C_CONV_DIM), F32), past[5][i]], axis=1)

        yb, ret_k = _retention(proj3, rs0, invf, g_ret[i], P)
        ret_new = _ret_state_from_kernel(ret_k)

        yc, ssm_k = _ssd(proj3, ss0, cs0, conv_w[i], conv_b[i], dtb_full, a_log[i], d_skip[i], g_ssm[i])
        ssm_new = jnp.transpose(ssm_k.reshape(B, C_STATE, C_HEADS, C_HEAD_DIM), (0, 2, 1, 3))
        xbc = proj3[:, :, COL_XBC:COL_XBC + C_CONV_DIM]
        if L >= C_CONV - 1:
            conv_new = xbc[:, L - (C_CONV - 1):]
        else:
            prev = jnp.zeros((B, C_CONV - 1, C_CONV_DIM), F32) if past is None else past[5][i]
            conv_new = jnp.concatenate([prev, xbc], axis=1)[:, -(C_CONV - 1):]

        h = _outproj(h, ya.reshape(T, A_W), yb.reshape(T, B_W), yc.reshape(T, C_INNER), pw["w_out"])
        h = _peer(h, g_ffn[i], pw["wq_t"], pw["wq_t_lo"], peer_k1[i], peer_k2[i], pw["u"], pw["vt"])
        h = _ple(h, p[i].reshape(T, PLE_DIM), g_ple[i], pw["wg"], pw["wp"], g_final, final_norm=(i == depth - 1))
        outs.append((k_new.reshape(B, L, A_HEADS, A_HEAD_DIM), v_new.reshape(B, L, A_HEADS, A_HEAD_DIM),
                     lf3[:, :, :A_HEADS], ret_new, ssm_new, conv_new))
    stacked = tuple(jnp.stack([o[n] for o in outs]) for n in range(6))
    return h.reshape(B, L, D_MODEL), stacked


def kernel(x_prompt, x_sample, cache_k_fox, cache_v_fox, cache_logf_fox, state_ret, state_ssm, state_conv, p_prompt, p_sample, g_mix, w_in, b_forget, g_ret, conv_w, conv_b, dt_bias, a_log, d_skip, g_ssm, w_out, g_ffn, peer_wq, peer_k1, peer_k2, peer_u, peer_v, g_ple, w_ple_gate, w_ple, g_final):
    weights = (g_mix, w_in, b_forget, g_ret, conv_w, conv_b, dt_bias, a_log, d_skip, g_ssm, w_out,
               g_ffn, peer_wq, peer_k1, peer_k2, peer_u, peer_v, g_ple, w_ple_gate, w_ple, g_final)
    prepped = [_prep_layer_weights(weights, i) for i in range(w_in.shape[0])]
    y_prompt, (k_p, v_p, lf_p, ret_p, ssm_p, conv_p) = _run_trunk(x_prompt, p_prompt, None, weights, prepped)
    past = (cache_k_fox, cache_v_fox, cache_logf_fox, state_ret, state_ssm, state_conv)
    y_sample, (k_s, v_s, lf_s, ret_s, ssm_s, conv_s) = _run_trunk(x_sample, p_sample, past, weights, prepped)
    return (y_prompt, y_sample, k_p, v_p, lf_p, ret_p, ssm_p, conv_p, k_s, v_s, lf_s, ret_s, ssm_s, conv_s)
```

```python
import functools
import math

import jax
import jax.numpy as jnp
from jax import lax
from jax.experimental import pallas as pl
from jax.experimental.pallas import tpu as pltpu

F32 = jnp.float32
BF16 = jnp.bfloat16

D_MODEL = 1024
PLE_DIM = 256
EPS = 1e-6
A_HEADS, A_HEAD_DIM = 4, 64
B_HEADS, B_HEAD_DIM = 4, 64
ROPE_BASE = 10000.0
C_HEADS, C_HEAD_DIM = 8, 64
C_INNER = C_HEADS * C_HEAD_DIM
C_GROUPS, C_STATE, C_CONV = 2, 128, 4
C_CONV_DIM = C_INNER + 2 * C_GROUPS * C_STATE
PEER_HEADS, PEER_NKEYS, PEER_KEY_DIM, PEER_TOPK = 8, 128, 256, 16
PEER_HALF = PEER_KEY_DIM // 2
PEER_EXPERTS = PEER_NKEYS * PEER_NKEYS

LANES = 128
BF16_ROWS = 16
A_W = A_HEADS * A_HEAD_DIM
B_W = B_HEADS * B_HEAD_DIM
G_W = C_GROUPS * C_STATE

COL_QA = 0
COL_KA = COL_QA + A_W
COL_VA = COL_KA + A_W
COL_QB = COL_VA + A_W
COL_KB = COL_QB + B_W
COL_VB = COL_KB + B_W
COL_GB = COL_VB + B_W
COL_FA = COL_GB + B_W
COL_ZC = COL_FA + 2 * LANES
COL_DT = COL_ZC + C_INNER
COL_XBC = COL_DT + C_INNER
PROJ_W = COL_XBC + C_CONV_DIM
assert all(c % A_W == 0 for c in (COL_QA, COL_KA, COL_VA, COL_QB, COL_KB, COL_VB, COL_GB))
assert COL_ZC % C_INNER == 0 and COL_DT % C_INNER == 0 and COL_XBC % C_CONV_DIM == 0 and COL_FA % LANES == 0

NEG_BIG = -1e30
VMEM_LIMIT_BYTES = 52 * 1024 * 1024


def _cparams(*sem):
    return pltpu.CompilerParams(dimension_semantics=sem, vmem_limit_bytes=VMEM_LIMIT_BYTES)


def _tile(n, pref):
    t = min(n, pref)
    while n % t:
        t -= 8
    assert t > 0
    return t


def _rms(x, g):
    return x * lax.rsqrt(jnp.mean(x * x, axis=-1, keepdims=True) + EPS) * g


def _split3(x):
    hi = x.astype(BF16)
    r1 = x - hi.astype(F32)
    mid = r1.astype(BF16)
    lo = (r1 - mid.astype(F32)).astype(BF16)
    return hi, mid, lo


def _dot_sel(sel_bf16, x):
    hi, mid, lo = _split3(x)
    d = lambda p: jnp.dot(sel_bf16, p, preferred_element_type=F32)
    return d(hi) + d(mid) + d(lo)


def _dot_sel_rhs(x, sel_bf16):
    hi, mid, lo = _split3(x)
    d = lambda p: jnp.dot(p, sel_bf16, preferred_element_type=F32)
    return d(hi) + d(mid) + d(lo)


def _inproj_kernel(x_ref, g_ref, w_ref, bf_ref, o_ref, lf_ref, kb_ref, vb_ref, kf_ref, vf_ref):
    xn = _rms(x_ref[...], g_ref[...])
    y = jnp.dot(xn.astype(BF16), w_ref[...], preferred_element_type=F32)
    o_ref[...] = y
    kf_ref[...] = y[:, COL_KA:COL_KA + A_W]
    vf_ref[...] = y[:, COL_VA:COL_VA + A_W]
    kb_ref[...] = y[:, COL_KA:COL_KA + A_W].astype(BF16)
    vb_ref[...] = y[:, COL_VA:COL_VA + A_W].astype(BF16)
    fa = y[:, COL_FA:COL_FA + LANES] + bf_ref[...]
    lane = lax.broadcasted_iota(jnp.int32, fa.shape, 1)
    lf_ref[...] = jnp.where(lane < A_HEADS, jax.nn.log_sigmoid(fa), 0.0)


def _inproj(x2d, g, w, bf_pad):
    T = x2d.shape[0]
    tm = _tile(T, 256)
    return pl.pallas_call(
        _inproj_kernel,
        grid=(T // tm,),
        in_specs=[pl.BlockSpec((tm, D_MODEL), lambda i: (i, 0)),
                  pl.BlockSpec((1, D_MODEL), lambda i: (0, 0)),
                  pl.BlockSpec((D_MODEL, PROJ_W), lambda i: (0, 0)),
                  pl.BlockSpec((1, LANES), lambda i: (0, 0))],
        out_specs=[pl.BlockSpec((tm, PROJ_W), lambda i: (i, 0)),
                   pl.BlockSpec((tm, LANES), lambda i: (i, 0)),
                   pl.BlockSpec((tm, A_W), lambda i: (i, 0)),
                   pl.BlockSpec((tm, A_W), lambda i: (i, 0)),
                   pl.BlockSpec((tm, A_W), lambda i: (i, 0)),
                   pl.BlockSpec((tm, A_W), lambda i: (i, 0))],
        out_shape=[jax.ShapeDtypeStruct((T, PROJ_W), F32),
                   jax.ShapeDtypeStruct((T, LANES), F32),
                   jax.ShapeDtypeStruct((T, A_W), BF16),
                   jax.ShapeDtypeStruct((T, A_W), BF16),
                   jax.ShapeDtypeStruct((T, A_W), F32),
                   jax.ShapeDtypeStruct((T, A_W), F32)],
        compiler_params=_cparams("parallel"),
        name="inproj",
    )(x2d, g.reshape(1, D_MODEL), w, bf_pad)


def _prep_w_in(w_in, b_forget, dt_bias):
    sizes = (A_W, A_W, A_W, A_HEADS, B_W, B_W, B_W, B_W, C_INNER, C_INNER, G_W, G_W, C_HEADS)
    pts = []
    acc = 0
    for s in sizes[:-1]:
        acc += s
        pts.append(acc)
    qa, ka, va, fa, qb, kb, vb, gb, zc, xc, bc, cc, dtc = jnp.split(w_in, pts, axis=-1)

    def rot_perm(w):
        w = w.reshape(D_MODEL, B_HEADS, 2, B_HEAD_DIM // 2)
        return jnp.transpose(w, (0, 2, 1, 3)).reshape(D_MODEL, B_W)

    z = lambda n: jnp.zeros((D_MODEL, n), w_in.dtype)
    w = jnp.concatenate([qa, ka, va, rot_perm(qb), rot_perm(kb), vb, gb, fa, z(2 * LANES - A_HEADS), zc,
                         jnp.repeat(dtc, C_HEAD_DIM, axis=1), xc, bc, cc], axis=1)
    assert w.shape[1] == PROJ_W
    bf_pad = jnp.concatenate([b_forget, jnp.zeros((LANES - A_HEADS,), F32)]).reshape(1, LANES)
    dtb_full = jnp.repeat(dt_bias, C_HEAD_DIM).reshape(1, C_INNER)
    return w.astype(BF16), bf_pad, dtb_full


FOX_TQ = 512
FOX_TK = 512
FOX_TQ_SUB = 512
FOX_XTRA = 6


def _fox_xtra_base(h):
    return A_HEAD_DIM * ((h + 1) % A_HEADS)


def _cumsum_kernel(lf_ref, eq_ref, ek_ref, carry_scr):
    @pl.when(pl.program_id(1) == 0)
    def _():
        carry_scr[...] = jnp.zeros_like(carry_scr)

    lf = lf_ref[0]
    tc = lf.shape[0]
    r = lax.broadcasted_iota(jnp.int32, (tc, tc), 0)
    c = lax.broadcasted_iota(jnp.int32, (tc, tc), 1)
    tri = jnp.where(r >= c, 1.0, 0.0).astype(BF16)
    cum = _dot_sel(tri, lf) + carry_scr[0:1, :]
    carry_scr[...] = jnp.broadcast_to(cum[tc - 1:tc, :], carry_scr.shape)

    pieces = _split3(cum)
    src = lax.broadcasted_iota(jnp.int32, (LANES, A_W), 0)
    dst = lax.broadcasted_iota(jnp.int32, (LANES, A_W), 1)
    lane = lax.broadcasted_iota(jnp.int32, (1, A_W), 1)
    one_q = jnp.zeros((1, A_W), F32)
    one_k = jnp.zeros((1, A_W), F32)
    for h in range(A_HEADS):
        base = _fox_xtra_base(h)
        one_q = jnp.where((lane >= base + 3) & (lane < base + FOX_XTRA), 1.0, one_q)
        one_k = jnp.where((lane >= base) & (lane < base + 3), 1.0, one_k)
    eq, ek = one_q, one_k
    for k, piece in enumerate(pieces):
        hit_q = functools.reduce(jnp.logical_or, [(src == h) & (dst == _fox_xtra_base(h) + k) for h in range(A_HEADS)])
        hit_k = functools.reduce(jnp.logical_or,
                                 [(src == h) & (dst == _fox_xtra_base(h) + 3 + k) for h in range(A_HEADS)])
        eq = eq + jnp.dot(piece, jnp.where(hit_q, 1.0, 0.0).astype(BF16), preferred_element_type=F32)
        ek = ek + jnp.dot(piece, jnp.where(hit_k, -1.0, 0.0).astype(BF16), preferred_element_type=F32)
    eq_ref[0] = eq.astype(BF16)
    ek_ref[0] = ek.astype(BF16)


def _cumsum(lf):
    B, L, _ = lf.shape
    tc = _tile(L, 512)
    return pl.pallas_call(
        _cumsum_kernel,
        grid=(B, L // tc),
        in_specs=[pl.BlockSpec((1, tc, LANES), lambda b, j: (b, j, 0))],
        out_specs=[pl.BlockSpec((1, tc, A_W), lambda b, j: (b, j, 0)),
                   pl.BlockSpec((1, tc, A_W), lambda b, j: (b, j, 0))],
        out_shape=[jax.ShapeDtypeStruct((B, L, A_W), BF16),
                   jax.ShapeDtypeStruct((B, L, A_W), BF16)],
        scratch_shapes=[pltpu.VMEM((8, LANES), F32)],
        compiler_params=_cparams("parallel", "arbitrary"),
        name="fox_cumsum",
    )(lf)


def _fox_kernel(q_ref, eq_ref, kb_ref, ek_ref, vb_ref, o_ref, qp_scr, m_scr, l_scr, acc_scr, *,
                tq, tk, offset, nk):
    i = pl.program_id(1)
    scale = A_HEAD_DIM ** -0.5
    q_lo = offset + i * tq

    def in_head(lane, h):
        return (lane >= h * A_HEAD_DIM) & (lane < (h + 1) * A_HEAD_DIM)

    def in_xtra(lane, h):
        return (lane >= _fox_xtra_base(h)) & (lane < _fox_xtra_base(h) + FOX_XTRA)

    lane_q = lax.broadcasted_iota(jnp.int32, (tq, A_W), 1)
    q = (q_ref[0] * scale).astype(BF16)
    eq = eq_ref[0]
    for h in range(A_HEADS):
        qp_scr[h] = jnp.where(in_xtra(lane_q, h), eq, jnp.where(in_head(lane_q, h), q, jnp.zeros((), BF16)))
    m_scr[...] = jnp.full(m_scr.shape, NEG_BIG, F32)
    l_scr[...] = jnp.zeros(l_scr.shape, F32)
    acc_scr[...] = jnp.zeros(acc_scr.shape, F32)

    lane_k = lax.broadcasted_iota(jnp.int32, (tk, A_W), 1)

    def block(jj, masked):
        ks = pl.multiple_of(jj * tk, tk)
        k = kb_ref[0, pl.ds(ks, tk), :]
        ek = ek_ref[0, pl.ds(ks, tk), :]
        v = vb_ref[0, pl.ds(ks, tk), :]
        if masked:
            kpos = ks + lax.broadcasted_iota(jnp.int32, (tk, tq), 0)
            qpos = q_lo + lax.broadcasted_iota(jnp.int32, (tk, tq), 1)
            vis = kpos <= qpos
        tqs = min(tq, FOX_TQ_SUB)
        ps, alphas = [], []
        for h in range(A_HEADS):
            kp = jnp.where(in_xtra(lane_k, h), ek, k)
            for qs in range(tq // tqs):
                cols = slice(qs * tqs, (qs + 1) * tqs)
                s = lax.dot_general(kp, qp_scr[h, cols, :], (((1,), (1,)), ((), ())),
                                    preferred_element_type=F32)
                if masked:
                    s = jnp.where(vis[:, cols], s, NEG_BIG)
                m_prev = m_scr[h, 0:1, cols]
                m_next = jnp.maximum(m_prev, jnp.max(s, axis=0, keepdims=True))
                alpha = jnp.exp(m_prev - m_next)
                p = jnp.exp(s - m_next)
                l_scr[h, :, cols] = jnp.broadcast_to(
                    alpha * l_scr[h, 0:1, cols] + jnp.sum(p, axis=0, keepdims=True), (8, tqs))
                m_scr[h, :, cols] = jnp.broadcast_to(m_next, (8, tqs))
                ps.append(p.astype(BF16))
                alphas.append(alpha)
        alphas = [jnp.concatenate(alphas[h * (tq // tqs):(h + 1) * (tq // tqs)], axis=1) for h in range(A_HEADS)]
        pv = lax.dot_general(v, jnp.concatenate(ps, axis=1), (((0,), (0,)), ((), ())),
                             preferred_element_type=F32)
        for h in range(A_HEADS):
            acc_scr[h] = acc_scr[h] * alphas[h] + pv[:, h * tq:(h + 1) * tq]

    n_full = (q_lo + 1) // tk
    n_vis = jnp.minimum((q_lo + tq - 1) // tk + 1, nk)

    def full_body(jj, carry):
        block(jj, False)
        return carry

    def edge_body(jj, carry):
        block(jj, True)
        return carry

    lax.fori_loop(0, n_full, full_body, 0)
    lax.fori_loop(n_full, n_vis, edge_body, 0)

    row = lax.broadcasted_iota(jnp.int32, (A_W, tq), 0)
    out_t = jnp.zeros((A_W, tq), F32)
    for h in range(A_HEADS):
        out_t = out_t + jnp.where(in_head(row, h), acc_scr[h] / l_scr[h, 0:1, :], 0.0)
    o_ref[0] = out_t.T.astype(o_ref.dtype)


def _fox(q_src, q_col, eq, kb, ek, vb, offset):
    B, Lq, _ = q_src.shape
    Lk = kb.shape[1]
    tq = _tile(Lq, FOX_TQ)
    tk = _tile(Lk, FOX_TK)
    assert tq % LANES == 0 and tk % LANES == 0
    nq, nk = Lq // tq, Lk // tk
    whole = lambda b, i: (b, 0, 0)
    kern = functools.partial(_fox_kernel, tq=tq, tk=tk, offset=offset, nk=nk)
    return pl.pallas_call(
        kern,
        grid=(B, nq),
        in_specs=[pl.BlockSpec((1, tq, A_W), lambda b, i: (b, i, q_col // A_W)),
                  pl.BlockSpec((1, tq, A_W), lambda b, i: (b, i, 0)),
                  pl.BlockSpec((1, Lk, A_W), whole),
                  pl.BlockSpec((1, Lk, A_W), whole),
                  pl.BlockSpec((1, Lk, A_W), whole)],
        out_specs=pl.BlockSpec((1, tq, A_W), lambda b, i: (b, i, 0)),
        out_shape=jax.ShapeDtypeStruct((B, Lq, A_W), BF16),
        scratch_shapes=[pltpu.VMEM((A_HEADS, tq, A_W), BF16),
                        pltpu.VMEM((A_HEADS, 8, tq), F32),
                        pltpu.VMEM((A_HEADS, 8, tq), F32),
                        pltpu.VMEM((A_HEADS, A_W, tq), F32)],
        compiler_params=_cparams("parallel", "arbitrary"),
        name="fox_attention",
    )(q_src, eq, kb, ek, vb)


def _ret_lane_head_v():
    return lax.broadcasted_iota(jnp.int32, (1, B_W), 1) // B_HEAD_DIM


def _ret_lane_head_qk():
    return (lax.broadcasted_iota(jnp.int32, (1, B_W), 1) % LANES) // (B_HEAD_DIM // 2)


def _log_gamma(h):
    return math.log1p(-2.0 ** (-5.0 - h))


def _ret_kernel(q_ref, k_ref, v_ref, g_ref, s0_ref, invf_ref, gret_ref, y_ref, sout_ref,
                state_scr, dec_scr, *, lc, offset, nchunks):
    c = pl.program_id(1)

    @pl.when(c == 0)
    def _():
        state_scr[...] = s0_ref[0]
        r = lax.broadcasted_iota(jnp.int32, (lc, lc), 0)
        s = lax.broadcasted_iota(jnp.int32, (lc, lc), 1)
        d = (r - s).astype(F32)
        for h in range(B_HEADS):
            dec_scr[h] = jnp.where(r >= s, jnp.exp(d * _log_gamma(h)), 0.0)

    hv = _ret_lane_head_v()
    hqk = _ret_lane_head_qk()
    lg_v = jnp.zeros((1, B_W), F32)
    lg_qk = jnp.zeros((1, B_W), F32)
    for h in range(B_HEADS):
        lg_v = jnp.where(hv == h, _log_gamma(h), lg_v)
        lg_qk = jnp.where(hqk == h, _log_gamma(h), lg_qk)

    pos = (offset + c * lc + lax.broadcasted_iota(jnp.int32, (lc, LANES), 0)).astype(F32)
    ang = pos * invf_ref[...]
    cos, sin = jnp.cos(ang), jnp.sin(ang)

    def rot(x):
        x1, x2 = x[:, :LANES], x[:, LANES:]
        return jnp.concatenate([x1 * cos - x2 * sin, x1 * sin + x2 * cos], axis=1)

    qr = rot(q_ref[0])
    kr = rot(k_ref[0]) * (B_HEAD_DIM ** -0.5)
    v = v_ref[0]
    vb = v.astype(BF16)
    krb = kr.astype(BF16)

    lpos = lax.broadcasted_iota(jnp.int32, (lc, B_W), 0).astype(F32)
    state = state_scr[...]
    y = jnp.dot(qr.astype(BF16), state.astype(BF16), preferred_element_type=F32) * jnp.exp((lpos + 1.0) * lg_v)
    for h in range(B_HEADS):
        qh = jnp.where(hqk == h, qr, 0.0).astype(BF16)
        sc = lax.dot_general(qh, krb, (((1,), (1,)), ((), ())), preferred_element_type=F32) * dec_scr[h]
        yh = jnp.dot(sc.astype(BF16), vb, preferred_element_type=F32)
        y = y + jnp.where(hv == h, yh, 0.0)

    kw = (kr * jnp.exp((lc - 1.0 - lpos) * lg_qk)).astype(BF16)
    upd = lax.dot_general(kw, vb, (((0,), (0,)), ((), ())), preferred_element_type=F32)
    row_head = (lax.broadcasted_iota(jnp.int32, (B_W, B_W), 0) % LANES) // (B_HEAD_DIM // 2)
    col_head = lax.broadcasted_iota(jnp.int32, (B_W, B_W), 1) // B_HEAD_DIM
    lg_rows = jnp.zeros((B_W, B_W), F32)
    for h in range(B_HEADS):
        lg_rows = jnp.where(row_head == h, _log_gamma(h), lg_rows)
    new_state = jnp.where(row_head == col_head, state * jnp.exp(lc * lg_rows) + upd, 0.0)
    state_scr[...] = new_state

    @pl.when(c == nchunks - 1)
    def _():
        sout_ref[0] = new_state

    ri = lax.broadcasted_iota(jnp.int32, (B_W, B_W), 0) // B_HEAD_DIM
    ci = lax.broadcasted_iota(jnp.int32, (B_W, B_W), 1) // B_HEAD_DIM
    avg = jnp.where(ri == ci, 1.0 / B_HEAD_DIM, 0.0).astype(BF16)
    yc = y - _dot_sel_rhs(y, avg)
    var = _dot_sel_rhs(yc * yc, avg)
    yn = yc * lax.rsqrt(var + EPS) * gret_ref[...]
    y_ref[0] = (jax.nn.silu(g_ref[0]) * yn).astype(y_ref.dtype)


def _retention(proj3, s0, invf, g_ret, offset):
    B, L, _ = proj3.shape
    lc = _tile(L, 256)
    nch = L // lc
    blk = lambda col: pl.BlockSpec((1, lc, B_W), lambda b, c: (b, c, col // B_W))
    kern = functools.partial(_ret_kernel, lc=lc, offset=offset, nchunks=nch)
    return pl.pallas_call(
        kern,
        grid=(B, nch),
        in_specs=[blk(COL_QB), blk(COL_KB), blk(COL_VB), blk(COL_GB),
                  pl.BlockSpec((1, B_W, B_W), lambda b, c: (b, 0, 0)),
                  pl.BlockSpec((1, LANES), lambda b, c: (0, 0)),
                  pl.BlockSpec((1, B_W), lambda b, c: (0, 0))],
        out_specs=[pl.BlockSpec((1, lc, B_W), lambda b, c: (b, c, 0)),
                   pl.BlockSpec((1, B_W, B_W), lambda b, c: (b, 0, 0))],
        out_shape=[jax.ShapeDtypeStruct((B, L, B_W), BF16),
                   jax.ShapeDtypeStruct((B, B_W, B_W), F32)],
        scratch_shapes=[pltpu.VMEM((B_W, B_W), F32),
                        pltpu.VMEM((B_HEADS, lc, lc), F32)],
        compiler_params=_cparams("parallel", "arbitrary"),
        name="retention",
    )(proj3, proj3, proj3, proj3, s0, invf, g_ret.reshape(1, B_W))


def _ret_state_to_kernel(s):
    B = s.shape[0]
    half = B_HEAD_DIM // 2
    s = s.reshape(B, B_HEADS, 2, half, B_HEAD_DIM)
    eye = jnp.eye(B_HEADS, dtype=s.dtype)
    full = jnp.einsum('bhkie,hg->bkhige', s, eye)
    return full.reshape(B, B_W, B_W)


def _ret_state_from_kernel(s):
    B = s.shape[0]
    half = B_HEAD_DIM // 2
    s = s.reshape(B, 2, B_HEADS, half, B_HEADS, B_HEAD_DIM)
    d = jnp.einsum('bkhihe->bhkie', s)
    return d.reshape(B, B_HEADS, B_HEAD_DIM, B_HEAD_DIM)


def _ssd_kernel(z_ref, xbc_ref, dt_ref, s0_ref, cs0_ref, cw_ref, cb_ref, dtb_ref, alog_ref, dsk_ref, gs_ref,
                y_ref, sout_ref, state_scr, xp_scr, *, lc, nchunks):
    c = pl.program_id(1)
    PADR = 8

    @pl.when(c == 0)
    def _():
        state_scr[...] = s0_ref[0]
        xp_scr[0:PADR, :] = cs0_ref[0]

    xbc = xbc_ref[0]
    xp_scr[PADR:PADR + lc, :] = xbc
    cw = cw_ref[...]
    conv = xp_scr[PADR - 3:PADR - 3 + lc, :] * cw[0:1, :]
    for jj in range(1, C_CONV):
        conv = conv + xp_scr[PADR - 3 + jj:PADR - 3 + jj + lc, :] * cw[jj:jj + 1, :]
    act = jax.nn.silu(conv + cb_ref[...])
    xp_scr[0:PADR, :] = xp_scr[lc:lc + PADR, :]

    xs = act[:, :C_INNER]
    bm = act[:, C_INNER:C_INNER + G_W]
    cm = act[:, C_INNER + G_W:]
    dt = jax.nn.softplus(dt_ref[0] + dtb_ref[...])
    loga = dt * (-jnp.exp(alog_ref[...]))

    r = lax.broadcasted_iota(jnp.int32, (lc, lc), 0)
    s = lax.broadcasted_iota(jnp.int32, (lc, lc), 1)
    causal = r >= s
    tri = jnp.where(causal, 1.0, 0.0).astype(BF16)
    cum = _dot_sel(tri, loga)
    total = cum[lc - 1:lc, :]
    xdt = xs * dt
    xdtb = xdt.astype(BF16)
    bmb = bm.astype(BF16)
    cmb = cm.astype(BF16)
    state = state_scr[...]

    gw = C_INNER // C_GROUPS
    cum_t = [cum[:, kk * LANES:(kk + 1) * LANES].T for kk in range(C_INNER // LANES)]
    ys = []
    for g in range(C_GROUPS):
        cg = cmb[:, g * C_STATE:(g + 1) * C_STATE]
        bg = bmb[:, g * C_STATE:(g + 1) * C_STATE]
        scores = lax.dot_general(cg, bg, (((1,), (1,)), ((), ())), preferred_element_type=F32)
        y_g = jnp.dot(cg, state[:, g * gw:(g + 1) * gw].astype(BF16), preferred_element_type=F32)
        y_g = y_g * jnp.exp(cum[:, g * gw:(g + 1) * gw])
        lane_head = lax.broadcasted_iota(jnp.int32, (1, gw), 1) // C_HEAD_DIM
        xg = xdtb[:, g * gw:(g + 1) * gw]
        for hh in range(C_HEADS // C_GROUPS):
            col = g * gw + hh * C_HEAD_DIM
            cum_col = cum[:, col:col + 1]
            cum_row = cum_t[col // LANES][col % LANES:col % LANES + 1, :]
            dec = jnp.exp(jnp.where(causal, cum_col - cum_row, NEG_BIG))
            yh = jnp.dot((scores * dec).astype(BF16), xg, preferred_element_type=F32)
            y_g = y_g + jnp.where(lane_head == hh, yh, 0.0)
        ys.append(y_g)
        xw = (xdt[:, g * gw:(g + 1) * gw] * jnp.exp(total[:, g * gw:(g + 1) * gw] - cum[:, g * gw:(g + 1) * gw]))
        upd = lax.dot_general(bg, xw.astype(BF16), (((0,), (0,)), ((), ())), preferred_element_type=F32)
        state_scr[:, g * gw:(g + 1) * gw] = state[:, g * gw:(g + 1) * gw] * jnp.exp(total[:, g * gw:(g + 1) * gw]) + upd
    y = jnp.concatenate(ys, axis=1)

    @pl.when(c == nchunks - 1)
    def _():
        sout_ref[0] = state_scr[...]

    yc = (y + xs * dsk_ref[...]) * jax.nn.silu(z_ref[0])
    y_ref[0] = _rms(yc, gs_ref[...]).astype(y_ref.dtype)


def _ssd(proj3, s0, cs0, conv_w, conv_b, dtb_full, a_log, d_skip, g_ssm):
    B, L, _ = proj3.shape
    lc = _tile(L, 256)
    nch = L // lc
    rep = lambda p: jnp.repeat(p, C_HEAD_DIM).reshape(1, C_INNER)
    kern = functools.partial(_ssd_kernel, lc=lc, nchunks=nch)
    cst = lambda shape: pl.BlockSpec(shape, lambda b, c: (0,) * len(shape))
    return pl.pallas_call(
        kern,
        grid=(B, nch),
        in_specs=[pl.BlockSpec((1, lc, C_INNER), lambda b, c: (b, c, COL_ZC // C_INNER)),
                  pl.BlockSpec((1, lc, C_CONV_DIM), lambda b, c: (b, c, COL_XBC // C_CONV_DIM)),
                  pl.BlockSpec((1, lc, C_INNER), lambda b, c: (b, c, COL_DT // C_INNER)),
                  pl.BlockSpec((1, C_STATE, C_INNER), lambda b, c: (b, 0, 0)),
                  pl.BlockSpec((1, 8, C_CONV_DIM), lambda b, c: (b, 0, 0)),
                  cst((C_CONV, C_CONV_DIM)), cst((1, C_CONV_DIM)), cst((1, C_INNER)), cst((1, C_INNER)),
                  cst((1, C_INNER)), cst((1, C_INNER))],
        out_specs=[pl.BlockSpec((1, lc, C_INNER), lambda b, c: (b, c, 0)),
                   pl.BlockSpec((1, C_STATE, C_INNER), lambda b, c: (b, 0, 0))],
        out_shape=[jax.ShapeDtypeStruct((B, L, C_INNER), BF16),
                   jax.ShapeDtypeStruct((B, C_STATE, C_INNER), F32)],
        scratch_shapes=[pltpu.VMEM((C_STATE, C_INNER), F32),
                        pltpu.VMEM((lc + 8, C_CONV_DIM), F32)],
        compiler_params=_cparams("parallel", "arbitrary"),
        name="ssd",
    )(proj3, proj3, proj3, s0, cs0, conv_w, conv_b.reshape(1, C_CONV_DIM), dtb_full, rep(a_log), rep(d_skip),
      g_ssm.reshape(1, C_INNER))


def _outproj_kernel(h_ref, ya_ref, yb_ref, yc_ref, w_ref, o_ref):
    acc = jnp.dot(ya_ref[...], w_ref[0:A_W, :], preferred_element_type=F32)
    acc = acc + jnp.dot(yb_ref[...], w_ref[A_W:A_W + B_W, :], preferred_element_type=F32)
    acc = acc + jnp.dot(yc_ref[...], w_ref[A_W + B_W:, :], preferred_element_type=F32)
    o_ref[...] = h_ref[...] + acc


def _outproj(h2d, ya, yb, yc, w_out_bf16):
    T = h2d.shape[0]
    tm = _tile(T, 512)
    row = lambda w: pl.BlockSpec((tm, w), lambda i: (i, 0))
    return pl.pallas_call(
        _outproj_kernel,
        grid=(T // tm,),
        in_specs=[row(D_MODEL), row(A_W), row(B_W), row(C_INNER),
                  pl.BlockSpec((D_MODEL, D_MODEL), lambda i: (0, 0))],
        out_specs=row(D_MODEL),
        out_shape=jax.ShapeDtypeStruct((T, D_MODEL), F32),
        compiler_params=_cparams("parallel"),
        name="outproj",
    )(h2d, ya, yb, yc, w_out_bf16)


PEER_SUB = 128
PEER_STEP = 2048
PEER_CAND_COLS = 4


def _gelu_tanh(x):
    k0 = math.sqrt(2.0 / math.pi)
    inner = x * (k0 + (k0 * 0.044715) * (x * x))
    return (0.5 * x) * (1.0 + jnp.tanh(inner))


def _top16(x, want_rank):
    vals = []
    rank = jnp.full(x.shape, float(PEER_TOPK), F32) if want_rank else None
    for r in range(PEER_TOPK):
        m = jnp.max(x, axis=0, keepdims=True)
        vals.append(m)
        hit = x == m
        if want_rank:
            rank = jnp.where(hit, float(r), rank)
        x = jnp.where(hit, NEG_BIG, x)
    return jnp.concatenate(vals, axis=0), rank


def _sort_pairs(n):
    def merge(lo, hi, r):
        step = r * 2
        if step < hi - lo:
            yield from merge(lo, hi, step)
            yield from merge(lo + r, hi, step)
            yield from [(i, i + r) for i in range(lo + r, hi - r, step)]
        else:
            yield (lo, lo + r)

    def sort(lo, hi):
        if hi - lo >= 1:
            mid = lo + (hi - lo) // 2
            yield from sort(lo, mid)
            yield from sort(mid + 1, hi)
            yield from merge(lo, hi, 1)

    return tuple(sort(0, n - 1))


def _top16_values(x):
    v = [x[8 * g:8 * g + 8, :] for g in range(PEER_NKEYS // 8)]
    for a, b in _sort_pairs(PEER_NKEYS // 8):
        v[a], v[b] = jnp.maximum(v[a], v[b]), jnp.minimum(v[a], v[b])
    vals = []
    for r in range(PEER_TOPK):
        m = jnp.max(v[0], axis=0, keepdims=True)
        vals.append(m)
        hit = v[0] == m
        for k in range(PEER_TOPK - 1 - r):
            v[k] = jnp.where(hit, v[k + 1], v[k])
    return jnp.concatenate(vals, axis=0)


def _peer_select(s1, s2):
    n = s1.shape[1]
    a16 = _top16_values(s1)
    b16, rank2 = _top16(s2, True)
    ridx = lax.broadcasted_iota(jnp.int32, (PEER_TOPK, n), 0)
    sub = lax.broadcasted_iota(jnp.int32, (8, n), 0)
    lo, hi = slice(0, 8), slice(8, PEER_TOPK)
    col = lambda r2, rows: a16[rows, :] + b16[r2:r2 + 1, :]
    row = lambda r1, cols: b16[cols, :] + a16[r1:r1 + 1, :]
    assert PEER_CAND_COLS == 4 and PEER_TOPK == 16
    groups = [col(0, lo), col(0, hi), col(1, lo),
              jnp.where(sub <= 4, col(2, lo), NEG_BIG),
              jnp.where(sub <= 3, col(3, lo), row(1, lo)),
              jnp.where(sub >= 4, row(0, lo), NEG_BIG),
              row(0, hi),
              jnp.where(sub == 4, row(2, lo), NEG_BIG)]
    cand = jnp.concatenate(groups, axis=0)
    v = list(groups)
    for a_i, b_i in _sort_pairs(8):
        v[a_i], v[b_i] = jnp.maximum(v[a_i], v[b_i]), jnp.minimum(v[a_i], v[b_i])
    tau = None
    for r in range(PEER_TOPK):
        tau = jnp.max(v[0], axis=0, keepdims=True)
        hit = v[0] == tau
        depth = min(8, PEER_TOPK - r)
        for k in range(depth - 1):
            v[k] = jnp.where(hit, v[k + 1], v[k])
        if depth == 8:
            v[7] = jnp.where(hit, NEG_BIG, v[7])
    top = a16[0:1, :] + b16[0:1, :]
    z = jnp.sum(jnp.where(cand >= tau, jnp.exp(cand - top), 0.0), axis=0, keepdims=True)
    cnt = jnp.zeros(s1.shape, F32)
    for r2 in range(PEER_CAND_COLS):
        cnt = cnt + jnp.where(s1 + b16[r2:r2 + 1, :] >= tau, 1.0, 0.0)
    for r1 in range(PEER_TOPK // (PEER_CAND_COLS + 1)):
        a_r = a16[r1:r1 + 1, :]
        tail = jnp.where((ridx >= PEER_CAND_COLS) & (b16 + a_r >= tau), 1.0, 0.0)
        cnt = cnt + jnp.where(s1 == a_r, jnp.sum(tail, axis=0, keepdims=True), 0.0)
    a = jnp.exp(s1 - a16[0:1, :]) / z
    b = jnp.exp(s2 - b16[0:1, :])
    return a, cnt, rank2, b


def _peer_kernel(h_ref, g_ref, wq_ref, k1_ref, k2_ref, u_ref, vt_ref, o_ref,
                 xt_scr, a_scr, cnt_scr, rank_scr, b_scr, acc_scr, *, tm, te, ne):
    j = pl.program_id(1)
    n1 = te // PEER_NKEYS
    nchunk = tm // LANES
    nsub = te // PEER_SUB
    slabs = PEER_SUB // PEER_NKEYS

    def gate_weights(sb):
        ws = []
        for il in range(slabs):
            i1 = j * n1 + sb * slabs + il
            w = jnp.zeros((PEER_NKEYS, tm), BF16)
            for h in range(PEER_HEADS):
                row = lambda ref: jnp.concatenate(
                    [jnp.broadcast_to(ref[h, cch, pl.ds(i1, 1), :], (BF16_ROWS, LANES)) for cch in range(nchunk)],
                    axis=1).astype(BF16)
                cnt_b = jnp.tile(row(cnt_scr), (PEER_NKEYS // BF16_ROWS, 1))
                a_b = jnp.tile(row(a_scr), (PEER_NKEYS // BF16_ROWS, 1))
                w = w + jnp.where(rank_scr[h] < cnt_b, b_scr[h], jnp.zeros((), BF16)) * a_b
            ws.append(w)
        return jnp.concatenate(ws, axis=0)

    @pl.when(j == 0)
    def _():
        xn = _rms(h_ref[...], g_ref[...])
        xt_scr[...] = xn.T.astype(BF16)
        acc_scr[...] = jnp.zeros(acc_scr.shape, F32)

        def head_body(h, carry):
            base = pl.multiple_of(h * PEER_KEY_DIM, PEER_KEY_DIM)
            k1 = k1_ref[h].astype(BF16)
            k2 = k2_ref[h].astype(BF16)
            qh = jnp.dot(wq_ref[pl.ds(base, PEER_KEY_DIM), :], xt_scr[...], preferred_element_type=F32)
            for cch in range(nchunk):
                sl = slice(cch * LANES, (cch + 1) * LANES)
                q1 = qh[0:PEER_HALF, sl].astype(BF16)
                q2 = qh[PEER_HALF:, sl].astype(BF16)
                s1 = jnp.dot(k1, q1, preferred_element_type=F32)
                s2 = jnp.dot(k2, q2, preferred_element_type=F32)
                a, cnt, rank2, b = _peer_select(s1, s2)
                a_scr[h, cch] = a
                cnt_scr[h, cch] = cnt
                rank_scr[h, :, sl] = rank2.astype(BF16)
                b_scr[h, :, sl] = b.astype(BF16)
            return carry

        lax.fori_loop(0, PEER_HEADS, head_body, 0)

    acts = []
    for sb in range(nsub):
        ht = jnp.dot(u_ref[sb * PEER_SUB:(sb + 1) * PEER_SUB, :], xt_scr[...], preferred_element_type=F32)
        acts.append(_gelu_tanh(ht.astype(BF16)) * gate_weights(sb))
    acc_scr[...] += jnp.dot(vt_ref[...], jnp.concatenate(acts, axis=0), preferred_element_type=F32)

    @pl.when(j == ne - 1)
    def _():
        o_ref[...] = h_ref[...] + acc_scr[...].T


def _transpose_cast_kernel(x_ref, o_ref):
    o_ref[...] = x_ref[...].T.astype(o_ref.dtype)


def _transpose_cast(x, dtype):
    R, C = x.shape
    tr = _tile(R, 512)
    return pl.pallas_call(
        _transpose_cast_kernel,
        grid=(R // tr,),
        in_specs=[pl.BlockSpec((tr, C), lambda i: (i, 0))],
        out_specs=pl.BlockSpec((C, tr), lambda i: (0, i)),
        out_shape=jax.ShapeDtypeStruct((C, R), dtype),
        compiler_params=_cparams("parallel"),
        name="transpose_cast",
    )(x)


def _peer(h2d, g, wq_t, k1, k2, u_bf16, vt_bf16):
    T = h2d.shape[0]
    tm = _tile(T, 512)
    assert tm % LANES == 0
    te = PEER_STEP
    ne = PEER_EXPERTS // te
    kern = functools.partial(_peer_kernel, tm=tm, te=te, ne=ne)
    hk = (PEER_HEADS, PEER_NKEYS, tm)
    hkc = (PEER_HEADS, tm // LANES, PEER_NKEYS, LANES)
    return pl.pallas_call(
        kern,
        grid=(T // tm, ne),
        in_specs=[pl.BlockSpec((tm, D_MODEL), lambda i, j: (i, 0)),
                  pl.BlockSpec((1, D_MODEL), lambda i, j: (0, 0)),
                  pl.BlockSpec((PEER_HEADS * PEER_KEY_DIM, D_MODEL), lambda i, j: (0, 0)),
                  pl.BlockSpec((PEER_HEADS, PEER_NKEYS, PEER_HALF), lambda i, j: (0, 0, 0)),
                  pl.BlockSpec((PEER_HEADS, PEER_NKEYS, PEER_HALF), lambda i, j: (0, 0, 0)),
                  pl.BlockSpec((te, D_MODEL), lambda i, j: (j, 0)),
                  pl.BlockSpec((D_MODEL, te), lambda i, j: (0, j))],
        out_specs=pl.BlockSpec((tm, D_MODEL), lambda i, j: (i, 0)),
        out_shape=jax.ShapeDtypeStruct((T, D_MODEL), F32),
        scratch_shapes=[pltpu.VMEM((D_MODEL, tm), BF16),
                        pltpu.VMEM(hkc, F32), pltpu.VMEM(hkc, F32), pltpu.VMEM(hk, BF16), pltpu.VMEM(hk, BF16),
                        pltpu.VMEM((D_MODEL, tm), F32)],
        compiler_params=_cparams("parallel", "arbitrary"),
        name="peer",
    )(h2d, g.reshape(1, D_MODEL), wq_t, k1, k2, u_bf16, vt_bf16)


def _ple_kernel(h_ref, p_ref, g_ref, wg_ref, wp_ref, gf_ref, o_ref, *, final_norm):
    h = h_ref[...]
    xn = _rms(h, g_ref[...])
    gate = jax.nn.sigmoid(jnp.dot(xn.astype(BF16), wg_ref[...], preferred_element_type=F32))
    emb = jnp.dot(p_ref[...].astype(BF16), wp_ref[...], preferred_element_type=F32)
    out = h + emb * gate
    if final_norm:
        out = _rms(out, gf_ref[...])
    o_ref[...] = out


def _ple(h2d, p2d, g, wg_bf16, wp_bf16, g_final, final_norm):
    T = h2d.shape[0]
    tm = _tile(T, 512)
    kern = functools.partial(_ple_kernel, final_norm=final_norm)
    return pl.pallas_call(
        kern,
        grid=(T // tm,),
        in_specs=[pl.BlockSpec((tm, D_MODEL), lambda i: (i, 0)),
                  pl.BlockSpec((tm, PLE_DIM), lambda i: (i, 0)),
                  pl.BlockSpec((1, D_MODEL), lambda i: (0, 0)),
                  pl.BlockSpec((D_MODEL, D_MODEL), lambda i: (0, 0)),
                  pl.BlockSpec((PLE_DIM, D_MODEL), lambda i: (0, 0)),
                  pl.BlockSpec((1, D_MODEL), lambda i: (0, 0))],
        out_specs=pl.BlockSpec((tm, D_MODEL), lambda i: (i, 0)),
        out_shape=jax.ShapeDtypeStruct((T, D_MODEL), F32),
        compiler_params=_cparams("parallel"),
        name="ple",
    )(h2d, p2d, g.reshape(1, D_MODEL), wg_bf16, wp_bf16, g_final.reshape(1, D_MODEL))


def _pad_to(x, n, axis):
    pad = n - x.shape[axis]
    if pad == 0:
        return x
    widths = [(0, 0)] * x.ndim
    widths[axis] = (0, pad)
    return jnp.pad(x, widths)


def _prep_layer_weights(weights, i):
    (g_mix, w_in, b_forget, g_ret, conv_w, conv_b, dt_bias, a_log, d_skip, g_ssm, w_out,
     g_ffn, peer_wq, peer_k1, peer_k2, peer_u, peer_v, g_ple, w_ple_gate, w_ple, g_final) = weights
    w_proj, bf_pad, dtb_full = _prep_w_in(w_in[i], b_forget[i], dt_bias[i])
    return dict(w_proj=w_proj, bf_pad=bf_pad, dtb_full=dtb_full, w_out=w_out[i].astype(BF16),
                wq_t=_transpose_cast(peer_wq[i], BF16), u=peer_u[i].astype(BF16),
                vt=_transpose_cast(peer_v[i], BF16),
                wg=w_ple_gate[i].astype(BF16), wp=w_ple[i].astype(BF16))


def _run_trunk(x, p, past, weights, prepped):
    (g_mix, w_in, b_forget, g_ret, conv_w, conv_b, dt_bias, a_log, d_skip, g_ssm, w_out,
     g_ffn, peer_wq, peer_k1, peer_k2, peer_u, peer_v, g_ple, w_ple_gate, w_ple, g_final) = weights
    B, L, _ = x.shape
    depth = w_in.shape[0]
    T = B * L
    P = 0 if past is None else past[0].shape[2]
    half = B_HEAD_DIM // 2
    invf = jnp.tile(ROPE_BASE ** (-jnp.arange(half, dtype=F32) / half), LANES // half).reshape(1, LANES)

    h = x.reshape(T, D_MODEL)
    outs = []
    for i in range(depth):
        pw = prepped[i]
        dtb_full = pw["dtb_full"]
        proj, lf, kb, vb, k_new, v_new = _inproj(h, g_mix[i], pw["w_proj"], pw["bf_pad"])
        proj3 = proj.reshape(B, L, PROJ_W)
        lf3 = lf.reshape(B, L, LANES)
        kb = kb.reshape(B, L, A_W)
        vb = vb.reshape(B, L, A_W)

        if past is None:
            eq, ek = _cumsum(lf3)
            ya = _fox(proj3, COL_QA, eq, kb, ek, vb, 0)
            rs0 = jnp.zeros((B, B_W, B_W), F32)
            ss0 = jnp.zeros((B, C_STATE, C_INNER), F32)
            cs0 = jnp.zeros((B, 8, C_CONV_DIM), F32)
        else:
            lk_pad = -(-(P + L) // FOX_TK) * FOX_TK
            lq_pad = -(-L // LANES) * LANES
            past_k = past[0][i].reshape(B, P, A_W).astype(BF16)
            past_v = past[1][i].reshape(B, P, A_W).astype(BF16)
            past_lf = _pad_to(past[2][i], LANES, 2)
            k_all = _pad_to(jnp.concatenate([past_k, kb], axis=1), lk_pad, 1)
            v_all = _pad_to(jnp.concatenate([past_v, vb], axis=1), lk_pad, 1)
            lf_all = _pad_to(jnp.concatenate([past_lf, lf3], axis=1), lk_pad, 1)
            eq, ek = _cumsum(lf_all)
            q_pad = _pad_to(proj3[:, :, COL_QA:COL_QA + A_W], lq_pad, 1)
            ya = _fox(q_pad, 0, _pad_to(eq[:, P:P + L], lq_pad, 1), k_all, ek, v_all, P)[:, :L]
            rs0 = _ret_state_to_kernel(past[3][i])
            ss0 = jnp.transpose(past[4][i], (0, 2, 1, 3)).reshape(B, C_STATE, C_INNER)
            cs0 = jnp.concatenate([jnp.zeros((B, 8 - (C_CONV - 1), C_CONV_DIM), F32), past[5][i]], axis=1)

        yb, ret_k = _retention(proj3, rs0, invf, g_ret[i], P)
        ret_new = _ret_state_from_kernel(ret_k)

        yc, ssm_k = _ssd(proj3, ss0, cs0, conv_w[i], conv_b[i], dtb_full, a_log[i], d_skip[i], g_ssm[i])
        ssm_new = jnp.transpose(ssm_k.reshape(B, C_STATE, C_HEADS, C_HEAD_DIM), (0, 2, 1, 3))
        xbc = proj3[:, :, COL_XBC:COL_XBC + C_CONV_DIM]
        if L >= C_CONV - 1:
            conv_new = xbc[:, L - (C_CONV - 1):]
        else:
            prev = jnp.zeros((B, C_CONV - 1, C_CONV_DIM), F32) if past is None else past[5][i]
            conv_new = jnp.concatenate([prev, xbc], axis=1)[:, -(C_CONV - 1):]

        h = _outproj(h, ya.reshape(T, A_W), yb.reshape(T, B_W), yc.reshape(T, C_INNER), pw["w_out"])
        h = _peer(h, g_ffn[i], pw["wq_t"], peer_k1[i], peer_k2[i], pw["u"], pw["vt"])
        h = _ple(h, p[i].reshape(T, PLE_DIM), g_ple[i], pw["wg"], pw["wp"], g_final, final_norm=(i == depth - 1))
        outs.append((k_new.reshape(B, L, A_HEADS, A_HEAD_DIM), v_new.reshape(B, L, A_HEADS, A_HEAD_DIM),
                     lf3[:, :, :A_HEADS], ret_new, ssm_new, conv_new))
    stacked = tuple(jnp.stack([o[n] for o in outs]) for n in range(6))
    return h.reshape(B, L, D_MODEL), stacked


def kernel(x_prompt, x_sample, cache_k_fox, cache_v_fox, cache_logf_fox, state_ret, state_ssm, state_conv, p_prompt, p_sample, g_mix, w_in, b_forget, g_ret, conv_w, conv_b, dt_bias, a_log, d_skip, g_ssm, w_out, g_ffn, peer_wq, peer_k1, peer_k2, peer_u, peer_v, g_ple, w_ple_gate, w_ple, g_final):
    weights = (g_mix, w_in, b_forget, g_ret, conv_w, conv_b, dt_bias, a_log, d_skip, g_ssm, w_out,
               g_ffn, peer_wq, peer_k1, peer_k2, peer_u, peer_v, g_ple, w_ple_gate, w_ple, g_final)
    prepped = [_prep_layer_weights(weights, i) for i in range(w_in.shape[0])]
    y_prompt, (k_p, v_p, lf_p, ret_p, ssm_p, conv_p) = _run_trunk(x_prompt, p_prompt, None, weights, prepped)
    past = (cache_k_fox, cache_v_fox, cache_logf_fox, state_ret, state_ssm, state_conv)
    y_sample, (k_s, v_s, lf_s, ret_s, ssm_s, conv_s) = _run_trunk(x_sample, p_sample, past, weights, prepped)
    return (y_prompt, y_sample, k_p, v_p, lf_p, ret_p, ssm_p, conv_p, k_s, v_s, lf_s, ret_s, ssm_s, conv_s)
```

```python
import functools
import math

import jax
import jax.numpy as jnp
from jax import lax
from jax.experimental import pallas as pl
from jax.experimental.pallas import tpu as pltpu

F32 = jnp.float32
BF16 = jnp.bfloat16

D_MODEL = 1024
PLE_DIM = 256
EPS = 1e-6
A_HEADS, A_HEAD_DIM = 4, 64
B_HEADS, B_HEAD_DIM = 4, 64
ROPE_BASE = 10000.0
C_HEADS, C_HEAD_DIM = 8, 64
C_INNER = C_HEADS * C_HEAD_DIM
C_GROUPS, C_STATE, C_CONV = 2, 128, 4
C_CONV_DIM = C_INNER + 2 * C_GROUPS * C_STATE
PEER_HEADS, PEER_NKEYS, PEER_KEY_DIM, PEER_TOPK = 8, 128, 256, 16
PEER_HALF = PEER_KEY_DIM // 2
PEER_EXPERTS = PEER_NKEYS * PEER_NKEYS

LANES = 128
BF16_ROWS = 16
A_W = A_HEADS * A_HEAD_DIM
B_W = B_HEADS * B_HEAD_DIM
G_W = C_GROUPS * C_STATE

COL_QA = 0
COL_KA = COL_QA + A_W
COL_VA = COL_KA + A_W
COL_QB = COL_VA + A_W
COL_KB = COL_QB + B_W
COL_VB = COL_KB + B_W
COL_GB = COL_VB + B_W
COL_FA = COL_GB + B_W
COL_ZC = COL_FA + 2 * LANES
COL_DT = COL_ZC + C_INNER
COL_XBC = COL_DT + C_INNER
PROJ_W = COL_XBC + C_CONV_DIM
assert all(c % A_W == 0 for c in (COL_QA, COL_KA, COL_VA, COL_QB, COL_KB, COL_VB, COL_GB))
assert COL_ZC % C_INNER == 0 and COL_DT % C_INNER == 0 and COL_XBC % C_CONV_DIM == 0 and COL_FA % LANES == 0

NEG_BIG = -1e30
VMEM_LIMIT_BYTES = 52 * 1024 * 1024


def _cparams(*sem):
    return pltpu.CompilerParams(dimension_semantics=sem, vmem_limit_bytes=VMEM_LIMIT_BYTES)


def _tile(n, pref):
    t = min(n, pref)
    while n % t:
        t -= 8
    assert t > 0
    return t


def _rms(x, g):
    return x * lax.rsqrt(jnp.mean(x * x, axis=-1, keepdims=True) + EPS) * g


def _split3(x):
    hi = x.astype(BF16)
    r1 = x - hi.astype(F32)
    mid = r1.astype(BF16)
    lo = (r1 - mid.astype(F32)).astype(BF16)
    return hi, mid, lo


def _dot_sel(sel_bf16, x):
    hi, mid, lo = _split3(x)
    d = lambda p: jnp.dot(sel_bf16, p, preferred_element_type=F32)
    return d(hi) + d(mid) + d(lo)


def _dot_sel_rhs(x, sel_bf16):
    hi, mid, lo = _split3(x)
    d = lambda p: jnp.dot(p, sel_bf16, preferred_element_type=F32)
    return d(hi) + d(mid) + d(lo)


def _inproj_kernel(x_ref, g_ref, w_ref, bf_ref, o_ref, lf_ref, kb_ref, vb_ref, kf_ref, vf_ref):
    xn = _rms(x_ref[...], g_ref[...])
    y = jnp.dot(xn.astype(BF16), w_ref[...], preferred_element_type=F32)
    o_ref[...] = y
    kf_ref[...] = y[:, COL_KA:COL_KA + A_W]
    vf_ref[...] = y[:, COL_VA:COL_VA + A_W]
    kb_ref[...] = y[:, COL_KA:COL_KA + A_W].astype(BF16)
    vb_ref[...] = y[:, COL_VA:COL_VA + A_W].astype(BF16)
    fa = y[:, COL_FA:COL_FA + LANES] + bf_ref[...]
    lane = lax.broadcasted_iota(jnp.int32, fa.shape, 1)
    lf_ref[...] = jnp.where(lane < A_HEADS, jax.nn.log_sigmoid(fa), 0.0)


def _inproj(x2d, g, w, bf_pad):
    T = x2d.shape[0]
    tm = _tile(T, 256)
    return pl.pallas_call(
        _inproj_kernel,
        grid=(T // tm,),
        in_specs=[pl.BlockSpec((tm, D_MODEL), lambda i: (i, 0)),
                  pl.BlockSpec((1, D_MODEL), lambda i: (0, 0)),
                  pl.BlockSpec((D_MODEL, PROJ_W), lambda i: (0, 0)),
                  pl.BlockSpec((1, LANES), lambda i: (0, 0))],
        out_specs=[pl.BlockSpec((tm, PROJ_W), lambda i: (i, 0)),
                   pl.BlockSpec((tm, LANES), lambda i: (i, 0)),
                   pl.BlockSpec((tm, A_W), lambda i: (i, 0)),
                   pl.BlockSpec((tm, A_W), lambda i: (i, 0)),
                   pl.BlockSpec((tm, A_W), lambda i: (i, 0)),
                   pl.BlockSpec((tm, A_W), lambda i: (i, 0))],
        out_shape=[jax.ShapeDtypeStruct((T, PROJ_W), F32),
                   jax.ShapeDtypeStruct((T, LANES), F32),
                   jax.ShapeDtypeStruct((T, A_W), BF16),
                   jax.ShapeDtypeStruct((T, A_W), BF16),
                   jax.ShapeDtypeStruct((T, A_W), F32),
                   jax.ShapeDtypeStruct((T, A_W), F32)],
        compiler_params=_cparams("parallel"),
        name="inproj",
    )(x2d, g.reshape(1, D_MODEL), w, bf_pad)


SRC_QA = 0
SRC_KA = SRC_QA + A_W
SRC_VA = SRC_KA + A_W
SRC_FA = SRC_VA + A_W
SRC_QB = SRC_FA + A_HEADS
SRC_KB = SRC_QB + B_W
SRC_VB = SRC_KB + B_W
SRC_GB = SRC_VB + B_W
SRC_ZC = SRC_GB + B_W
SRC_XBC = SRC_ZC + C_INNER
SRC_DT = SRC_XBC + C_CONV_DIM
IN_WIDTH = SRC_DT + C_HEADS


def _reorder_w_in_kernel(x_ref, o_ref):
    rows = x_ref.shape[0]

    def copy(dst, src, width):
        o_ref[:, dst:dst + width] = x_ref[:, src:src + width].astype(BF16)

    copy(COL_QA, SRC_QA, 3 * A_W)
    copy(COL_VB, SRC_VB, 2 * B_W)
    copy(COL_ZC, SRC_ZC, C_INNER)
    copy(COL_XBC, SRC_XBC, C_CONV_DIM)
    half = B_HEAD_DIM // 2
    for dst, src in ((COL_QB, SRC_QB), (COL_KB, SRC_KB)):
        for k in range(2):
            for h in range(B_HEADS):
                copy(dst + k * LANES + h * half, src + h * B_HEAD_DIM + k * half, half)
    o_ref[:, COL_FA:COL_FA + 2 * LANES] = jnp.zeros((rows, 2 * LANES), BF16)
    copy(COL_FA, SRC_FA, A_HEADS)
    for h in range(C_HEADS):
        col = x_ref[:, SRC_DT + h:SRC_DT + h + 1].astype(BF16)
        o_ref[:, COL_DT + h * C_HEAD_DIM:COL_DT + (h + 1) * C_HEAD_DIM] = jnp.broadcast_to(col, (rows, C_HEAD_DIM))


def _prep_w_in(w_in_all, layer, b_forget, dt_bias):
    assert w_in_all.shape[1:] == (D_MODEL, IN_WIDTH)
    tr = 256
    w = pl.pallas_call(
        _reorder_w_in_kernel,
        grid=(D_MODEL // tr,),
        in_specs=[pl.BlockSpec((None, tr, IN_WIDTH), lambda i: (layer, i, 0))],
        out_specs=pl.BlockSpec((tr, PROJ_W), lambda i: (i, 0)),
        out_shape=jax.ShapeDtypeStruct((D_MODEL, PROJ_W), BF16),
        compiler_params=_cparams("parallel"),
        name="reorder_w_in",
    )(w_in_all)
    bf_pad = jnp.concatenate([b_forget, jnp.zeros((LANES - A_HEADS,), F32)]).reshape(1, LANES)
    dtb_full = jnp.repeat(dt_bias, C_HEAD_DIM).reshape(1, C_INNER)
    return w, bf_pad, dtb_full


FOX_TQ = 512
FOX_TK = 512
FOX_TQ_SUB = 512
FOX_XTRA = 6


def _fox_xtra_base(h):
    return A_HEAD_DIM * ((h + 1) % A_HEADS)


def _cumsum_kernel(lf_ref, eq_ref, ek_ref, carry_scr):
    @pl.when(pl.program_id(1) == 0)
    def _():
        carry_scr[...] = jnp.zeros_like(carry_scr)

    lf = lf_ref[0]
    tc = lf.shape[0]
    r = lax.broadcasted_iota(jnp.int32, (tc, tc), 0)
    c = lax.broadcasted_iota(jnp.int32, (tc, tc), 1)
    tri = jnp.where(r >= c, 1.0, 0.0).astype(BF16)
    cum = _dot_sel(tri, lf) + carry_scr[0:1, :]
    carry_scr[...] = jnp.broadcast_to(cum[tc - 1:tc, :], carry_scr.shape)

    pieces = _split3(cum)
    src = lax.broadcasted_iota(jnp.int32, (LANES, A_W), 0)
    dst = lax.broadcasted_iota(jnp.int32, (LANES, A_W), 1)
    lane = lax.broadcasted_iota(jnp.int32, (1, A_W), 1)
    one_q = jnp.zeros((1, A_W), F32)
    one_k = jnp.zeros((1, A_W), F32)
    for h in range(A_HEADS):
        base = _fox_xtra_base(h)
        one_q = jnp.where((lane >= base + 3) & (lane < base + FOX_XTRA), 1.0, one_q)
        one_k = jnp.where((lane >= base) & (lane < base + 3), 1.0, one_k)
    eq, ek = one_q, one_k
    for k, piece in enumerate(pieces):
        hit_q = functools.reduce(jnp.logical_or, [(src == h) & (dst == _fox_xtra_base(h) + k) for h in range(A_HEADS)])
        hit_k = functools.reduce(jnp.logical_or,
                                 [(src == h) & (dst == _fox_xtra_base(h) + 3 + k) for h in range(A_HEADS)])
        eq = eq + jnp.dot(piece, jnp.where(hit_q, 1.0, 0.0).astype(BF16), preferred_element_type=F32)
        ek = ek + jnp.dot(piece, jnp.where(hit_k, -1.0, 0.0).astype(BF16), preferred_element_type=F32)
    eq_ref[0] = eq.astype(BF16)
    ek_ref[0] = ek.astype(BF16)


def _cumsum(lf):
    B, L, _ = lf.shape
    tc = _tile(L, 512)
    return pl.pallas_call(
        _cumsum_kernel,
        grid=(B, L // tc),
        in_specs=[pl.BlockSpec((1, tc, LANES), lambda b, j: (b, j, 0))],
        out_specs=[pl.BlockSpec((1, tc, A_W), lambda b, j: (b, j, 0)),
                   pl.BlockSpec((1, tc, A_W), lambda b, j: (b, j, 0))],
        out_shape=[jax.ShapeDtypeStruct((B, L, A_W), BF16),
                   jax.ShapeDtypeStruct((B, L, A_W), BF16)],
        scratch_shapes=[pltpu.VMEM((8, LANES), F32)],
        compiler_params=_cparams("parallel", "arbitrary"),
        name="fox_cumsum",
    )(lf)


def _fox_kernel(q_ref, eq_ref, kb_ref, ek_ref, vb_ref, o_ref, qp_scr, m_scr, l_scr, acc_scr, *,
                tq, tk, offset, nk):
    i = pl.program_id(1)
    scale = A_HEAD_DIM ** -0.5
    q_lo = offset + i * tq

    def in_head(lane, h):
        return (lane >= h * A_HEAD_DIM) & (lane < (h + 1) * A_HEAD_DIM)

    def in_xtra(lane, h):
        return (lane >= _fox_xtra_base(h)) & (lane < _fox_xtra_base(h) + FOX_XTRA)

    lane_q = lax.broadcasted_iota(jnp.int32, (tq, A_W), 1)
    q = (q_ref[0] * scale).astype(BF16)
    eq = eq_ref[0]
    for h in range(A_HEADS):
        qp_scr[h] = jnp.where(in_xtra(lane_q, h), eq, jnp.where(in_head(lane_q, h), q, jnp.zeros((), BF16)))
    m_scr[...] = jnp.full(m_scr.shape, NEG_BIG, F32)
    l_scr[...] = jnp.zeros(l_scr.shape, F32)
    acc_scr[...] = jnp.zeros(acc_scr.shape, F32)

    lane_k = lax.broadcasted_iota(jnp.int32, (tk, A_W), 1)

    def block(jj, masked):
        ks = pl.multiple_of(jj * tk, tk)
        k = kb_ref[0, pl.ds(ks, tk), :]
        ek = ek_ref[0, pl.ds(ks, tk), :]
        v = vb_ref[0, pl.ds(ks, tk), :]
        if masked:
            kpos = ks + lax.broadcasted_iota(jnp.int32, (tk, tq), 0)
            qpos = q_lo + lax.broadcasted_iota(jnp.int32, (tk, tq), 1)
            vis = kpos <= qpos
        tqs = min(tq, FOX_TQ_SUB)
        ps, alphas = [], []
        for h in range(A_HEADS):
            kp = jnp.where(in_xtra(lane_k, h), ek, k)
            for qs in range(tq // tqs):
                cols = slice(qs * tqs, (qs + 1) * tqs)
                s = lax.dot_general(kp, qp_scr[h, cols, :], (((1,), (1,)), ((), ())),
                                    preferred_element_type=F32)
                if masked:
                    s = jnp.where(vis[:, cols], s, NEG_BIG)
                m_prev = m_scr[h, 0:1, cols]
                m_next = jnp.maximum(m_prev, jnp.max(s, axis=0, keepdims=True))
                alpha = jnp.exp(m_prev - m_next)
                p = jnp.exp(s - m_next)
                l_scr[h, :, cols] = jnp.broadcast_to(
                    alpha * l_scr[h, 0:1, cols] + jnp.sum(p, axis=0, keepdims=True), (8, tqs))
                m_scr[h, :, cols] = jnp.broadcast_to(m_next, (8, tqs))
                ps.append(p.astype(BF16))
                alphas.append(alpha)
        alphas = [jnp.concatenate(alphas[h * (tq // tqs):(h + 1) * (tq // tqs)], axis=1) for h in range(A_HEADS)]
        pv = lax.dot_general(v, jnp.concatenate(ps, axis=1), (((0,), (0,)), ((), ())),
                             preferred_element_type=F32)
        for h in range(A_HEADS):
            acc_scr[h] = acc_scr[h] * alphas[h] + pv[:, h * tq:(h + 1) * tq]

    n_full = (q_lo + 1) // tk
    n_vis = jnp.minimum((q_lo + tq - 1) // tk + 1, nk)

    def full_body(jj, carry):
        block(jj, False)
        return carry

    def edge_body(jj, carry):
        block(jj, True)
        return carry

    lax.fori_loop(0, n_full, full_body, 0)
    lax.fori_loop(n_full, n_vis, edge_body, 0)

    row = lax.broadcasted_iota(jnp.int32, (A_W, tq), 0)
    out_t = jnp.zeros((A_W, tq), F32)
    for h in range(A_HEADS):
        out_t = out_t + jnp.where(in_head(row, h), acc_scr[h] / l_scr[h, 0:1, :], 0.0)
    o_ref[0] = out_t.T.astype(o_ref.dtype)


def _fox(q_src, q_col, eq, kb, ek, vb, offset):
    B, Lq, _ = q_src.shape
    Lk = kb.shape[1]
    tq = _tile(Lq, FOX_TQ)
    tk = _tile(Lk, FOX_TK)
    assert tq % LANES == 0 and tk % LANES == 0
    nq, nk = Lq // tq, Lk // tk
    whole = lambda b, i: (b, 0, 0)
    kern = functools.partial(_fox_kernel, tq=tq, tk=tk, offset=offset, nk=nk)
    return pl.pallas_call(
        kern,
        grid=(B, nq),
        in_specs=[pl.BlockSpec((1, tq, A_W), lambda b, i: (b, i, q_col // A_W)),
                  pl.BlockSpec((1, tq, A_W), lambda b, i: (b, i, 0)),
                  pl.BlockSpec((1, Lk, A_W), whole),
                  pl.BlockSpec((1, Lk, A_W), whole),
                  pl.BlockSpec((1, Lk, A_W), whole)],
        out_specs=pl.BlockSpec((1, tq, A_W), lambda b, i: (b, i, 0)),
        out_shape=jax.ShapeDtypeStruct((B, Lq, A_W), BF16),
        scratch_shapes=[pltpu.VMEM((A_HEADS, tq, A_W), BF16),
                        pltpu.VMEM((A_HEADS, 8, tq), F32),
                        pltpu.VMEM((A_HEADS, 8, tq), F32),
                        pltpu.VMEM((A_HEADS, A_W, tq), F32)],
        compiler_params=_cparams("parallel", "arbitrary"),
        name="fox_attention",
    )(q_src, eq, kb, ek, vb)


def _ret_lane_head_v():
    return lax.broadcasted_iota(jnp.int32, (1, B_W), 1) // B_HEAD_DIM


def _ret_lane_head_qk():
    return (lax.broadcasted_iota(jnp.int32, (1, B_W), 1) % LANES) // (B_HEAD_DIM // 2)


def _log_gamma(h):
    return math.log1p(-2.0 ** (-5.0 - h))


def _ret_kernel(q_ref, k_ref, v_ref, g_ref, s0_ref, invf_ref, gret_ref, y_ref, sout_ref,
                state_scr, dec_scr, *, lc, offset, nchunks):
    c = pl.program_id(1)

    @pl.when(c == 0)
    def _():
        state_scr[...] = s0_ref[0]
        r = lax.broadcasted_iota(jnp.int32, (lc, lc), 0)
        s = lax.broadcasted_iota(jnp.int32, (lc, lc), 1)
        d = (r - s).astype(F32)
        for h in range(B_HEADS):
            dec_scr[h] = jnp.where(r >= s, jnp.exp(d * _log_gamma(h)), 0.0)

    hv = _ret_lane_head_v()
    hqk = _ret_lane_head_qk()
    lg_v = jnp.zeros((1, B_W), F32)
    lg_qk = jnp.zeros((1, B_W), F32)
    for h in range(B_HEADS):
        lg_v = jnp.where(hv == h, _log_gamma(h), lg_v)
        lg_qk = jnp.where(hqk == h, _log_gamma(h), lg_qk)

    pos = (offset + c * lc + lax.broadcasted_iota(jnp.int32, (lc, LANES), 0)).astype(F32)
    ang = pos * invf_ref[...]
    cos, sin = jnp.cos(ang), jnp.sin(ang)

    def rot(x):
        x1, x2 = x[:, :LANES], x[:, LANES:]
        return jnp.concatenate([x1 * cos - x2 * sin, x1 * sin + x2 * cos], axis=1)

    qr = rot(q_ref[0])
    kr = rot(k_ref[0]) * (B_HEAD_DIM ** -0.5)
    v = v_ref[0]
    vb = v.astype(BF16)
    krb = kr.astype(BF16)

    lpos = lax.broadcasted_iota(jnp.int32, (lc, B_W), 0).astype(F32)
    state = state_scr[...]
    y = jnp.dot(qr.astype(BF16), state.astype(BF16), preferred_element_type=F32) * jnp.exp((lpos + 1.0) * lg_v)
    for h in range(B_HEADS):
        qh = jnp.where(hqk == h, qr, 0.0).astype(BF16)
        sc = lax.dot_general(qh, krb, (((1,), (1,)), ((), ())), preferred_element_type=F32) * dec_scr[h]
        yh = jnp.dot(sc.astype(BF16), vb, preferred_element_type=F32)
        y = y + jnp.where(hv == h, yh, 0.0)

    kw = (kr * jnp.exp((lc - 1.0 - lpos) * lg_qk)).astype(BF16)
    upd = lax.dot_general(kw, vb, (((0,), (0,)), ((), ())), preferred_element_type=F32)
    row_head = (lax.broadcasted_iota(jnp.int32, (B_W, B_W), 0) % LANES) // (B_HEAD_DIM // 2)
    col_head = lax.broadcasted_iota(jnp.int32, (B_W, B_W), 1) // B_HEAD_DIM
    lg_rows = jnp.zeros((B_W, B_W), F32)
    for h in range(B_HEADS):
        lg_rows = jnp.where(row_head == h, _log_gamma(h), lg_rows)
    new_state = jnp.where(row_head == col_head, state * jnp.exp(lc * lg_rows) + upd, 0.0)
    state_scr[...] = new_state

    @pl.when(c == nchunks - 1)
    def _():
        sout_ref[0] = new_state

    ri = lax.broadcasted_iota(jnp.int32, (B_W, B_W), 0) // B_HEAD_DIM
    ci = lax.broadcasted_iota(jnp.int32, (B_W, B_W), 1) // B_HEAD_DIM
    avg = jnp.where(ri == ci, 1.0 / B_HEAD_DIM, 0.0).astype(BF16)
    yc = y - _dot_sel_rhs(y, avg)
    var = _dot_sel_rhs(yc * yc, avg)
    yn = yc * lax.rsqrt(var + EPS) * gret_ref[...]
    y_ref[0] = (jax.nn.silu(g_ref[0]) * yn).astype(y_ref.dtype)


def _retention(proj3, s0, invf, g_ret, offset):
    B, L, _ = proj3.shape
    lc = _tile(L, 256)
    nch = L // lc
    blk = lambda col: pl.BlockSpec((1, lc, B_W), lambda b, c: (b, c, col // B_W))
    kern = functools.partial(_ret_kernel, lc=lc, offset=offset, nchunks=nch)
    return pl.pallas_call(
        kern,
        grid=(B, nch),
        in_specs=[blk(COL_QB), blk(COL_KB), blk(COL_VB), blk(COL_GB),
                  pl.BlockSpec((1, B_W, B_W), lambda b, c: (b, 0, 0)),
                  pl.BlockSpec((1, LANES), lambda b, c: (0, 0)),
                  pl.BlockSpec((1, B_W), lambda b, c: (0, 0))],
        out_specs=[pl.BlockSpec((1, lc, B_W), lambda b, c: (b, c, 0)),
                   pl.BlockSpec((1, B_W, B_W), lambda b, c: (b, 0, 0))],
        out_shape=[jax.ShapeDtypeStruct((B, L, B_W), BF16),
                   jax.ShapeDtypeStruct((B, B_W, B_W), F32)],
        scratch_shapes=[pltpu.VMEM((B_W, B_W), F32),
                        pltpu.VMEM((B_HEADS, lc, lc), F32)],
        compiler_params=_cparams("parallel", "arbitrary"),
        name="retention",
    )(proj3, proj3, proj3, proj3, s0, invf, g_ret.reshape(1, B_W))


def _ret_state_to_kernel(s):
    B = s.shape[0]
    half = B_HEAD_DIM // 2
    s = s.reshape(B, B_HEADS, 2, half, B_HEAD_DIM)
    eye = jnp.eye(B_HEADS, dtype=s.dtype)
    full = jnp.einsum('bhkie,hg->bkhige', s, eye)
    return full.reshape(B, B_W, B_W)


def _ret_state_from_kernel(s):
    B = s.shape[0]
    half = B_HEAD_DIM // 2
    s = s.reshape(B, 2, B_HEADS, half, B_HEADS, B_HEAD_DIM)
    d = jnp.einsum('bkhihe->bhkie', s)
    return d.reshape(B, B_HEADS, B_HEAD_DIM, B_HEAD_DIM)


def _ssd_kernel(z_ref, xbc_ref, dt_ref, s0_ref, cs0_ref, cw_ref, cb_ref, dtb_ref, alog_ref, dsk_ref, gs_ref,
                y_ref, sout_ref, state_scr, xp_scr, *, lc, nchunks):
    c = pl.program_id(1)
    PADR = 8

    @pl.when(c == 0)
    def _():
        state_scr[...] = s0_ref[0]
        xp_scr[0:PADR, :] = cs0_ref[0]

    xbc = xbc_ref[0]
    xp_scr[PADR:PADR + lc, :] = xbc
    cw = cw_ref[...]
    conv = xp_scr[PADR - 3:PADR - 3 + lc, :] * cw[0:1, :]
    for jj in range(1, C_CONV):
        conv = conv + xp_scr[PADR - 3 + jj:PADR - 3 + jj + lc, :] * cw[jj:jj + 1, :]
    act = jax.nn.silu(conv + cb_ref[...])
    xp_scr[0:PADR, :] = xp_scr[lc:lc + PADR, :]

    xs = act[:, :C_INNER]
    bm = act[:, C_INNER:C_INNER + G_W]
    cm = act[:, C_INNER + G_W:]
    dt = jax.nn.softplus(dt_ref[0] + dtb_ref[...])
    loga = dt * (-jnp.exp(alog_ref[...]))

    r = lax.broadcasted_iota(jnp.int32, (lc, lc), 0)
    s = lax.broadcasted_iota(jnp.int32, (lc, lc), 1)
    causal = r >= s
    tri = jnp.where(causal, 1.0, 0.0).astype(BF16)
    cum = _dot_sel(tri, loga)
    total = cum[lc - 1:lc, :]
    xdt = xs * dt
    xdtb = xdt.astype(BF16)
    bmb = bm.astype(BF16)
    cmb = cm.astype(BF16)
    state = state_scr[...]

    gw = C_INNER // C_GROUPS
    cum_t = [cum[:, kk * LANES:(kk + 1) * LANES].T for kk in range(C_INNER // LANES)]
    ys = []
    for g in range(C_GROUPS):
        cg = cmb[:, g * C_STATE:(g + 1) * C_STATE]
        bg = bmb[:, g * C_STATE:(g + 1) * C_STATE]
        scores = lax.dot_general(cg, bg, (((1,), (1,)), ((), ())), preferred_element_type=F32)
        y_g = jnp.dot(cg, state[:, g * gw:(g + 1) * gw].astype(BF16), preferred_element_type=F32)
        y_g = y_g * jnp.exp(cum[:, g * gw:(g + 1) * gw])
        lane_head = lax.broadcasted_iota(jnp.int32, (1, gw), 1) // C_HEAD_DIM
        xg = xdtb[:, g * gw:(g + 1) * gw]
        for hh in range(C_HEADS // C_GROUPS):
            col = g * gw + hh * C_HEAD_DIM
            cum_col = cum[:, col:col + 1]
            cum_row = cum_t[col // LANES][col % LANES:col % LANES + 1, :]
            dec = jnp.exp(jnp.where(causal, cum_col - cum_row, NEG_BIG))
            yh = jnp.dot((scores * dec).astype(BF16), xg, preferred_element_type=F32)
            y_g = y_g + jnp.where(lane_head == hh, yh, 0.0)
        ys.append(y_g)
        xw = (xdt[:, g * gw:(g + 1) * gw] * jnp.exp(total[:, g * gw:(g + 1) * gw] - cum[:, g * gw:(g + 1) * gw]))
        upd = lax.dot_general(bg, xw.astype(BF16), (((0,), (0,)), ((), ())), preferred_element_type=F32)
        state_scr[:, g * gw:(g + 1) * gw] = state[:, g * gw:(g + 1) * gw] * jnp.exp(total[:, g * gw:(g + 1) * gw]) + upd
    y = jnp.concatenate(ys, axis=1)

    @pl.when(c == nchunks - 1)
    def _():
        sout_ref[0] = state_scr[...]

    yc = (y + xs * dsk_ref[...]) * jax.nn.silu(z_ref[0])
    y_ref[0] = _rms(yc, gs_ref[...]).astype(y_ref.dtype)


def _ssd(proj3, s0, cs0, conv_w, conv_b, dtb_full, a_log, d_skip, g_ssm):
    B, L, _ = proj3.shape
    lc = _tile(L, 256)
    nch = L // lc
    rep = lambda p: jnp.repeat(p, C_HEAD_DIM).reshape(1, C_INNER)
    kern = functools.partial(_ssd_kernel, lc=lc, nchunks=nch)
    cst = lambda shape: pl.BlockSpec(shape, lambda b, c: (0,) * len(shape))
    return pl.pallas_call(
        kern,
        grid=(B, nch),
        in_specs=[pl.BlockSpec((1, lc, C_INNER), lambda b, c: (b, c, COL_ZC // C_INNER)),
                  pl.BlockSpec((1, lc, C_CONV_DIM), lambda b, c: (b, c, COL_XBC // C_CONV_DIM)),
                  pl.BlockSpec((1, lc, C_INNER), lambda b, c: (b, c, COL_DT // C_INNER)),
                  pl.BlockSpec((1, C_STATE, C_INNER), lambda b, c: (b, 0, 0)),
                  pl.BlockSpec((1, 8, C_CONV_DIM), lambda b, c: (b, 0, 0)),
                  cst((C_CONV, C_CONV_DIM)), cst((1, C_CONV_DIM)), cst((1, C_INNER)), cst((1, C_INNER)),
                  cst((1, C_INNER)), cst((1, C_INNER))],
        out_specs=[pl.BlockSpec((1, lc, C_INNER), lambda b, c: (b, c, 0)),
                   pl.BlockSpec((1, C_STATE, C_INNER), lambda b, c: (b, 0, 0))],
        out_shape=[jax.ShapeDtypeStruct((B, L, C_INNER), BF16),
                   jax.ShapeDtypeStruct((B, C_STATE, C_INNER), F32)],
        scratch_shapes=[pltpu.VMEM((C_STATE, C_INNER), F32),
                        pltpu.VMEM((lc + 8, C_CONV_DIM), F32)],
        compiler_params=_cparams("parallel", "arbitrary"),
        name="ssd",
    )(proj3, proj3, proj3, s0, cs0, conv_w, conv_b.reshape(1, C_CONV_DIM), dtb_full, rep(a_log), rep(d_skip),
      g_ssm.reshape(1, C_INNER))


def _outproj_kernel(h_ref, ya_ref, yb_ref, yc_ref, w_ref, o_ref):
    acc = jnp.dot(ya_ref[...], w_ref[0:A_W, :], preferred_element_type=F32)
    acc = acc + jnp.dot(yb_ref[...], w_ref[A_W:A_W + B_W, :], preferred_element_type=F32)
    acc = acc + jnp.dot(yc_ref[...], w_ref[A_W + B_W:, :], preferred_element_type=F32)
    o_ref[...] = h_ref[...] + acc


def _outproj(h2d, ya, yb, yc, w_out_bf16):
    T = h2d.shape[0]
    tm = _tile(T, 512)
    row = lambda w: pl.BlockSpec((tm, w), lambda i: (i, 0))
    return pl.pallas_call(
        _outproj_kernel,
        grid=(T // tm,),
        in_specs=[row(D_MODEL), row(A_W), row(B_W), row(C_INNER),
                  pl.BlockSpec((D_MODEL, D_MODEL), lambda i: (0, 0))],
        out_specs=row(D_MODEL),
        out_shape=jax.ShapeDtypeStruct((T, D_MODEL), F32),
        compiler_params=_cparams("parallel"),
        name="outproj",
    )(h2d, ya, yb, yc, w_out_bf16)


PEER_SUB = 128
PEER_STEP = 2048
PEER_CAND_COLS = 4


def _gelu_tanh(x):
    k0 = math.sqrt(2.0 / math.pi)
    inner = x * (k0 + (k0 * 0.044715) * (x * x))
    return (0.5 * x) * (1.0 + jnp.tanh(inner))


def _top16(x, want_rank):
    vals = []
    rank = jnp.full(x.shape, float(PEER_TOPK), F32) if want_rank else None
    for r in range(PEER_TOPK):
        m = jnp.max(x, axis=0, keepdims=True)
        vals.append(m)
        hit = x == m
        if want_rank:
            rank = jnp.where(hit, float(r), rank)
        x = jnp.where(hit, NEG_BIG, x)
    return jnp.concatenate(vals, axis=0), rank


def _sort_pairs(n):
    def merge(lo, hi, r):
        step = r * 2
        if step < hi - lo:
            yield from merge(lo, hi, step)
            yield from merge(lo + r, hi, step)
            yield from [(i, i + r) for i in range(lo + r, hi - r, step)]
        else:
            yield (lo, lo + r)

    def sort(lo, hi):
        if hi - lo >= 1:
            mid = lo + (hi - lo) // 2
            yield from sort(lo, mid)
            yield from sort(mid + 1, hi)
            yield from merge(lo, hi, 1)

    return tuple(sort(0, n - 1))


def _top16_values(x):
    v = [x[8 * g:8 * g + 8, :] for g in range(PEER_NKEYS // 8)]
    for a, b in _sort_pairs(PEER_NKEYS // 8):
        v[a], v[b] = jnp.maximum(v[a], v[b]), jnp.minimum(v[a], v[b])
    vals = []
    for r in range(PEER_TOPK):
        m = jnp.max(v[0], axis=0, keepdims=True)
        vals.append(m)
        hit = v[0] == m
        for k in range(PEER_TOPK - 1 - r):
            v[k] = jnp.where(hit, v[k + 1], v[k])
    return jnp.concatenate(vals, axis=0)


def _peer_select(s1, s2):
    n = s1.shape[1]
    a16 = _top16_values(s1)
    b16, rank2 = _top16(s2, True)
    ridx = lax.broadcasted_iota(jnp.int32, (PEER_TOPK, n), 0)
    sub = lax.broadcasted_iota(jnp.int32, (8, n), 0)
    lo, hi = slice(0, 8), slice(8, PEER_TOPK)
    col = lambda r2, rows: a16[rows, :] + b16[r2:r2 + 1, :]
    row = lambda r1, cols: b16[cols, :] + a16[r1:r1 + 1, :]
    assert PEER_CAND_COLS == 4 and PEER_TOPK == 16
    groups = [col(0, lo), col(0, hi), col(1, lo),
              jnp.where(sub <= 4, col(2, lo), NEG_BIG),
              jnp.where(sub <= 3, col(3, lo), row(1, lo)),
              jnp.where(sub >= 4, row(0, lo), NEG_BIG),
              row(0, hi),
              jnp.where(sub == 4, row(2, lo), NEG_BIG)]
    cand = jnp.concatenate(groups, axis=0)
    v = list(groups)
    for a_i, b_i in _sort_pairs(8):
        v[a_i], v[b_i] = jnp.maximum(v[a_i], v[b_i]), jnp.minimum(v[a_i], v[b_i])
    tau = None
    for r in range(PEER_TOPK):
        tau = jnp.max(v[0], axis=0, keepdims=True)
        hit = v[0] == tau
        depth = min(8, PEER_TOPK - r)
        for k in range(depth - 1):
            v[k] = jnp.where(hit, v[k + 1], v[k])
        if depth == 8:
            v[7] = jnp.where(hit, NEG_BIG, v[7])
    top = a16[0:1, :] + b16[0:1, :]
    z = jnp.sum(jnp.where(cand >= tau, jnp.exp(cand - top), 0.0), axis=0, keepdims=True)
    cnt = jnp.zeros(s1.shape, F32)
    for r2 in range(PEER_CAND_COLS):
        cnt = cnt + jnp.where(s1 + b16[r2:r2 + 1, :] >= tau, 1.0, 0.0)
    for r1 in range(PEER_TOPK // (PEER_CAND_COLS + 1)):
        a_r = a16[r1:r1 + 1, :]
        tail = jnp.where((ridx >= PEER_CAND_COLS) & (b16 + a_r >= tau), 1.0, 0.0)
        cnt = cnt + jnp.where(s1 == a_r, jnp.sum(tail, axis=0, keepdims=True), 0.0)
    a = jnp.exp(s1 - a16[0:1, :]) / z
    b = jnp.exp(s2 - b16[0:1, :])
    return a, cnt, rank2, b


def _peer_kernel(h_ref, g_ref, wq_ref, k1_ref, k2_ref, u_ref, vt_ref, o_ref,
                 xt_scr, a_scr, cnt_scr, rank_scr, b_scr, acc_scr, *, tm, te, ne):
    j = pl.program_id(1)
    n1 = te // PEER_NKEYS
    nchunk = tm // LANES
    nsub = te // PEER_SUB
    slabs = PEER_SUB // PEER_NKEYS

    def gate_weights(sb):
        ws = []
        for il in range(slabs):
            i1 = j * n1 + sb * slabs + il
            w = jnp.zeros((PEER_NKEYS, tm), BF16)
            for h in range(PEER_HEADS):
                row = lambda ref: jnp.concatenate(
                    [jnp.broadcast_to(ref[h, cch, pl.ds(i1, 1), :], (BF16_ROWS, LANES)) for cch in range(nchunk)],
                    axis=1).astype(BF16)
                cnt_b = jnp.tile(row(cnt_scr), (PEER_NKEYS // BF16_ROWS, 1))
                a_b = jnp.tile(row(a_scr), (PEER_NKEYS // BF16_ROWS, 1))
                w = w + jnp.where(rank_scr[h] < cnt_b, b_scr[h], jnp.zeros((), BF16)) * a_b
            ws.append(w)
        return jnp.concatenate(ws, axis=0)

    @pl.when(j == 0)
    def _():
        xn = _rms(h_ref[...], g_ref[...])
        xt_scr[...] = xn.T.astype(BF16)
        acc_scr[...] = jnp.zeros(acc_scr.shape, F32)

        def head_body(h, carry):
            base = pl.multiple_of(h * PEER_KEY_DIM, PEER_KEY_DIM)
            k1 = k1_ref[h].astype(BF16)
            k2 = k2_ref[h].astype(BF16)
            qh = jnp.dot(wq_ref[pl.ds(base, PEER_KEY_DIM), :], xt_scr[...], preferred_element_type=F32)
            for cch in range(nchunk):
                sl = slice(cch * LANES, (cch + 1) * LANES)
                q1 = qh[0:PEER_HALF, sl].astype(BF16)
                q2 = qh[PEER_HALF:, sl].astype(BF16)
                s1 = jnp.dot(k1, q1, preferred_element_type=F32)
                s2 = jnp.dot(k2, q2, preferred_element_type=F32)
                a, cnt, rank2, b = _peer_select(s1, s2)
                a_scr[h, cch] = a
                cnt_scr[h, cch] = cnt
                rank_scr[h, :, sl] = rank2.astype(BF16)
                b_scr[h, :, sl] = b.astype(BF16)
            return carry

        lax.fori_loop(0, PEER_HEADS, head_body, 0)

    acts = []
    for sb in range(nsub):
        ht = jnp.dot(u_ref[sb * PEER_SUB:(sb + 1) * PEER_SUB, :], xt_scr[...], preferred_element_type=F32)
        acts.append(_gelu_tanh(ht.astype(BF16)) * gate_weights(sb))
    acc_scr[...] += jnp.dot(vt_ref[...], jnp.concatenate(acts, axis=0), preferred_element_type=F32)

    @pl.when(j == ne - 1)
    def _():
        o_ref[...] = h_ref[...] + acc_scr[...].T


def _transpose_cast_kernel(x_ref, o_ref):
    o_ref[...] = x_ref[...].T.astype(o_ref.dtype)


def _transpose_cast(x, dtype):
    R, C = x.shape
    tr = _tile(R, 512)
    return pl.pallas_call(
        _transpose_cast_kernel,
        grid=(R // tr,),
        in_specs=[pl.BlockSpec((tr, C), lambda i: (i, 0))],
        out_specs=pl.BlockSpec((C, tr), lambda i: (0, i)),
        out_shape=jax.ShapeDtypeStruct((C, R), dtype),
        compiler_params=_cparams("parallel"),
        name="transpose_cast",
    )(x)


def _peer(h2d, g, wq_t, k1, k2, u_bf16, vt_bf16):
    T = h2d.shape[0]
    tm = _tile(T, 512)
    assert tm % LANES == 0
    te = PEER_STEP
    ne = PEER_EXPERTS // te
    kern = functools.partial(_peer_kernel, tm=tm, te=te, ne=ne)
    hk = (PEER_HEADS, PEER_NKEYS, tm)
    hkc = (PEER_HEADS, tm // LANES, PEER_NKEYS, LANES)
    return pl.pallas_call(
        kern,
        grid=(T // tm, ne),
        in_specs=[pl.BlockSpec((tm, D_MODEL), lambda i, j: (i, 0)),
                  pl.BlockSpec((1, D_MODEL), lambda i, j: (0, 0)),
                  pl.BlockSpec((PEER_HEADS * PEER_KEY_DIM, D_MODEL), lambda i, j: (0, 0)),
                  pl.BlockSpec((PEER_HEADS, PEER_NKEYS, PEER_HALF), lambda i, j: (0, 0, 0)),
                  pl.BlockSpec((PEER_HEADS, PEER_NKEYS, PEER_HALF), lambda i, j: (0, 0, 0)),
                  pl.BlockSpec((te, D_MODEL), lambda i, j: (j, 0)),
                  pl.BlockSpec((D_MODEL, te), lambda i, j: (0, j))],
        out_specs=pl.BlockSpec((tm, D_MODEL), lambda i, j: (i, 0)),
        out_shape=jax.ShapeDtypeStruct((T, D_MODEL), F32),
        scratch_shapes=[pltpu.VMEM((D_MODEL, tm), BF16),
                        pltpu.VMEM(hkc, F32), pltpu.VMEM(hkc, F32), pltpu.VMEM(hk, BF16), pltpu.VMEM(hk, BF16),
                        pltpu.VMEM((D_MODEL, tm), F32)],
        compiler_params=_cparams("parallel", "arbitrary"),
        name="peer",
    )(h2d, g.reshape(1, D_MODEL), wq_t, k1, k2, u_bf16, vt_bf16)


def _ple_kernel(h_ref, p_ref, g_ref, wg_ref, wp_ref, gf_ref, o_ref, *, final_norm):
    h = h_ref[...]
    xn = _rms(h, g_ref[...])
    gate = jax.nn.sigmoid(jnp.dot(xn.astype(BF16), wg_ref[...], preferred_element_type=F32))
    emb = jnp.dot(p_ref[...].astype(BF16), wp_ref[...], preferred_element_type=F32)
    out = h + emb * gate
    if final_norm:
        out = _rms(out, gf_ref[...])
    o_ref[...] = out


def _ple(h2d, p2d, g, wg_bf16, wp_bf16, g_final, final_norm):
    T = h2d.shape[0]
    tm = _tile(T, 512)
    kern = functools.partial(_ple_kernel, final_norm=final_norm)
    return pl.pallas_call(
        kern,
        grid=(T // tm,),
        in_specs=[pl.BlockSpec((tm, D_MODEL), lambda i: (i, 0)),
                  pl.BlockSpec((tm, PLE_DIM), lambda i: (i, 0)),
                  pl.BlockSpec((1, D_MODEL), lambda i: (0, 0)),
                  pl.BlockSpec((D_MODEL, D_MODEL), lambda i: (0, 0)),
                  pl.BlockSpec((PLE_DIM, D_MODEL), lambda i: (0, 0)),
                  pl.BlockSpec((1, D_MODEL), lambda i: (0, 0))],
        out_specs=pl.BlockSpec((tm, D_MODEL), lambda i: (i, 0)),
        out_shape=jax.ShapeDtypeStruct((T, D_MODEL), F32),
        compiler_params=_cparams("parallel"),
        name="ple",
    )(h2d, p2d, g.reshape(1, D_MODEL), wg_bf16, wp_bf16, g_final.reshape(1, D_MODEL))


def _pad_to(x, n, axis):
    pad = n - x.shape[axis]
    if pad == 0:
        return x
    widths = [(0, 0)] * x.ndim
    widths[axis] = (0, pad)
    return jnp.pad(x, widths)


def _prep_layer_weights(weights, i):
    (g_mix, w_in, b_forget, g_ret, conv_w, conv_b, dt_bias, a_log, d_skip, g_ssm, w_out,
     g_ffn, peer_wq, peer_k1, peer_k2, peer_u, peer_v, g_ple, w_ple_gate, w_ple, g_final) = weights
    w_proj, bf_pad, dtb_full = _prep_w_in(w_in, i, b_forget[i], dt_bias[i])
    return dict(w_proj=w_proj, bf_pad=bf_pad, dtb_full=dtb_full, w_out=w_out[i].astype(BF16),
                wq_t=_transpose_cast(peer_wq[i], BF16), u=peer_u[i].astype(BF16),
                vt=_transpose_cast(peer_v[i], BF16),
                wg=w_ple_gate[i].astype(BF16), wp=w_ple[i].astype(BF16))


def _run_trunk(x, p, past, weights, prepped):
    (g_mix, w_in, b_forget, g_ret, conv_w, conv_b, dt_bias, a_log, d_skip, g_ssm, w_out,
     g_ffn, peer_wq, peer_k1, peer_k2, peer_u, peer_v, g_ple, w_ple_gate, w_ple, g_final) = weights
    B, L, _ = x.shape
    depth = w_in.shape[0]
    T = B * L
    P = 0 if past is None else past[0].shape[2]
    half = B_HEAD_DIM // 2
    invf = jnp.tile(ROPE_BASE ** (-jnp.arange(half, dtype=F32) / half), LANES // half).reshape(1, LANES)

    h = x.reshape(T, D_MODEL)
    outs = []
    for i in range(depth):
        pw = prepped[i]
        dtb_full = pw["dtb_full"]
        proj, lf, kb, vb, k_new, v_new = _inproj(h, g_mix[i], pw["w_proj"], pw["bf_pad"])
        proj3 = proj.reshape(B, L, PROJ_W)
        lf3 = lf.reshape(B, L, LANES)
        kb = kb.reshape(B, L, A_W)
        vb = vb.reshape(B, L, A_W)

        if past is None:
            eq, ek = _cumsum(lf3)
            ya = _fox(proj3, COL_QA, eq, kb, ek, vb, 0)
            rs0 = jnp.zeros((B, B_W, B_W), F32)
            ss0 = jnp.zeros((B, C_STATE, C_INNER), F32)
            cs0 = jnp.zeros((B, 8, C_CONV_DIM), F32)
        else:
            lk_pad = -(-(P + L) // FOX_TK) * FOX_TK
            lq_pad = -(-L // LANES) * LANES
            past_k = past[0][i].reshape(B, P, A_W).astype(BF16)
            past_v = past[1][i].reshape(B, P, A_W).astype(BF16)
            past_lf = _pad_to(past[2][i], LANES, 2)
            k_all = _pad_to(jnp.concatenate([past_k, kb], axis=1), lk_pad, 1)
            v_all = _pad_to(jnp.concatenate([past_v, vb], axis=1), lk_pad, 1)
            lf_all = _pad_to(jnp.concatenate([past_lf, lf3], axis=1), lk_pad, 1)
            eq, ek = _cumsum(lf_all)
            q_pad = _pad_to(proj3[:, :, COL_QA:COL_QA + A_W], lq_pad, 1)
            ya = _fox(q_pad, 0, _pad_to(eq[:, P:P + L], lq_pad, 1), k_all, ek, v_all, P)[:, :L]
            rs0 = _ret_state_to_kernel(past[3][i])
            ss0 = jnp.transpose(past[4][i], (0, 2, 1, 3)).reshape(B, C_STATE, C_INNER)
            cs0 = jnp.concatenate([jnp.zeros((B, 8 - (C_CONV - 1), C_CONV_DIM), F32), past[5][i]], axis=1)

        yb, ret_k = _retention(proj3, rs0, invf, g_ret[i], P)
        ret_new = _ret_state_from_kernel(ret_k)

        yc, ssm_k = _ssd(proj3, ss0, cs0, conv_w[i], conv_b[i], dtb_full, a_log[i], d_skip[i], g_ssm[i])
        ssm_new = jnp.transpose(ssm_k.reshape(B, C_STATE, C_HEADS, C_HEAD_DIM), (0, 2, 1, 3))
        xbc = proj3[:, :, COL_XBC:COL_XBC + C_CONV_DIM]
        if L >= C_CONV - 1:
            conv_new = xbc[:, L - (C_CONV - 1):]
        else:
            prev = jnp.zeros((B, C_CONV - 1, C_CONV_DIM), F32) if past is None else past[5][i]
            conv_new = jnp.concatenate([prev, xbc], axis=1)[:, -(C_CONV - 1):]

        h = _outproj(h, ya.reshape(T, A_W), yb.reshape(T, B_W), yc.reshape(T, C_INNER), pw["w_out"])
        h = _peer(h, g_ffn[i], pw["wq_t"], peer_k1[i], peer_k2[i], pw["u"], pw["vt"])
        h = _ple(h, p[i].reshape(T, PLE_DIM), g_ple[i], pw["wg"], pw["wp"], g_final, final_norm=(i == depth - 1))
        outs.append((k_new.reshape(B, L, A_HEADS, A_HEAD_DIM), v_new.reshape(B, L, A_HEADS, A_HEAD_DIM),
                     lf3[:, :, :A_HEADS], ret_new, ssm_new, conv_new))
    stacked = tuple(jnp.stack([o[n] for o in outs]) for n in range(6))
    return h.reshape(B, L, D_MODEL), stacked


def kernel(x_prompt, x_sample, cache_k_fox, cache_v_fox, cache_logf_fox, state_ret, state_ssm, state_conv, p_prompt, p_sample, g_mix, w_in, b_forget, g_ret, conv_w, conv_b, dt_bias, a_log, d_skip, g_ssm, w_out, g_ffn, peer_wq, peer_k1, peer_k2, peer_u, peer_v, g_ple, w_ple_gate, w_ple, g_final):
    weights = (g_mix, w_in, b_forget, g_ret, conv_w, conv_b, dt_bias, a_log, d_skip, g_ssm, w_out,
               g_ffn, peer_wq, peer_k1, peer_k2, peer_u, peer_v, g_ple, w_ple_gate, w_ple, g_final)
    prepped = [_prep_layer_weights(weights, i) for i in range(w_in.shape[0])]
    y_prompt, (k_p, v_p, lf_p, ret_p, ssm_p, conv_p) = _run_trunk(x_prompt, p_prompt, None, weights, prepped)
    past = (cache_k_fox, cache_v_fox, cache_logf_fox, state_ret, state_ssm, state_conv)
    y_sample, (k_s, v_s, lf_s, ret_s, ssm_s, conv_s) = _run_trunk(x_sample, p_sample, past, weights, prepped)
    return (y_prompt, y_sample, k_p, v_p, lf_p, ret_p, ssm_p, conv_p, k_s, v_s, lf_s, ret_s, ssm_s, conv_s)
```

```python
import functools
import math

import jax
import jax.numpy as jnp
from jax import lax
from jax.experimental import pallas as pl
from jax.experimental.pallas import tpu as pltpu

F32 = jnp.float32
BF16 = jnp.bfloat16

D_MODEL = 1024
PLE_DIM = 256
EPS = 1e-6
A_HEADS, A_HEAD_DIM = 4, 64
B_HEADS, B_HEAD_DIM = 4, 64
ROPE_BASE = 10000.0
C_HEADS, C_HEAD_DIM = 8, 64
C_INNER = C_HEADS * C_HEAD_DIM
C_GROUPS, C_STATE, C_CONV = 2, 128, 4
C_CONV_DIM = C_INNER + 2 * C_GROUPS * C_STATE
PEER_HEADS, PEER_NKEYS, PEER_KEY_DIM, PEER_TOPK = 8, 128, 256, 16
PEER_HALF = PEER_KEY_DIM // 2
PEER_EXPERTS = PEER_NKEYS * PEER_NKEYS

LANES = 128
BF16_ROWS = 16
A_W = A_HEADS * A_HEAD_DIM
B_W = B_HEADS * B_HEAD_DIM
G_W = C_GROUPS * C_STATE

COL_QA = 0
COL_KA = COL_QA + A_W
COL_VA = COL_KA + A_W
COL_QB = COL_VA + A_W
COL_KB = COL_QB + B_W
COL_VB = COL_KB + B_W
COL_GB = COL_VB + B_W
COL_FA = COL_GB + B_W
COL_ZC = COL_FA + 2 * LANES
COL_DT = COL_ZC + C_INNER
COL_XBC = COL_DT + C_INNER
PROJ_W = COL_XBC + C_CONV_DIM
assert all(c % A_W == 0 for c in (COL_QA, COL_KA, COL_VA, COL_QB, COL_KB, COL_VB, COL_GB))
assert COL_ZC % C_INNER == 0 and COL_DT % C_INNER == 0 and COL_XBC % C_CONV_DIM == 0 and COL_FA % LANES == 0

NEG_BIG = -1e30
VMEM_LIMIT_BYTES = 52 * 1024 * 1024


def _cparams(*sem):
    return pltpu.CompilerParams(dimension_semantics=sem, vmem_limit_bytes=VMEM_LIMIT_BYTES)


def _tile(n, pref):
    t = min(n, pref)
    while n % t:
        t -= 8
    assert t > 0
    return t


def _rms(x, g):
    return x * lax.rsqrt(jnp.mean(x * x, axis=-1, keepdims=True) + EPS) * g


def _split3(x):
    hi = x.astype(BF16)
    r1 = x - hi.astype(F32)
    mid = r1.astype(BF16)
    lo = (r1 - mid.astype(F32)).astype(BF16)
    return hi, mid, lo


def _dot_sel(sel_bf16, x):
    hi, mid, lo = _split3(x)
    d = lambda p: jnp.dot(sel_bf16, p, preferred_element_type=F32)
    return d(hi) + d(mid) + d(lo)


def _dot_sel_rhs(x, sel_bf16):
    hi, mid, lo = _split3(x)
    d = lambda p: jnp.dot(p, sel_bf16, preferred_element_type=F32)
    return d(hi) + d(mid) + d(lo)


def _inproj_kernel(x_ref, g_ref, w_ref, bf_ref, o_ref, lf_ref, kb_ref, vb_ref, kf_ref, vf_ref):
    xn = _rms(x_ref[...], g_ref[...])
    y = jnp.dot(xn.astype(BF16), w_ref[...], preferred_element_type=F32)
    o_ref[...] = y
    kf_ref[...] = y[:, COL_KA:COL_KA + A_W]
    vf_ref[...] = y[:, COL_VA:COL_VA + A_W]
    kb_ref[...] = y[:, COL_KA:COL_KA + A_W].astype(BF16)
    vb_ref[...] = y[:, COL_VA:COL_VA + A_W].astype(BF16)
    fa = y[:, COL_FA:COL_FA + LANES] + bf_ref[...]
    lane = lax.broadcasted_iota(jnp.int32, fa.shape, 1)
    lf_ref[...] = jnp.where(lane < A_HEADS, jax.nn.log_sigmoid(fa), 0.0)


def _inproj(x2d, g, w, bf_pad):
    T = x2d.shape[0]
    tm = _tile(T, 256)
    return pl.pallas_call(
        _inproj_kernel,
        grid=(T // tm,),
        in_specs=[pl.BlockSpec((tm, D_MODEL), lambda i: (i, 0)),
                  pl.BlockSpec((1, D_MODEL), lambda i: (0, 0)),
                  pl.BlockSpec((D_MODEL, PROJ_W), lambda i: (0, 0)),
                  pl.BlockSpec((1, LANES), lambda i: (0, 0))],
        out_specs=[pl.BlockSpec((tm, PROJ_W), lambda i: (i, 0)),
                   pl.BlockSpec((tm, LANES), lambda i: (i, 0)),
                   pl.BlockSpec((tm, A_W), lambda i: (i, 0)),
                   pl.BlockSpec((tm, A_W), lambda i: (i, 0)),
                   pl.BlockSpec((tm, A_W), lambda i: (i, 0)),
                   pl.BlockSpec((tm, A_W), lambda i: (i, 0))],
        out_shape=[jax.ShapeDtypeStruct((T, PROJ_W), F32),
                   jax.ShapeDtypeStruct((T, LANES), F32),
                   jax.ShapeDtypeStruct((T, A_W), BF16),
                   jax.ShapeDtypeStruct((T, A_W), BF16),
                   jax.ShapeDtypeStruct((T, A_W), F32),
                   jax.ShapeDtypeStruct((T, A_W), F32)],
        compiler_params=_cparams("parallel"),
        name="inproj",
    )(x2d, g.reshape(1, D_MODEL), w, bf_pad)


SRC_QA = 0
SRC_KA = SRC_QA + A_W
SRC_VA = SRC_KA + A_W
SRC_FA = SRC_VA + A_W
SRC_QB = SRC_FA + A_HEADS
SRC_KB = SRC_QB + B_W
SRC_VB = SRC_KB + B_W
SRC_GB = SRC_VB + B_W
SRC_ZC = SRC_GB + B_W
SRC_XBC = SRC_ZC + C_INNER
SRC_DT = SRC_XBC + C_CONV_DIM
IN_WIDTH = SRC_DT + C_HEADS


def _reorder_w_in_kernel(x_ref, o_ref):
    rows = x_ref.shape[0]

    def copy(dst, src, width):
        o_ref[:, dst:dst + width] = x_ref[:, src:src + width].astype(BF16)

    copy(COL_QA, SRC_QA, 3 * A_W)
    copy(COL_VB, SRC_VB, 2 * B_W)
    copy(COL_ZC, SRC_ZC, C_INNER)
    copy(COL_XBC, SRC_XBC, C_CONV_DIM)
    half = B_HEAD_DIM // 2
    for dst, src in ((COL_QB, SRC_QB), (COL_KB, SRC_KB)):
        for k in range(2):
            for h in range(B_HEADS):
                copy(dst + k * LANES + h * half, src + h * B_HEAD_DIM + k * half, half)
    o_ref[:, COL_FA:COL_FA + 2 * LANES] = jnp.zeros((rows, 2 * LANES), BF16)
    copy(COL_FA, SRC_FA, A_HEADS)
    for h in range(C_HEADS):
        col = x_ref[:, SRC_DT + h:SRC_DT + h + 1].astype(BF16)
        o_ref[:, COL_DT + h * C_HEAD_DIM:COL_DT + (h + 1) * C_HEAD_DIM] = jnp.broadcast_to(col, (rows, C_HEAD_DIM))


def _prep_w_in(w_in_all, layer, b_forget, dt_bias):
    assert w_in_all.shape[1:] == (D_MODEL, IN_WIDTH)
    tr = 256
    w = pl.pallas_call(
        _reorder_w_in_kernel,
        grid=(D_MODEL // tr,),
        in_specs=[pl.BlockSpec((None, tr, IN_WIDTH), lambda i: (layer, i, 0))],
        out_specs=pl.BlockSpec((tr, PROJ_W), lambda i: (i, 0)),
        out_shape=jax.ShapeDtypeStruct((D_MODEL, PROJ_W), BF16),
        compiler_params=_cparams("parallel"),
        name="reorder_w_in",
    )(w_in_all)
    bf_pad = jnp.concatenate([b_forget, jnp.zeros((LANES - A_HEADS,), F32)]).reshape(1, LANES)
    dtb_full = jnp.repeat(dt_bias, C_HEAD_DIM).reshape(1, C_INNER)
    return w, bf_pad, dtb_full


FOX_TQ = 512
FOX_TK = 512
FOX_TQ_SUB = 512
FOX_XTRA = 6


def _fox_xtra_base(h):
    return A_HEAD_DIM * ((h + 1) % A_HEADS)


def _cumsum_kernel(lf_ref, eq_ref, ek_ref, carry_scr):
    @pl.when(pl.program_id(1) == 0)
    def _():
        carry_scr[...] = jnp.zeros_like(carry_scr)

    lf = lf_ref[0]
    tc = lf.shape[0]
    r = lax.broadcasted_iota(jnp.int32, (tc, tc), 0)
    c = lax.broadcasted_iota(jnp.int32, (tc, tc), 1)
    tri = jnp.where(r >= c, 1.0, 0.0).astype(BF16)
    cum = _dot_sel(tri, lf) + carry_scr[0:1, :]
    carry_scr[...] = jnp.broadcast_to(cum[tc - 1:tc, :], carry_scr.shape)

    pieces = _split3(cum)
    src = lax.broadcasted_iota(jnp.int32, (LANES, A_W), 0)
    dst = lax.broadcasted_iota(jnp.int32, (LANES, A_W), 1)
    lane = lax.broadcasted_iota(jnp.int32, (1, A_W), 1)
    one_q = jnp.zeros((1, A_W), F32)
    one_k = jnp.zeros((1, A_W), F32)
    for h in range(A_HEADS):
        base = _fox_xtra_base(h)
        one_q = jnp.where((lane >= base + 3) & (lane < base + FOX_XTRA), 1.0, one_q)
        one_k = jnp.where((lane >= base) & (lane < base + 3), 1.0, one_k)
    eq, ek = one_q, one_k
    for k, piece in enumerate(pieces):
        hit_q = functools.reduce(jnp.logical_or, [(src == h) & (dst == _fox_xtra_base(h) + k) for h in range(A_HEADS)])
        hit_k = functools.reduce(jnp.logical_or,
                                 [(src == h) & (dst == _fox_xtra_base(h) + 3 + k) for h in range(A_HEADS)])
        eq = eq + jnp.dot(piece, jnp.where(hit_q, 1.0, 0.0).astype(BF16), preferred_element_type=F32)
        ek = ek + jnp.dot(piece, jnp.where(hit_k, -1.0, 0.0).astype(BF16), preferred_element_type=F32)
    eq_ref[0] = eq.astype(BF16)
    ek_ref[0] = ek.astype(BF16)


def _cumsum(lf):
    B, L, _ = lf.shape
    tc = _tile(L, 512)
    return pl.pallas_call(
        _cumsum_kernel,
        grid=(B, L // tc),
        in_specs=[pl.BlockSpec((1, tc, LANES), lambda b, j: (b, j, 0))],
        out_specs=[pl.BlockSpec((1, tc, A_W), lambda b, j: (b, j, 0)),
                   pl.BlockSpec((1, tc, A_W), lambda b, j: (b, j, 0))],
        out_shape=[jax.ShapeDtypeStruct((B, L, A_W), BF16),
                   jax.ShapeDtypeStruct((B, L, A_W), BF16)],
        scratch_shapes=[pltpu.VMEM((8, LANES), F32)],
        compiler_params=_cparams("parallel", "arbitrary"),
        name="fox_cumsum",
    )(lf)


def _fox_kernel(q_ref, eq_ref, kb_ref, ek_ref, vb_ref, o_ref, qp_scr, m_scr, l_scr, acc_scr, *,
                tq, tk, offset, nk):
    i = pl.program_id(1)
    scale = A_HEAD_DIM ** -0.5
    q_lo = offset + i * tq

    def in_head(lane, h):
        return (lane >= h * A_HEAD_DIM) & (lane < (h + 1) * A_HEAD_DIM)

    def in_xtra(lane, h):
        return (lane >= _fox_xtra_base(h)) & (lane < _fox_xtra_base(h) + FOX_XTRA)

    lane_q = lax.broadcasted_iota(jnp.int32, (tq, A_W), 1)
    q = (q_ref[0] * scale).astype(BF16)
    eq = eq_ref[0]
    for h in range(A_HEADS):
        qp_scr[h] = jnp.where(in_xtra(lane_q, h), eq, jnp.where(in_head(lane_q, h), q, jnp.zeros((), BF16)))
    m_scr[...] = jnp.full(m_scr.shape, NEG_BIG, F32)
    l_scr[...] = jnp.zeros(l_scr.shape, F32)
    acc_scr[...] = jnp.zeros(acc_scr.shape, F32)

    lane_k = lax.broadcasted_iota(jnp.int32, (tk, A_W), 1)

    def block(jj, masked):
        ks = pl.multiple_of(jj * tk, tk)
        k = kb_ref[0, pl.ds(ks, tk), :]
        ek = ek_ref[0, pl.ds(ks, tk), :]
        v = vb_ref[0, pl.ds(ks, tk), :]
        if masked:
            kpos = ks + lax.broadcasted_iota(jnp.int32, (tk, tq), 0)
            qpos = q_lo + lax.broadcasted_iota(jnp.int32, (tk, tq), 1)
            vis = kpos <= qpos
        tqs = min(tq, FOX_TQ_SUB)
        ps, alphas = [], []
        for h in range(A_HEADS):
            kp = jnp.where(in_xtra(lane_k, h), ek, k)
            for qs in range(tq // tqs):
                cols = slice(qs * tqs, (qs + 1) * tqs)
                s = lax.dot_general(kp, qp_scr[h, cols, :], (((1,), (1,)), ((), ())),
                                    preferred_element_type=F32)
                if masked:
                    s = jnp.where(vis[:, cols], s, NEG_BIG)
                m_prev = m_scr[h, 0:1, cols]
                m_next = jnp.maximum(m_prev, jnp.max(s, axis=0, keepdims=True))
                alpha = jnp.exp(m_prev - m_next)
                p = jnp.exp(s - m_next)
                l_scr[h, :, cols] = jnp.broadcast_to(
                    alpha * l_scr[h, 0:1, cols] + jnp.sum(p, axis=0, keepdims=True), (8, tqs))
                m_scr[h, :, cols] = jnp.broadcast_to(m_next, (8, tqs))
                ps.append(p.astype(BF16))
                alphas.append(alpha)
        alphas = [jnp.concatenate(alphas[h * (tq // tqs):(h + 1) * (tq // tqs)], axis=1) for h in range(A_HEADS)]
        pv = lax.dot_general(v, jnp.concatenate(ps, axis=1), (((0,), (0,)), ((), ())),
                             preferred_element_type=F32)
        for h in range(A_HEADS):
            acc_scr[h] = acc_scr[h] * alphas[h] + pv[:, h * tq:(h + 1) * tq]

    n_full = (q_lo + 1) // tk
    n_vis = jnp.minimum((q_lo + tq - 1) // tk + 1, nk)

    def full_body(jj, carry):
        block(jj, False)
        return carry

    def edge_body(jj, carry):
        block(jj, True)
        return carry

    lax.fori_loop(0, n_full, full_body, 0)
    lax.fori_loop(n_full, n_vis, edge_body, 0)

    row = lax.broadcasted_iota(jnp.int32, (A_W, tq), 0)
    out_t = jnp.zeros((A_W, tq), F32)
    for h in range(A_HEADS):
        out_t = out_t + jnp.where(in_head(row, h), acc_scr[h] / l_scr[h, 0:1, :], 0.0)
    o_ref[0] = out_t.T.astype(o_ref.dtype)


def _fox(q_src, q_col, eq, kb, ek, vb, offset):
    B, Lq, _ = q_src.shape
    Lk = kb.shape[1]
    tq = _tile(Lq, FOX_TQ)
    tk = _tile(Lk, FOX_TK)
    assert tq % LANES == 0 and tk % LANES == 0
    nq, nk = Lq // tq, Lk // tk
    whole = lambda b, i: (b, 0, 0)
    kern = functools.partial(_fox_kernel, tq=tq, tk=tk, offset=offset, nk=nk)
    return pl.pallas_call(
        kern,
        grid=(B, nq),
        in_specs=[pl.BlockSpec((1, tq, A_W), lambda b, i: (b, i, q_col // A_W)),
                  pl.BlockSpec((1, tq, A_W), lambda b, i: (b, i, 0)),
                  pl.BlockSpec((1, Lk, A_W), whole),
                  pl.BlockSpec((1, Lk, A_W), whole),
                  pl.BlockSpec((1, Lk, A_W), whole)],
        out_specs=pl.BlockSpec((1, tq, A_W), lambda b, i: (b, i, 0)),
        out_shape=jax.ShapeDtypeStruct((B, Lq, A_W), BF16),
        scratch_shapes=[pltpu.VMEM((A_HEADS, tq, A_W), BF16),
                        pltpu.VMEM((A_HEADS, 8, tq), F32),
                        pltpu.VMEM((A_HEADS, 8, tq), F32),
                        pltpu.VMEM((A_HEADS, A_W, tq), F32)],
        compiler_params=_cparams("parallel", "arbitrary"),
        name="fox_attention",
    )(q_src, eq, kb, ek, vb)


def _ret_lane_head_v():
    return lax.broadcasted_iota(jnp.int32, (1, B_W), 1) // B_HEAD_DIM


def _ret_lane_head_qk():
    return (lax.broadcasted_iota(jnp.int32, (1, B_W), 1) % LANES) // (B_HEAD_DIM // 2)


def _log_gamma(h):
    return math.log1p(-2.0 ** (-5.0 - h))


def _ret_kernel(q_ref, k_ref, v_ref, g_ref, s0_ref, invf_ref, gret_ref, y_ref, sout_ref,
                state_scr, dec_scr, *, lc, offset, nchunks):
    c = pl.program_id(1)

    @pl.when(c == 0)
    def _():
        state_scr[...] = s0_ref[0]
        r = lax.broadcasted_iota(jnp.int32, (lc, lc), 0)
        s = lax.broadcasted_iota(jnp.int32, (lc, lc), 1)
        d = (r - s).astype(F32)
        for h in range(B_HEADS):
            dec_scr[h] = jnp.where(r >= s, jnp.exp(d * _log_gamma(h)), 0.0)

    hv = _ret_lane_head_v()
    hqk = _ret_lane_head_qk()
    lg_v = jnp.zeros((1, B_W), F32)
    lg_qk = jnp.zeros((1, B_W), F32)
    for h in range(B_HEADS):
        lg_v = jnp.where(hv == h, _log_gamma(h), lg_v)
        lg_qk = jnp.where(hqk == h, _log_gamma(h), lg_qk)

    pos = (offset + c * lc + lax.broadcasted_iota(jnp.int32, (lc, LANES), 0)).astype(F32)
    ang = pos * invf_ref[...]
    cos, sin = jnp.cos(ang), jnp.sin(ang)

    def rot(x):
        x1, x2 = x[:, :LANES], x[:, LANES:]
        return jnp.concatenate([x1 * cos - x2 * sin, x1 * sin + x2 * cos], axis=1)

    qr = rot(q_ref[0])
    kr = rot(k_ref[0]) * (B_HEAD_DIM ** -0.5)
    v = v_ref[0]
    vb = v.astype(BF16)
    krb = kr.astype(BF16)

    lpos = lax.broadcasted_iota(jnp.int32, (lc, B_W), 0).astype(F32)
    state = state_scr[...]
    y = jnp.dot(qr.astype(BF16), state.astype(BF16), preferred_element_type=F32) * jnp.exp((lpos + 1.0) * lg_v)
    for h in range(B_HEADS):
        qh = jnp.where(hqk == h, qr, 0.0).astype(BF16)
        sc = lax.dot_general(qh, krb, (((1,), (1,)), ((), ())), preferred_element_type=F32) * dec_scr[h]
        yh = jnp.dot(sc.astype(BF16), vb, preferred_element_type=F32)
        y = y + jnp.where(hv == h, yh, 0.0)

    kw = (kr * jnp.exp((lc - 1.0 - lpos) * lg_qk)).astype(BF16)
    upd = lax.dot_general(kw, vb, (((0,), (0,)), ((), ())), preferred_element_type=F32)
    row_head = (lax.broadcasted_iota(jnp.int32, (B_W, B_W), 0) % LANES) // (B_HEAD_DIM // 2)
    col_head = lax.broadcasted_iota(jnp.int32, (B_W, B_W), 1) // B_HEAD_DIM
    lg_rows = jnp.zeros((B_W, B_W), F32)
    for h in range(B_HEADS):
        lg_rows = jnp.where(row_head == h, _log_gamma(h), lg_rows)
    new_state = jnp.where(row_head == col_head, state * jnp.exp(lc * lg_rows) + upd, 0.0)
    state_scr[...] = new_state

    @pl.when(c == nchunks - 1)
    def _():
        sout_ref[0] = new_state

    ri = lax.broadcasted_iota(jnp.int32, (B_W, B_W), 0) // B_HEAD_DIM
    ci = lax.broadcasted_iota(jnp.int32, (B_W, B_W), 1) // B_HEAD_DIM
    avg = jnp.where(ri == ci, 1.0 / B_HEAD_DIM, 0.0).astype(BF16)
    yc = y - _dot_sel_rhs(y, avg)
    var = _dot_sel_rhs(yc * yc, avg)
    yn = yc * lax.rsqrt(var + EPS) * gret_ref[...]
    y_ref[0] = (jax.nn.silu(g_ref[0]) * yn).astype(y_ref.dtype)


def _retention(proj3, s0, invf, g_ret, offset):
    B, L, _ = proj3.shape
    lc = _tile(L, 256)
    nch = L // lc
    blk = lambda col: pl.BlockSpec((1, lc, B_W), lambda b, c: (b, c, col // B_W))
    kern = functools.partial(_ret_kernel, lc=lc, offset=offset, nchunks=nch)
    return pl.pallas_call(
        kern,
        grid=(B, nch),
        in_specs=[blk(COL_QB), blk(COL_KB), blk(COL_VB), blk(COL_GB),
                  pl.BlockSpec((1, B_W, B_W), lambda b, c: (b, 0, 0)),
                  pl.BlockSpec((1, LANES), lambda b, c: (0, 0)),
                  pl.BlockSpec((1, B_W), lambda b, c: (0, 0))],
        out_specs=[pl.BlockSpec((1, lc, B_W), lambda b, c: (b, c, 0)),
                   pl.BlockSpec((1, B_W, B_W), lambda b, c: (b, 0, 0))],
        out_shape=[jax.ShapeDtypeStruct((B, L, B_W), BF16),
                   jax.ShapeDtypeStruct((B, B_W, B_W), F32)],
        scratch_shapes=[pltpu.VMEM((B_W, B_W), F32),
                        pltpu.VMEM((B_HEADS, lc, lc), F32)],
        compiler_params=_cparams("parallel", "arbitrary"),
        name="retention",
    )(proj3, proj3, proj3, proj3, s0, invf, g_ret.reshape(1, B_W))


def _ret_state_to_kernel(s):
    B = s.shape[0]
    half = B_HEAD_DIM // 2
    s = s.reshape(B, B_HEADS, 2, half, B_HEAD_DIM)
    eye = jnp.eye(B_HEADS, dtype=s.dtype)
    full = jnp.einsum('bhkie,hg->bkhige', s, eye)
    return full.reshape(B, B_W, B_W)


def _ret_state_from_kernel(s):
    B = s.shape[0]
    half = B_HEAD_DIM // 2
    s = s.reshape(B, 2, B_HEADS, half, B_HEADS, B_HEAD_DIM)
    d = jnp.einsum('bkhihe->bhkie', s)
    return d.reshape(B, B_HEADS, B_HEAD_DIM, B_HEAD_DIM)


def _ssd_kernel(z_ref, xbc_ref, dt_ref, s0_ref, cs0_ref, cw_ref, cb_ref, dtb_ref, alog_ref, dsk_ref, gs_ref,
                y_ref, sout_ref, state_scr, xp_scr, *, lc, nchunks):
    c = pl.program_id(1)
    PADR = 8

    @pl.when(c == 0)
    def _():
        state_scr[...] = s0_ref[0]
        xp_scr[0:PADR, :] = cs0_ref[0]

    xbc = xbc_ref[0]
    xp_scr[PADR:PADR + lc, :] = xbc
    cw = cw_ref[...]
    conv = xp_scr[PADR - 3:PADR - 3 + lc, :] * cw[0:1, :]
    for jj in range(1, C_CONV):
        conv = conv + xp_scr[PADR - 3 + jj:PADR - 3 + jj + lc, :] * cw[jj:jj + 1, :]
    act = jax.nn.silu(conv + cb_ref[...])
    xp_scr[0:PADR, :] = xp_scr[lc:lc + PADR, :]

    xs = act[:, :C_INNER]
    bm = act[:, C_INNER:C_INNER + G_W]
    cm = act[:, C_INNER + G_W:]
    dt = jax.nn.softplus(dt_ref[0] + dtb_ref[...])
    loga = dt * (-jnp.exp(alog_ref[...]))

    r = lax.broadcasted_iota(jnp.int32, (lc, lc), 0)
    s = lax.broadcasted_iota(jnp.int32, (lc, lc), 1)
    causal = r >= s
    tri = jnp.where(causal, 1.0, 0.0).astype(BF16)
    cum = _dot_sel(tri, loga)
    total = cum[lc - 1:lc, :]
    xdt = xs * dt
    xdtb = xdt.astype(BF16)
    bmb = bm.astype(BF16)
    cmb = cm.astype(BF16)
    state = state_scr[...]

    gw = C_INNER // C_GROUPS
    cum_t = [cum[:, kk * LANES:(kk + 1) * LANES].T for kk in range(C_INNER // LANES)]
    ys = []
    for g in range(C_GROUPS):
        cg = cmb[:, g * C_STATE:(g + 1) * C_STATE]
        bg = bmb[:, g * C_STATE:(g + 1) * C_STATE]
        scores = lax.dot_general(cg, bg, (((1,), (1,)), ((), ())), preferred_element_type=F32)
        y_g = jnp.dot(cg, state[:, g * gw:(g + 1) * gw].astype(BF16), preferred_element_type=F32)
        y_g = y_g * jnp.exp(cum[:, g * gw:(g + 1) * gw])
        lane_head = lax.broadcasted_iota(jnp.int32, (1, gw), 1) // C_HEAD_DIM
        xg = xdtb[:, g * gw:(g + 1) * gw]
        for hh in range(C_HEADS // C_GROUPS):
            col = g * gw + hh * C_HEAD_DIM
            cum_col = cum[:, col:col + 1]
            cum_row = cum_t[col // LANES][col % LANES:col % LANES + 1, :]
            dec = jnp.exp(jnp.where(causal, cum_col - cum_row, NEG_BIG))
            yh = jnp.dot((scores * dec).astype(BF16), xg, preferred_element_type=F32)
            y_g = y_g + jnp.where(lane_head == hh, yh, 0.0)
        ys.append(y_g)
        xw = (xdt[:, g * gw:(g + 1) * gw] * jnp.exp(total[:, g * gw:(g + 1) * gw] - cum[:, g * gw:(g + 1) * gw]))
        upd = lax.dot_general(bg, xw.astype(BF16), (((0,), (0,)), ((), ())), preferred_element_type=F32)
        state_scr[:, g * gw:(g + 1) * gw] = state[:, g * gw:(g + 1) * gw] * jnp.exp(total[:, g * gw:(g + 1) * gw]) + upd
    y = jnp.concatenate(ys, axis=1)

    @pl.when(c == nchunks - 1)
    def _():
        sout_ref[0] = state_scr[...]

    yc = (y + xs * dsk_ref[...]) * jax.nn.silu(z_ref[0])
    y_ref[0] = _rms(yc, gs_ref[...]).astype(y_ref.dtype)


def _ssd(proj3, s0, cs0, conv_w, conv_b, dtb_full, a_log, d_skip, g_ssm):
    B, L, _ = proj3.shape
    lc = _tile(L, 256)
    nch = L // lc
    rep = lambda p: jnp.repeat(p, C_HEAD_DIM).reshape(1, C_INNER)
    kern = functools.partial(_ssd_kernel, lc=lc, nchunks=nch)
    cst = lambda shape: pl.BlockSpec(shape, lambda b, c: (0,) * len(shape))
    return pl.pallas_call(
        kern,
        grid=(B, nch),
        in_specs=[pl.BlockSpec((1, lc, C_INNER), lambda b, c: (b, c, COL_ZC // C_INNER)),
                  pl.BlockSpec((1, lc, C_CONV_DIM), lambda b, c: (b, c, COL_XBC // C_CONV_DIM)),
                  pl.BlockSpec((1, lc, C_INNER), lambda b, c: (b, c, COL_DT // C_INNER)),
                  pl.BlockSpec((1, C_STATE, C_INNER), lambda b, c: (b, 0, 0)),
                  pl.BlockSpec((1, 8, C_CONV_DIM), lambda b, c: (b, 0, 0)),
                  cst((C_CONV, C_CONV_DIM)), cst((1, C_CONV_DIM)), cst((1, C_INNER)), cst((1, C_INNER)),
                  cst((1, C_INNER)), cst((1, C_INNER))],
        out_specs=[pl.BlockSpec((1, lc, C_INNER), lambda b, c: (b, c, 0)),
                   pl.BlockSpec((1, C_STATE, C_INNER), lambda b, c: (b, 0, 0))],
        out_shape=[jax.ShapeDtypeStruct((B, L, C_INNER), BF16),
                   jax.ShapeDtypeStruct((B, C_STATE, C_INNER), F32)],
        scratch_shapes=[pltpu.VMEM((C_STATE, C_INNER), F32),
                        pltpu.VMEM((lc + 8, C_CONV_DIM), F32)],
        compiler_params=_cparams("parallel", "arbitrary"),
        name="ssd",
    )(proj3, proj3, proj3, s0, cs0, conv_w, conv_b.reshape(1, C_CONV_DIM), dtb_full, rep(a_log), rep(d_skip),
      g_ssm.reshape(1, C_INNER))


def _outproj_kernel(h_ref, ya_ref, yb_ref, yc_ref, w_ref, o_ref):
    acc = jnp.dot(ya_ref[...], w_ref[0:A_W, :], preferred_element_type=F32)
    acc = acc + jnp.dot(yb_ref[...], w_ref[A_W:A_W + B_W, :], preferred_element_type=F32)
    acc = acc + jnp.dot(yc_ref[...], w_ref[A_W + B_W:, :], preferred_element_type=F32)
    o_ref[...] = h_ref[...] + acc


def _outproj(h2d, ya, yb, yc, w_out_bf16):
    T = h2d.shape[0]
    tm = _tile(T, 512)
    row = lambda w: pl.BlockSpec((tm, w), lambda i: (i, 0))
    return pl.pallas_call(
        _outproj_kernel,
        grid=(T // tm,),
        in_specs=[row(D_MODEL), row(A_W), row(B_W), row(C_INNER),
                  pl.BlockSpec((D_MODEL, D_MODEL), lambda i: (0, 0))],
        out_specs=row(D_MODEL),
        out_shape=jax.ShapeDtypeStruct((T, D_MODEL), F32),
        compiler_params=_cparams("parallel"),
        name="outproj",
    )(h2d, ya, yb, yc, w_out_bf16)


PEER_SUB = 128
PEER_STEP = 2048
PEER_CAND_COLS = 4


def _gelu_tanh(x):
    k0 = math.sqrt(2.0 / math.pi)
    inner = x * (k0 + (k0 * 0.044715) * (x * x))
    return (0.5 * x) * (1.0 + jnp.tanh(inner))


def _top16(x, want_rank):
    vals = []
    rank = jnp.full(x.shape, float(PEER_TOPK), F32) if want_rank else None
    for r in range(PEER_TOPK):
        m = jnp.max(x, axis=0, keepdims=True)
        vals.append(m)
        hit = x == m
        if want_rank:
            rank = jnp.where(hit, float(r), rank)
        x = jnp.where(hit, NEG_BIG, x)
    return jnp.concatenate(vals, axis=0), rank


def _sort_pairs(n):
    def merge(lo, hi, r):
        step = r * 2
        if step < hi - lo:
            yield from merge(lo, hi, step)
            yield from merge(lo + r, hi, step)
            yield from [(i, i + r) for i in range(lo + r, hi - r, step)]
        else:
            yield (lo, lo + r)

    def sort(lo, hi):
        if hi - lo >= 1:
            mid = lo + (hi - lo) // 2
            yield from sort(lo, mid)
            yield from sort(mid + 1, hi)
            yield from merge(lo, hi, 1)

    return tuple(sort(0, n - 1))


def _top16_values(x):
    v = [x[8 * g:8 * g + 8, :] for g in range(PEER_NKEYS // 8)]
    for a, b in _sort_pairs(PEER_NKEYS // 8):
        v[a], v[b] = jnp.maximum(v[a], v[b]), jnp.minimum(v[a], v[b])
    vals = []
    for r in range(PEER_TOPK):
        m = jnp.max(v[0], axis=0, keepdims=True)
        vals.append(m)
        hit = v[0] == m
        for k in range(PEER_TOPK - 1 - r):
            v[k] = jnp.where(hit, v[k + 1], v[k])
    return jnp.concatenate(vals, axis=0)


def _peer_select(s1, s2):
    n = s1.shape[1]
    a16 = _top16_values(s1)
    b16, rank2 = _top16(s2, True)
    ridx = lax.broadcasted_iota(jnp.int32, (PEER_TOPK, n), 0)
    sub = lax.broadcasted_iota(jnp.int32, (8, n), 0)
    lo, hi = slice(0, 8), slice(8, PEER_TOPK)
    col = lambda r2, rows: a16[rows, :] + b16[r2:r2 + 1, :]
    row = lambda r1, cols: b16[cols, :] + a16[r1:r1 + 1, :]
    assert PEER_CAND_COLS == 4 and PEER_TOPK == 16
    groups = [col(0, lo), col(0, hi), col(1, lo),
              jnp.where(sub <= 4, col(2, lo), NEG_BIG),
              jnp.where(sub <= 3, col(3, lo), row(1, lo)),
              jnp.where(sub >= 4, row(0, lo), NEG_BIG),
              row(0, hi),
              jnp.where(sub == 4, row(2, lo), NEG_BIG)]
    cand = jnp.concatenate(groups, axis=0)
    v = list(groups)
    for a_i, b_i in _sort_pairs(8):
        v[a_i], v[b_i] = jnp.maximum(v[a_i], v[b_i]), jnp.minimum(v[a_i], v[b_i])
    tau = None
    for r in range(PEER_TOPK):
        tau = jnp.max(v[0], axis=0, keepdims=True)
        hit = v[0] == tau
        depth = min(8, PEER_TOPK - r)
        for k in range(depth - 1):
            v[k] = jnp.where(hit, v[k + 1], v[k])
        if depth == 8:
            v[7] = jnp.where(hit, NEG_BIG, v[7])
    top = a16[0:1, :] + b16[0:1, :]
    z = jnp.sum(jnp.where(cand >= tau, jnp.exp(cand - top), 0.0), axis=0, keepdims=True)
    cnt = jnp.zeros(s1.shape, F32)
    for r2 in range(PEER_CAND_COLS):
        cnt = cnt + jnp.where(s1 + b16[r2:r2 + 1, :] >= tau, 1.0, 0.0)
    for r1 in range(PEER_TOPK // (PEER_CAND_COLS + 1)):
        a_r = a16[r1:r1 + 1, :]
        tail = jnp.where((ridx >= PEER_CAND_COLS) & (b16 + a_r >= tau), 1.0, 0.0)
        cnt = cnt + jnp.where(s1 == a_r, jnp.sum(tail, axis=0, keepdims=True), 0.0)
    a = jnp.exp(s1 - a16[0:1, :]) / z
    b = jnp.exp(s2 - b16[0:1, :])
    return a, cnt, rank2, b


def _peer_kernel(h_ref, g_ref, wq_ref, k1_ref, k2_ref, u_ref, vt_ref, o_ref,
                 xt_scr, a_scr, cnt_scr, rank_scr, b_scr, acc_scr, *, tm, te, ne):
    j = pl.program_id(1)
    n1 = te // PEER_NKEYS
    nchunk = tm // LANES
    nsub = te // PEER_SUB
    slabs = PEER_SUB // PEER_NKEYS

    def gate_weights(sb):
        ws = []
        for il in range(slabs):
            i1 = j * n1 + sb * slabs + il
            w = jnp.zeros((PEER_NKEYS, tm), BF16)
            for h in range(PEER_HEADS):
                row = lambda ref: jnp.concatenate(
                    [jnp.broadcast_to(ref[h, cch, pl.ds(i1, 1), :], (BF16_ROWS, LANES)) for cch in range(nchunk)],
                    axis=1).astype(BF16)
                cnt_b = jnp.tile(row(cnt_scr), (PEER_NKEYS // BF16_ROWS, 1))
                a_b = jnp.tile(row(a_scr), (PEER_NKEYS // BF16_ROWS, 1))
                w = w + jnp.where(rank_scr[h] < cnt_b, b_scr[h], jnp.zeros((), BF16)) * a_b
            ws.append(w)
        return jnp.concatenate(ws, axis=0)

    @pl.when(j == 0)
    def _():
        xn = _rms(h_ref[...], g_ref[...])
        xt_scr[...] = xn.T.astype(BF16)
        acc_scr[...] = jnp.zeros(acc_scr.shape, F32)

        def head_body(h, carry):
            base = pl.multiple_of(h * PEER_KEY_DIM, PEER_KEY_DIM)
            k1 = k1_ref[h].astype(BF16)
            k2 = k2_ref[h].astype(BF16)
            qh = jnp.dot(wq_ref[pl.ds(base, PEER_KEY_DIM), :], xt_scr[...], preferred_element_type=F32)
            for cch in range(nchunk):
                sl = slice(cch * LANES, (cch + 1) * LANES)
                q1 = qh[0:PEER_HALF, sl].astype(BF16)
                q2 = qh[PEER_HALF:, sl].astype(BF16)
                s1 = jnp.dot(k1, q1, preferred_element_type=F32)
                s2 = jnp.dot(k2, q2, preferred_element_type=F32)
                a, cnt, rank2, b = _peer_select(s1, s2)
                a_scr[h, cch] = a
                cnt_scr[h, cch] = cnt
                rank_scr[h, :, sl] = rank2.astype(BF16)
                b_scr[h, :, sl] = b.astype(BF16)
            return carry

        lax.fori_loop(0, PEER_HEADS, head_body, 0)

    acts = []
    for sb in range(nsub):
        ht = jnp.dot(u_ref[sb * PEER_SUB:(sb + 1) * PEER_SUB, :], xt_scr[...], preferred_element_type=F32)
        acts.append(_gelu_tanh(ht.astype(BF16)) * gate_weights(sb))
    acc_scr[...] += jnp.dot(vt_ref[...], jnp.concatenate(acts, axis=0), preferred_element_type=F32)

    @pl.when(j == ne - 1)
    def _():
        o_ref[...] = h_ref[...] + acc_scr[...].T


def _layer_cast_kernel(x_ref, o_ref, *, transpose):
    x = x_ref[...]
    o_ref[...] = (x.T if transpose else x).astype(o_ref.dtype)


def _layer_cast(x_all, layer, dtype, transpose):
    _, R, C = x_all.shape
    tr = _tile(R, 512)
    return pl.pallas_call(
        functools.partial(_layer_cast_kernel, transpose=transpose),
        grid=(R // tr,),
        in_specs=[pl.BlockSpec((None, tr, C), lambda i: (layer, i, 0))],
        out_specs=pl.BlockSpec((C, tr), lambda i: (0, i)) if transpose else pl.BlockSpec((tr, C), lambda i: (i, 0)),
        out_shape=jax.ShapeDtypeStruct((C, R) if transpose else (R, C), dtype),
        compiler_params=_cparams("parallel"),
        name="layer_cast",
    )(x_all)


def _peer(h2d, g, wq_t, k1, k2, u_bf16, vt_bf16):
    T = h2d.shape[0]
    tm = _tile(T, 512)
    assert tm % LANES == 0
    te = PEER_STEP
    ne = PEER_EXPERTS // te
    kern = functools.partial(_peer_kernel, tm=tm, te=te, ne=ne)
    hk = (PEER_HEADS, PEER_NKEYS, tm)
    hkc = (PEER_HEADS, tm // LANES, PEER_NKEYS, LANES)
    return pl.pallas_call(
        kern,
        grid=(T // tm, ne),
        in_specs=[pl.BlockSpec((tm, D_MODEL), lambda i, j: (i, 0)),
                  pl.BlockSpec((1, D_MODEL), lambda i, j: (0, 0)),
                  pl.BlockSpec((PEER_HEADS * PEER_KEY_DIM, D_MODEL), lambda i, j: (0, 0)),
                  pl.BlockSpec((PEER_HEADS, PEER_NKEYS, PEER_HALF), lambda i, j: (0, 0, 0)),
                  pl.BlockSpec((PEER_HEADS, PEER_NKEYS, PEER_HALF), lambda i, j: (0, 0, 0)),
                  pl.BlockSpec((te, D_MODEL), lambda i, j: (j, 0)),
                  pl.BlockSpec((D_MODEL, te), lambda i, j: (0, j))],
        out_specs=pl.BlockSpec((tm, D_MODEL), lambda i, j: (i, 0)),
        out_shape=jax.ShapeDtypeStruct((T, D_MODEL), F32),
        scratch_shapes=[pltpu.VMEM((D_MODEL, tm), BF16),
                        pltpu.VMEM(hkc, F32), pltpu.VMEM(hkc, F32), pltpu.VMEM(hk, BF16), pltpu.VMEM(hk, BF16),
                        pltpu.VMEM((D_MODEL, tm), F32)],
        compiler_params=_cparams("parallel", "arbitrary"),
        name="peer",
    )(h2d, g.reshape(1, D_MODEL), wq_t, k1, k2, u_bf16, vt_bf16)


def _ple_kernel(h_ref, p_ref, g_ref, wg_ref, wp_ref, gf_ref, o_ref, *, final_norm):
    h = h_ref[...]
    xn = _rms(h, g_ref[...])
    gate = jax.nn.sigmoid(jnp.dot(xn.astype(BF16), wg_ref[...], preferred_element_type=F32))
    emb = jnp.dot(p_ref[...].astype(BF16), wp_ref[...], preferred_element_type=F32)
    out = h + emb * gate
    if final_norm:
        out = _rms(out, gf_ref[...])
    o_ref[...] = out


def _ple(h2d, p_all, layer, g, wg_bf16, wp_bf16, g_final, final_norm):
    T = h2d.shape[0]
    tm = _tile(T, 512)
    kern = functools.partial(_ple_kernel, final_norm=final_norm)
    return pl.pallas_call(
        kern,
        grid=(T // tm,),
        in_specs=[pl.BlockSpec((tm, D_MODEL), lambda i: (i, 0)),
                  pl.BlockSpec((None, tm, PLE_DIM), lambda i: (layer, i, 0)),
                  pl.BlockSpec((1, D_MODEL), lambda i: (0, 0)),
                  pl.BlockSpec((D_MODEL, D_MODEL), lambda i: (0, 0)),
                  pl.BlockSpec((PLE_DIM, D_MODEL), lambda i: (0, 0)),
                  pl.BlockSpec((1, D_MODEL), lambda i: (0, 0))],
        out_specs=pl.BlockSpec((tm, D_MODEL), lambda i: (i, 0)),
        out_shape=jax.ShapeDtypeStruct((T, D_MODEL), F32),
        compiler_params=_cparams("parallel"),
        name="ple",
    )(h2d, p_all, g.reshape(1, D_MODEL), wg_bf16, wp_bf16, g_final.reshape(1, D_MODEL))


def _pad_to(x, n, axis):
    pad = n - x.shape[axis]
    if pad == 0:
        return x
    widths = [(0, 0)] * x.ndim
    widths[axis] = (0, pad)
    return jnp.pad(x, widths)


def _prep_layer_weights(weights, i):
    (g_mix, w_in, b_forget, g_ret, conv_w, conv_b, dt_bias, a_log, d_skip, g_ssm, w_out,
     g_ffn, peer_wq, peer_k1, peer_k2, peer_u, peer_v, g_ple, w_ple_gate, w_ple, g_final) = weights
    w_proj, bf_pad, dtb_full = _prep_w_in(w_in, i, b_forget[i], dt_bias[i])
    return dict(w_proj=w_proj, bf_pad=bf_pad, dtb_full=dtb_full, w_out=w_out[i].astype(BF16),
                wq_t=_layer_cast(peer_wq, i, BF16, True), u=_layer_cast(peer_u, i, BF16, False),
                vt=_layer_cast(peer_v, i, BF16, True),
                wg=w_ple_gate[i].astype(BF16), wp=w_ple[i].astype(BF16))


def _run_trunk(x, p, past, weights, prepped):
    (g_mix, w_in, b_forget, g_ret, conv_w, conv_b, dt_bias, a_log, d_skip, g_ssm, w_out,
     g_ffn, peer_wq, peer_k1, peer_k2, peer_u, peer_v, g_ple, w_ple_gate, w_ple, g_final) = weights
    B, L, _ = x.shape
    depth = w_in.shape[0]
    T = B * L
    P = 0 if past is None else past[0].shape[2]
    half = B_HEAD_DIM // 2
    invf = jnp.tile(ROPE_BASE ** (-jnp.arange(half, dtype=F32) / half), LANES // half).reshape(1, LANES)

    h = x.reshape(T, D_MODEL)
    outs = []
    for i in range(depth):
        pw = prepped[i]
        dtb_full = pw["dtb_full"]
        proj, lf, kb, vb, k_new, v_new = _inproj(h, g_mix[i], pw["w_proj"], pw["bf_pad"])
        proj3 = proj.reshape(B, L, PROJ_W)
        lf3 = lf.reshape(B, L, LANES)
        kb = kb.reshape(B, L, A_W)
        vb = vb.reshape(B, L, A_W)

        if past is None:
            eq, ek = _cumsum(lf3)
            ya = _fox(proj3, COL_QA, eq, kb, ek, vb, 0)
            rs0 = jnp.zeros((B, B_W, B_W), F32)
            ss0 = jnp.zeros((B, C_STATE, C_INNER), F32)
            cs0 = jnp.zeros((B, 8, C_CONV_DIM), F32)
        else:
            lk_pad = -(-(P + L) // FOX_TK) * FOX_TK
            lq_pad = -(-L // LANES) * LANES
            past_k = past[0][i].reshape(B, P, A_W).astype(BF16)
            past_v = past[1][i].reshape(B, P, A_W).astype(BF16)
            past_lf = _pad_to(past[2][i], LANES, 2)
            k_all = _pad_to(jnp.concatenate([past_k, kb], axis=1), lk_pad, 1)
            v_all = _pad_to(jnp.concatenate([past_v, vb], axis=1), lk_pad, 1)
            lf_all = _pad_to(jnp.concatenate([past_lf, lf3], axis=1), lk_pad, 1)
            eq, ek = _cumsum(lf_all)
            q_pad = _pad_to(proj3[:, :, COL_QA:COL_QA + A_W], lq_pad, 1)
            ya = _fox(q_pad, 0, _pad_to(eq[:, P:P + L], lq_pad, 1), k_all, ek, v_all, P)[:, :L]
            rs0 = _ret_state_to_kernel(past[3][i])
            ss0 = jnp.transpose(past[4][i], (0, 2, 1, 3)).reshape(B, C_STATE, C_INNER)
            cs0 = jnp.concatenate([jnp.zeros((B, 8 - (C_CONV - 1), C_CONV_DIM), F32), past[5][i]], axis=1)

        yb, ret_k = _retention(proj3, rs0, invf, g_ret[i], P)
        ret_new = _ret_state_from_kernel(ret_k)

        yc, ssm_k = _ssd(proj3, ss0, cs0, conv_w[i], conv_b[i], dtb_full, a_log[i], d_skip[i], g_ssm[i])
        ssm_new = jnp.transpose(ssm_k.reshape(B, C_STATE, C_HEADS, C_HEAD_DIM), (0, 2, 1, 3))
        xbc = proj3[:, :, COL_XBC:COL_XBC + C_CONV_DIM]
        if L >= C_CONV - 1:
            conv_new = xbc[:, L - (C_CONV - 1):]
        else:
            prev = jnp.zeros((B, C_CONV - 1, C_CONV_DIM), F32) if past is None else past[5][i]
            conv_new = jnp.concatenate([prev, xbc], axis=1)[:, -(C_CONV - 1):]

        h = _outproj(h, ya.reshape(T, A_W), yb.reshape(T, B_W), yc.reshape(T, C_INNER), pw["w_out"])
        h = _peer(h, g_ffn[i], pw["wq_t"], peer_k1[i], peer_k2[i], pw["u"], pw["vt"])
        h = _ple(h, p.reshape(depth, T, PLE_DIM), i, g_ple[i], pw["wg"], pw["wp"], g_final,
                 final_norm=(i == depth - 1))
        outs.append((k_new.reshape(B, L, A_HEADS, A_HEAD_DIM), v_new.reshape(B, L, A_HEADS, A_HEAD_DIM),
                     lf3[:, :, :A_HEADS], ret_new, ssm_new, conv_new))
    stacked = tuple(jnp.stack([o[n] for o in outs]) for n in range(6))
    return h.reshape(B, L, D_MODEL), stacked


def kernel(x_prompt, x_sample, cache_k_fox, cache_v_fox, cache_logf_fox, state_ret, state_ssm, state_conv, p_prompt, p_sample, g_mix, w_in, b_forget, g_ret, conv_w, conv_b, dt_bias, a_log, d_skip, g_ssm, w_out, g_ffn, peer_wq, peer_k1, peer_k2, peer_u, peer_v, g_ple, w_ple_gate, w_ple, g_final):
    weights = (g_mix, w_in, b_forget, g_ret, conv_w, conv_b, dt_bias, a_log, d_skip, g_ssm, w_out,
               g_ffn, peer_wq, peer_k1, peer_k2, peer_u, peer_v, g_ple, w_ple_gate, w_ple, g_final)
    prepped = [_prep_layer_weights(weights, i) for i in range(w_in.shape[0])]
    y_prompt, (k_p, v_p, lf_p, ret_p, ssm_p, conv_p) = _run_trunk(x_prompt, p_prompt, None, weights, prepped)
    past = (cache_k_fox, cache_v_fox, cache_logf_fox, state_ret, state_ssm, state_conv)
    y_sample, (k_s, v_s, lf_s, ret_s, ssm_s, conv_s) = _run_trunk(x_sample, p_sample, past, weights, prepped)
    return (y_prompt, y_sample, k_p, v_p, lf_p, ret_p, ssm_p, conv_p, k_s, v_s, lf_s, ret_s, ssm_s, conv_s)
```

```python
import functools
import math

import jax
import jax.numpy as jnp
from jax import lax
from jax.experimental import pallas as pl
from jax.experimental.pallas import tpu as pltpu

F32 = jnp.float32
BF16 = jnp.bfloat16

D_MODEL = 1024
PLE_DIM = 256
EPS = 1e-6
A_HEADS, A_HEAD_DIM = 4, 64
B_HEADS, B_HEAD_DIM = 4, 64
ROPE_BASE = 10000.0
C_HEADS, C_HEAD_DIM = 8, 64
C_INNER = C_HEADS * C_HEAD_DIM
C_GROUPS, C_STATE, C_CONV = 2, 128, 4
C_CONV_DIM = C_INNER + 2 * C_GROUPS * C_STATE
PEER_HEADS, PEER_NKEYS, PEER_KEY_DIM, PEER_TOPK = 8, 128, 256, 16
PEER_HALF = PEER_KEY_DIM // 2
PEER_EXPERTS = PEER_NKEYS * PEER_NKEYS

LANES = 128
BF16_ROWS = 16
A_W = A_HEADS * A_HEAD_DIM
B_W = B_HEADS * B_HEAD_DIM
G_W = C_GROUPS * C_STATE

COL_QA = 0
COL_KA = COL_QA + A_W
COL_VA = COL_KA + A_W
COL_QB = COL_VA + A_W
COL_KB = COL_QB + B_W
COL_VB = COL_KB + B_W
COL_GB = COL_VB + B_W
COL_FA = COL_GB + B_W
COL_ZC = COL_FA + 2 * LANES
COL_DT = COL_ZC + C_INNER
COL_XBC = COL_DT + C_INNER
PROJ_W = COL_XBC + C_CONV_DIM
assert all(c % A_W == 0 for c in (COL_QA, COL_KA, COL_VA, COL_QB, COL_KB, COL_VB, COL_GB))
assert COL_ZC % C_INNER == 0 and COL_DT % C_INNER == 0 and COL_XBC % C_CONV_DIM == 0 and COL_FA % LANES == 0

NEG_BIG = -1e30
VMEM_LIMIT_BYTES = 52 * 1024 * 1024


def _cparams(*sem):
    return pltpu.CompilerParams(dimension_semantics=sem, vmem_limit_bytes=VMEM_LIMIT_BYTES)


def _tile(n, pref):
    t = min(n, pref)
    while n % t:
        t -= 8
    assert t > 0
    return t


def _rms(x, g):
    return x * lax.rsqrt(jnp.mean(x * x, axis=-1, keepdims=True) + EPS) * g


def _split3(x):
    hi = x.astype(BF16)
    r1 = x - hi.astype(F32)
    mid = r1.astype(BF16)
    lo = (r1 - mid.astype(F32)).astype(BF16)
    return hi, mid, lo


def _dot_sel(sel_bf16, x):
    hi, mid, lo = _split3(x)
    d = lambda p: jnp.dot(sel_bf16, p, preferred_element_type=F32)
    return d(hi) + d(mid) + d(lo)


def _dot_sel_rhs(x, sel_bf16):
    hi, mid, lo = _split3(x)
    d = lambda p: jnp.dot(p, sel_bf16, preferred_element_type=F32)
    return d(hi) + d(mid) + d(lo)


def _inproj_kernel(x_ref, g_ref, w_ref, bf_ref, o_ref, lf_ref, kb_ref, vb_ref, kf_ref, vf_ref):
    xn = _rms(x_ref[...], g_ref[...])
    y = jnp.dot(xn.astype(BF16), w_ref[...], preferred_element_type=F32)
    o_ref[...] = y
    kf_ref[...] = y[:, COL_KA:COL_KA + A_W]
    vf_ref[...] = y[:, COL_VA:COL_VA + A_W]
    kb_ref[...] = y[:, COL_KA:COL_KA + A_W].astype(BF16)
    vb_ref[...] = y[:, COL_VA:COL_VA + A_W].astype(BF16)
    fa = y[:, COL_FA:COL_FA + LANES] + bf_ref[...]
    lane = lax.broadcasted_iota(jnp.int32, fa.shape, 1)
    lf_ref[...] = jnp.where(lane < A_HEADS, jax.nn.log_sigmoid(fa), 0.0)


def _inproj(x2d, g, w, bf_pad):
    T = x2d.shape[0]
    tm = _tile(T, 256)
    return pl.pallas_call(
        _inproj_kernel,
        grid=(T // tm,),
        in_specs=[pl.BlockSpec((tm, D_MODEL), lambda i: (i, 0)),
                  pl.BlockSpec((1, D_MODEL), lambda i: (0, 0)),
                  pl.BlockSpec((D_MODEL, PROJ_W), lambda i: (0, 0)),
                  pl.BlockSpec((1, LANES), lambda i: (0, 0))],
        out_specs=[pl.BlockSpec((tm, PROJ_W), lambda i: (i, 0)),
                   pl.BlockSpec((tm, LANES), lambda i: (i, 0)),
                   pl.BlockSpec((tm, A_W), lambda i: (i, 0)),
                   pl.BlockSpec((tm, A_W), lambda i: (i, 0)),
                   pl.BlockSpec((tm, A_W), lambda i: (i, 0)),
                   pl.BlockSpec((tm, A_W), lambda i: (i, 0))],
        out_shape=[jax.ShapeDtypeStruct((T, PROJ_W), F32),
                   jax.ShapeDtypeStruct((T, LANES), F32),
                   jax.ShapeDtypeStruct((T, A_W), BF16),
                   jax.ShapeDtypeStruct((T, A_W), BF16),
                   jax.ShapeDtypeStruct((T, A_W), F32),
                   jax.ShapeDtypeStruct((T, A_W), F32)],
        compiler_params=_cparams("parallel"),
        name="inproj",
    )(x2d, g.reshape(1, D_MODEL), w, bf_pad)


SRC_QA = 0
SRC_KA = SRC_QA + A_W
SRC_VA = SRC_KA + A_W
SRC_FA = SRC_VA + A_W
SRC_QB = SRC_FA + A_HEADS
SRC_KB = SRC_QB + B_W
SRC_VB = SRC_KB + B_W
SRC_GB = SRC_VB + B_W
SRC_ZC = SRC_GB + B_W
SRC_XBC = SRC_ZC + C_INNER
SRC_DT = SRC_XBC + C_CONV_DIM
IN_WIDTH = SRC_DT + C_HEADS


def _reorder_w_in_kernel(x_ref, o_ref):
    rows = x_ref.shape[0]

    def copy(dst, src, width):
        o_ref[:, dst:dst + width] = x_ref[:, src:src + width].astype(BF16)

    copy(COL_QA, SRC_QA, 3 * A_W)
    copy(COL_VB, SRC_VB, 2 * B_W)
    copy(COL_ZC, SRC_ZC, C_INNER)
    copy(COL_XBC, SRC_XBC, C_CONV_DIM)
    half = B_HEAD_DIM // 2
    for dst, src in ((COL_QB, SRC_QB), (COL_KB, SRC_KB)):
        for k in range(2):
            for h in range(B_HEADS):
                copy(dst + k * LANES + h * half, src + h * B_HEAD_DIM + k * half, half)
    o_ref[:, COL_FA:COL_FA + 2 * LANES] = jnp.zeros((rows, 2 * LANES), BF16)
    copy(COL_FA, SRC_FA, A_HEADS)
    for h in range(C_HEADS):
        col = x_ref[:, SRC_DT + h:SRC_DT + h + 1].astype(BF16)
        o_ref[:, COL_DT + h * C_HEAD_DIM:COL_DT + (h + 1) * C_HEAD_DIM] = jnp.broadcast_to(col, (rows, C_HEAD_DIM))


def _prep_w_in(w_in_all, layer, b_forget, dt_bias):
    assert w_in_all.shape[1:] == (D_MODEL, IN_WIDTH)
    tr = 256
    w = pl.pallas_call(
        _reorder_w_in_kernel,
        grid=(D_MODEL // tr,),
        in_specs=[pl.BlockSpec((None, tr, IN_WIDTH), lambda i: (layer, i, 0))],
        out_specs=pl.BlockSpec((tr, PROJ_W), lambda i: (i, 0)),
        out_shape=jax.ShapeDtypeStruct((D_MODEL, PROJ_W), BF16),
        compiler_params=_cparams("parallel"),
        name="reorder_w_in",
    )(w_in_all)
    bf_pad = jnp.concatenate([b_forget, jnp.zeros((LANES - A_HEADS,), F32)]).reshape(1, LANES)
    dtb_full = jnp.repeat(dt_bias, C_HEAD_DIM).reshape(1, C_INNER)
    return w, bf_pad, dtb_full


FOX_TQ = 512
FOX_TK = 512
FOX_TQ_SUB = 512
FOX_XTRA = 6


def _fox_xtra_base(h):
    return A_HEAD_DIM * ((h + 1) % A_HEADS)


def _cumsum_kernel(lf_ref, eq_ref, ek_ref, carry_scr):
    @pl.when(pl.program_id(1) == 0)
    def _():
        carry_scr[...] = jnp.zeros_like(carry_scr)

    lf = lf_ref[0]
    tc = lf.shape[0]
    r = lax.broadcasted_iota(jnp.int32, (tc, tc), 0)
    c = lax.broadcasted_iota(jnp.int32, (tc, tc), 1)
    tri = jnp.where(r >= c, 1.0, 0.0).astype(BF16)
    cum = _dot_sel(tri, lf) + carry_scr[0:1, :]
    carry_scr[...] = jnp.broadcast_to(cum[tc - 1:tc, :], carry_scr.shape)

    pieces = _split3(cum)
    src = lax.broadcasted_iota(jnp.int32, (LANES, A_W), 0)
    dst = lax.broadcasted_iota(jnp.int32, (LANES, A_W), 1)
    lane = lax.broadcasted_iota(jnp.int32, (1, A_W), 1)
    one_q = jnp.zeros((1, A_W), F32)
    one_k = jnp.zeros((1, A_W), F32)
    for h in range(A_HEADS):
        base = _fox_xtra_base(h)
        one_q = jnp.where((lane >= base + 3) & (lane < base + FOX_XTRA), 1.0, one_q)
        one_k = jnp.where((lane >= base) & (lane < base + 3), 1.0, one_k)
    eq, ek = one_q, one_k
    for k, piece in enumerate(pieces):
        hit_q = functools.reduce(jnp.logical_or, [(src == h) & (dst == _fox_xtra_base(h) + k) for h in range(A_HEADS)])
        hit_k = functools.reduce(jnp.logical_or,
                                 [(src == h) & (dst == _fox_xtra_base(h) + 3 + k) for h in range(A_HEADS)])
        eq = eq + jnp.dot(piece, jnp.where(hit_q, 1.0, 0.0).astype(BF16), preferred_element_type=F32)
        ek = ek + jnp.dot(piece, jnp.where(hit_k, -1.0, 0.0).astype(BF16), preferred_element_type=F32)
    eq_ref[0] = eq.astype(BF16)
    ek_ref[0] = ek.astype(BF16)


def _cumsum(lf):
    B, L, _ = lf.shape
    tc = _tile(L, 512)
    return pl.pallas_call(
        _cumsum_kernel,
        grid=(B, L // tc),
        in_specs=[pl.BlockSpec((1, tc, LANES), lambda b, j: (b, j, 0))],
        out_specs=[pl.BlockSpec((1, tc, A_W), lambda b, j: (b, j, 0)),
                   pl.BlockSpec((1, tc, A_W), lambda b, j: (b, j, 0))],
        out_shape=[jax.ShapeDtypeStruct((B, L, A_W), BF16),
                   jax.ShapeDtypeStruct((B, L, A_W), BF16)],
        scratch_shapes=[pltpu.VMEM((8, LANES), F32)],
        compiler_params=_cparams("parallel", "arbitrary"),
        name="fox_cumsum",
    )(lf)


def _fox_kernel(q_ref, eq_ref, kb_ref, ek_ref, vb_ref, o_ref, qp_scr, m_scr, l_scr, acc_scr, *,
                tq, tk, offset, nk):
    i = pl.program_id(1)
    scale = A_HEAD_DIM ** -0.5
    q_lo = offset + i * tq

    def in_head(lane, h):
        return (lane >= h * A_HEAD_DIM) & (lane < (h + 1) * A_HEAD_DIM)

    def in_xtra(lane, h):
        return (lane >= _fox_xtra_base(h)) & (lane < _fox_xtra_base(h) + FOX_XTRA)

    lane_q = lax.broadcasted_iota(jnp.int32, (tq, A_W), 1)
    q = (q_ref[0] * scale).astype(BF16)
    eq = eq_ref[0]
    for h in range(A_HEADS):
        qp_scr[h] = jnp.where(in_xtra(lane_q, h), eq, jnp.where(in_head(lane_q, h), q, jnp.zeros((), BF16)))
    m_scr[...] = jnp.full(m_scr.shape, NEG_BIG, F32)
    l_scr[...] = jnp.zeros(l_scr.shape, F32)
    acc_scr[...] = jnp.zeros(acc_scr.shape, F32)

    lane_k = lax.broadcasted_iota(jnp.int32, (tk, A_W), 1)

    def block(jj, masked):
        ks = pl.multiple_of(jj * tk, tk)
        k = kb_ref[0, pl.ds(ks, tk), :]
        ek = ek_ref[0, pl.ds(ks, tk), :]
        v = vb_ref[0, pl.ds(ks, tk), :]
        if masked:
            kpos = ks + lax.broadcasted_iota(jnp.int32, (tk, tq), 0)
            qpos = q_lo + lax.broadcasted_iota(jnp.int32, (tk, tq), 1)
            vis = kpos <= qpos
        tqs = min(tq, FOX_TQ_SUB)
        ps, alphas = [], []
        for h in range(A_HEADS):
            kp = jnp.where(in_xtra(lane_k, h), ek, k)
            for qs in range(tq // tqs):
                cols = slice(qs * tqs, (qs + 1) * tqs)
                s = lax.dot_general(kp, qp_scr[h, cols, :], (((1,), (1,)), ((), ())),
                                    preferred_element_type=F32)
                if masked:
                    s = jnp.where(vis[:, cols], s, NEG_BIG)
                m_prev = m_scr[h, 0:1, cols]
                m_next = jnp.maximum(m_prev, jnp.max(s, axis=0, keepdims=True))
                alpha = jnp.exp(m_prev - m_next)
                p = jnp.exp(s - m_next)
                l_scr[h, :, cols] = jnp.broadcast_to(
                    alpha * l_scr[h, 0:1, cols] + jnp.sum(p, axis=0, keepdims=True), (8, tqs))
                m_scr[h, :, cols] = jnp.broadcast_to(m_next, (8, tqs))
                ps.append(p.astype(BF16))
                alphas.append(alpha)
        alphas = [jnp.concatenate(alphas[h * (tq // tqs):(h + 1) * (tq // tqs)], axis=1) for h in range(A_HEADS)]
        pv = lax.dot_general(v, jnp.concatenate(ps, axis=1), (((0,), (0,)), ((), ())),
                             preferred_element_type=F32)
        for h in range(A_HEADS):
            acc_scr[h] = acc_scr[h] * alphas[h] + pv[:, h * tq:(h + 1) * tq]

    n_full = (q_lo + 1) // tk
    n_vis = jnp.minimum((q_lo + tq - 1) // tk + 1, nk)

    def full_body(jj, carry):
        block(jj, False)
        return carry

    def edge_body(jj, carry):
        block(jj, True)
        return carry

    lax.fori_loop(0, n_full, full_body, 0)
    lax.fori_loop(n_full, n_vis, edge_body, 0)

    row = lax.broadcasted_iota(jnp.int32, (A_W, tq), 0)
    out_t = jnp.zeros((A_W, tq), F32)
    for h in range(A_HEADS):
        out_t = out_t + jnp.where(in_head(row, h), acc_scr[h] / l_scr[h, 0:1, :], 0.0)
    o_ref[0] = out_t.T.astype(o_ref.dtype)


def _fox(q_src, q_col, eq, kb, ek, vb, offset):
    B, Lq, _ = q_src.shape
    Lk = kb.shape[1]
    tq = _tile(Lq, FOX_TQ)
    tk = _tile(Lk, FOX_TK)
    assert tq % LANES == 0 and tk % LANES == 0
    nq, nk = Lq // tq, Lk // tk
    whole = lambda b, i: (b, 0, 0)
    kern = functools.partial(_fox_kernel, tq=tq, tk=tk, offset=offset, nk=nk)
    return pl.pallas_call(
        kern,
        grid=(B, nq),
        in_specs=[pl.BlockSpec((1, tq, A_W), lambda b, i: (b, i, q_col // A_W)),
                  pl.BlockSpec((1, tq, A_W), lambda b, i: (b, i, 0)),
                  pl.BlockSpec((1, Lk, A_W), whole),
                  pl.BlockSpec((1, Lk, A_W), whole),
                  pl.BlockSpec((1, Lk, A_W), whole)],
        out_specs=pl.BlockSpec((1, tq, A_W), lambda b, i: (b, i, 0)),
        out_shape=jax.ShapeDtypeStruct((B, Lq, A_W), BF16),
        scratch_shapes=[pltpu.VMEM((A_HEADS, tq, A_W), BF16),
                        pltpu.VMEM((A_HEADS, 8, tq), F32),
                        pltpu.VMEM((A_HEADS, 8, tq), F32),
                        pltpu.VMEM((A_HEADS, A_W, tq), F32)],
        compiler_params=_cparams("parallel", "arbitrary"),
        name="fox_attention",
    )(q_src, eq, kb, ek, vb)


def _ret_lane_head_v():
    return lax.broadcasted_iota(jnp.int32, (1, B_W), 1) // B_HEAD_DIM


def _ret_lane_head_qk():
    return (lax.broadcasted_iota(jnp.int32, (1, B_W), 1) % LANES) // (B_HEAD_DIM // 2)


def _log_gamma(h):
    return math.log1p(-2.0 ** (-5.0 - h))


def _ret_kernel(q_ref, k_ref, v_ref, g_ref, s0_ref, invf_ref, gret_ref, y_ref, sout_ref,
                state_scr, dec_scr, *, lc, offset, nchunks):
    c = pl.program_id(1)

    @pl.when(c == 0)
    def _():
        state_scr[...] = s0_ref[0]
        r = lax.broadcasted_iota(jnp.int32, (lc, lc), 0)
        s = lax.broadcasted_iota(jnp.int32, (lc, lc), 1)
        d = (r - s).astype(F32)
        for h in range(B_HEADS):
            dec_scr[h] = jnp.where(r >= s, jnp.exp(d * _log_gamma(h)), 0.0)

    hv = _ret_lane_head_v()
    hqk = _ret_lane_head_qk()
    lg_v = jnp.zeros((1, B_W), F32)
    lg_qk = jnp.zeros((1, B_W), F32)
    for h in range(B_HEADS):
        lg_v = jnp.where(hv == h, _log_gamma(h), lg_v)
        lg_qk = jnp.where(hqk == h, _log_gamma(h), lg_qk)

    pos = (offset + c * lc + lax.broadcasted_iota(jnp.int32, (lc, LANES), 0)).astype(F32)
    ang = pos * invf_ref[...]
    cos, sin = jnp.cos(ang), jnp.sin(ang)

    def rot(x):
        x1, x2 = x[:, :LANES], x[:, LANES:]
        return jnp.concatenate([x1 * cos - x2 * sin, x1 * sin + x2 * cos], axis=1)

    qr = rot(q_ref[0])
    kr = rot(k_ref[0]) * (B_HEAD_DIM ** -0.5)
    v = v_ref[0]
    vb = v.astype(BF16)
    krb = kr.astype(BF16)

    lpos = lax.broadcasted_iota(jnp.int32, (lc, B_W), 0).astype(F32)
    state = state_scr[...]
    y = jnp.dot(qr.astype(BF16), state.astype(BF16), preferred_element_type=F32) * jnp.exp((lpos + 1.0) * lg_v)
    for h in range(B_HEADS):
        qh = jnp.where(hqk == h, qr, 0.0).astype(BF16)
        sc = lax.dot_general(qh, krb, (((1,), (1,)), ((), ())), preferred_element_type=F32) * dec_scr[h]
        yh = jnp.dot(sc.astype(BF16), vb, preferred_element_type=F32)
        y = y + jnp.where(hv == h, yh, 0.0)

    kw = (kr * jnp.exp((lc - 1.0 - lpos) * lg_qk)).astype(BF16)
    upd = lax.dot_general(kw, vb, (((0,), (0,)), ((), ())), preferred_element_type=F32)
    row_head = (lax.broadcasted_iota(jnp.int32, (B_W, B_W), 0) % LANES) // (B_HEAD_DIM // 2)
    col_head = lax.broadcasted_iota(jnp.int32, (B_W, B_W), 1) // B_HEAD_DIM
    lg_rows = jnp.zeros((B_W, B_W), F32)
    for h in range(B_HEADS):
        lg_rows = jnp.where(row_head == h, _log_gamma(h), lg_rows)
    new_state = jnp.where(row_head == col_head, state * jnp.exp(lc * lg_rows) + upd, 0.0)
    state_scr[...] = new_state

    @pl.when(c == nchunks - 1)
    def _():
        sout_ref[0] = new_state

    ri = lax.broadcasted_iota(jnp.int32, (B_W, B_W), 0) // B_HEAD_DIM
    ci = lax.broadcasted_iota(jnp.int32, (B_W, B_W), 1) // B_HEAD_DIM
    avg = jnp.where(ri == ci, 1.0 / B_HEAD_DIM, 0.0).astype(BF16)
    yc = y - _dot_sel_rhs(y, avg)
    var = _dot_sel_rhs(yc * yc, avg)
    yn = yc * lax.rsqrt(var + EPS) * gret_ref[...]
    y_ref[0] = (jax.nn.silu(g_ref[0]) * yn).astype(y_ref.dtype)


def _retention(proj3, s0, invf, g_ret, offset):
    B, L, _ = proj3.shape
    lc = _tile(L, 256)
    nch = L // lc
    blk = lambda col: pl.BlockSpec((1, lc, B_W), lambda b, c: (b, c, col // B_W))
    kern = functools.partial(_ret_kernel, lc=lc, offset=offset, nchunks=nch)
    return pl.pallas_call(
        kern,
        grid=(B, nch),
        in_specs=[blk(COL_QB), blk(COL_KB), blk(COL_VB), blk(COL_GB),
                  pl.BlockSpec((1, B_W, B_W), lambda b, c: (b, 0, 0)),
                  pl.BlockSpec((1, LANES), lambda b, c: (0, 0)),
                  pl.BlockSpec((1, B_W), lambda b, c: (0, 0))],
        out_specs=[pl.BlockSpec((1, lc, B_W), lambda b, c: (b, c, 0)),
                   pl.BlockSpec((1, B_W, B_W), lambda b, c: (b, 0, 0))],
        out_shape=[jax.ShapeDtypeStruct((B, L, B_W), BF16),
                   jax.ShapeDtypeStruct((B, B_W, B_W), F32)],
        scratch_shapes=[pltpu.VMEM((B_W, B_W), F32),
                        pltpu.VMEM((B_HEADS, lc, lc), F32)],
        compiler_params=_cparams("parallel", "arbitrary"),
        name="retention",
    )(proj3, proj3, proj3, proj3, s0, invf, g_ret.reshape(1, B_W))


def _ret_state_to_kernel(s):
    B = s.shape[0]
    half = B_HEAD_DIM // 2
    s = s.reshape(B, B_HEADS, 2, half, B_HEAD_DIM)
    eye = jnp.eye(B_HEADS, dtype=s.dtype)
    full = jnp.einsum('bhkie,hg->bkhige', s, eye)
    return full.reshape(B, B_W, B_W)


def _ret_state_from_kernel(s):
    B = s.shape[0]
    half = B_HEAD_DIM // 2
    s = s.reshape(B, 2, B_HEADS, half, B_HEADS, B_HEAD_DIM)
    d = jnp.einsum('bkhihe->bhkie', s)
    return d.reshape(B, B_HEADS, B_HEAD_DIM, B_HEAD_DIM)


SSD_CHUNK = 256


def _ssd_kernel(z_ref, xbc_ref, dt_ref, s0_ref, cs0_ref, cw_ref, cb_ref, dtb_ref, alog_ref, dsk_ref, gs_ref,
                y_ref, sout_ref, state_scr, xp_scr, *, lc, nchunks):
    c = pl.program_id(1)
    PADR = 8

    @pl.when(c == 0)
    def _():
        state_scr[...] = s0_ref[0]
        xp_scr[0:PADR, :] = cs0_ref[0]

    xbc = xbc_ref[0]
    xp_scr[PADR:PADR + lc, :] = xbc
    cw = cw_ref[...]
    conv = xp_scr[PADR - 3:PADR - 3 + lc, :] * cw[0:1, :]
    for jj in range(1, C_CONV):
        conv = conv + xp_scr[PADR - 3 + jj:PADR - 3 + jj + lc, :] * cw[jj:jj + 1, :]
    act = jax.nn.silu(conv + cb_ref[...])
    xp_scr[0:PADR, :] = xp_scr[lc:lc + PADR, :]

    xs = act[:, :C_INNER]
    bm = act[:, C_INNER:C_INNER + G_W]
    cm = act[:, C_INNER + G_W:]
    dt = jax.nn.softplus(dt_ref[0] + dtb_ref[...])
    loga = dt * (-jnp.exp(alog_ref[...]))

    r = lax.broadcasted_iota(jnp.int32, (lc, lc), 0)
    s = lax.broadcasted_iota(jnp.int32, (lc, lc), 1)
    causal = r >= s
    tri = jnp.where(causal, 1.0, 0.0).astype(BF16)
    cum = _dot_sel(tri, loga)
    total = cum[lc - 1:lc, :]
    xdt = xs * dt
    xdtb = xdt.astype(BF16)
    bmb = bm.astype(BF16)
    cmb = cm.astype(BF16)
    state = state_scr[...]

    gw = C_INNER // C_GROUPS
    cum_t = [cum[:, kk * LANES:(kk + 1) * LANES].T for kk in range(C_INNER // LANES)]
    ys = []
    for g in range(C_GROUPS):
        cg = cmb[:, g * C_STATE:(g + 1) * C_STATE]
        bg = bmb[:, g * C_STATE:(g + 1) * C_STATE]
        scores = lax.dot_general(cg, bg, (((1,), (1,)), ((), ())), preferred_element_type=F32)
        y_g = jnp.dot(cg, state[:, g * gw:(g + 1) * gw].astype(BF16), preferred_element_type=F32)
        y_g = y_g * jnp.exp(cum[:, g * gw:(g + 1) * gw])
        lane_head = lax.broadcasted_iota(jnp.int32, (1, gw), 1) // C_HEAD_DIM
        xg = xdtb[:, g * gw:(g + 1) * gw]
        for hh in range(C_HEADS // C_GROUPS):
            col = g * gw + hh * C_HEAD_DIM
            cum_col = cum[:, col:col + 1]
            cum_row = cum_t[col // LANES][col % LANES:col % LANES + 1, :]
            dec = jnp.exp(jnp.where(causal, cum_col - cum_row, NEG_BIG))
            yh = jnp.dot((scores * dec).astype(BF16), xg, preferred_element_type=F32)
            y_g = y_g + jnp.where(lane_head == hh, yh, 0.0)
        ys.append(y_g)
        xw = (xdt[:, g * gw:(g + 1) * gw] * jnp.exp(total[:, g * gw:(g + 1) * gw] - cum[:, g * gw:(g + 1) * gw]))
        upd = lax.dot_general(bg, xw.astype(BF16), (((0,), (0,)), ((), ())), preferred_element_type=F32)
        state_scr[:, g * gw:(g + 1) * gw] = state[:, g * gw:(g + 1) * gw] * jnp.exp(total[:, g * gw:(g + 1) * gw]) + upd
    y = jnp.concatenate(ys, axis=1)

    @pl.when(c == nchunks - 1)
    def _():
        sout_ref[0] = state_scr[...]

    yc = (y + xs * dsk_ref[...]) * jax.nn.silu(z_ref[0])
    y_ref[0] = _rms(yc, gs_ref[...]).astype(y_ref.dtype)


def _ssd(proj3, s0, cs0, conv_w, conv_b, dtb_full, a_log, d_skip, g_ssm):
    B, L, _ = proj3.shape
    lc = _tile(L, SSD_CHUNK)
    nch = L // lc
    rep = lambda p: jnp.repeat(p, C_HEAD_DIM).reshape(1, C_INNER)
    kern = functools.partial(_ssd_kernel, lc=lc, nchunks=nch)
    cst = lambda shape: pl.BlockSpec(shape, lambda b, c: (0,) * len(shape))
    return pl.pallas_call(
        kern,
        grid=(B, nch),
        in_specs=[pl.BlockSpec((1, lc, C_INNER), lambda b, c: (b, c, COL_ZC // C_INNER)),
                  pl.BlockSpec((1, lc, C_CONV_DIM), lambda b, c: (b, c, COL_XBC // C_CONV_DIM)),
                  pl.BlockSpec((1, lc, C_INNER), lambda b, c: (b, c, COL_DT // C_INNER)),
                  pl.BlockSpec((1, C_STATE, C_INNER), lambda b, c: (b, 0, 0)),
                  pl.BlockSpec((1, 8, C_CONV_DIM), lambda b, c: (b, 0, 0)),
                  cst((C_CONV, C_CONV_DIM)), cst((1, C_CONV_DIM)), cst((1, C_INNER)), cst((1, C_INNER)),
                  cst((1, C_INNER)), cst((1, C_INNER))],
        out_specs=[pl.BlockSpec((1, lc, C_INNER), lambda b, c: (b, c, 0)),
                   pl.BlockSpec((1, C_STATE, C_INNER), lambda b, c: (b, 0, 0))],
        out_shape=[jax.ShapeDtypeStruct((B, L, C_INNER), BF16),
                   jax.ShapeDtypeStruct((B, C_STATE, C_INNER), F32)],
        scratch_shapes=[pltpu.VMEM((C_STATE, C_INNER), F32),
                        pltpu.VMEM((lc + 8, C_CONV_DIM), F32)],
        compiler_params=_cparams("parallel", "arbitrary"),
        name="ssd",
    )(proj3, proj3, proj3, s0, cs0, conv_w, conv_b.reshape(1, C_CONV_DIM), dtb_full, rep(a_log), rep(d_skip),
      g_ssm.reshape(1, C_INNER))


PEER_SUB = 128
PEER_STEP = 2048
PEER_CAND_COLS = 4


def _gelu_tanh(x):
    k0 = math.sqrt(2.0 / math.pi)
    inner = x * (k0 + (k0 * 0.044715) * (x * x))
    return (0.5 * x) * (1.0 + jnp.tanh(inner))


def _top16(x, want_rank):
    vals = []
    rank = jnp.full(x.shape, float(PEER_TOPK), F32) if want_rank else None
    for r in range(PEER_TOPK):
        m = jnp.max(x, axis=0, keepdims=True)
        vals.append(m)
        hit = x == m
        if want_rank:
            rank = jnp.where(hit, float(r), rank)
        x = jnp.where(hit, NEG_BIG, x)
    return jnp.concatenate(vals, axis=0), rank


def _sort_pairs(n):
    def merge(lo, hi, r):
        step = r * 2
        if step < hi - lo:
            yield from merge(lo, hi, step)
            yield from merge(lo + r, hi, step)
            yield from [(i, i + r) for i in range(lo + r, hi - r, step)]
        else:
            yield (lo, lo + r)

    def sort(lo, hi):
        if hi - lo >= 1:
            mid = lo + (hi - lo) // 2
            yield from sort(lo, mid)
            yield from sort(mid + 1, hi)
            yield from merge(lo, hi, 1)

    return tuple(sort(0, n - 1))


def _top16_values(x):
    v = [x[8 * g:8 * g + 8, :] for g in range(PEER_NKEYS // 8)]
    for a, b in _sort_pairs(PEER_NKEYS // 8):
        v[a], v[b] = jnp.maximum(v[a], v[b]), jnp.minimum(v[a], v[b])
    vals = []
    for r in range(PEER_TOPK):
        m = jnp.max(v[0], axis=0, keepdims=True)
        vals.append(m)
        hit = v[0] == m
        for k in range(PEER_TOPK - 1 - r):
            v[k] = jnp.where(hit, v[k + 1], v[k])
    return jnp.concatenate(vals, axis=0)


def _peer_select(s1, s2):
    n = s1.shape[1]
    a16 = _top16_values(s1)
    b16, rank2 = _top16(s2, True)
    ridx = lax.broadcasted_iota(jnp.int32, (PEER_TOPK, n), 0)
    sub = lax.broadcasted_iota(jnp.int32, (8, n), 0)
    lo, hi = slice(0, 8), slice(8, PEER_TOPK)
    col = lambda r2, rows: a16[rows, :] + b16[r2:r2 + 1, :]
    row = lambda r1, cols: b16[cols, :] + a16[r1:r1 + 1, :]
    assert PEER_CAND_COLS == 4 and PEER_TOPK == 16
    groups = [col(0, lo), col(0, hi), col(1, lo),
              jnp.where(sub <= 4, col(2, lo), NEG_BIG),
              jnp.where(sub <= 3, col(3, lo), row(1, lo)),
              jnp.where(sub >= 4, row(0, lo), NEG_BIG),
              row(0, hi),
              jnp.where(sub == 4, row(2, lo), NEG_BIG)]
    cand = jnp.concatenate(groups, axis=0)
    v = list(groups)
    for a_i, b_i in _sort_pairs(8):
        v[a_i], v[b_i] = jnp.maximum(v[a_i], v[b_i]), jnp.minimum(v[a_i], v[b_i])
    tau = None
    for r in range(PEER_TOPK):
        tau = jnp.max(v[0], axis=0, keepdims=True)
        hit = v[0] == tau
        depth = min(8, PEER_TOPK - r)
        for k in range(depth - 1):
            v[k] = jnp.where(hit, v[k + 1], v[k])
        if depth == 8:
            v[7] = jnp.where(hit, NEG_BIG, v[7])
    top = a16[0:1, :] + b16[0:1, :]
    z = jnp.sum(jnp.where(cand >= tau, jnp.exp(cand - top), 0.0), axis=0, keepdims=True)
    cnt = jnp.zeros(s1.shape, F32)
    for r2 in range(PEER_CAND_COLS):
        cnt = cnt + jnp.where(s1 + b16[r2:r2 + 1, :] >= tau, 1.0, 0.0)
    for r1 in range(PEER_TOPK // (PEER_CAND_COLS + 1)):
        a_r = a16[r1:r1 + 1, :]
        tail = jnp.where((ridx >= PEER_CAND_COLS) & (b16 + a_r >= tau), 1.0, 0.0)
        cnt = cnt + jnp.where(s1 == a_r, jnp.sum(tail, axis=0, keepdims=True), 0.0)
    a = jnp.exp(s1 - a16[0:1, :]) / z
    b = jnp.exp(s2 - b16[0:1, :])
    return a, cnt, rank2, b


def _peer_kernel(h_ref, ya_ref, yb_ref, yc_ref, wo_ref, g_ref, wq_ref, k1_ref, k2_ref, u_ref, vt_ref, o_ref,
                 xt_scr, a_scr, cnt_scr, rank_scr, b_scr, acc_scr, h1_scr, *, tm, te, ne):
    j = pl.program_id(1)
    n1 = te // PEER_NKEYS
    nchunk = tm // LANES
    nsub = te // PEER_SUB
    slabs = PEER_SUB // PEER_NKEYS

    def gate_weights(sb):
        ws = []
        for il in range(slabs):
            i1 = j * n1 + sb * slabs + il
            w = jnp.zeros((PEER_NKEYS, tm), BF16)
            for h in range(PEER_HEADS):
                row = lambda ref: jnp.concatenate(
                    [jnp.broadcast_to(ref[h, cch, pl.ds(i1, 1), :], (BF16_ROWS, LANES)) for cch in range(nchunk)],
                    axis=1).astype(BF16)
                cnt_b = jnp.tile(row(cnt_scr), (PEER_NKEYS // BF16_ROWS, 1))
                a_b = jnp.tile(row(a_scr), (PEER_NKEYS // BF16_ROWS, 1))
                w = w + jnp.where(rank_scr[h] < cnt_b, b_scr[h], jnp.zeros((), BF16)) * a_b
            ws.append(w)
        return jnp.concatenate(ws, axis=0)

    @pl.when(j == 0)
    def _():
        h1 = h_ref[...]
        h1 = h1 + jnp.dot(ya_ref[...], wo_ref[0:A_W, :], preferred_element_type=F32)
        h1 = h1 + jnp.dot(yb_ref[...], wo_ref[A_W:A_W + B_W, :], preferred_element_type=F32)
        h1 = h1 + jnp.dot(yc_ref[...], wo_ref[A_W + B_W:, :], preferred_element_type=F32)
        h1_scr[...] = h1
        xn = _rms(h1, g_ref[...])
        xt_scr[...] = xn.T.astype(BF16)
        acc_scr[...] = jnp.zeros(acc_scr.shape, F32)

        def head_body(h, carry):
            base = pl.multiple_of(h * PEER_KEY_DIM, PEER_KEY_DIM)
            k1 = k1_ref[h].astype(BF16)
            k2 = k2_ref[h].astype(BF16)
            qh = jnp.dot(wq_ref[pl.ds(base, PEER_KEY_DIM), :], xt_scr[...], preferred_element_type=F32)
            for cch in range(nchunk):
                sl = slice(cch * LANES, (cch + 1) * LANES)
                q1 = qh[0:PEER_HALF, sl].astype(BF16)
                q2 = qh[PEER_HALF:, sl].astype(BF16)
                s1 = jnp.dot(k1, q1, preferred_element_type=F32)
                s2 = jnp.dot(k2, q2, preferred_element_type=F32)
                a, cnt, rank2, b = _peer_select(s1, s2)
                a_scr[h, cch] = a
                cnt_scr[h, cch] = cnt
                rank_scr[h, :, sl] = rank2.astype(BF16)
                b_scr[h, :, sl] = b.astype(BF16)
            return carry

        lax.fori_loop(0, PEER_HEADS, head_body, 0)

    acts = []
    for sb in range(nsub):
        ht = jnp.dot(u_ref[sb * PEER_SUB:(sb + 1) * PEER_SUB, :], xt_scr[...], preferred_element_type=F32)
        acts.append(_gelu_tanh(ht.astype(BF16)) * gate_weights(sb))
    acc_scr[...] += jnp.dot(vt_ref[...], jnp.concatenate(acts, axis=0), preferred_element_type=F32)

    @pl.when(j == ne - 1)
    def _():
        o_ref[...] = h1_scr[...] + acc_scr[...].T


def _layer_cast_kernel(x_ref, o_ref, *, transpose):
    x = x_ref[...]
    o_ref[...] = (x.T if transpose else x).astype(o_ref.dtype)


def _layer_cast(x_all, layer, dtype, transpose):
    _, R, C = x_all.shape
    tr = _tile(R, 512)
    return pl.pallas_call(
        functools.partial(_layer_cast_kernel, transpose=transpose),
        grid=(R // tr,),
        in_specs=[pl.BlockSpec((None, tr, C), lambda i: (layer, i, 0))],
        out_specs=pl.BlockSpec((C, tr), lambda i: (0, i)) if transpose else pl.BlockSpec((tr, C), lambda i: (i, 0)),
        out_shape=jax.ShapeDtypeStruct((C, R) if transpose else (R, C), dtype),
        compiler_params=_cparams("parallel"),
        name="layer_cast",
    )(x_all)


def _peer(h2d, ya, yb, yc, w_out_bf16, g, wq_t, k1, k2, u_bf16, vt_bf16):
    T = h2d.shape[0]
    row = lambda w: pl.BlockSpec((tm, w), lambda i, j: (i, 0))
    tm = _tile(T, 512)
    assert tm % LANES == 0
    te = PEER_STEP
    ne = PEER_EXPERTS // te
    kern = functools.partial(_peer_kernel, tm=tm, te=te, ne=ne)
    hk = (PEER_HEADS, PEER_NKEYS, tm)
    hkc = (PEER_HEADS, tm // LANES, PEER_NKEYS, LANES)
    return pl.pallas_call(
        kern,
        grid=(T // tm, ne),
        in_specs=[row(D_MODEL), row(A_W), row(B_W), row(C_INNER),
                  pl.BlockSpec((D_MODEL, D_MODEL), lambda i, j: (0, 0)),
                  pl.BlockSpec((1, D_MODEL), lambda i, j: (0, 0)),
                  pl.BlockSpec((PEER_HEADS * PEER_KEY_DIM, D_MODEL), lambda i, j: (0, 0)),
                  pl.BlockSpec((PEER_HEADS, PEER_NKEYS, PEER_HALF), lambda i, j: (0, 0, 0)),
                  pl.BlockSpec((PEER_HEADS, PEER_NKEYS, PEER_HALF), lambda i, j: (0, 0, 0)),
                  pl.BlockSpec((te, D_MODEL), lambda i, j: (j, 0)),
                  pl.BlockSpec((D_MODEL, te), lambda i, j: (0, j))],
        out_specs=pl.BlockSpec((tm, D_MODEL), lambda i, j: (i, 0)),
        out_shape=jax.ShapeDtypeStruct((T, D_MODEL), F32),
        scratch_shapes=[pltpu.VMEM((D_MODEL, tm), BF16),
                        pltpu.VMEM(hkc, F32), pltpu.VMEM(hkc, F32), pltpu.VMEM(hk, BF16), pltpu.VMEM(hk, BF16),
                        pltpu.VMEM((D_MODEL, tm), F32),
                        pltpu.VMEM((tm, D_MODEL), F32)],
        compiler_params=_cparams("parallel", "arbitrary"),
        name="peer",
    )(h2d, ya, yb, yc, w_out_bf16, g.reshape(1, D_MODEL), wq_t, k1, k2, u_bf16, vt_bf16)


def _ple_kernel(h_ref, p_ref, g_ref, wg_ref, wp_ref, gf_ref, o_ref, *, final_norm):
    h = h_ref[...]
    xn = _rms(h, g_ref[...])
    gate = jax.nn.sigmoid(jnp.dot(xn.astype(BF16), wg_ref[...], preferred_element_type=F32))
    emb = jnp.dot(p_ref[...].astype(BF16), wp_ref[...], preferred_element_type=F32)
    out = h + emb * gate
    if final_norm:
        out = _rms(out, gf_ref[...])
    o_ref[...] = out


def _ple(h2d, p_all, layer, g, wg_bf16, wp_bf16, g_final, final_norm):
    T = h2d.shape[0]
    tm = _tile(T, 512)
    kern = functools.partial(_ple_kernel, final_norm=final_norm)
    return pl.pallas_call(
        kern,
        grid=(T // tm,),
        in_specs=[pl.BlockSpec((tm, D_MODEL), lambda i: (i, 0)),
                  pl.BlockSpec((None, tm, PLE_DIM), lambda i: (layer, i, 0)),
                  pl.BlockSpec((1, D_MODEL), lambda i: (0, 0)),
                  pl.BlockSpec((D_MODEL, D_MODEL), lambda i: (0, 0)),
                  pl.BlockSpec((PLE_DIM, D_MODEL), lambda i: (0, 0)),
                  pl.BlockSpec((1, D_MODEL), lambda i: (0, 0))],
        out_specs=pl.BlockSpec((tm, D_MODEL), lambda i: (i, 0)),
        out_shape=jax.ShapeDtypeStruct((T, D_MODEL), F32),
        compiler_params=_cparams("parallel"),
        name="ple",
    )(h2d, p_all, g.reshape(1, D_MODEL), wg_bf16, wp_bf16, g_final.reshape(1, D_MODEL))


def _pad_to(x, n, axis):
    pad = n - x.shape[axis]
    if pad == 0:
        return x
    widths = [(0, 0)] * x.ndim
    widths[axis] = (0, pad)
    return jnp.pad(x, widths)


def _prep_layer_weights(weights, i):
    (g_mix, w_in, b_forget, g_ret, conv_w, conv_b, dt_bias, a_log, d_skip, g_ssm, w_out,
     g_ffn, peer_wq, peer_k1, peer_k2, peer_u, peer_v, g_ple, w_ple_gate, w_ple, g_final) = weights
    w_proj, bf_pad, dtb_full = _prep_w_in(w_in, i, b_forget[i], dt_bias[i])
    return dict(w_proj=w_proj, bf_pad=bf_pad, dtb_full=dtb_full, w_out=w_out[i].astype(BF16),
                wq_t=_layer_cast(peer_wq, i, BF16, True), u=_layer_cast(peer_u, i, BF16, False),
                vt=_layer_cast(peer_v, i, BF16, True),
                wg=w_ple_gate[i].astype(BF16), wp=w_ple[i].astype(BF16))


def _run_trunk(x, p, past, weights, prepped):
    (g_mix, w_in, b_forget, g_ret, conv_w, conv_b, dt_bias, a_log, d_skip, g_ssm, w_out,
     g_ffn, peer_wq, peer_k1, peer_k2, peer_u, peer_v, g_ple, w_ple_gate, w_ple, g_final) = weights
    B, L, _ = x.shape
    depth = w_in.shape[0]
    T = B * L
    P = 0 if past is None else past[0].shape[2]
    half = B_HEAD_DIM // 2
    invf = jnp.tile(ROPE_BASE ** (-jnp.arange(half, dtype=F32) / half), LANES // half).reshape(1, LANES)

    h = x.reshape(T, D_MODEL)
    outs = []
    for i in range(depth):
        pw = prepped[i]
        dtb_full = pw["dtb_full"]
        proj, lf, kb, vb, k_new, v_new = _inproj(h, g_mix[i], pw["w_proj"], pw["bf_pad"])
        proj3 = proj.reshape(B, L, PROJ_W)
        lf3 = lf.reshape(B, L, LANES)
        kb = kb.reshape(B, L, A_W)
        vb = vb.reshape(B, L, A_W)

        if past is None:
            eq, ek = _cumsum(lf3)
            ya = _fox(proj3, COL_QA, eq, kb, ek, vb, 0)
            rs0 = jnp.zeros((B, B_W, B_W), F32)
            ss0 = jnp.zeros((B, C_STATE, C_INNER), F32)
            cs0 = jnp.zeros((B, 8, C_CONV_DIM), F32)
        else:
            lk_pad = -(-(P + L) // FOX_TK) * FOX_TK
            lq_pad = -(-L // LANES) * LANES
            past_k = past[0][i].reshape(B, P, A_W).astype(BF16)
            past_v = past[1][i].reshape(B, P, A_W).astype(BF16)
            past_lf = _pad_to(past[2][i], LANES, 2)
            k_all = _pad_to(jnp.concatenate([past_k, kb], axis=1), lk_pad, 1)
            v_all = _pad_to(jnp.concatenate([past_v, vb], axis=1), lk_pad, 1)
            lf_all = _pad_to(jnp.concatenate([past_lf, lf3], axis=1), lk_pad, 1)
            eq, ek = _cumsum(lf_all)
            q_pad = _pad_to(proj3[:, :, COL_QA:COL_QA + A_W], lq_pad, 1)
            ya = _fox(q_pad, 0, _pad_to(eq[:, P:P + L], lq_pad, 1), k_all, ek, v_all, P)[:, :L]
            rs0 = _ret_state_to_kernel(past[3][i])
            ss0 = jnp.transpose(past[4][i], (0, 2, 1, 3)).reshape(B, C_STATE, C_INNER)
            cs0 = jnp.concatenate([jnp.zeros((B, 8 - (C_CONV - 1), C_CONV_DIM), F32), past[5][i]], axis=1)

        yb, ret_k = _retention(proj3, rs0, invf, g_ret[i], P)
        ret_new = _ret_state_from_kernel(ret_k)

        yc, ssm_k = _ssd(proj3, ss0, cs0, conv_w[i], conv_b[i], dtb_full, a_log[i], d_skip[i], g_ssm[i])
        ssm_new = jnp.transpose(ssm_k.reshape(B, C_STATE, C_HEADS, C_HEAD_DIM), (0, 2, 1, 3))
        xbc = proj3[:, :, COL_XBC:COL_XBC + C_CONV_DIM]
        if L >= C_CONV - 1:
            conv_new = xbc[:, L - (C_CONV - 1):]
        else:
            prev = jnp.zeros((B, C_CONV - 1, C_CONV_DIM), F32) if past is None else past[5][i]
            conv_new = jnp.concatenate([prev, xbc], axis=1)[:, -(C_CONV - 1):]

        h = _peer(h, ya.reshape(T, A_W), yb.reshape(T, B_W), yc.reshape(T, C_INNER), pw["w_out"],
                  g_ffn[i], pw["wq_t"], peer_k1[i], peer_k2[i], pw["u"], pw["vt"])
        h = _ple(h, p.reshape(depth, T, PLE_DIM), i, g_ple[i], pw["wg"], pw["wp"], g_final,
                 final_norm=(i == depth - 1))
        outs.append((k_new.reshape(B, L, A_HEADS, A_HEAD_DIM), v_new.reshape(B, L, A_HEADS, A_HEAD_DIM),
                     lf3[:, :, :A_HEADS], ret_new, ssm_new, conv_new))
    stacked = tuple(jnp.stack([o[n] for o in outs]) for n in range(6))
    return h.reshape(B, L, D_MODEL), stacked


def kernel(x_prompt, x_sample, cache_k_fox, cache_v_fox, cache_logf_fox, state_ret, state_ssm, state_conv, p_prompt, p_sample, g_mix, w_in, b_forget, g_ret, conv_w, conv_b, dt_bias, a_log, d_skip, g_ssm, w_out, g_ffn, peer_wq, peer_k1, peer_k2, peer_u, peer_v, g_ple, w_ple_gate, w_ple, g_final):
    weights = (g_mix, w_in, b_forget, g_ret, conv_w, conv_b, dt_bias, a_log, d_skip, g_ssm, w_out,
               g_ffn, peer_wq, peer_k1, peer_k2, peer_u, peer_v, g_ple, w_ple_gate, w_ple, g_final)
    prepped = [_prep_layer_weights(weights, i) for i in range(w_in.shape[0])]
    y_prompt, (k_p, v_p, lf_p, ret_p, ssm_p, conv_p) = _run_trunk(x_prompt, p_prompt, None, weights, prepped)
    past = (cache_k_fox, cache_v_fox, cache_logf_fox, state_ret, state_ssm, state_conv)
    y_sample, (k_s, v_s, lf_s, ret_s, ssm_s, conv_s) = _run_trunk(x_sample, p_sample, past, weights, prepped)
    return (y_prompt, y_sample, k_p, v_p, lf_p, ret_p, ssm_p, conv_p, k_s, v_s, lf_s, ret_s, ssm_s, conv_s)
```

```python
import functools
import math

import jax
import jax.numpy as jnp
from jax import lax
from jax.experimental import pallas as pl
from jax.experimental.pallas import tpu as pltpu

F32 = jnp.float32
BF16 = jnp.bfloat16

D_MODEL = 1024
PLE_DIM = 256
EPS = 1e-6
A_HEADS, A_HEAD_DIM = 4, 64
B_HEADS, B_HEAD_DIM = 4, 64
ROPE_BASE = 10000.0
C_HEADS, C_HEAD_DIM = 8, 64
C_INNER = C_HEADS * C_HEAD_DIM
C_GROUPS, C_STATE, C_CONV = 2, 128, 4
C_CONV_DIM = C_INNER + 2 * C_GROUPS * C_STATE
PEER_HEADS, PEER_NKEYS, PEER_KEY_DIM, PEER_TOPK = 8, 128, 256, 16
PEER_HALF = PEER_KEY_DIM // 2
PEER_EXPERTS = PEER_NKEYS * PEER_NKEYS

LANES = 128
BF16_ROWS = 16
A_W = A_HEADS * A_HEAD_DIM
B_W = B_HEADS * B_HEAD_DIM
G_W = C_GROUPS * C_STATE

COL_QA = 0
COL_KA = COL_QA + A_W
COL_VA = COL_KA + A_W
COL_QB = COL_VA + A_W
COL_KB = COL_QB + B_W
COL_VB = COL_KB + B_W
COL_GB = COL_VB + B_W
COL_FA = COL_GB + B_W
COL_ZC = COL_FA + 2 * LANES
COL_DT = COL_ZC + C_INNER
COL_XBC = COL_DT + C_INNER
PROJ_W = COL_XBC + C_CONV_DIM
assert all(c % A_W == 0 for c in (COL_QA, COL_KA, COL_VA, COL_QB, COL_KB, COL_VB, COL_GB))
assert COL_ZC % C_INNER == 0 and COL_DT % C_INNER == 0 and COL_XBC % C_CONV_DIM == 0 and COL_FA % LANES == 0

NEG_BIG = -1e30
VMEM_LIMIT_BYTES = 52 * 1024 * 1024


def _cparams(*sem):
    return pltpu.CompilerParams(dimension_semantics=sem, vmem_limit_bytes=VMEM_LIMIT_BYTES)


def _tile(n, pref):
    t = min(n, pref)
    while n % t:
        t -= 8
    assert t > 0
    return t


def _rms(x, g):
    return x * lax.rsqrt(jnp.mean(x * x, axis=-1, keepdims=True) + EPS) * g


def _split3(x):
    hi = x.astype(BF16)
    r1 = x - hi.astype(F32)
    mid = r1.astype(BF16)
    lo = (r1 - mid.astype(F32)).astype(BF16)
    return hi, mid, lo


def _dot_sel(sel_bf16, x):
    hi, mid, lo = _split3(x)
    d = lambda p: jnp.dot(sel_bf16, p, preferred_element_type=F32)
    return d(hi) + d(mid) + d(lo)


def _dot_sel_rhs(x, sel_bf16):
    hi, mid, lo = _split3(x)
    d = lambda p: jnp.dot(p, sel_bf16, preferred_element_type=F32)
    return d(hi) + d(mid) + d(lo)


def _inproj_kernel(x_ref, g_ref, w_ref, bf_ref, o_ref, lf_ref, kb_ref, vb_ref, kf_ref, vf_ref):
    xn = _rms(x_ref[...], g_ref[...])
    y = jnp.dot(xn.astype(BF16), w_ref[...], preferred_element_type=F32)
    o_ref[...] = y
    kf_ref[...] = y[:, COL_KA:COL_KA + A_W]
    vf_ref[...] = y[:, COL_VA:COL_VA + A_W]
    kb_ref[...] = y[:, COL_KA:COL_KA + A_W].astype(BF16)
    vb_ref[...] = y[:, COL_VA:COL_VA + A_W].astype(BF16)
    fa = y[:, COL_FA:COL_FA + LANES] + bf_ref[...]
    lane = lax.broadcasted_iota(jnp.int32, fa.shape, 1)
    lf_ref[...] = jnp.where(lane < A_HEADS, jax.nn.log_sigmoid(fa), 0.0)


def _inproj(x2d, g, w, bf_pad):
    T = x2d.shape[0]
    tm = _tile(T, 256)
    return pl.pallas_call(
        _inproj_kernel,
        grid=(T // tm,),
        in_specs=[pl.BlockSpec((tm, D_MODEL), lambda i: (i, 0)),
                  pl.BlockSpec((1, D_MODEL), lambda i: (0, 0)),
                  pl.BlockSpec((D_MODEL, PROJ_W), lambda i: (0, 0)),
                  pl.BlockSpec((1, LANES), lambda i: (0, 0))],
        out_specs=[pl.BlockSpec((tm, PROJ_W), lambda i: (i, 0)),
                   pl.BlockSpec((tm, LANES), lambda i: (i, 0)),
                   pl.BlockSpec((tm, A_W), lambda i: (i, 0)),
                   pl.BlockSpec((tm, A_W), lambda i: (i, 0)),
                   pl.BlockSpec((tm, A_W), lambda i: (i, 0)),
                   pl.BlockSpec((tm, A_W), lambda i: (i, 0))],
        out_shape=[jax.ShapeDtypeStruct((T, PROJ_W), F32),
                   jax.ShapeDtypeStruct((T, LANES), F32),
                   jax.ShapeDtypeStruct((T, A_W), BF16),
                   jax.ShapeDtypeStruct((T, A_W), BF16),
                   jax.ShapeDtypeStruct((T, A_W), F32),
                   jax.ShapeDtypeStruct((T, A_W), F32)],
        compiler_params=_cparams("parallel"),
        name="inproj",
    )(x2d, g.reshape(1, D_MODEL), w, bf_pad)


SRC_QA = 0
SRC_KA = SRC_QA + A_W
SRC_VA = SRC_KA + A_W
SRC_FA = SRC_VA + A_W
SRC_QB = SRC_FA + A_HEADS
SRC_KB = SRC_QB + B_W
SRC_VB = SRC_KB + B_W
SRC_GB = SRC_VB + B_W
SRC_ZC = SRC_GB + B_W
SRC_XBC = SRC_ZC + C_INNER
SRC_DT = SRC_XBC + C_CONV_DIM
IN_WIDTH = SRC_DT + C_HEADS


def _reorder_w_in_kernel(x_ref, o_ref):
    rows = x_ref.shape[0]

    def copy(dst, src, width):
        o_ref[:, dst:dst + width] = x_ref[:, src:src + width].astype(BF16)

    copy(COL_QA, SRC_QA, 3 * A_W)
    copy(COL_VB, SRC_VB, 2 * B_W)
    copy(COL_ZC, SRC_ZC, C_INNER)
    copy(COL_XBC, SRC_XBC, C_CONV_DIM)
    half = B_HEAD_DIM // 2
    for dst, src in ((COL_QB, SRC_QB), (COL_KB, SRC_KB)):
        for k in range(2):
            for h in range(B_HEADS):
                copy(dst + k * LANES + h * half, src + h * B_HEAD_DIM + k * half, half)
    o_ref[:, COL_FA:COL_FA + 2 * LANES] = jnp.zeros((rows, 2 * LANES), BF16)
    copy(COL_FA, SRC_FA, A_HEADS)
    for h in range(C_HEADS):
        col = x_ref[:, SRC_DT + h:SRC_DT + h + 1].astype(BF16)
        o_ref[:, COL_DT + h * C_HEAD_DIM:COL_DT + (h + 1) * C_HEAD_DIM] = jnp.broadcast_to(col, (rows, C_HEAD_DIM))


def _prep_w_in(w_in_all, layer, b_forget, dt_bias):
    assert w_in_all.shape[1:] == (D_MODEL, IN_WIDTH)
    tr = 256
    w = pl.pallas_call(
        _reorder_w_in_kernel,
        grid=(D_MODEL // tr,),
        in_specs=[pl.BlockSpec((None, tr, IN_WIDTH), lambda i: (layer, i, 0))],
        out_specs=pl.BlockSpec((tr, PROJ_W), lambda i: (i, 0)),
        out_shape=jax.ShapeDtypeStruct((D_MODEL, PROJ_W), BF16),
        compiler_params=_cparams("parallel"),
        name="reorder_w_in",
    )(w_in_all)
    bf_pad = jnp.concatenate([b_forget, jnp.zeros((LANES - A_HEADS,), F32)]).reshape(1, LANES)
    dtb_full = jnp.repeat(dt_bias, C_HEAD_DIM).reshape(1, C_INNER)
    return w, bf_pad, dtb_full


FOX_TQ = 512
FOX_TK = 512
FOX_TQ_SUB = 512
FOX_XTRA = 6


def _fox_xtra_base(h):
    return A_HEAD_DIM * ((h + 1) % A_HEADS)


def _cumsum_kernel(lf_ref, eq_ref, ek_ref, carry_scr):
    @pl.when(pl.program_id(1) == 0)
    def _():
        carry_scr[...] = jnp.zeros_like(carry_scr)

    lf = lf_ref[0]
    tc = lf.shape[0]
    r = lax.broadcasted_iota(jnp.int32, (tc, tc), 0)
    c = lax.broadcasted_iota(jnp.int32, (tc, tc), 1)
    tri = jnp.where(r >= c, 1.0, 0.0).astype(BF16)
    cum = _dot_sel(tri, lf) + carry_scr[0:1, :]
    carry_scr[...] = jnp.broadcast_to(cum[tc - 1:tc, :], carry_scr.shape)

    pieces = _split3(cum)
    src = lax.broadcasted_iota(jnp.int32, (LANES, A_W), 0)
    dst = lax.broadcasted_iota(jnp.int32, (LANES, A_W), 1)
    lane = lax.broadcasted_iota(jnp.int32, (1, A_W), 1)
    one_q = jnp.zeros((1, A_W), F32)
    one_k = jnp.zeros((1, A_W), F32)
    for h in range(A_HEADS):
        base = _fox_xtra_base(h)
        one_q = jnp.where((lane >= base + 3) & (lane < base + FOX_XTRA), 1.0, one_q)
        one_k = jnp.where((lane >= base) & (lane < base + 3), 1.0, one_k)
    eq, ek = one_q, one_k
    for k, piece in enumerate(pieces):
        hit_q = functools.reduce(jnp.logical_or, [(src == h) & (dst == _fox_xtra_base(h) + k) for h in range(A_HEADS)])
        hit_k = functools.reduce(jnp.logical_or,
                                 [(src == h) & (dst == _fox_xtra_base(h) + 3 + k) for h in range(A_HEADS)])
        eq = eq + jnp.dot(piece, jnp.where(hit_q, 1.0, 0.0).astype(BF16), preferred_element_type=F32)
        ek = ek + jnp.dot(piece, jnp.where(hit_k, -1.0, 0.0).astype(BF16), preferred_element_type=F32)
    eq_ref[0] = eq.astype(BF16)
    ek_ref[0] = ek.astype(BF16)


def _cumsum(lf):
    B, L, _ = lf.shape
    tc = _tile(L, 512)
    return pl.pallas_call(
        _cumsum_kernel,
        grid=(B, L // tc),
        in_specs=[pl.BlockSpec((1, tc, LANES), lambda b, j: (b, j, 0))],
        out_specs=[pl.BlockSpec((1, tc, A_W), lambda b, j: (b, j, 0)),
                   pl.BlockSpec((1, tc, A_W), lambda b, j: (b, j, 0))],
        out_shape=[jax.ShapeDtypeStruct((B, L, A_W), BF16),
                   jax.ShapeDtypeStruct((B, L, A_W), BF16)],
        scratch_shapes=[pltpu.VMEM((8, LANES), F32)],
        compiler_params=_cparams("parallel", "arbitrary"),
        name="fox_cumsum",
    )(lf)


def _fox_kernel(q_ref, eq_ref, kb_ref, ek_ref, vb_ref, o_ref, qp_scr, m_scr, l_scr, acc_scr, *,
                tq, tk, offset, nk):
    i = pl.program_id(1)
    scale = A_HEAD_DIM ** -0.5
    q_lo = offset + i * tq

    def in_head(lane, h):
        return (lane >= h * A_HEAD_DIM) & (lane < (h + 1) * A_HEAD_DIM)

    def in_xtra(lane, h):
        return (lane >= _fox_xtra_base(h)) & (lane < _fox_xtra_base(h) + FOX_XTRA)

    lane_q = lax.broadcasted_iota(jnp.int32, (tq, A_W), 1)
    q = (q_ref[0] * scale).astype(BF16)
    eq = eq_ref[0]
    for h in range(A_HEADS):
        qp_scr[h] = jnp.where(in_xtra(lane_q, h), eq, jnp.where(in_head(lane_q, h), q, jnp.zeros((), BF16)))
    m_scr[...] = jnp.full(m_scr.shape, NEG_BIG, F32)
    l_scr[...] = jnp.zeros(l_scr.shape, F32)
    acc_scr[...] = jnp.zeros(acc_scr.shape, F32)

    lane_k = lax.broadcasted_iota(jnp.int32, (tk, A_W), 1)

    def block(jj, masked):
        ks = pl.multiple_of(jj * tk, tk)
        k = kb_ref[0, pl.ds(ks, tk), :]
        ek = ek_ref[0, pl.ds(ks, tk), :]
        v = vb_ref[0, pl.ds(ks, tk), :]
        if masked:
            kpos = ks + lax.broadcasted_iota(jnp.int32, (tk, tq), 0)
            qpos = q_lo + lax.broadcasted_iota(jnp.int32, (tk, tq), 1)
            vis = kpos <= qpos
        tqs = min(tq, FOX_TQ_SUB)
        ps, alphas = [], []
        for h in range(A_HEADS):
            kp = jnp.where(in_xtra(lane_k, h), ek, k)
            for qs in range(tq // tqs):
                cols = slice(qs * tqs, (qs + 1) * tqs)
                s = lax.dot_general(kp, qp_scr[h, cols, :], (((1,), (1,)), ((), ())),
                                    preferred_element_type=F32)
                if masked:
                    s = jnp.where(vis[:, cols], s, NEG_BIG)
                m_prev = m_scr[h, 0:1, cols]
                m_next = jnp.maximum(m_prev, jnp.max(s, axis=0, keepdims=True))
                alpha = jnp.exp(m_prev - m_next)
                p = jnp.exp(s - m_next)
                l_scr[h, :, cols] = jnp.broadcast_to(
                    alpha * l_scr[h, 0:1, cols] + jnp.sum(p, axis=0, keepdims=True), (8, tqs))
                m_scr[h, :, cols] = jnp.broadcast_to(m_next, (8, tqs))
                ps.append(p.astype(BF16))
                alphas.append(alpha)
        alphas = [jnp.concatenate(alphas[h * (tq // tqs):(h + 1) * (tq // tqs)], axis=1) for h in range(A_HEADS)]
        pv = lax.dot_general(v, jnp.concatenate(ps, axis=1), (((0,), (0,)), ((), ())),
                             preferred_element_type=F32)
        for h in range(A_HEADS):
            acc_scr[h] = acc_scr[h] * alphas[h] + pv[:, h * tq:(h + 1) * tq]

    n_full = (q_lo + 1) // tk
    n_vis = jnp.minimum((q_lo + tq - 1) // tk + 1, nk)

    def full_body(jj, carry):
        block(jj, False)
        return carry

    def edge_body(jj, carry):
        block(jj, True)
        return carry

    lax.fori_loop(0, n_full, full_body, 0)
    lax.fori_loop(n_full, n_vis, edge_body, 0)

    row = lax.broadcasted_iota(jnp.int32, (A_W, tq), 0)
    out_t = jnp.zeros((A_W, tq), F32)
    for h in range(A_HEADS):
        out_t = out_t + jnp.where(in_head(row, h), acc_scr[h] / l_scr[h, 0:1, :], 0.0)
    o_ref[0] = out_t.T.astype(o_ref.dtype)


def _fox(q_src, q_col, eq, kb, ek, vb, offset):
    B, Lq, _ = q_src.shape
    Lk = kb.shape[1]
    tq = _tile(Lq, FOX_TQ)
    tk = _tile(Lk, FOX_TK)
    assert tq % LANES == 0 and tk % LANES == 0
    nq, nk = Lq // tq, Lk // tk
    whole = lambda b, i: (b, 0, 0)
    kern = functools.partial(_fox_kernel, tq=tq, tk=tk, offset=offset, nk=nk)
    return pl.pallas_call(
        kern,
        grid=(B, nq),
        in_specs=[pl.BlockSpec((1, tq, A_W), lambda b, i: (b, i, q_col // A_W)),
                  pl.BlockSpec((1, tq, A_W), lambda b, i: (b, i, 0)),
                  pl.BlockSpec((1, Lk, A_W), whole),
                  pl.BlockSpec((1, Lk, A_W), whole),
                  pl.BlockSpec((1, Lk, A_W), whole)],
        out_specs=pl.BlockSpec((1, tq, A_W), lambda b, i: (b, i, 0)),
        out_shape=jax.ShapeDtypeStruct((B, Lq, A_W), BF16),
        scratch_shapes=[pltpu.VMEM((A_HEADS, tq, A_W), BF16),
                        pltpu.VMEM((A_HEADS, 8, tq), F32),
                        pltpu.VMEM((A_HEADS, 8, tq), F32),
                        pltpu.VMEM((A_HEADS, A_W, tq), F32)],
        compiler_params=_cparams("parallel", "arbitrary"),
        name="fox_attention",
    )(q_src, eq, kb, ek, vb)


def _ret_lane_head_v():
    return lax.broadcasted_iota(jnp.int32, (1, B_W), 1) // B_HEAD_DIM


def _ret_lane_head_qk():
    return (lax.broadcasted_iota(jnp.int32, (1, B_W), 1) % LANES) // (B_HEAD_DIM // 2)


def _log_gamma(h):
    return math.log1p(-2.0 ** (-5.0 - h))


def _ret_kernel(q_ref, k_ref, v_ref, g_ref, s0_ref, invf_ref, gret_ref, y_ref, sout_ref,
                state_scr, dec_scr, *, lc, offset, nchunks):
    c = pl.program_id(1)

    @pl.when(c == 0)
    def _():
        state_scr[...] = s0_ref[0]
        r = lax.broadcasted_iota(jnp.int32, (lc, lc), 0)
        s = lax.broadcasted_iota(jnp.int32, (lc, lc), 1)
        d = (r - s).astype(F32)
        for h in range(B_HEADS):
            dec_scr[h] = jnp.where(r >= s, jnp.exp(d * _log_gamma(h)), 0.0)

    hv = _ret_lane_head_v()
    hqk = _ret_lane_head_qk()
    lg_v = jnp.zeros((1, B_W), F32)
    lg_qk = jnp.zeros((1, B_W), F32)
    for h in range(B_HEADS):
        lg_v = jnp.where(hv == h, _log_gamma(h), lg_v)
        lg_qk = jnp.where(hqk == h, _log_gamma(h), lg_qk)

    pos = (offset + c * lc + lax.broadcasted_iota(jnp.int32, (lc, LANES), 0)).astype(F32)
    ang = pos * invf_ref[...]
    cos, sin = jnp.cos(ang), jnp.sin(ang)

    def rot(x):
        x1, x2 = x[:, :LANES], x[:, LANES:]
        return jnp.concatenate([x1 * cos - x2 * sin, x1 * sin + x2 * cos], axis=1)

    qr = rot(q_ref[0])
    kr = rot(k_ref[0]) * (B_HEAD_DIM ** -0.5)
    v = v_ref[0]
    vb = v.astype(BF16)
    krb = kr.astype(BF16)

    lpos = lax.broadcasted_iota(jnp.int32, (lc, B_W), 0).astype(F32)
    state = state_scr[...]
    y = jnp.dot(qr.astype(BF16), state.astype(BF16), preferred_element_type=F32) * jnp.exp((lpos + 1.0) * lg_v)
    for h in range(B_HEADS):
        qh = jnp.where(hqk == h, qr, 0.0).astype(BF16)
        sc = lax.dot_general(qh, krb, (((1,), (1,)), ((), ())), preferred_element_type=F32) * dec_scr[h]
        yh = jnp.dot(sc.astype(BF16), vb, preferred_element_type=F32)
        y = y + jnp.where(hv == h, yh, 0.0)

    kw = (kr * jnp.exp((lc - 1.0 - lpos) * lg_qk)).astype(BF16)
    upd = lax.dot_general(kw, vb, (((0,), (0,)), ((), ())), preferred_element_type=F32)
    row_head = (lax.broadcasted_iota(jnp.int32, (B_W, B_W), 0) % LANES) // (B_HEAD_DIM // 2)
    col_head = lax.broadcasted_iota(jnp.int32, (B_W, B_W), 1) // B_HEAD_DIM
    lg_rows = jnp.zeros((B_W, B_W), F32)
    for h in range(B_HEADS):
        lg_rows = jnp.where(row_head == h, _log_gamma(h), lg_rows)
    new_state = jnp.where(row_head == col_head, state * jnp.exp(lc * lg_rows) + upd, 0.0)
    state_scr[...] = new_state

    @pl.when(c == nchunks - 1)
    def _():
        sout_ref[0] = new_state

    ri = lax.broadcasted_iota(jnp.int32, (B_W, B_W), 0) // B_HEAD_DIM
    ci = lax.broadcasted_iota(jnp.int32, (B_W, B_W), 1) // B_HEAD_DIM
    avg = jnp.where(ri == ci, 1.0 / B_HEAD_DIM, 0.0).astype(BF16)
    yc = y - _dot_sel_rhs(y, avg)
    var = _dot_sel_rhs(yc * yc, avg)
    yn = yc * lax.rsqrt(var + EPS) * gret_ref[...]
    y_ref[0] = (jax.nn.silu(g_ref[0]) * yn).astype(y_ref.dtype)


def _retention(proj3, s0, invf, g_ret, offset):
    B, L, _ = proj3.shape
    lc = _tile(L, 256)
    nch = L // lc
    blk = lambda col: pl.BlockSpec((1, lc, B_W), lambda b, c: (b, c, col // B_W))
    kern = functools.partial(_ret_kernel, lc=lc, offset=offset, nchunks=nch)
    return pl.pallas_call(
        kern,
        grid=(B, nch),
        in_specs=[blk(COL_QB), blk(COL_KB), blk(COL_VB), blk(COL_GB),
                  pl.BlockSpec((1, B_W, B_W), lambda b, c: (b, 0, 0)),
                  pl.BlockSpec((1, LANES), lambda b, c: (0, 0)),
                  pl.BlockSpec((1, B_W), lambda b, c: (0, 0))],
        out_specs=[pl.BlockSpec((1, lc, B_W), lambda b, c: (b, c, 0)),
                   pl.BlockSpec((1, B_W, B_W), lambda b, c: (b, 0, 0))],
        out_shape=[jax.ShapeDtypeStruct((B, L, B_W), BF16),
                   jax.ShapeDtypeStruct((B, B_W, B_W), F32)],
        scratch_shapes=[pltpu.VMEM((B_W, B_W), F32),
                        pltpu.VMEM((B_HEADS, lc, lc), F32)],
        compiler_params=_cparams("parallel", "arbitrary"),
        name="retention",
    )(proj3, proj3, proj3, proj3, s0, invf, g_ret.reshape(1, B_W))


def _ret_state_to_kernel(s):
    B = s.shape[0]
    half = B_HEAD_DIM // 2
    s = s.reshape(B, B_HEADS, 2, half, B_HEAD_DIM)
    eye = jnp.eye(B_HEADS, dtype=s.dtype)
    full = jnp.einsum('bhkie,hg->bkhige', s, eye)
    return full.reshape(B, B_W, B_W)


def _ret_state_from_kernel(s):
    B = s.shape[0]
    half = B_HEAD_DIM // 2
    s = s.reshape(B, 2, B_HEADS, half, B_HEADS, B_HEAD_DIM)
    d = jnp.einsum('bkhihe->bhkie', s)
    return d.reshape(B, B_HEADS, B_HEAD_DIM, B_HEAD_DIM)


SSD_CHUNK = 256


def _ssd_kernel(z_ref, xbc_ref, dt_ref, s0_ref, cs0_ref, cw_ref, cb_ref, dtb_ref, alog_ref, dsk_ref, gs_ref,
                y_ref, sout_ref, state_scr, xp_scr, *, lc, nchunks):
    c = pl.program_id(1)
    PADR = 8

    @pl.when(c == 0)
    def _():
        state_scr[...] = s0_ref[0]
        xp_scr[0:PADR, :] = cs0_ref[0]

    xbc = xbc_ref[0]
    xp_scr[PADR:PADR + lc, :] = xbc
    cw = cw_ref[...]
    conv = xp_scr[PADR - 3:PADR - 3 + lc, :] * cw[0:1, :]
    for jj in range(1, C_CONV):
        conv = conv + xp_scr[PADR - 3 + jj:PADR - 3 + jj + lc, :] * cw[jj:jj + 1, :]
    act = jax.nn.silu(conv + cb_ref[...])
    xp_scr[0:PADR, :] = xp_scr[lc:lc + PADR, :]

    xs = act[:, :C_INNER]
    bm = act[:, C_INNER:C_INNER + G_W]
    cm = act[:, C_INNER + G_W:]
    dt = jax.nn.softplus(dt_ref[0] + dtb_ref[...])
    loga = dt * (-jnp.exp(alog_ref[...]))

    r = lax.broadcasted_iota(jnp.int32, (lc, lc), 0)
    s = lax.broadcasted_iota(jnp.int32, (lc, lc), 1)
    causal = r >= s
    tri = jnp.where(causal, 1.0, 0.0).astype(BF16)
    cum = _dot_sel(tri, loga)
    total = cum[lc - 1:lc, :]
    xdt = xs * dt
    xdtb = xdt.astype(BF16)
    bmb = bm.astype(BF16)
    cmb = cm.astype(BF16)
    state = state_scr[...]

    gw = C_INNER // C_GROUPS
    cum_t = [cum[:, kk * LANES:(kk + 1) * LANES].T for kk in range(C_INNER // LANES)]
    ys = []
    for g in range(C_GROUPS):
        cg = cmb[:, g * C_STATE:(g + 1) * C_STATE]
        bg = bmb[:, g * C_STATE:(g + 1) * C_STATE]
        scores = lax.dot_general(cg, bg, (((1,), (1,)), ((), ())), preferred_element_type=F32)
        y_g = jnp.dot(cg, state[:, g * gw:(g + 1) * gw].astype(BF16), preferred_element_type=F32)
        y_g = y_g * jnp.exp(cum[:, g * gw:(g + 1) * gw])
        lane_head = lax.broadcasted_iota(jnp.int32, (1, gw), 1) // C_HEAD_DIM
        xg = xdtb[:, g * gw:(g + 1) * gw]
        for hh in range(C_HEADS // C_GROUPS):
            col = g * gw + hh * C_HEAD_DIM
            cum_col = cum[:, col:col + 1]
            cum_row = cum_t[col // LANES][col % LANES:col % LANES + 1, :]
            dec = jnp.exp(jnp.where(causal, cum_col - cum_row, NEG_BIG))
            yh = jnp.dot((scores * dec).astype(BF16), xg, preferred_element_type=F32)
            y_g = y_g + jnp.where(lane_head == hh, yh, 0.0)
        ys.append(y_g)
        xw = (xdt[:, g * gw:(g + 1) * gw] * jnp.exp(total[:, g * gw:(g + 1) * gw] - cum[:, g * gw:(g + 1) * gw]))
        upd = lax.dot_general(bg, xw.astype(BF16), (((0,), (0,)), ((), ())), preferred_element_type=F32)
        state_scr[:, g * gw:(g + 1) * gw] = state[:, g * gw:(g + 1) * gw] * jnp.exp(total[:, g * gw:(g + 1) * gw]) + upd
    y = jnp.concatenate(ys, axis=1)

    @pl.when(c == nchunks - 1)
    def _():
        sout_ref[0] = state_scr[...]

    yc = (y + xs * dsk_ref[...]) * jax.nn.silu(z_ref[0])
    y_ref[0] = _rms(yc, gs_ref[...]).astype(y_ref.dtype)


def _ssd(proj3, s0, cs0, conv_w, conv_b, dtb_full, a_log, d_skip, g_ssm):
    B, L, _ = proj3.shape
    lc = _tile(L, SSD_CHUNK)
    nch = L // lc
    rep = lambda p: jnp.repeat(p, C_HEAD_DIM).reshape(1, C_INNER)
    kern = functools.partial(_ssd_kernel, lc=lc, nchunks=nch)
    cst = lambda shape: pl.BlockSpec(shape, lambda b, c: (0,) * len(shape))
    return pl.pallas_call(
        kern,
        grid=(B, nch),
        in_specs=[pl.BlockSpec((1, lc, C_INNER), lambda b, c: (b, c, COL_ZC // C_INNER)),
                  pl.BlockSpec((1, lc, C_CONV_DIM), lambda b, c: (b, c, COL_XBC // C_CONV_DIM)),
                  pl.BlockSpec((1, lc, C_INNER), lambda b, c: (b, c, COL_DT // C_INNER)),
                  pl.BlockSpec((1, C_STATE, C_INNER), lambda b, c: (b, 0, 0)),
                  pl.BlockSpec((1, 8, C_CONV_DIM), lambda b, c: (b, 0, 0)),
                  cst((C_CONV, C_CONV_DIM)), cst((1, C_CONV_DIM)), cst((1, C_INNER)), cst((1, C_INNER)),
                  cst((1, C_INNER)), cst((1, C_INNER))],
        out_specs=[pl.BlockSpec((1, lc, C_INNER), lambda b, c: (b, c, 0)),
                   pl.BlockSpec((1, C_STATE, C_INNER), lambda b, c: (b, 0, 0))],
        out_shape=[jax.ShapeDtypeStruct((B, L, C_INNER), BF16),
                   jax.ShapeDtypeStruct((B, C_STATE, C_INNER), F32)],
        scratch_shapes=[pltpu.VMEM((C_STATE, C_INNER), F32),
                        pltpu.VMEM((lc + 8, C_CONV_DIM), F32)],
        compiler_params=_cparams("parallel", "arbitrary"),
        name="ssd",
    )(proj3, proj3, proj3, s0, cs0, conv_w, conv_b.reshape(1, C_CONV_DIM), dtb_full, rep(a_log), rep(d_skip),
      g_ssm.reshape(1, C_INNER))


PEER_SUB = 128
PEER_STEP = 2048
PEER_CAND_COLS = 4


def _gelu_tanh(x):
    k0 = math.sqrt(2.0 / math.pi)
    inner = x * (k0 + (k0 * 0.044715) * (x * x))
    return (0.5 * x) * (1.0 + jnp.tanh(inner))


def _sort_pairs(n):
    def merge(lo, hi, r):
        step = r * 2
        if step < hi - lo:
            yield from merge(lo, hi, step)
            yield from merge(lo + r, hi, step)
            yield from [(i, i + r) for i in range(lo + r, hi - r, step)]
        else:
            yield (lo, lo + r)

    def sort(lo, hi):
        if hi - lo >= 1:
            mid = lo + (hi - lo) // 2
            yield from sort(lo, mid)
            yield from sort(mid + 1, hi)
            yield from merge(lo, hi, 1)

    return tuple(sort(0, n - 1))


def _top16_values(x):
    v = [x[8 * g:8 * g + 8, :] for g in range(PEER_NKEYS // 8)]
    for a, b in _sort_pairs(PEER_NKEYS // 8):
        v[a], v[b] = jnp.maximum(v[a], v[b]), jnp.minimum(v[a], v[b])
    vals = []
    for r in range(PEER_TOPK):
        m = jnp.max(v[0], axis=0, keepdims=True)
        vals.append(m)
        hit = v[0] == m
        for k in range(PEER_TOPK - 1 - r):
            v[k] = jnp.where(hit, v[k + 1], v[k])
    return jnp.concatenate(vals, axis=0)


def _peer_select(s1, s2):
    n = s1.shape[1]
    a16 = _top16_values(s1)
    b16 = _top16_values(s2)
    rank2 = jnp.zeros(s2.shape, F32)
    for r in range(PEER_TOPK):
        rank2 = rank2 + jnp.where(b16[r:r + 1, :] > s2, 1.0, 0.0)
    ridx = lax.broadcasted_iota(jnp.int32, (PEER_TOPK, n), 0)
    sub = lax.broadcasted_iota(jnp.int32, (8, n), 0)
    lo, hi = slice(0, 8), slice(8, PEER_TOPK)
    col = lambda r2, rows: a16[rows, :] + b16[r2:r2 + 1, :]
    row = lambda r1, cols: b16[cols, :] + a16[r1:r1 + 1, :]
    assert PEER_CAND_COLS == 4 and PEER_TOPK == 16
    groups = [col(0, lo), col(0, hi), col(1, lo),
              jnp.where(sub <= 4, col(2, lo), NEG_BIG),
              jnp.where(sub <= 3, col(3, lo), row(1, lo)),
              jnp.where(sub >= 4, row(0, lo), NEG_BIG),
              row(0, hi),
              jnp.where(sub == 4, row(2, lo), NEG_BIG)]
    cand = jnp.concatenate(groups, axis=0)
    v = list(groups)
    for a_i, b_i in _sort_pairs(8):
        v[a_i], v[b_i] = jnp.maximum(v[a_i], v[b_i]), jnp.minimum(v[a_i], v[b_i])
    tau = None
    for r in range(PEER_TOPK):
        tau = jnp.max(v[0], axis=0, keepdims=True)
        hit = v[0] == tau
        depth = min(8, PEER_TOPK - r)
        for k in range(depth - 1):
            v[k] = jnp.where(hit, v[k + 1], v[k])
        if depth == 8:
            v[7] = jnp.where(hit, NEG_BIG, v[7])
    top = a16[0:1, :] + b16[0:1, :]
    z = jnp.sum(jnp.where(cand >= tau, jnp.exp(cand - top), 0.0), axis=0, keepdims=True)
    cnt = jnp.zeros(s1.shape, F32)
    for r2 in range(PEER_CAND_COLS):
        cnt = cnt + jnp.where(s1 + b16[r2:r2 + 1, :] >= tau, 1.0, 0.0)
    for r1 in range(PEER_TOPK // (PEER_CAND_COLS + 1)):
        a_r = a16[r1:r1 + 1, :]
        tail = jnp.where((ridx >= PEER_CAND_COLS) & (b16 + a_r >= tau), 1.0, 0.0)
        cnt = cnt + jnp.where(s1 == a_r, jnp.sum(tail, axis=0, keepdims=True), 0.0)
    a = jnp.exp(s1 - a16[0:1, :]) / z
    b = jnp.exp(s2 - b16[0:1, :])
    return a, cnt, rank2, b


def _peer_kernel(h_ref, ya_ref, yb_ref, yc_ref, wo_ref, g_ref, wq_ref, k1_ref, k2_ref, u_ref, vt_ref, o_ref,
                 xt_scr, a_scr, cnt_scr, rank_scr, b_scr, acc_scr, h1_scr, *, tm, te, ne):
    j = pl.program_id(1)
    n1 = te // PEER_NKEYS
    nchunk = tm // LANES
    nsub = te // PEER_SUB
    slabs = PEER_SUB // PEER_NKEYS

    def gate_weights(sb):
        ws = []
        for il in range(slabs):
            i1 = j * n1 + sb * slabs + il
            w = jnp.zeros((PEER_NKEYS, tm), BF16)
            for h in range(PEER_HEADS):
                row = lambda ref: jnp.concatenate(
                    [jnp.broadcast_to(ref[h, cch, pl.ds(i1, 1), :], (BF16_ROWS, LANES)) for cch in range(nchunk)],
                    axis=1).astype(BF16)
                cnt_b = jnp.tile(row(cnt_scr), (PEER_NKEYS // BF16_ROWS, 1))
                a_b = jnp.tile(row(a_scr), (PEER_NKEYS // BF16_ROWS, 1))
                w = w + jnp.where(rank_scr[h] < cnt_b, b_scr[h], jnp.zeros((), BF16)) * a_b
            ws.append(w)
        return jnp.concatenate(ws, axis=0)

    @pl.when(j == 0)
    def _():
        h1 = h_ref[...]
        h1 = h1 + jnp.dot(ya_ref[...], wo_ref[0:A_W, :], preferred_element_type=F32)
        h1 = h1 + jnp.dot(yb_ref[...], wo_ref[A_W:A_W + B_W, :], preferred_element_type=F32)
        h1 = h1 + jnp.dot(yc_ref[...], wo_ref[A_W + B_W:, :], preferred_element_type=F32)
        h1_scr[...] = h1
        xn = _rms(h1, g_ref[...])
        xt_scr[...] = xn.T.astype(BF16)
        acc_scr[...] = jnp.zeros(acc_scr.shape, F32)

        def head_body(h, carry):
            base = pl.multiple_of(h * PEER_KEY_DIM, PEER_KEY_DIM)
            k1 = k1_ref[h].astype(BF16)
            k2 = k2_ref[h].astype(BF16)
            qh = jnp.dot(wq_ref[pl.ds(base, PEER_KEY_DIM), :], xt_scr[...], preferred_element_type=F32)
            for cch in range(nchunk):
                sl = slice(cch * LANES, (cch + 1) * LANES)
                q1 = qh[0:PEER_HALF, sl].astype(BF16)
                q2 = qh[PEER_HALF:, sl].astype(BF16)
                s1 = jnp.dot(k1, q1, preferred_element_type=F32)
                s2 = jnp.dot(k2, q2, preferred_element_type=F32)
                a, cnt, rank2, b = _peer_select(s1, s2)
                a_scr[h, cch] = a
                cnt_scr[h, cch] = cnt
                rank_scr[h, :, sl] = rank2.astype(BF16)
                b_scr[h, :, sl] = b.astype(BF16)
            return carry

        lax.fori_loop(0, PEER_HEADS, head_body, 0)

    acts = []
    for sb in range(nsub):
        ht = jnp.dot(u_ref[sb * PEER_SUB:(sb + 1) * PEER_SUB, :], xt_scr[...], preferred_element_type=F32)
        acts.append(_gelu_tanh(ht.astype(BF16)) * gate_weights(sb))
    acc_scr[...] += jnp.dot(vt_ref[...], jnp.concatenate(acts, axis=0), preferred_element_type=F32)

    @pl.when(j == ne - 1)
    def _():
        o_ref[...] = h1_scr[...] + acc_scr[...].T


def _layer_cast_kernel(x_ref, o_ref, *, transpose):
    x = x_ref[...]
    o_ref[...] = (x.T if transpose else x).astype(o_ref.dtype)


def _layer_cast(x_all, layer, dtype, transpose):
    _, R, C = x_all.shape
    tr = _tile(R, 512)
    return pl.pallas_call(
        functools.partial(_layer_cast_kernel, transpose=transpose),
        grid=(R // tr,),
        in_specs=[pl.BlockSpec((None, tr, C), lambda i: (layer, i, 0))],
        out_specs=pl.BlockSpec((C, tr), lambda i: (0, i)) if transpose else pl.BlockSpec((tr, C), lambda i: (i, 0)),
        out_shape=jax.ShapeDtypeStruct((C, R) if transpose else (R, C), dtype),
        compiler_params=_cparams("parallel"),
        name="layer_cast",
    )(x_all)


def _peer(h2d, ya, yb, yc, w_out_bf16, g, wq_t, k1, k2, u_bf16, vt_bf16):
    T = h2d.shape[0]
    row = lambda w: pl.BlockSpec((tm, w), lambda i, j: (i, 0))
    tm = _tile(T, 512)
    assert tm % LANES == 0
    te = PEER_STEP
    ne = PEER_EXPERTS // te
    kern = functools.partial(_peer_kernel, tm=tm, te=te, ne=ne)
    hk = (PEER_HEADS, PEER_NKEYS, tm)
    hkc = (PEER_HEADS, tm // LANES, PEER_NKEYS, LANES)
    return pl.pallas_call(
        kern,
        grid=(T // tm, ne),
        in_specs=[row(D_MODEL), row(A_W), row(B_W), row(C_INNER),
                  pl.BlockSpec((D_MODEL, D_MODEL), lambda i, j: (0, 0)),
                  pl.BlockSpec((1, D_MODEL), lambda i, j: (0, 0)),
                  pl.BlockSpec((PEER_HEADS * PEER_KEY_DIM, D_MODEL), lambda i, j: (0, 0)),
                  pl.BlockSpec((PEER_HEADS, PEER_NKEYS, PEER_HALF), lambda i, j: (0, 0, 0)),
                  pl.BlockSpec((PEER_HEADS, PEER_NKEYS, PEER_HALF), lambda i, j: (0, 0, 0)),
                  pl.BlockSpec((te, D_MODEL), lambda i, j: (j, 0)),
                  pl.BlockSpec((D_MODEL, te), lambda i, j: (0, j))],
        out_specs=pl.BlockSpec((tm, D_MODEL), lambda i, j: (i, 0)),
        out_shape=jax.ShapeDtypeStruct((T, D_MODEL), F32),
        scratch_shapes=[pltpu.VMEM((D_MODEL, tm), BF16),
                        pltpu.VMEM(hkc, F32), pltpu.VMEM(hkc, F32), pltpu.VMEM(hk, BF16), pltpu.VMEM(hk, BF16),
                        pltpu.VMEM((D_MODEL, tm), F32),
                        pltpu.VMEM((tm, D_MODEL), F32)],
        compiler_params=_cparams("parallel", "arbitrary"),
        name="peer",
    )(h2d, ya, yb, yc, w_out_bf16, g.reshape(1, D_MODEL), wq_t, k1, k2, u_bf16, vt_bf16)


def _ple_kernel(h_ref, p_ref, g_ref, wg_ref, wp_ref, gf_ref, o_ref, *, final_norm):
    h = h_ref[...]
    xn = _rms(h, g_ref[...])
    gate = jax.nn.sigmoid(jnp.dot(xn.astype(BF16), wg_ref[...], preferred_element_type=F32))
    emb = jnp.dot(p_ref[...].astype(BF16), wp_ref[...], preferred_element_type=F32)
    out = h + emb * gate
    if final_norm:
        out = _rms(out, gf_ref[...])
    o_ref[...] = out


def _ple(h2d, p_all, layer, g, wg_bf16, wp_bf16, g_final, final_norm):
    T = h2d.shape[0]
    tm = _tile(T, 512)
    kern = functools.partial(_ple_kernel, final_norm=final_norm)
    return pl.pallas_call(
        kern,
        grid=(T // tm,),
        in_specs=[pl.BlockSpec((tm, D_MODEL), lambda i: (i, 0)),
                  pl.BlockSpec((None, tm, PLE_DIM), lambda i: (layer, i, 0)),
                  pl.BlockSpec((1, D_MODEL), lambda i: (0, 0)),
                  pl.BlockSpec((D_MODEL, D_MODEL), lambda i: (0, 0)),
                  pl.BlockSpec((PLE_DIM, D_MODEL), lambda i: (0, 0)),
                  pl.BlockSpec((1, D_MODEL), lambda i: (0, 0))],
        out_specs=pl.BlockSpec((tm, D_MODEL), lambda i: (i, 0)),
        out_shape=jax.ShapeDtypeStruct((T, D_MODEL), F32),
        compiler_params=_cparams("parallel"),
        name="ple",
    )(h2d, p_all, g.reshape(1, D_MODEL), wg_bf16, wp_bf16, g_final.reshape(1, D_MODEL))


def _pad_to(x, n, axis):
    pad = n - x.shape[axis]
    if pad == 0:
        return x
    widths = [(0, 0)] * x.ndim
    widths[axis] = (0, pad)
    return jnp.pad(x, widths)


def _prep_layer_weights(weights, i):
    (g_mix, w_in, b_forget, g_ret, conv_w, conv_b, dt_bias, a_log, d_skip, g_ssm, w_out,
     g_ffn, peer_wq, peer_k1, peer_k2, peer_u, peer_v, g_ple, w_ple_gate, w_ple, g_final) = weights
    w_proj, bf_pad, dtb_full = _prep_w_in(w_in, i, b_forget[i], dt_bias[i])
    return dict(w_proj=w_proj, bf_pad=bf_pad, dtb_full=dtb_full, w_out=w_out[i].astype(BF16),
                wq_t=_layer_cast(peer_wq, i, BF16, True), u=_layer_cast(peer_u, i, BF16, False),
                vt=_layer_cast(peer_v, i, BF16, True),
                wg=w_ple_gate[i].astype(BF16), wp=w_ple[i].astype(BF16))


def _run_trunk(x, p, past, weights, prepped):
    (g_mix, w_in, b_forget, g_ret, conv_w, conv_b, dt_bias, a_log, d_skip, g_ssm, w_out,
     g_ffn, peer_wq, peer_k1, peer_k2, peer_u, peer_v, g_ple, w_ple_gate, w_ple, g_final) = weights
    B, L, _ = x.shape
    depth = w_in.shape[0]
    T = B * L
    P = 0 if past is None else past[0].shape[2]
    half = B_HEAD_DIM // 2
    invf = jnp.tile(ROPE_BASE ** (-jnp.arange(half, dtype=F32) / half), LANES // half).reshape(1, LANES)

    h = x.reshape(T, D_MODEL)
    outs = []
    for i in range(depth):
        pw = prepped[i]
        dtb_full = pw["dtb_full"]
        proj, lf, kb, vb, k_new, v_new = _inproj(h, g_mix[i], pw["w_proj"], pw["bf_pad"])
        proj3 = proj.reshape(B, L, PROJ_W)
        lf3 = lf.reshape(B, L, LANES)
        kb = kb.reshape(B, L, A_W)
        vb = vb.reshape(B, L, A_W)

        if past is None:
            eq, ek = _cumsum(lf3)
            ya = _fox(proj3, COL_QA, eq, kb, ek, vb, 0)
            rs0 = jnp.zeros((B, B_W, B_W), F32)
            ss0 = jnp.zeros((B, C_STATE, C_INNER), F32)
            cs0 = jnp.zeros((B, 8, C_CONV_DIM), F32)
        else:
            lk_pad = -(-(P + L) // FOX_TK) * FOX_TK
            lq_pad = -(-L // LANES) * LANES
            past_k = past[0][i].reshape(B, P, A_W).astype(BF16)
            past_v = past[1][i].reshape(B, P, A_W).astype(BF16)
            past_lf = _pad_to(past[2][i], LANES, 2)
            k_all = _pad_to(jnp.concatenate([past_k, kb], axis=1), lk_pad, 1)
            v_all = _pad_to(jnp.concatenate([past_v, vb], axis=1), lk_pad, 1)
            lf_all = _pad_to(jnp.concatenate([past_lf, lf3], axis=1), lk_pad, 1)
            eq, ek = _cumsum(lf_all)
            q_pad = _pad_to(proj3[:, :, COL_QA:COL_QA + A_W], lq_pad, 1)
            ya = _fox(q_pad, 0, _pad_to(eq[:, P:P + L], lq_pad, 1), k_all, ek, v_all, P)[:, :L]
            rs0 = _ret_state_to_kernel(past[3][i])
            ss0 = jnp.transpose(past[4][i], (0, 2, 1, 3)).reshape(B, C_STATE, C_INNER)
            cs0 = jnp.concatenate([jnp.zeros((B, 8 - (C_CONV - 1), C_CONV_DIM), F32), past[5][i]], axis=1)

        yb, ret_k = _retention(proj3, rs0, invf, g_ret[i], P)
        ret_new = _ret_state_from_kernel(ret_k)

        yc, ssm_k = _ssd(proj3, ss0, cs0, conv_w[i], conv_b[i], dtb_full, a_log[i], d_skip[i], g_ssm[i])
        ssm_new = jnp.transpose(ssm_k.reshape(B, C_STATE, C_HEADS, C_HEAD_DIM), (0, 2, 1, 3))
        xbc = proj3[:, :, COL_XBC:COL_XBC + C_CONV_DIM]
        if L >= C_CONV - 1:
            conv_new = xbc[:, L - (C_CONV - 1):]
        else:
            prev = jnp.zeros((B, C_CONV - 1, C_CONV_DIM), F32) if past is None else past[5][i]
            conv_new = jnp.concatenate([prev, xbc], axis=1)[:, -(C_CONV - 1):]

        h = _peer(h, ya.reshape(T, A_W), yb.reshape(T, B_W), yc.reshape(T, C_INNER), pw["w_out"],
                  g_ffn[i], pw["wq_t"], peer_k1[i], peer_k2[i], pw["u"], pw["vt"])
        h = _ple(h, p.reshape(depth, T, PLE_DIM), i, g_ple[i], pw["wg"], pw["wp"], g_final,
                 final_norm=(i == depth - 1))
        outs.append((k_new.reshape(B, L, A_HEADS, A_HEAD_DIM), v_new.reshape(B, L, A_HEADS, A_HEAD_DIM),
                     lf3[:, :, :A_HEADS], ret_new, ssm_new, conv_new))
    stacked = tuple(jnp.stack([o[n] for o in outs]) for n in range(6))
    return h.reshape(B, L, D_MODEL), stacked


def kernel(x_prompt, x_sample, cache_k_fox, cache_v_fox, cache_logf_fox, state_ret, state_ssm, state_conv, p_prompt, p_sample, g_mix, w_in, b_forget, g_ret, conv_w, conv_b, dt_bias, a_log, d_skip, g_ssm, w_out, g_ffn, peer_wq, peer_k1, peer_k2, peer_u, peer_v, g_ple, w_ple_gate, w_ple, g_final):
    weights = (g_mix, w_in, b_forget, g_ret, conv_w, conv_b, dt_bias, a_log, d_skip, g_ssm, w_out,
               g_ffn, peer_wq, peer_k1, peer_k2, peer_u, peer_v, g_ple, w_ple_gate, w_ple, g_final)
    prepped = [_prep_layer_weights(weights, i) for i in range(w_in.shape[0])]
    y_prompt, (k_p, v_p, lf_p, ret_p, ssm_p, conv_p) = _run_trunk(x_prompt, p_prompt, None, weights, prepped)
    past = (cache_k_fox, cache_v_fox, cache_logf_fox, state_ret, state_ssm, state_conv)
    y_sample, (k_s, v_s, lf_s, ret_s, ssm_s, conv_s) = _run_trunk(x_sample, p_sample, past, weights, prepped)
    return (y_prompt, y_sample, k_p, v_p, lf_p, ret_p, ssm_p, conv_p, k_s, v_s, lf_s, ret_s, ssm_s, conv_s)
```

```python
import functools
import math

import jax
import jax.numpy as jnp
from jax import lax
from jax.experimental import pallas as pl
from jax.experimental.pallas import tpu as pltpu

F32 = jnp.float32
BF16 = jnp.bfloat16

D_MODEL = 1024
PLE_DIM = 256
EPS = 1e-6
A_HEADS, A_HEAD_DIM = 4, 64
B_HEADS, B_HEAD_DIM = 4, 64
ROPE_BASE = 10000.0
C_HEADS, C_HEAD_DIM = 8, 64
C_INNER = C_HEADS * C_HEAD_DIM
C_GROUPS, C_STATE, C_CONV = 2, 128, 4
C_CONV_DIM = C_INNER + 2 * C_GROUPS * C_STATE
PEER_HEADS, PEER_NKEYS, PEER_KEY_DIM, PEER_TOPK = 8, 128, 256, 16
PEER_HALF = PEER_KEY_DIM // 2
PEER_EXPERTS = PEER_NKEYS * PEER_NKEYS

LANES = 128
BF16_ROWS = 16
A_W = A_HEADS * A_HEAD_DIM
B_W = B_HEADS * B_HEAD_DIM
G_W = C_GROUPS * C_STATE

COL_QA = 0
COL_KA = COL_QA + A_W
COL_VA = COL_KA + A_W
COL_QB = COL_VA + A_W
COL_KB = COL_QB + B_W
COL_VB = COL_KB + B_W
COL_GB = COL_VB + B_W
COL_FA = COL_GB + B_W
COL_ZC = COL_FA + 2 * LANES
COL_DT = COL_ZC + C_INNER
COL_XBC = COL_DT + C_INNER
PROJ_W = COL_XBC + C_CONV_DIM
assert all(c % A_W == 0 for c in (COL_QA, COL_KA, COL_VA, COL_QB, COL_KB, COL_VB, COL_GB))
assert COL_ZC % C_INNER == 0 and COL_DT % C_INNER == 0 and COL_XBC % C_CONV_DIM == 0 and COL_FA % LANES == 0

NEG_BIG = -1e30
VMEM_LIMIT_BYTES = 56 * 1024 * 1024


def _cparams(*sem):
    return pltpu.CompilerParams(dimension_semantics=sem, vmem_limit_bytes=VMEM_LIMIT_BYTES)


def _tile(n, pref):
    t = min(n, pref)
    while n % t:
        t -= 8
    assert t > 0
    return t


def _rms(x, g):
    return x * lax.rsqrt(jnp.mean(x * x, axis=-1, keepdims=True) + EPS) * g


def _split3(x):
    hi = x.astype(BF16)
    r1 = x - hi.astype(F32)
    mid = r1.astype(BF16)
    lo = (r1 - mid.astype(F32)).astype(BF16)
    return hi, mid, lo


def _dot_sel(sel_bf16, x):
    hi, mid, lo = _split3(x)
    d = lambda p: jnp.dot(sel_bf16, p, preferred_element_type=F32)
    return d(hi) + d(mid) + d(lo)


def _dot_sel_rhs(x, sel_bf16):
    hi, mid, lo = _split3(x)
    d = lambda p: jnp.dot(p, sel_bf16, preferred_element_type=F32)
    return d(hi) + d(mid) + d(lo)


def _inproj_kernel(x_ref, g_ref, w_ref, bf_ref, o_ref, lf_ref, kb_ref, vb_ref, kf_ref, vf_ref):
    xn = _rms(x_ref[...], g_ref[...])
    y = jnp.dot(xn.astype(BF16), w_ref[...], preferred_element_type=F32)
    o_ref[...] = y
    kf_ref[...] = y[:, COL_KA:COL_KA + A_W]
    vf_ref[...] = y[:, COL_VA:COL_VA + A_W]
    kb_ref[...] = y[:, COL_KA:COL_KA + A_W].astype(BF16)
    vb_ref[...] = y[:, COL_VA:COL_VA + A_W].astype(BF16)
    fa = y[:, COL_FA:COL_FA + LANES] + bf_ref[...]
    lane = lax.broadcasted_iota(jnp.int32, fa.shape, 1)
    lf_ref[...] = jnp.where(lane < A_HEADS, jax.nn.log_sigmoid(fa), 0.0)


def _inproj(x2d, g, w, bf_pad):
    T = x2d.shape[0]
    tm = _tile(T, 256)
    return pl.pallas_call(
        _inproj_kernel,
        grid=(T // tm,),
        in_specs=[pl.BlockSpec((tm, D_MODEL), lambda i: (i, 0)),
                  pl.BlockSpec((1, D_MODEL), lambda i: (0, 0)),
                  pl.BlockSpec((D_MODEL, PROJ_W), lambda i: (0, 0)),
                  pl.BlockSpec((1, LANES), lambda i: (0, 0))],
        out_specs=[pl.BlockSpec((tm, PROJ_W), lambda i: (i, 0)),
                   pl.BlockSpec((tm, LANES), lambda i: (i, 0)),
                   pl.BlockSpec((tm, A_W), lambda i: (i, 0)),
                   pl.BlockSpec((tm, A_W), lambda i: (i, 0)),
                   pl.BlockSpec((tm, A_W), lambda i: (i, 0)),
                   pl.BlockSpec((tm, A_W), lambda i: (i, 0))],
        out_shape=[jax.ShapeDtypeStruct((T, PROJ_W), F32),
                   jax.ShapeDtypeStruct((T, LANES), F32),
                   jax.ShapeDtypeStruct((T, A_W), BF16),
                   jax.ShapeDtypeStruct((T, A_W), BF16),
                   jax.ShapeDtypeStruct((T, A_W), F32),
                   jax.ShapeDtypeStruct((T, A_W), F32)],
        compiler_params=_cparams("parallel"),
        name="inproj",
    )(x2d, g.reshape(1, D_MODEL), w, bf_pad)


SRC_QA = 0
SRC_KA = SRC_QA + A_W
SRC_VA = SRC_KA + A_W
SRC_FA = SRC_VA + A_W
SRC_QB = SRC_FA + A_HEADS
SRC_KB = SRC_QB + B_W
SRC_VB = SRC_KB + B_W
SRC_GB = SRC_VB + B_W
SRC_ZC = SRC_GB + B_W
SRC_XBC = SRC_ZC + C_INNER
SRC_DT = SRC_XBC + C_CONV_DIM
IN_WIDTH = SRC_DT + C_HEADS


def _reorder_w_in_kernel(x_ref, o_ref):
    rows = x_ref.shape[0]

    def copy(dst, src, width):
        o_ref[:, dst:dst + width] = x_ref[:, src:src + width].astype(BF16)

    copy(COL_QA, SRC_QA, 3 * A_W)
    copy(COL_VB, SRC_VB, 2 * B_W)
    copy(COL_ZC, SRC_ZC, C_INNER)
    copy(COL_XBC, SRC_XBC, C_CONV_DIM)
    half = B_HEAD_DIM // 2
    for dst, src in ((COL_QB, SRC_QB), (COL_KB, SRC_KB)):
        for k in range(2):
            for h in range(B_HEADS):
                copy(dst + k * LANES + h * half, src + h * B_HEAD_DIM + k * half, half)
    o_ref[:, COL_FA:COL_FA + 2 * LANES] = jnp.zeros((rows, 2 * LANES), BF16)
    copy(COL_FA, SRC_FA, A_HEADS)
    for h in range(C_HEADS):
        col = x_ref[:, SRC_DT + h:SRC_DT + h + 1].astype(BF16)
        o_ref[:, COL_DT + h * C_HEAD_DIM:COL_DT + (h + 1) * C_HEAD_DIM] = jnp.broadcast_to(col, (rows, C_HEAD_DIM))


def _prep_w_in(w_in_all, layer, b_forget, dt_bias):
    assert w_in_all.shape[1:] == (D_MODEL, IN_WIDTH)
    tr = 256
    w = pl.pallas_call(
        _reorder_w_in_kernel,
        grid=(D_MODEL // tr,),
        in_specs=[pl.BlockSpec((None, tr, IN_WIDTH), lambda i: (layer, i, 0))],
        out_specs=pl.BlockSpec((tr, PROJ_W), lambda i: (i, 0)),
        out_shape=jax.ShapeDtypeStruct((D_MODEL, PROJ_W), BF16),
        compiler_params=_cparams("parallel"),
        name="reorder_w_in",
    )(w_in_all)
    bf_pad = jnp.concatenate([b_forget, jnp.zeros((LANES - A_HEADS,), F32)]).reshape(1, LANES)
    dtb_full = jnp.repeat(dt_bias, C_HEAD_DIM).reshape(1, C_INNER)
    return w, bf_pad, dtb_full


FOX_TQ = 512
FOX_TK = 512
FOX_TQ_SUB = 512
FOX_XTRA = 6


def _fox_xtra_base(h):
    return A_HEAD_DIM * ((h + 1) % A_HEADS)


def _cumsum_kernel(lf_ref, eq_ref, ek_ref, carry_scr):
    @pl.when(pl.program_id(1) == 0)
    def _():
        carry_scr[...] = jnp.zeros_like(carry_scr)

    lf = lf_ref[0]
    tc = lf.shape[0]
    r = lax.broadcasted_iota(jnp.int32, (tc, tc), 0)
    c = lax.broadcasted_iota(jnp.int32, (tc, tc), 1)
    tri = jnp.where(r >= c, 1.0, 0.0).astype(BF16)
    cum = _dot_sel(tri, lf) + carry_scr[0:1, :]
    carry_scr[...] = jnp.broadcast_to(cum[tc - 1:tc, :], carry_scr.shape)

    pieces = _split3(cum)
    src = lax.broadcasted_iota(jnp.int32, (LANES, A_W), 0)
    dst = lax.broadcasted_iota(jnp.int32, (LANES, A_W), 1)
    lane = lax.broadcasted_iota(jnp.int32, (1, A_W), 1)
    one_q = jnp.zeros((1, A_W), F32)
    one_k = jnp.zeros((1, A_W), F32)
    for h in range(A_HEADS):
        base = _fox_xtra_base(h)
        one_q = jnp.where((lane >= base + 3) & (lane < base + FOX_XTRA), 1.0, one_q)
        one_k = jnp.where((lane >= base) & (lane < base + 3), 1.0, one_k)
    eq, ek = one_q, one_k
    for k, piece in enumerate(pieces):
        hit_q = functools.reduce(jnp.logical_or, [(src == h) & (dst == _fox_xtra_base(h) + k) for h in range(A_HEADS)])
        hit_k = functools.reduce(jnp.logical_or,
                                 [(src == h) & (dst == _fox_xtra_base(h) + 3 + k) for h in range(A_HEADS)])
        eq = eq + jnp.dot(piece, jnp.where(hit_q, 1.0, 0.0).astype(BF16), preferred_element_type=F32)
        ek = ek + jnp.dot(piece, jnp.where(hit_k, -1.0, 0.0).astype(BF16), preferred_element_type=F32)
    eq_ref[0] = eq.astype(BF16)
    ek_ref[0] = ek.astype(BF16)


def _cumsum(lf):
    B, L, _ = lf.shape
    tc = _tile(L, 512)
    return pl.pallas_call(
        _cumsum_kernel,
        grid=(B, L // tc),
        in_specs=[pl.BlockSpec((1, tc, LANES), lambda b, j: (b, j, 0))],
        out_specs=[pl.BlockSpec((1, tc, A_W), lambda b, j: (b, j, 0)),
                   pl.BlockSpec((1, tc, A_W), lambda b, j: (b, j, 0))],
        out_shape=[jax.ShapeDtypeStruct((B, L, A_W), BF16),
                   jax.ShapeDtypeStruct((B, L, A_W), BF16)],
        scratch_shapes=[pltpu.VMEM((8, LANES), F32)],
        compiler_params=_cparams("parallel", "arbitrary"),
        name="fox_cumsum",
    )(lf)


def _fox_kernel(q_ref, eq_ref, kb_ref, ek_ref, vb_ref, o_ref, qp_scr, m_scr, l_scr, acc_scr, *,
                tq, tk, offset, nk):
    i = pl.program_id(1)
    scale = A_HEAD_DIM ** -0.5
    q_lo = offset + i * tq

    def in_head(lane, h):
        return (lane >= h * A_HEAD_DIM) & (lane < (h + 1) * A_HEAD_DIM)

    def in_xtra(lane, h):
        return (lane >= _fox_xtra_base(h)) & (lane < _fox_xtra_base(h) + FOX_XTRA)

    lane_q = lax.broadcasted_iota(jnp.int32, (tq, A_W), 1)
    q = (q_ref[0] * scale).astype(BF16)
    eq = eq_ref[0]
    for h in range(A_HEADS):
        qp_scr[h] = jnp.where(in_xtra(lane_q, h), eq, jnp.where(in_head(lane_q, h), q, jnp.zeros((), BF16)))
    m_scr[...] = jnp.full(m_scr.shape, NEG_BIG, F32)
    l_scr[...] = jnp.zeros(l_scr.shape, F32)
    acc_scr[...] = jnp.zeros(acc_scr.shape, F32)

    lane_k = lax.broadcasted_iota(jnp.int32, (tk, A_W), 1)

    def block(jj, masked):
        ks = pl.multiple_of(jj * tk, tk)
        k = kb_ref[0, pl.ds(ks, tk), :]
        ek = ek_ref[0, pl.ds(ks, tk), :]
        v = vb_ref[0, pl.ds(ks, tk), :]
        if masked:
            kpos = ks + lax.broadcasted_iota(jnp.int32, (tk, tq), 0)
            qpos = q_lo + lax.broadcasted_iota(jnp.int32, (tk, tq), 1)
            vis = kpos <= qpos
        tqs = min(tq, FOX_TQ_SUB)
        ps, alphas = [], []
        for h in range(A_HEADS):
            kp = jnp.where(in_xtra(lane_k, h), ek, k)
            for qs in range(tq // tqs):
                cols = slice(qs * tqs, (qs + 1) * tqs)
                s = lax.dot_general(kp, qp_scr[h, cols, :], (((1,), (1,)), ((), ())),
                                    preferred_element_type=F32)
                if masked:
                    s = jnp.where(vis[:, cols], s, NEG_BIG)
                m_prev = m_scr[h, 0:1, cols]
                m_next = jnp.maximum(m_prev, jnp.max(s, axis=0, keepdims=True))
                alpha = jnp.exp(m_prev - m_next)
                p = jnp.exp(s - m_next)
                l_scr[h, :, cols] = jnp.broadcast_to(
                    alpha * l_scr[h, 0:1, cols] + jnp.sum(p, axis=0, keepdims=True), (8, tqs))
                m_scr[h, :, cols] = jnp.broadcast_to(m_next, (8, tqs))
                ps.append(p.astype(BF16))
                alphas.append(alpha)
        alphas = [jnp.concatenate(alphas[h * (tq // tqs):(h + 1) * (tq // tqs)], axis=1) for h in range(A_HEADS)]
        pv = lax.dot_general(v, jnp.concatenate(ps, axis=1), (((0,), (0,)), ((), ())),
                             preferred_element_type=F32)
        for h in range(A_HEADS):
            acc_scr[h] = acc_scr[h] * alphas[h] + pv[:, h * tq:(h + 1) * tq]

    n_full = (q_lo + 1) // tk
    n_vis = jnp.minimum((q_lo + tq - 1) // tk + 1, nk)

    def full_body(jj, carry):
        block(jj, False)
        return carry

    def edge_body(jj, carry):
        block(jj, True)
        return carry

    lax.fori_loop(0, n_full, full_body, 0)
    lax.fori_loop(n_full, n_vis, edge_body, 0)

    row = lax.broadcasted_iota(jnp.int32, (A_W, tq), 0)
    out_t = jnp.zeros((A_W, tq), F32)
    for h in range(A_HEADS):
        out_t = out_t + jnp.where(in_head(row, h), acc_scr[h] / l_scr[h, 0:1, :], 0.0)
    o_ref[0] = out_t.T.astype(o_ref.dtype)


def _fox(q_src, q_col, eq, kb, ek, vb, offset):
    B, Lq, _ = q_src.shape
    Lk = kb.shape[1]
    tq = _tile(Lq, FOX_TQ)
    tk = _tile(Lk, FOX_TK)
    assert tq % LANES == 0 and tk % LANES == 0
    nq, nk = Lq // tq, Lk // tk
    whole = lambda b, i: (b, 0, 0)
    kern = functools.partial(_fox_kernel, tq=tq, tk=tk, offset=offset, nk=nk)
    return pl.pallas_call(
        kern,
        grid=(B, nq),
        in_specs=[pl.BlockSpec((1, tq, A_W), lambda b, i: (b, i, q_col // A_W)),
                  pl.BlockSpec((1, tq, A_W), lambda b, i: (b, i, 0)),
                  pl.BlockSpec((1, Lk, A_W), whole),
                  pl.BlockSpec((1, Lk, A_W), whole),
                  pl.BlockSpec((1, Lk, A_W), whole)],
        out_specs=pl.BlockSpec((1, tq, A_W), lambda b, i: (b, i, 0)),
        out_shape=jax.ShapeDtypeStruct((B, Lq, A_W), BF16),
        scratch_shapes=[pltpu.VMEM((A_HEADS, tq, A_W), BF16),
                        pltpu.VMEM((A_HEADS, 8, tq), F32),
                        pltpu.VMEM((A_HEADS, 8, tq), F32),
                        pltpu.VMEM((A_HEADS, A_W, tq), F32)],
        compiler_params=_cparams("parallel", "arbitrary"),
        name="fox_attention",
    )(q_src, eq, kb, ek, vb)


def _ret_lane_head_v():
    return lax.broadcasted_iota(jnp.int32, (1, B_W), 1) // B_HEAD_DIM


def _ret_lane_head_qk():
    return (lax.broadcasted_iota(jnp.int32, (1, B_W), 1) % LANES) // (B_HEAD_DIM // 2)


def _log_gamma(h):
    return math.log1p(-2.0 ** (-5.0 - h))


def _ret_kernel(q_ref, k_ref, v_ref, g_ref, s0_ref, invf_ref, gret_ref, y_ref, sout_ref,
                state_scr, dec_scr, *, lc, offset, nchunks):
    c = pl.program_id(1)

    @pl.when(c == 0)
    def _():
        state_scr[...] = s0_ref[0]
        r = lax.broadcasted_iota(jnp.int32, (lc, lc), 0)
        s = lax.broadcasted_iota(jnp.int32, (lc, lc), 1)
        d = (r - s).astype(F32)
        for h in range(B_HEADS):
            dec_scr[h] = jnp.where(r >= s, jnp.exp(d * _log_gamma(h)), 0.0)

    hv = _ret_lane_head_v()
    hqk = _ret_lane_head_qk()
    lg_v = jnp.zeros((1, B_W), F32)
    lg_qk = jnp.zeros((1, B_W), F32)
    for h in range(B_HEADS):
        lg_v = jnp.where(hv == h, _log_gamma(h), lg_v)
        lg_qk = jnp.where(hqk == h, _log_gamma(h), lg_qk)

    pos = (offset + c * lc + lax.broadcasted_iota(jnp.int32, (lc, LANES), 0)).astype(F32)
    ang = pos * invf_ref[...]
    cos, sin = jnp.cos(ang), jnp.sin(ang)

    def rot(x):
        x1, x2 = x[:, :LANES], x[:, LANES:]
        return jnp.concatenate([x1 * cos - x2 * sin, x1 * sin + x2 * cos], axis=1)

    qr = rot(q_ref[0])
    kr = rot(k_ref[0]) * (B_HEAD_DIM ** -0.5)
    v = v_ref[0]
    vb = v.astype(BF16)
    krb = kr.astype(BF16)

    lpos = lax.broadcasted_iota(jnp.int32, (lc, B_W), 0).astype(F32)
    state = state_scr[...]
    y = jnp.dot(qr.astype(BF16), state.astype(BF16), preferred_element_type=F32) * jnp.exp((lpos + 1.0) * lg_v)
    for h in range(B_HEADS):
        qh = jnp.where(hqk == h, qr, 0.0).astype(BF16)
        sc = lax.dot_general(qh, krb, (((1,), (1,)), ((), ())), preferred_element_type=F32) * dec_scr[h]
        yh = jnp.dot(sc.astype(BF16), vb, preferred_element_type=F32)
        y = y + jnp.where(hv == h, yh, 0.0)

    kw = (kr * jnp.exp((lc - 1.0 - lpos) * lg_qk)).astype(BF16)
    upd = lax.dot_general(kw, vb, (((0,), (0,)), ((), ())), preferred_element_type=F32)
    row_head = (lax.broadcasted_iota(jnp.int32, (B_W, B_W), 0) % LANES) // (B_HEAD_DIM // 2)
    col_head = lax.broadcasted_iota(jnp.int32, (B_W, B_W), 1) // B_HEAD_DIM
    lg_rows = jnp.zeros((B_W, B_W), F32)
    for h in range(B_HEADS):
        lg_rows = jnp.where(row_head == h, _log_gamma(h), lg_rows)
    new_state = jnp.where(row_head == col_head, state * jnp.exp(lc * lg_rows) + upd, 0.0)
    state_scr[...] = new_state

    @pl.when(c == nchunks - 1)
    def _():
        sout_ref[0] = new_state

    ri = lax.broadcasted_iota(jnp.int32, (B_W, B_W), 0) // B_HEAD_DIM
    ci = lax.broadcasted_iota(jnp.int32, (B_W, B_W), 1) // B_HEAD_DIM
    avg = jnp.where(ri == ci, 1.0 / B_HEAD_DIM, 0.0).astype(BF16)
    yc = y - _dot_sel_rhs(y, avg)
    var = _dot_sel_rhs(yc * yc, avg)
    yn = yc * lax.rsqrt(var + EPS) * gret_ref[...]
    y_ref[0] = (jax.nn.silu(g_ref[0]) * yn).astype(y_ref.dtype)


def _retention(proj3, s0, invf, g_ret, offset):
    B, L, _ = proj3.shape
    lc = _tile(L, 256)
    nch = L // lc
    blk = lambda col: pl.BlockSpec((1, lc, B_W), lambda b, c: (b, c, col // B_W))
    kern = functools.partial(_ret_kernel, lc=lc, offset=offset, nchunks=nch)
    return pl.pallas_call(
        kern,
        grid=(B, nch),
        in_specs=[blk(COL_QB), blk(COL_KB), blk(COL_VB), blk(COL_GB),
                  pl.BlockSpec((1, B_W, B_W), lambda b, c: (b, 0, 0)),
                  pl.BlockSpec((1, LANES), lambda b, c: (0, 0)),
                  pl.BlockSpec((1, B_W), lambda b, c: (0, 0))],
        out_specs=[pl.BlockSpec((1, lc, B_W), lambda b, c: (b, c, 0)),
                   pl.BlockSpec((1, B_W, B_W), lambda b, c: (b, 0, 0))],
        out_shape=[jax.ShapeDtypeStruct((B, L, B_W), BF16),
                   jax.ShapeDtypeStruct((B, B_W, B_W), F32)],
        scratch_shapes=[pltpu.VMEM((B_W, B_W), F32),
                        pltpu.VMEM((B_HEADS, lc, lc), F32)],
        compiler_params=_cparams("parallel", "arbitrary"),
        name="retention",
    )(proj3, proj3, proj3, proj3, s0, invf, g_ret.reshape(1, B_W))


def _ret_state_to_kernel(s):
    B = s.shape[0]
    half = B_HEAD_DIM // 2
    s = s.reshape(B, B_HEADS, 2, half, B_HEAD_DIM)
    eye = jnp.eye(B_HEADS, dtype=s.dtype)
    full = jnp.einsum('bhkie,hg->bkhige', s, eye)
    return full.reshape(B, B_W, B_W)


def _ret_state_from_kernel(s):
    B = s.shape[0]
    half = B_HEAD_DIM // 2
    s = s.reshape(B, 2, B_HEADS, half, B_HEADS, B_HEAD_DIM)
    d = jnp.einsum('bkhihe->bhkie', s)
    return d.reshape(B, B_HEADS, B_HEAD_DIM, B_HEAD_DIM)


SSD_CHUNK = 256


def _ssd_kernel(z_ref, xbc_ref, dt_ref, s0_ref, cs0_ref, cw_ref, cb_ref, dtb_ref, alog_ref, dsk_ref, gs_ref,
                y_ref, sout_ref, state_scr, xp_scr, *, lc, nchunks):
    c = pl.program_id(1)
    PADR = 8

    @pl.when(c == 0)
    def _():
        state_scr[...] = s0_ref[0]
        xp_scr[0:PADR, :] = cs0_ref[0]

    xbc = xbc_ref[0]
    xp_scr[PADR:PADR + lc, :] = xbc
    cw = cw_ref[...]
    conv = xp_scr[PADR - 3:PADR - 3 + lc, :] * cw[0:1, :]
    for jj in range(1, C_CONV):
        conv = conv + xp_scr[PADR - 3 + jj:PADR - 3 + jj + lc, :] * cw[jj:jj + 1, :]
    act = jax.nn.silu(conv + cb_ref[...])
    xp_scr[0:PADR, :] = xp_scr[lc:lc + PADR, :]

    xs = act[:, :C_INNER]
    bm = act[:, C_INNER:C_INNER + G_W]
    cm = act[:, C_INNER + G_W:]
    dt = jax.nn.softplus(dt_ref[0] + dtb_ref[...])
    loga = dt * (-jnp.exp(alog_ref[...]))

    r = lax.broadcasted_iota(jnp.int32, (lc, lc), 0)
    s = lax.broadcasted_iota(jnp.int32, (lc, lc), 1)
    causal = r >= s
    tri = jnp.where(causal, 1.0, 0.0).astype(BF16)
    cum = _dot_sel(tri, loga)
    total = cum[lc - 1:lc, :]
    xdt = xs * dt
    xdtb = xdt.astype(BF16)
    bmb = bm.astype(BF16)
    cmb = cm.astype(BF16)
    state = state_scr[...]

    gw = C_INNER // C_GROUPS
    cum_t = [cum[:, kk * LANES:(kk + 1) * LANES].T for kk in range(C_INNER // LANES)]
    ys = []
    for g in range(C_GROUPS):
        cg = cmb[:, g * C_STATE:(g + 1) * C_STATE]
        bg = bmb[:, g * C_STATE:(g + 1) * C_STATE]
        scores = lax.dot_general(cg, bg, (((1,), (1,)), ((), ())), preferred_element_type=F32)
        y_g = jnp.dot(cg, state[:, g * gw:(g + 1) * gw].astype(BF16), preferred_element_type=F32)
        y_g = y_g * jnp.exp(cum[:, g * gw:(g + 1) * gw])
        lane_head = lax.broadcasted_iota(jnp.int32, (1, gw), 1) // C_HEAD_DIM
        xg = xdtb[:, g * gw:(g + 1) * gw]
        for hh in range(C_HEADS // C_GROUPS):
            col = g * gw + hh * C_HEAD_DIM
            cum_col = cum[:, col:col + 1]
            cum_row = cum_t[col // LANES][col % LANES:col % LANES + 1, :]
            dec = jnp.exp(jnp.where(causal, cum_col - cum_row, NEG_BIG))
            yh = jnp.dot((scores * dec).astype(BF16), xg, preferred_element_type=F32)
            y_g = y_g + jnp.where(lane_head == hh, yh, 0.0)
        ys.append(y_g)
        xw = (xdt[:, g * gw:(g + 1) * gw] * jnp.exp(total[:, g * gw:(g + 1) * gw] - cum[:, g * gw:(g + 1) * gw]))
        upd = lax.dot_general(bg, xw.astype(BF16), (((0,), (0,)), ((), ())), preferred_element_type=F32)
        state_scr[:, g * gw:(g + 1) * gw] = state[:, g * gw:(g + 1) * gw] * jnp.exp(total[:, g * gw:(g + 1) * gw]) + upd
    y = jnp.concatenate(ys, axis=1)

    @pl.when(c == nchunks - 1)
    def _():
        sout_ref[0] = state_scr[...]

    yc = (y + xs * dsk_ref[...]) * jax.nn.silu(z_ref[0])
    y_ref[0] = _rms(yc, gs_ref[...]).astype(y_ref.dtype)


def _ssd(proj3, s0, cs0, conv_w, conv_b, dtb_full, a_log, d_skip, g_ssm):
    B, L, _ = proj3.shape
    lc = _tile(L, SSD_CHUNK)
    nch = L // lc
    rep = lambda p: jnp.repeat(p, C_HEAD_DIM).reshape(1, C_INNER)
    kern = functools.partial(_ssd_kernel, lc=lc, nchunks=nch)
    cst = lambda shape: pl.BlockSpec(shape, lambda b, c: (0,) * len(shape))
    return pl.pallas_call(
        kern,
        grid=(B, nch),
        in_specs=[pl.BlockSpec((1, lc, C_INNER), lambda b, c: (b, c, COL_ZC // C_INNER)),
                  pl.BlockSpec((1, lc, C_CONV_DIM), lambda b, c: (b, c, COL_XBC // C_CONV_DIM)),
                  pl.BlockSpec((1, lc, C_INNER), lambda b, c: (b, c, COL_DT // C_INNER)),
                  pl.BlockSpec((1, C_STATE, C_INNER), lambda b, c: (b, 0, 0)),
                  pl.BlockSpec((1, 8, C_CONV_DIM), lambda b, c: (b, 0, 0)),
                  cst((C_CONV, C_CONV_DIM)), cst((1, C_CONV_DIM)), cst((1, C_INNER)), cst((1, C_INNER)),
                  cst((1, C_INNER)), cst((1, C_INNER))],
        out_specs=[pl.BlockSpec((1, lc, C_INNER), lambda b, c: (b, c, 0)),
                   pl.BlockSpec((1, C_STATE, C_INNER), lambda b, c: (b, 0, 0))],
        out_shape=[jax.ShapeDtypeStruct((B, L, C_INNER), BF16),
                   jax.ShapeDtypeStruct((B, C_STATE, C_INNER), F32)],
        scratch_shapes=[pltpu.VMEM((C_STATE, C_INNER), F32),
                        pltpu.VMEM((lc + 8, C_CONV_DIM), F32)],
        compiler_params=_cparams("parallel", "arbitrary"),
        name="ssd",
    )(proj3, proj3, proj3, s0, cs0, conv_w, conv_b.reshape(1, C_CONV_DIM), dtb_full, rep(a_log), rep(d_skip),
      g_ssm.reshape(1, C_INNER))


PEER_SUB = 128
PEER_STEP = 2048
PEER_CAND_COLS = 4


def _gelu_tanh(x):
    k0 = math.sqrt(2.0 / math.pi)
    inner = x * (k0 + (k0 * 0.044715) * (x * x))
    return (0.5 * x) * (1.0 + jnp.tanh(inner))


def _sort_pairs(n):
    def merge(lo, hi, r):
        step = r * 2
        if step < hi - lo:
            yield from merge(lo, hi, step)
            yield from merge(lo + r, hi, step)
            yield from [(i, i + r) for i in range(lo + r, hi - r, step)]
        else:
            yield (lo, lo + r)

    def sort(lo, hi):
        if hi - lo >= 1:
            mid = lo + (hi - lo) // 2
            yield from sort(lo, mid)
            yield from sort(mid + 1, hi)
            yield from merge(lo, hi, 1)

    return tuple(sort(0, n - 1))


def _top16_values(x):
    v = [x[8 * g:8 * g + 8, :] for g in range(PEER_NKEYS // 8)]
    for a, b in _sort_pairs(PEER_NKEYS // 8):
        v[a], v[b] = jnp.maximum(v[a], v[b]), jnp.minimum(v[a], v[b])
    vals = []
    for r in range(PEER_TOPK):
        m = jnp.max(v[0], axis=0, keepdims=True)
        vals.append(m)
        hit = v[0] == m
        for k in range(PEER_TOPK - 1 - r):
            v[k] = jnp.where(hit, v[k + 1], v[k])
    return jnp.concatenate(vals, axis=0)


def _peer_select(s1, s2):
    n = s1.shape[1]
    a16 = _top16_values(s1)
    b16 = _top16_values(s2)
    rank2 = jnp.zeros(s2.shape, F32)
    for r in range(PEER_TOPK):
        rank2 = rank2 + jnp.where(b16[r:r + 1, :] > s2, 1.0, 0.0)
    ridx = lax.broadcasted_iota(jnp.int32, (PEER_TOPK, n), 0)
    sub = lax.broadcasted_iota(jnp.int32, (8, n), 0)
    lo, hi = slice(0, 8), slice(8, PEER_TOPK)
    col = lambda r2, rows: a16[rows, :] + b16[r2:r2 + 1, :]
    row = lambda r1, cols: b16[cols, :] + a16[r1:r1 + 1, :]
    assert PEER_CAND_COLS == 4 and PEER_TOPK == 16
    groups = [col(0, lo), col(0, hi), col(1, lo),
              jnp.where(sub <= 4, col(2, lo), NEG_BIG),
              jnp.where(sub <= 3, col(3, lo), row(1, lo)),
              jnp.where(sub >= 4, row(0, lo), NEG_BIG),
              row(0, hi),
              jnp.where(sub == 4, row(2, lo), NEG_BIG)]
    cand = jnp.concatenate(groups, axis=0)
    v = list(groups)
    for a_i, b_i in _sort_pairs(8):
        v[a_i], v[b_i] = jnp.maximum(v[a_i], v[b_i]), jnp.minimum(v[a_i], v[b_i])
    tau = None
    for r in range(PEER_TOPK):
        tau = jnp.max(v[0], axis=0, keepdims=True)
        hit = v[0] == tau
        depth = min(8, PEER_TOPK - r)
        for k in range(depth - 1):
            v[k] = jnp.where(hit, v[k + 1], v[k])
        if depth == 8:
            v[7] = jnp.where(hit, NEG_BIG, v[7])
    top = a16[0:1, :] + b16[0:1, :]
    z = jnp.sum(jnp.where(cand >= tau, jnp.exp(cand - top), 0.0), axis=0, keepdims=True)
    cnt = jnp.zeros(s1.shape, F32)
    for r2 in range(PEER_CAND_COLS):
        cnt = cnt + jnp.where(s1 + b16[r2:r2 + 1, :] >= tau, 1.0, 0.0)
    for r1 in range(PEER_TOPK // (PEER_CAND_COLS + 1)):
        a_r = a16[r1:r1 + 1, :]
        tail = jnp.where((ridx >= PEER_CAND_COLS) & (b16 + a_r >= tau), 1.0, 0.0)
        cnt = cnt + jnp.where(s1 == a_r, jnp.sum(tail, axis=0, keepdims=True), 0.0)
    a = jnp.exp(s1 - a16[0:1, :]) / z
    b = jnp.exp(s2 - b16[0:1, :])
    return a, cnt, rank2, b


def _peer_kernel(h_ref, ya_ref, yb_ref, yc_ref, wo_ref, g_ref, wq_ref, k1_ref, k2_ref, u_ref, vt_ref,
                 p_ref, gp_ref, wg_ref, wp_ref, gf_ref, o_ref,
                 xt_scr, a_scr, cnt_scr, rank_scr, b_scr, acc_scr, h1_scr, *, tm, te, ne, final_norm):
    j = pl.program_id(1)
    n1 = te // PEER_NKEYS
    nchunk = tm // LANES
    nsub = te // PEER_SUB
    slabs = PEER_SUB // PEER_NKEYS

    def gate_weights(sb):
        ws = []
        for il in range(slabs):
            i1 = j * n1 + sb * slabs + il
            w = jnp.zeros((PEER_NKEYS, tm), BF16)
            for h in range(PEER_HEADS):
                row = lambda ref: jnp.concatenate(
                    [jnp.broadcast_to(ref[h, cch, pl.ds(i1, 1), :], (BF16_ROWS, LANES)) for cch in range(nchunk)],
                    axis=1).astype(BF16)
                cnt_b = jnp.tile(row(cnt_scr), (PEER_NKEYS // BF16_ROWS, 1))
                a_b = jnp.tile(row(a_scr), (PEER_NKEYS // BF16_ROWS, 1))
                w = w + jnp.where(rank_scr[h] < cnt_b, b_scr[h], jnp.zeros((), BF16)) * a_b
            ws.append(w)
        return jnp.concatenate(ws, axis=0)

    @pl.when(j == 0)
    def _():
        h1 = h_ref[...]
        h1 = h1 + jnp.dot(ya_ref[...], wo_ref[0:A_W, :], preferred_element_type=F32)
        h1 = h1 + jnp.dot(yb_ref[...], wo_ref[A_W:A_W + B_W, :], preferred_element_type=F32)
        h1 = h1 + jnp.dot(yc_ref[...], wo_ref[A_W + B_W:, :], preferred_element_type=F32)
        h1_scr[...] = h1
        xn = _rms(h1, g_ref[...])
        xt_scr[...] = xn.T.astype(BF16)
        acc_scr[...] = jnp.zeros(acc_scr.shape, F32)

        def head_body(h, carry):
            base = pl.multiple_of(h * PEER_KEY_DIM, PEER_KEY_DIM)
            k1 = k1_ref[h].astype(BF16)
            k2 = k2_ref[h].astype(BF16)
            qh = jnp.dot(wq_ref[pl.ds(base, PEER_KEY_DIM), :], xt_scr[...], preferred_element_type=F32)
            for cch in range(nchunk):
                sl = slice(cch * LANES, (cch + 1) * LANES)
                q1 = qh[0:PEER_HALF, sl].astype(BF16)
                q2 = qh[PEER_HALF:, sl].astype(BF16)
                s1 = jnp.dot(k1, q1, preferred_element_type=F32)
                s2 = jnp.dot(k2, q2, preferred_element_type=F32)
                a, cnt, rank2, b = _peer_select(s1, s2)
                a_scr[h, cch] = a
                cnt_scr[h, cch] = cnt
                rank_scr[h, :, sl] = rank2.astype(BF16)
                b_scr[h, :, sl] = b.astype(BF16)
            return carry

        lax.fori_loop(0, PEER_HEADS, head_body, 0)

    acts = []
    for sb in range(nsub):
        ht = jnp.dot(u_ref[sb * PEER_SUB:(sb + 1) * PEER_SUB, :], xt_scr[...], preferred_element_type=F32)
        acts.append(_gelu_tanh(ht.astype(BF16)) * gate_weights(sb))
    acc_scr[...] += jnp.dot(vt_ref[...], jnp.concatenate(acts, axis=0), preferred_element_type=F32)

    @pl.when(j == ne - 1)
    def _():
        h2 = h1_scr[...] + acc_scr[...].T
        xn = _rms(h2, gp_ref[...])
        gate = jax.nn.sigmoid(jnp.dot(xn.astype(BF16), wg_ref[...], preferred_element_type=F32))
        emb = jnp.dot(p_ref[...].astype(BF16), wp_ref[...], preferred_element_type=F32)
        out = h2 + emb * gate
        if final_norm:
            out = _rms(out, gf_ref[...])
        o_ref[...] = out


def _layer_cast_kernel(x_ref, o_ref, *, transpose):
    x = x_ref[...]
    o_ref[...] = (x.T if transpose else x).astype(o_ref.dtype)


def _layer_cast(x_all, layer, dtype, transpose):
    _, R, C = x_all.shape
    tr = _tile(R, 512)
    return pl.pallas_call(
        functools.partial(_layer_cast_kernel, transpose=transpose),
        grid=(R // tr,),
        in_specs=[pl.BlockSpec((None, tr, C), lambda i: (layer, i, 0))],
        out_specs=pl.BlockSpec((C, tr), lambda i: (0, i)) if transpose else pl.BlockSpec((tr, C), lambda i: (i, 0)),
        out_shape=jax.ShapeDtypeStruct((C, R) if transpose else (R, C), dtype),
        compiler_params=_cparams("parallel"),
        name="layer_cast",
    )(x_all)


def _peer(h2d, ya, yb, yc, w_out_bf16, g, wq_t, k1, k2, u_bf16, vt_bf16,
          p_all, layer, g_ple, wg_bf16, wp_bf16, g_final, final_norm):
    T = h2d.shape[0]
    row = lambda w: pl.BlockSpec((tm, w), lambda i, j: (i, 0))
    tm = _tile(T, 512)
    assert tm % LANES == 0
    te = PEER_STEP
    ne = PEER_EXPERTS // te
    kern = functools.partial(_peer_kernel, tm=tm, te=te, ne=ne, final_norm=final_norm)
    hk = (PEER_HEADS, PEER_NKEYS, tm)
    hkc = (PEER_HEADS, tm // LANES, PEER_NKEYS, LANES)
    return pl.pallas_call(
        kern,
        grid=(T // tm, ne),
        in_specs=[row(D_MODEL), row(A_W), row(B_W), row(C_INNER),
                  pl.BlockSpec((D_MODEL, D_MODEL), lambda i, j: (0, 0)),
                  pl.BlockSpec((1, D_MODEL), lambda i, j: (0, 0)),
                  pl.BlockSpec((PEER_HEADS * PEER_KEY_DIM, D_MODEL), lambda i, j: (0, 0)),
                  pl.BlockSpec((PEER_HEADS, PEER_NKEYS, PEER_HALF), lambda i, j: (0, 0, 0)),
                  pl.BlockSpec((PEER_HEADS, PEER_NKEYS, PEER_HALF), lambda i, j: (0, 0, 0)),
                  pl.BlockSpec((te, D_MODEL), lambda i, j: (j, 0)),
                  pl.BlockSpec((D_MODEL, te), lambda i, j: (0, j)),
                  pl.BlockSpec((None, tm, PLE_DIM), lambda i, j: (layer, i, 0)),
                  pl.BlockSpec((1, D_MODEL), lambda i, j: (0, 0)),
                  pl.BlockSpec((D_MODEL, D_MODEL), lambda i, j: (0, 0)),
                  pl.BlockSpec((PLE_DIM, D_MODEL), lambda i, j: (0, 0)),
                  pl.BlockSpec((1, D_MODEL), lambda i, j: (0, 0))],
        out_specs=pl.BlockSpec((tm, D_MODEL), lambda i, j: (i, 0)),
        out_shape=jax.ShapeDtypeStruct((T, D_MODEL), F32),
        scratch_shapes=[pltpu.VMEM((D_MODEL, tm), BF16),
                        pltpu.VMEM(hkc, F32), pltpu.VMEM(hkc, F32), pltpu.VMEM(hk, BF16), pltpu.VMEM(hk, BF16),
                        pltpu.VMEM((D_MODEL, tm), F32),
                        pltpu.VMEM((tm, D_MODEL), F32)],
        compiler_params=_cparams("parallel", "arbitrary"),
        name="peer",
    )(h2d, ya, yb, yc, w_out_bf16, g.reshape(1, D_MODEL), wq_t, k1, k2, u_bf16, vt_bf16,
      p_all, g_ple.reshape(1, D_MODEL), wg_bf16, wp_bf16, g_final.reshape(1, D_MODEL))


def _ple_kernel(h_ref, p_ref, g_ref, wg_ref, wp_ref, gf_ref, o_ref, *, final_norm):
    h = h_ref[...]
    xn = _rms(h, g_ref[...])
    gate = jax.nn.sigmoid(jnp.dot(xn.astype(BF16), wg_ref[...], preferred_element_type=F32))
    emb = jnp.dot(p_ref[...].astype(BF16), wp_ref[...], preferred_element_type=F32)
    out = h + emb * gate
    if final_norm:
        out = _rms(out, gf_ref[...])
    o_ref[...] = out


def _ple(h2d, p_all, layer, g, wg_bf16, wp_bf16, g_final, final_norm):
    T = h2d.shape[0]
    tm = _tile(T, 512)
    kern = functools.partial(_ple_kernel, final_norm=final_norm)
    return pl.pallas_call(
        kern,
        grid=(T // tm,),
        in_specs=[pl.BlockSpec((tm, D_MODEL), lambda i: (i, 0)),
                  pl.BlockSpec((None, tm, PLE_DIM), lambda i: (layer, i, 0)),
                  pl.BlockSpec((1, D_MODEL), lambda i: (0, 0)),
                  pl.BlockSpec((D_MODEL, D_MODEL), lambda i: (0, 0)),
                  pl.BlockSpec((PLE_DIM, D_MODEL), lambda i: (0, 0)),
                  pl.BlockSpec((1, D_MODEL), lambda i: (0, 0))],
        out_specs=pl.BlockSpec((tm, D_MODEL), lambda i: (i, 0)),
        out_shape=jax.ShapeDtypeStruct((T, D_MODEL), F32),
        compiler_params=_cparams("parallel"),
        name="ple",
    )(h2d, p_all, g.reshape(1, D_MODEL), wg_bf16, wp_bf16, g_final.reshape(1, D_MODEL))


def _pad_to(x, n, axis):
    pad = n - x.shape[axis]
    if pad == 0:
        return x
    widths = [(0, 0)] * x.ndim
    widths[axis] = (0, pad)
    return jnp.pad(x, widths)


def _prep_layer_weights(weights, i):
    (g_mix, w_in, b_forget, g_ret, conv_w, conv_b, dt_bias, a_log, d_skip, g_ssm, w_out,
     g_ffn, peer_wq, peer_k1, peer_k2, peer_u, peer_v, g_ple, w_ple_gate, w_ple, g_final) = weights
    w_proj, bf_pad, dtb_full = _prep_w_in(w_in, i, b_forget[i], dt_bias[i])
    return dict(w_proj=w_proj, bf_pad=bf_pad, dtb_full=dtb_full, w_out=w_out[i].astype(BF16),
                wq_t=_layer_cast(peer_wq, i, BF16, True), u=_layer_cast(peer_u, i, BF16, False),
                vt=_layer_cast(peer_v, i, BF16, True),
                wg=w_ple_gate[i].astype(BF16), wp=w_ple[i].astype(BF16))


def _run_trunk(x, p, past, weights, prepped):
    (g_mix, w_in, b_forget, g_ret, conv_w, conv_b, dt_bias, a_log, d_skip, g_ssm, w_out,
     g_ffn, peer_wq, peer_k1, peer_k2, peer_u, peer_v, g_ple, w_ple_gate, w_ple, g_final) = weights
    B, L, _ = x.shape
    depth = w_in.shape[0]
    T = B * L
    P = 0 if past is None else past[0].shape[2]
    half = B_HEAD_DIM // 2
    invf = jnp.tile(ROPE_BASE ** (-jnp.arange(half, dtype=F32) / half), LANES // half).reshape(1, LANES)

    h = x.reshape(T, D_MODEL)
    outs = []
    for i in range(depth):
        pw = prepped[i]
        dtb_full = pw["dtb_full"]
        proj, lf, kb, vb, k_new, v_new = _inproj(h, g_mix[i], pw["w_proj"], pw["bf_pad"])
        proj3 = proj.reshape(B, L, PROJ_W)
        lf3 = lf.reshape(B, L, LANES)
        kb = kb.reshape(B, L, A_W)
        vb = vb.reshape(B, L, A_W)

        if past is None:
            eq, ek = _cumsum(lf3)
            ya = _fox(proj3, COL_QA, eq, kb, ek, vb, 0)
            rs0 = jnp.zeros((B, B_W, B_W), F32)
            ss0 = jnp.zeros((B, C_STATE, C_INNER), F32)
            cs0 = jnp.zeros((B, 8, C_CONV_DIM), F32)
        else:
            lk_pad = -(-(P + L) // FOX_TK) * FOX_TK
            lq_pad = -(-L // LANES) * LANES
            past_k = past[0][i].reshape(B, P, A_W).astype(BF16)
            past_v = past[1][i].reshape(B, P, A_W).astype(BF16)
            past_lf = _pad_to(past[2][i], LANES, 2)
            k_all = _pad_to(jnp.concatenate([past_k, kb], axis=1), lk_pad, 1)
            v_all = _pad_to(jnp.concatenate([past_v, vb], axis=1), lk_pad, 1)
            lf_all = _pad_to(jnp.concatenate([past_lf, lf3], axis=1), lk_pad, 1)
            eq, ek = _cumsum(lf_all)
            q_pad = _pad_to(proj3[:, :, COL_QA:COL_QA + A_W], lq_pad, 1)
            ya = _fox(q_pad, 0, _pad_to(eq[:, P:P + L], lq_pad, 1), k_all, ek, v_all, P)[:, :L]
            rs0 = _ret_state_to_kernel(past[3][i])
            ss0 = jnp.transpose(past[4][i], (0, 2, 1, 3)).reshape(B, C_STATE, C_INNER)
            cs0 = jnp.concatenate([jnp.zeros((B, 8 - (C_CONV - 1), C_CONV_DIM), F32), past[5][i]], axis=1)

        yb, ret_k = _retention(proj3, rs0, invf, g_ret[i], P)
        ret_new = _ret_state_from_kernel(ret_k)

        yc, ssm_k = _ssd(proj3, ss0, cs0, conv_w[i], conv_b[i], dtb_full, a_log[i], d_skip[i], g_ssm[i])
        ssm_new = jnp.transpose(ssm_k.reshape(B, C_STATE, C_HEADS, C_HEAD_DIM), (0, 2, 1, 3))
        xbc = proj3[:, :, COL_XBC:COL_XBC + C_CONV_DIM]
        if L >= C_CONV - 1:
            conv_new = xbc[:, L - (C_CONV - 1):]
        else:
            prev = jnp.zeros((B, C_CONV - 1, C_CONV_DIM), F32) if past is None else past[5][i]
            conv_new = jnp.concatenate([prev, xbc], axis=1)[:, -(C_CONV - 1):]

        h = _peer(h, ya.reshape(T, A_W), yb.reshape(T, B_W), yc.reshape(T, C_INNER), pw["w_out"],
                  g_ffn[i], pw["wq_t"], peer_k1[i], peer_k2[i], pw["u"], pw["vt"],
                  p.reshape(depth, T, PLE_DIM), i, g_ple[i], pw["wg"], pw["wp"], g_final,
                  final_norm=(i == depth - 1))
        outs.append((k_new.reshape(B, L, A_HEADS, A_HEAD_DIM), v_new.reshape(B, L, A_HEADS, A_HEAD_DIM),
                     lf3[:, :, :A_HEADS], ret_new, ssm_new, conv_new))
    stacked = tuple(jnp.stack([o[n] for o in outs]) for n in range(6))
    return h.reshape(B, L, D_MODEL), stacked


def kernel(x_prompt, x_sample, cache_k_fox, cache_v_fox, cache_logf_fox, state_ret, state_ssm, state_conv, p_prompt, p_sample, g_mix, w_in, b_forget, g_ret, conv_w, conv_b, dt_bias, a_log, d_skip, g_ssm, w_out, g_ffn, peer_wq, peer_k1, peer_k2, peer_u, peer_v, g_ple, w_ple_gate, w_ple, g_final):
    weights = (g_mix, w_in, b_forget, g_ret, conv_w, conv_b, dt_bias, a_log, d_skip, g_ssm, w_out,
               g_ffn, peer_wq, peer_k1, peer_k2, peer_u, peer_v, g_ple, w_ple_gate, w_ple, g_final)
    prepped = [_prep_layer_weights(weights, i) for i in range(w_in.shape[0])]
    y_prompt, (k_p, v_p, lf_p, ret_p, ssm_p, conv_p) = _run_trunk(x_prompt, p_prompt, None, weights, prepped)
    past = (cache_k_fox, cache_v_fox, cache_logf_fox, state_ret, state_ssm, state_conv)
    y_sample, (k_s, v_s, lf_s, ret_s, ssm_s, conv_s) = _run_trunk(x_sample, p_sample, past, weights, prepped)
    return (y_prompt, y_sample, k_p, v_p, lf_p, ret_p, ssm_p, conv_p, k_s, v_s, lf_s, ret_s, ssm_s, conv_s)
```
